```python
import jax, jax.numpy as jnp
from jax import lax
import numpy as np

D_MODEL = 1024
BATCH = 2
SEQ = 8192
DEPTH = 4

N_EVEN = (DEPTH + 1) // 2
N_ODD = DEPTH // 2
EPS = 1e-6
N_MOD = 6

ATTN_WIDTH = D_MODEL // 2
HEAD_DIM = 64
N_HEADS = ATTN_WIDTH // HEAD_DIM
Q_PER_KV = 4
N_KV_HEADS = N_HEADS // Q_PER_KV
KV_WIDTH = N_KV_HEADS * HEAD_DIM
WINDOW = 128
BLOCK = 128

CONV_WIDTH = D_MODEL - ATTN_WIDTH
CONV_K = 3

IN_WIDTH = ATTN_WIDTH + 2 * KV_WIDTH + 3 * CONV_WIDTH
SPLITS = (ATTN_WIDTH,
          ATTN_WIDTH + KV_WIDTH,
          ATTN_WIDTH + 2 * KV_WIDTH,
          ATTN_WIDTH + 2 * KV_WIDTH + CONV_WIDTH,
          ATTN_WIDTH + 2 * KV_WIDTH + 2 * CONV_WIDTH)
MIX_OUT_WIDTH = ATTN_WIDTH + CONV_WIDTH

POOL_SIZES = (2, 4, 8, 16)
N_POOL_GROUPS = len(POOL_SIZES)
POOL_GROUP = D_MODEL // N_POOL_GROUPS

N_EXPERTS = 16
N_EXPERT_GROUPS = 4
EXPERTS_PER_GROUP = N_EXPERTS // N_EXPERT_GROUPS
TOP_K = 2
EXPERT_FF = 256

kernel_name = "hybrid_swa_conv_pool_grouped_moe_adaln"


def rms_norm(x, g):
    xf = x.astype(jnp.float32)
    y = xf * lax.rsqrt(jnp.mean(xf * xf, axis=-1, keepdims=True) + EPS)
    return (y * g.astype(jnp.float32)).astype(x.dtype)


def alibi_slopes(n):
    return 2.0 ** (-8.0 * (jnp.arange(n, dtype=jnp.float32) + 1.0) / n)


def sliding_window_attention(q, k, v, sinks):
    b, s, _ = q.shape
    nb = s // BLOCK
    qb = q.reshape(b, nb, BLOCK, N_KV_HEADS, Q_PER_KV, HEAD_DIM).astype(jnp.float32)
    kb = k.reshape(b, nb, BLOCK, N_KV_HEADS, HEAD_DIM).astype(jnp.float32)
    vb = v.reshape(b, nb, BLOCK, N_KV_HEADS, HEAD_DIM).astype(jnp.float32)

    def with_prev(t):
        prev = jnp.pad(t, ((0, 0), (1, 0), (0, 0), (0, 0), (0, 0)))[:, :nb]
        return jnp.concatenate([prev, t], axis=2)

    kk, vv = with_prev(kb), with_prev(vb)
    scores = jnp.einsum('bnqkgd,bnskd->bnkgqs', qb, kk) * (HEAD_DIM ** -0.5)
    dist = (jnp.arange(BLOCK)[:, None] + BLOCK) - jnp.arange(2 * BLOCK)[None, :]
    in_window = (dist >= 0) & (dist < WINDOW)
    has_prev = (jnp.arange(nb)[:, None, None] > 0) | (jnp.arange(2 * BLOCK) >= BLOCK)[None, None, :]
    mask = in_window[None] & has_prev
    slopes = alibi_slopes(N_HEADS).reshape(N_KV_HEADS, Q_PER_KV)
    scores = scores - slopes[:, :, None, None] * dist.astype(jnp.float32)
    scores = jnp.where(mask[None, :, None, None], scores, -jnp.inf)
    sink = sinks.astype(jnp.float32).reshape(N_KV_HEADS, Q_PER_KV)[None, None, :, :, None]
    m = jnp.maximum(scores.max(axis=-1), sink)
    p = jnp.exp(scores - m[..., None])
    denom = p.sum(axis=-1) + jnp.exp(sink - m)
    out = jnp.einsum('bnkgqs,bnskd->bnqkgd', p, vv)
    out = out / jnp.transpose(denom, (0, 1, 4, 2, 3))[..., None]
    return out.reshape(b, s, ATTN_WIDTH).astype(q.dtype)


def short_conv_mixer(b_gate, c_gate, xv, w):
    u = c_gate * xv
    y = lax.conv_general_dilated(u, w[:, None, :].astype(u.dtype), window_strides=(1,),
                                 padding=[(CONV_K - 1, 0)],
                                 dimension_numbers=('NWC', 'WIO', 'NWC'),
                                 feature_group_count=CONV_WIDTH)
    return b_gate * y


def multiscale_pool_mixer(h, w_pool, scale):
    b, s, d = h.shape
    hf = h.astype(jnp.float32)
    csum = jnp.cumsum(hf, axis=1)
    pos = jnp.arange(1, s + 1, dtype=jnp.float32)
    outs = []
    for g, w in enumerate(POOL_SIZES):
        sl = slice(g * POOL_GROUP, (g + 1) * POOL_GROUP)
        cg = csum[..., sl]
        lag = jnp.pad(cg, ((0, 0), (w, 0), (0, 0)))[:, :s]
        mean = (cg - lag) / jnp.minimum(pos, float(w))[None, :, None]
        outs.append(mean - hf[..., sl])
    pooled = jnp.stack(outs, axis=2)
    mixed = jnp.einsum('bsgc,gce->bsge', pooled, w_pool.astype(jnp.float32)).reshape(b, s, d)
    return (mixed * scale.astype(jnp.float32)).astype(h.dtype)


def grouped_moe(h, w_router, b_router, w_gate, w_up, w_down):
    b, s, d = h.shape
    t = h.reshape(b * s, d)
    scores = jax.nn.sigmoid(jnp.dot(t.astype(jnp.float32), w_router.astype(jnp.float32)))
    biased = (scores + b_router.astype(jnp.float32)).reshape(-1, N_EXPERT_GROUPS, EXPERTS_PER_GROUP)
    group_score = lax.top_k(biased, TOP_K)[0].sum(axis=-1)
    best_group = jnp.argmax(group_score, axis=-1)
    in_group = best_group[:, None] == jnp.arange(N_EXPERT_GROUPS)[None, :]
    masked = jnp.where(in_group[:, :, None], biased, -jnp.inf).reshape(-1, N_EXPERTS)
    _, idx = lax.top_k(masked, TOP_K)
    wts = jnp.take_along_axis(scores, idx, axis=-1)
    wts = wts / wts.sum(axis=-1, keepdims=True)
    gates = jnp.sum(jax.nn.one_hot(idx, N_EXPERTS, dtype=jnp.float32) * wts[..., None], axis=1)
    out = jnp.zeros((b * s, d), jnp.float32)
    for e in range(N_EXPERTS):
        hid = jax.nn.silu(t @ w_gate[e]) * (t @ w_up[e])
        out = out + gates[:, e:e + 1] * (hid @ w_down[e]).astype(jnp.float32)
    return out.reshape(b, s, d).astype(h.dtype)


def setup_inputs(seed: int = 0) -> dict:
    key = jax.random.key(seed)
    ks = jax.random.split(key, 20)
    f32 = jnp.float32
    nrm = lambda k, shape, sc: jax.random.normal(k, shape, f32) * sc
    return {
        "x": nrm(ks[0], (BATCH, SEQ, D_MODEL), 1.0),
        "c": nrm(ks[1], (BATCH, D_MODEL), 1.0),
        "w_ada": nrm(ks[2], (DEPTH, D_MODEL, N_MOD * D_MODEL), 0.5 * D_MODEL ** -0.5),
        "b_ada": nrm(ks[3], (DEPTH, N_MOD * D_MODEL), 0.02),
        "norm_mix": 1.0 + nrm(ks[4], (DEPTH, D_MODEL), 0.05),
        "norm_ffn": 1.0 + nrm(ks[5], (DEPTH, D_MODEL), 0.05),
        "w_in": nrm(ks[6], (N_EVEN, D_MODEL, IN_WIDTH), D_MODEL ** -0.5),
        "w_out": nrm(ks[7], (N_EVEN, MIX_OUT_WIDTH, D_MODEL), MIX_OUT_WIDTH ** -0.5),
        "sinks": nrm(ks[8], (N_EVEN, N_HEADS), 0.5),
        "conv_w": nrm(ks[9], (N_EVEN, CONV_K, CONV_WIDTH), CONV_K ** -0.5),
        "w_pool": nrm(ks[10], (N_ODD, N_POOL_GROUPS, POOL_GROUP, POOL_GROUP), POOL_GROUP ** -0.5),
        "pool_scale": 1.0 + nrm(ks[11], (N_ODD, D_MODEL), 0.1),
        "w_router": nrm(ks[12], (D_MODEL, N_EXPERTS), D_MODEL ** -0.5),
        "b_router": nrm(ks[13], (N_EXPERTS,), 0.01),
        "w_gate": nrm(ks[14], (DEPTH, N_EXPERTS, D_MODEL, EXPERT_FF), D_MODEL ** -0.5),
        "w_up": nrm(ks[15], (DEPTH, N_EXPERTS, D_MODEL, EXPERT_FF), D_MODEL ** -0.5),
        "w_down": nrm(ks[16], (DEPTH, N_EXPERTS, EXPERT_FF, D_MODEL), EXPERT_FF ** -0.5),
        "norm_final": 1.0 + nrm(ks[17], (D_MODEL,), 0.05),
    }


def reference(x, c, w_ada, b_ada, norm_mix, norm_ffn, w_in, w_out, sinks, conv_w,
              w_pool, pool_scale, w_router, b_router, w_gate, w_up, w_down, norm_final):
    cond = jax.nn.silu(c)
    h = x
    for layer in range(DEPTH):
        mod = cond @ w_ada[layer] + b_ada[layer]
        sh1, sc1, g1, sh2, sc2, g2 = [m[:, None, :] for m in jnp.split(mod, N_MOD, axis=-1)]
        u = rms_norm(h, norm_mix[layer]) * (1 + sc1) + sh1
        j = layer // 2
        if layer % 2 == 0:
            proj = u @ w_in[j]
            q, k, v, bg, cg, xv = jnp.split(proj, SPLITS, axis=-1)
            attn = sliding_window_attention(q, k, v, sinks[j])
            conv = short_conv_mixer(bg, cg, xv, conv_w[j])
            mix = jnp.concatenate([attn, conv], axis=-1) @ w_out[j]
        else:
            mix = multiscale_pool_mixer(u, w_pool[j], pool_scale[j])
        h = h + g1 * mix
        u = rms_norm(h, norm_ffn[layer]) * (1 + sc2) + sh2
        h = h + g2 * grouped_moe(u, w_router, b_router, w_gate[layer], w_up[layer], w_down[layer])
    return rms_norm(h, norm_final)
```

```python
import functools

import jax
import jax.numpy as jnp
from jax import lax
from jax.experimental import pallas as pl
from jax.experimental.pallas import tpu as pltpu

F32 = jnp.float32
BF16 = jnp.bfloat16

D_MODEL = 1024
DEPTH = 4
EPS = 1e-6
N_MOD = 6

ATTN_WIDTH = 512
HEAD_DIM = 64
N_HEADS = 8
Q_PER_KV = 4
KV_WIDTH = 128
WINDOW = 128
BLOCK = 128
CONV_WIDTH = 512
CONV_K = 3
IN_WIDTH = 2304

POOL_SIZES = (2, 4, 8, 16)
POOL_GROUP = 256
POOL_HALO = 16

N_EXPERTS = 16
N_EXPERT_GROUPS = 4
EXPERTS_PER_GROUP = 4
EXPERT_FF = 256
GROUP_FF = EXPERTS_PER_GROUP * EXPERT_FF

LANES = 128
NEG_BIG = -1e30

TB_MIX = 256
TB_MOE = 1024
MOD_COLS = 1536
VMEM_LIMIT = 56 * 1024 * 1024


def _sigmoid(x):
    return 1.0 / (1.0 + jnp.exp(-x))


def _rms_mod(x, g, sc, sh):
    ms = jnp.mean(x * x, axis=-1, keepdims=True)
    return (x * lax.rsqrt(ms + EPS)) * (g * (1.0 + sc)) + sh


def _mod_kernel(c_ref, w_ref, b_ref, o_ref):
    c = c_ref[...]
    cond = c * _sigmoid(c)
    o_ref[0] = jnp.dot(cond, w_ref[0], precision=lax.Precision.HIGHEST,
                       preferred_element_type=F32) + b_ref[0]


def _mod_call(c8, w_ada, b_ada3):
    n_col = (N_MOD * D_MODEL) // MOD_COLS
    return pl.pallas_call(
        _mod_kernel,
        grid=(DEPTH, n_col),
        in_specs=[
            pl.BlockSpec((8, D_MODEL), lambda l, j: (0, 0)),
            pl.BlockSpec((1, D_MODEL, MOD_COLS), lambda l, j: (l, 0, j)),
            pl.BlockSpec((1, 1, MOD_COLS), lambda l, j: (l, 0, j)),
        ],
        out_specs=pl.BlockSpec((1, 8, MOD_COLS), lambda l, j: (l, 0, j)),
        out_shape=jax.ShapeDtypeStruct((DEPTH, 8, N_MOD * D_MODEL), F32),
        compiler_params=pltpu.CompilerParams(
            dimension_semantics=("arbitrary", "arbitrary"), vmem_limit_bytes=VMEM_LIMIT),
        name="adaln_mod",
    )(c8, w_ada, b_ada3)


def _route(u2, wrt_ref, br_ref):
    logits = lax.dot_general(wrt_ref[...], u2, (((1,), (1,)), ((), ())),
                             precision=lax.Precision.HIGHEST,
                             preferred_element_type=F32)
    scores = _sigmoid(logits)
    biased = scores + br_ref[...]
    rows_b = [biased[e:e + 1] for e in range(N_EXPERTS)]
    rows_s = [scores[e:e + 1] for e in range(N_EXPERTS)]

    gscore = []
    for g in range(N_EXPERT_GROUPS):
        b0, b1, b2, b3 = rows_b[4 * g:4 * g + 4]
        hi1, lo1 = jnp.maximum(b0, b1), jnp.minimum(b0, b1)
        hi2, lo2 = jnp.maximum(b2, b3), jnp.minimum(b2, b3)
        top1 = jnp.maximum(hi1, hi2)
        second = jnp.maximum(jnp.minimum(hi1, hi2), jnp.where(hi1 >= hi2, lo1, lo2))
        gscore.append(top1 + second)

    best = gscore[0]
    bgrp = jnp.zeros_like(best, dtype=jnp.int32)
    for g in range(1, N_EXPERT_GROUPS):
        better = gscore[g] > best
        bgrp = jnp.where(better, g, bgrp)
        best = jnp.where(better, gscore[g], best)

    vb, vs = [], []
    for j in range(EXPERTS_PER_GROUP):
        b = rows_b[j]
        s = rows_s[j]
        for g in range(1, N_EXPERT_GROUPS):
            sel = bgrp == g
            b = jnp.where(sel, rows_b[4 * g + j], b)
            s = jnp.where(sel, rows_s[4 * g + j], s)
        vb.append(b)
        vs.append(s)

    m1 = vb[0]
    i1 = jnp.zeros_like(bgrp)
    for j in range(1, EXPERTS_PER_GROUP):
        gt = vb[j] > m1
        i1 = jnp.where(gt, j, i1)
        m1 = jnp.where(gt, vb[j], m1)
    cands = [jnp.where(i1 == j, -jnp.inf, vb[j]) for j in range(EXPERTS_PER_GROUP)]
    m2 = cands[0]
    i2 = jnp.zeros_like(i1)
    for j in range(1, EXPERTS_PER_GROUP):
        gt = cands[j] > m2
        i2 = jnp.where(gt, j, i2)
        m2 = jnp.where(gt, cands[j], m2)

    w1 = vs[0]
    w2 = vs[0]
    for j in range(1, EXPERTS_PER_GROUP):
        w1 = jnp.where(i1 == j, vs[j], w1)
        w2 = jnp.where(i2 == j, vs[j], w2)
    tot = w1 + w2
    w1n = w1 / tot
    w2n = w2 / tot
    gate_in_grp = [jnp.where(i1 == j, w1n, 0.0) + jnp.where(i2 == j, w2n, 0.0)
                   for j in range(EXPERTS_PER_GROUP)]

    out = []
    for g in range(N_EXPERT_GROUPS):
        sel = bgrp == g
        out.append(jnp.concatenate([jnp.where(sel, gj, 0.0) for gj in gate_in_grp], axis=0))
    return out


def _ffn_pre(h1, mod, nrm_ref, wrt_ref, br_ref, u2_ref, gates_ref, tb):
    sh2, sc2 = mod[3:4], mod[4:5]
    u2 = _rms_mod(h1, nrm_ref[1:2], sc2, sh2)
    u2_ref[0] = u2.astype(BF16)
    gate_groups = _route(u2, wrt_ref, br_ref)
    for g in range(N_EXPERT_GROUPS):
        padded = jnp.concatenate(
            [gate_groups[g], jnp.zeros((LANES - EXPERTS_PER_GROUP, tb), F32)], axis=0)
        gates_ref[0, :, g * LANES:(g + 1) * LANES] = padded.T


def _even_kernel(h_ref, mod_ref, nrm_ref, win_ref, wout_ref, bias_ref, sink_ref, convw_ref,
                 wrt_ref, br_ref, hout_ref, u2_ref, gates_ref, kvprev_ref, cprev_ref):
    tb = TB_MIX
    t = pl.program_id(1)

    @pl.when(t == 0)
    def _():
        kvprev_ref[...] = jnp.zeros_like(kvprev_ref)
        cprev_ref[...] = jnp.zeros_like(cprev_ref)

    x = h_ref[0]
    mod = mod_ref[0]
    sh1, sc1, g1 = mod[0:1], mod[1:2], mod[2:3]
    u = _rms_mod(x, nrm_ref[0:1], sc1, sh1)
    proj = jnp.dot(u.astype(BF16), win_ref[...], preferred_element_type=F32)

    q = proj[:, 0:ATTN_WIDTH].astype(BF16)
    kf = proj[:, 512:640]
    vf = proj[:, 640:768]
    bgate = proj[:, 768:1280]
    cgate = proj[:, 1280:1792]
    xv = proj[:, 1792:2304]

    kv_prev = kvprev_ref[...]
    kext = jnp.concatenate([kv_prev[:, 0:KV_WIDTH], kf], axis=0)
    vext = jnp.concatenate([kv_prev[:, KV_WIDTH:], vf], axis=0)
    kvprev_ref[:, 0:KV_WIDTH] = kf[tb - BLOCK:tb]
    kvprev_ref[:, KV_WIDTH:] = vf[tb - BLOCK:tb]

    lane = lax.broadcasted_iota(jnp.int32, kext.shape, 1)
    lo = lane < HEAD_DIM
    krol = pltpu.roll(kext, HEAD_DIM, axis=1)
    vrol = pltpu.roll(vext, HEAD_DIM, axis=1)
    zero = jnp.zeros_like(kext)
    one_at_64 = jnp.where(lane == HEAD_DIM, 1.0, 0.0)
    one_at_0 = jnp.where(lane == 0, 1.0, 0.0)
    k_ops = [(jnp.where(lo, kext, zero).astype(BF16), jnp.where(lo, zero, krol).astype(BF16)),
             (jnp.where(lo, krol, zero).astype(BF16), jnp.where(lo, zero, kext).astype(BF16))]
    v_ops = [(jnp.where(lo, vext, one_at_64).astype(BF16), jnp.where(lo, one_at_0, vrol).astype(BF16)),
             (jnp.where(lo, vrol, one_at_64).astype(BF16), jnp.where(lo, one_at_0, vext).astype(BF16))]

    col = lax.broadcasted_iota(jnp.int32, (BLOCK, 4 * BLOCK), 1)
    prev_cols = (col % (2 * BLOCK)) < BLOCK
    lane_o = lax.broadcasted_iota(jnp.int32, (BLOCK, LANES), 1)
    lo_o = lane_o < HEAD_DIM

    attn_rows = []
    for bi in range(tb // BLOCK):
        r0 = bi * BLOCK
        pair_out = []
        for pr in range(N_HEADS // 2):
            kvh = pr // 2
            ka, kb = k_ops[kvh]
            va, vb = v_ops[kvh]
            rhs = jnp.concatenate([ka[r0:r0 + 2 * BLOCK], kb[r0:r0 + 2 * BLOCK]], axis=0)
            qp = q[r0:r0 + BLOCK, pr * LANES:(pr + 1) * LANES]
            s = lax.dot_general(qp, rhs, (((1,), (1,)), ((), ())),
                                preferred_element_type=F32)
            bias = bias_ref[pr]
            if bi == 0:
                bias = jnp.where(prev_cols & (t == 0), NEG_BIG, bias)
            s = s + bias
            outs = []
            for hh in range(2):
                sh = s[:, hh * 2 * BLOCK:(hh + 1) * 2 * BLOCK]
                sink = sink_ref[2 * pr + hh]
                m = jnp.maximum(jnp.max(sh, axis=-1, keepdims=True), sink)
                p = jnp.exp(sh - m).astype(BF16)
                vop = (va, vb)[hh][r0:r0 + 2 * BLOCK]
                o = jnp.dot(p, vop, preferred_element_type=F32)
                den_col = HEAD_DIM if hh == 0 else 0
                den = o[:, den_col:den_col + 1] + jnp.exp(sink - m)
                outs.append(o / den)
            pair_out.append(jnp.where(lo_o, outs[0], outs[1]))
        attn_rows.append(jnp.concatenate(pair_out, axis=1))
    attn = jnp.concatenate(attn_rows, axis=0)

    uc = cgate * xv
    cprev = cprev_ref[...]
    row = lax.broadcasted_iota(jnp.int32, uc.shape, 0)
    r1 = jnp.where(row == 0, cprev[7:8], pltpu.roll(uc, 1, axis=0))
    r2 = jnp.where(row == 0, cprev[6:7], jnp.where(row == 1, cprev[7:8], pltpu.roll(uc, 2, axis=0)))
    cprev_ref[...] = uc[tb - 8:tb]
    cw = convw_ref[...]
    conv = bgate * (cw[0:1] * r2 + cw[1:2] * r1 + cw[2:3] * uc)

    mix = (jnp.dot(attn.astype(BF16), wout_ref[0:ATTN_WIDTH, :], preferred_element_type=F32)
           + jnp.dot(conv.astype(BF16), wout_ref[ATTN_WIDTH:, :], preferred_element_type=F32))
    h1 = x + g1 * mix
    hout_ref[0] = h1
    _ffn_pre(h1, mod, nrm_ref, wrt_ref, br_ref, u2_ref, gates_ref, tb)


def _mixer_out(b, s):
    shapes = (jax.ShapeDtypeStruct((b, s, D_MODEL), F32),
              jax.ShapeDtypeStruct((b, s, D_MODEL), BF16),
              jax.ShapeDtypeStruct((b, s, N_EXPERT_GROUPS * LANES), F32))
    specs = (pl.BlockSpec((1, TB_MIX, D_MODEL), lambda bb, t: (bb, t, 0)),
             pl.BlockSpec((1, TB_MIX, D_MODEL), lambda bb, t: (bb, t, 0)),
             pl.BlockSpec((1, TB_MIX, N_EXPERT_GROUPS * LANES), lambda bb, t: (bb, t, 0)))
    return shapes, specs


def _full(shape):
    nd = len(shape)
    return pl.BlockSpec(shape, lambda bb, t: (0,) * nd)


def _even_call(h, mod8, nrm8, win, wout, bias_tab, sinks, convw8, wrt, br):
    b, s, _ = h.shape
    out_shape, out_specs = _mixer_out(b, s)
    return pl.pallas_call(
        _even_kernel,
        grid=(b, s // TB_MIX),
        in_specs=[
            pl.BlockSpec((1, TB_MIX, D_MODEL), lambda bb, t: (bb, t, 0)),
            pl.BlockSpec((1, 8, D_MODEL), lambda bb, t: (bb, 0, 0)),
            _full((8, D_MODEL)),
            _full((D_MODEL, IN_WIDTH)),
            _full((D_MODEL, D_MODEL)),
            _full((N_HEADS // 2, BLOCK, 4 * BLOCK)),
            pl.BlockSpec(memory_space=pltpu.SMEM),
            _full((8, CONV_WIDTH)),
            _full((N_EXPERTS, D_MODEL)),
            _full((N_EXPERTS, 1)),
        ],
        out_specs=out_specs,
        out_shape=out_shape,
        scratch_shapes=[pltpu.VMEM((BLOCK, 2 * KV_WIDTH), F32),
                        pltpu.VMEM((8, CONV_WIDTH), F32)],
        compiler_params=pltpu.CompilerParams(
            dimension_semantics=("arbitrary", "arbitrary"), vmem_limit_bytes=VMEM_LIMIT),
        name="even_mixer",
    )(h, mod8, nrm8, win, wout, bias_tab, sinks, convw8, wrt, br)


def _odd_kernel(h_ref, mod_ref, nrm_ref, wpool_ref, pscale_ref, wrt_ref, br_ref,
                hout_ref, u2_ref, gates_ref, uprev_ref):
    tb = TB_MIX
    t = pl.program_id(1)

    @pl.when(t == 0)
    def _():
        uprev_ref[...] = jnp.zeros_like(uprev_ref)

    x = h_ref[0]
    mod = mod_ref[0]
    sh1, sc1, g1 = mod[0:1], mod[1:2], mod[2:3]
    u = _rms_mod(x, nrm_ref[0:1], sc1, sh1)
    ext = jnp.concatenate([uprev_ref[...], u], axis=0)
    uprev_ref[...] = u[tb - POOL_HALO:tb]

    pos = (t * tb + 1 + lax.broadcasted_iota(jnp.int32, (tb, 1), 0)).astype(F32)
    mixed = []
    for gi, w in enumerate(POOL_SIZES):
        sl = slice(gi * POOL_GROUP, (gi + 1) * POOL_GROUP)
        acc = ext[:, sl]
        shift = 1
        while shift < w:
            acc = acc + pltpu.roll(acc, shift, axis=0)
            shift *= 2
        mean = acc[POOL_HALO:] / jnp.minimum(pos, float(w))
        pooled = mean - u[:, sl]
        mixed.append(jnp.dot(pooled.astype(BF16), wpool_ref[gi], preferred_element_type=F32))
    mix = jnp.concatenate(mixed, axis=1) * pscale_ref[0:1]
    h1 = x + g1 * mix
    hout_ref[0] = h1
    _ffn_pre(h1, mod, nrm_ref, wrt_ref, br_ref, u2_ref, gates_ref, tb)


def _odd_call(h, mod8, nrm8, wpool, pscale8, wrt, br):
    b, s, _ = h.shape
    out_shape, out_specs = _mixer_out(b, s)
    return pl.pallas_call(
        _odd_kernel,
        grid=(b, s // TB_MIX),
        in_specs=[
            pl.BlockSpec((1, TB_MIX, D_MODEL), lambda bb, t: (bb, t, 0)),
            pl.BlockSpec((1, 8, D_MODEL), lambda bb, t: (bb, 0, 0)),
            _full((8, D_MODEL)),
            _full((len(POOL_SIZES), POOL_GROUP, POOL_GROUP)),
            _full((8, D_MODEL)),
            _full((N_EXPERTS, D_MODEL)),
            _full((N_EXPERTS, 1)),
        ],
        out_specs=out_specs,
        out_shape=out_shape,
        scratch_shapes=[pltpu.VMEM((POOL_HALO, D_MODEL), F32)],
        compiler_params=pltpu.CompilerParams(
            dimension_semantics=("arbitrary", "arbitrary"), vmem_limit_bytes=VMEM_LIMIT),
        name="odd_mixer",
    )(h, mod8, nrm8, wpool, pscale8, wrt, br)


def _moe_kernel(u2_ref, gates_ref, h1_ref, g2_ref, nf_ref, wg_ref, wu_ref, wd_ref, o_ref, *,
                final):
    g = pl.program_id(2)
    xb = u2_ref[0]
    gates = gates_ref[0]
    parts = []
    for e in range(EXPERTS_PER_GROUP):
        a = jnp.dot(xb, wg_ref[e], preferred_element_type=F32)
        bu = jnp.dot(xb, wu_ref[e], preferred_element_type=F32)
        hid = (a * _sigmoid(a)) * bu
        parts.append((hid * gates[:, e:e + 1]).astype(BF16))
    hid_all = jnp.concatenate(parts, axis=1)
    y = jnp.dot(hid_all, wd_ref[0], preferred_element_type=F32)

    @pl.when(g == 0)
    def _():
        o_ref[0] = y

    @pl.when(g > 0)
    def _():
        o_ref[0] += y

    @pl.when(g == N_EXPERT_GROUPS - 1)
    def _():
        h2 = h1_ref[0] + g2_ref[0, 5:6] * o_ref[0]
        if final:
            ms = jnp.mean(h2 * h2, axis=-1, keepdims=True)
            h2 = (h2 * lax.rsqrt(ms + EPS)) * nf_ref[0:1]
        o_ref[0] = h2


def _moe_call(u2, gates, h1, mod8, nf8, wg, wu, wd, final):
    b, s, _ = h1.shape
    tok = lambda bb, t, g: (bb, t, 0)
    return pl.pallas_call(
        functools.partial(_moe_kernel, final=final),
        grid=(b, s // TB_MOE, N_EXPERT_GROUPS),
        in_specs=[
            pl.BlockSpec((1, TB_MOE, D_MODEL), tok),
            pl.BlockSpec((1, TB_MOE, LANES), lambda bb, t, g: (bb, t, g)),
            pl.BlockSpec((1, TB_MOE, D_MODEL), tok),
            pl.BlockSpec((1, 8, D_MODEL), lambda bb, t, g: (bb, 0, 0)),
            pl.BlockSpec((8, D_MODEL), lambda bb, t, g: (0, 0)),
            pl.BlockSpec((EXPERTS_PER_GROUP, D_MODEL, EXPERT_FF), lambda bb, t, g: (g, 0, 0)),
            pl.BlockSpec((EXPERTS_PER_GROUP, D_MODEL, EXPERT_FF), lambda bb, t, g: (g, 0, 0)),
            pl.BlockSpec((1, GROUP_FF, D_MODEL), lambda bb, t, g: (g, 0, 0)),
        ],
        out_specs=pl.BlockSpec((1, TB_MOE, D_MODEL), tok),
        out_shape=jax.ShapeDtypeStruct((b, s, D_MODEL), F32),
        compiler_params=pltpu.CompilerParams(
            dimension_semantics=("arbitrary", "arbitrary", "arbitrary"),
            vmem_limit_bytes=VMEM_LIMIT),
        name="moe_experts",
    )(u2, gates, h1, mod8, nf8, wg, wu, wd)


def _pad_rows(a, rows=8):
    return jnp.pad(a, ((0, rows - a.shape[0]), (0, 0)))


def _attn_bias_table():
    slopes = 2.0 ** (-8.0 * (jnp.arange(N_HEADS, dtype=F32) + 1.0) / N_HEADS)
    dist = (jnp.arange(BLOCK)[:, None] + BLOCK) - jnp.arange(2 * BLOCK)[None, :]
    ok = (dist >= 0) & (dist < WINDOW)
    per_head = jnp.where(ok[None], -slopes[:, None, None] * dist.astype(F32)[None], NEG_BIG)
    return per_head.reshape(N_HEADS // 2, 2, BLOCK, 2 * BLOCK).transpose(0, 2, 1, 3).reshape(
        N_HEADS // 2, BLOCK, 4 * BLOCK)


def kernel(x, c, w_ada, b_ada, norm_mix, norm_ffn, w_in, w_out, sinks, conv_w, w_pool,
           pool_scale, w_router, b_router, w_gate, w_up, w_down, norm_final):
    b = x.shape[0]
    mod_all = _mod_call(_pad_rows(c), w_ada, b_ada.reshape(DEPTH, 1, N_MOD * D_MODEL))
    bias_tab = _attn_bias_table()
    wrt = w_router.T
    br = b_router.reshape(N_EXPERTS, 1)
    nf8 = _pad_rows(norm_final[None, :])
    qscale = jnp.concatenate([jnp.full((ATTN_WIDTH,), HEAD_DIM ** -0.5, F32),
                              jnp.ones((IN_WIDTH - ATTN_WIDTH,), F32)])

    h = x
    for layer in range(DEPTH):
        j = layer // 2
        mod8 = jnp.pad(mod_all[layer, :b].reshape(b, N_MOD, D_MODEL), ((0, 0), (0, 2), (0, 0)))
        nrm8 = _pad_rows(jnp.stack([norm_mix[layer], norm_ffn[layer]]))
        if layer % 2 == 0:
            win = (w_in[j] * qscale).astype(BF16)
            h1, u2, gates = _even_call(h, mod8, nrm8, win, w_out[j].astype(BF16), bias_tab,
                                       sinks[j], _pad_rows(conv_w[j]), wrt, br)
        else:
            h1, u2, gates = _odd_call(h, mod8, nrm8, w_pool[j].astype(BF16),
                                      _pad_rows(pool_scale[j][None, :]), wrt, br)
        wg = w_gate[layer].astype(BF16)
        wu = w_up[layer].astype(BF16)
        wd = w_down[layer].astype(BF16).reshape(N_EXPERT_GROUPS, GROUP_FF, D_MODEL)
        h = _moe_call(u2, gates, h1, mod8, nf8, wg, wu, wd, final=(layer == DEPTH - 1))
    return h
```

```python
import functools

import jax
import jax.numpy as jnp
from jax import lax
from jax.experimental import pallas as pl
from jax.experimental.pallas import tpu as pltpu

F32 = jnp.float32
BF16 = jnp.bfloat16
I32 = jnp.int32

D_MODEL = 1024
DEPTH = 4
EPS = 1e-6
N_MOD = 6

ATTN_WIDTH = 512
HEAD_DIM = 64
N_HEADS = 8
KV_WIDTH = 128
WINDOW = 128
BLOCK = 128
CONV_WIDTH = 512
IN_WIDTH = 2304

POOL_SIZES = (2, 4, 8, 16)
POOL_GROUP = 256
POOL_HALO = 16

N_EXPERTS = 16
N_GROUPS = 4
EXPERTS_PER_GROUP = 4
EXPERT_FF = 256
GROUP_FF = EXPERTS_PER_GROUP * EXPERT_FF

LANES = 128
NEG_BIG = -1e30

TB = 256
ALIGN = 16
N_CHUNK = 5
STG_P = TB + N_GROUPS * ALIGN
STG_C = 384
TBM = 512
ROW_W = D_MODEL + LANES
META_W = 16
LP_LANE = 8
MOD_COLS = 1536
VMEM_LIMIT = 56 * 1024 * 1024

assert ALIGN << (N_CHUNK - 1) == TB
assert STG_C >= STG_P and STG_C % LANES == 0


def _sizes(n_tok):
    n_tiles = n_tok // TB
    cap = -(-(n_tok + n_tiles * (ALIGN - 1) + TBM) // TBM) * TBM
    n_work = (n_tok + n_tiles * N_GROUPS * (ALIGN - 1)) // TBM + N_GROUPS
    return n_tiles, cap, n_work


def _sigmoid(x):
    return 1.0 / (1.0 + jnp.exp(-x))


def _rms_mod(x, g, sc, sh):
    ms = jnp.mean(x * x, axis=-1, keepdims=True)
    return (x * lax.rsqrt(ms + EPS)) * (g * (1.0 + sc)) + sh


def _mod_kernel(c_ref, w_ref, b_ref, o_ref):
    c = c_ref[...]
    cond = c * _sigmoid(c)
    o_ref[0] = jnp.dot(cond, w_ref[0], precision=lax.Precision.HIGHEST,
                       preferred_element_type=F32) + b_ref[0]


def _mod_call(c8, w_ada, b_ada3):
    n_col = (N_MOD * D_MODEL) // MOD_COLS
    return pl.pallas_call(
        _mod_kernel,
        grid=(DEPTH, n_col),
        in_specs=[
            pl.BlockSpec((8, D_MODEL), lambda l, j: (0, 0)),
            pl.BlockSpec((1, D_MODEL, MOD_COLS), lambda l, j: (l, 0, j)),
            pl.BlockSpec((1, 1, MOD_COLS), lambda l, j: (l, 0, j)),
        ],
        out_specs=pl.BlockSpec((1, 8, MOD_COLS), lambda l, j: (l, 0, j)),
        out_shape=jax.ShapeDtypeStruct((DEPTH, 8, N_MOD * D_MODEL), F32),
        compiler_params=pltpu.CompilerParams(
            dimension_semantics=("arbitrary", "arbitrary"), vmem_limit_bytes=VMEM_LIMIT),
        name="adaln_mod",
    )(c8, w_ada, b_ada3)


def _chunk_copies(hbm_ref, stage_ref, npad, off, dst, sem, to_hbm):
    out = []
    for k in range(N_CHUNK):
        size = ALIGN << k
        shift = 4 + k
        bit = (npad >> shift) & 1
        start = (npad >> (shift + 1)) << (shift + 1)
        s_view = stage_ref.at[pl.ds(pl.multiple_of(off + start, ALIGN), size), :]
        h_view = hbm_ref.at[pl.ds(pl.multiple_of(dst + start, ALIGN), size), :]
        cp = (pltpu.make_async_copy(s_view, h_view, sem) if to_hbm
              else pltpu.make_async_copy(h_view, s_view, sem))
        out.append((bit == 1, cp))
    return out


def _tile_copies(hbm_ref, stage_ref, sem, npad, off, dst, to_hbm):
    out = []
    for g in range(N_GROUPS):
        out += _chunk_copies(hbm_ref, stage_ref, npad[g], off[g], dst[g], sem, to_hbm)
    return out


def _head(i, n_tiles, pm_ref, hprev_ref, modprev_ref, tokmeta_ref, ys_hbm, ystage_ref, gsem):
    slot = i % 2

    def copies(tile, slot_):
        base = tile * META_W
        npad = [pm_ref[base + g] for g in range(N_GROUPS)]
        off = [pm_ref[base + N_GROUPS + g] for g in range(N_GROUPS)]
        dst = [pm_ref[base + 2 * N_GROUPS + g] for g in range(N_GROUPS)]
        return _tile_copies(ys_hbm, ystage_ref.at[slot_], gsem.at[slot_], npad, off, dst, False)

    @pl.when(i == 0)
    def _():
        ystage_ref[...] = jnp.zeros_like(ystage_ref)
        for cond, cp in copies(0, 0):
            pl.when(cond)(cp.start)

    @pl.when(i + 1 < n_tiles)
    def _():
        for cond, cp in copies(i + 1, 1 - slot):
            pl.when(cond)(cp.start)

    for cond, cp in copies(i, slot):
        pl.when(cond)(cp.wait)

    lp_col = tokmeta_ref[:, LP_LANE:LP_LANE + 1].astype(I32)
    c_iota = lax.broadcasted_iota(I32, (TB, STG_C), 1)
    unsort = jnp.where(c_iota == lp_col, 1.0, 0.0).astype(BF16)
    y = jnp.dot(unsort, ystage_ref[slot], preferred_element_type=F32)
    return hprev_ref[...] + modprev_ref[0, 5:6] * y


def _route(u2, wrt_ref, br_ref):
    logits = lax.dot_general(wrt_ref[...], u2, (((1,), (1,)), ((), ())),
                             precision=lax.Precision.HIGHEST,
                             preferred_element_type=F32)
    scores = _sigmoid(logits)
    biased = scores + br_ref[...]
    rows_b = [biased[e:e + 1] for e in range(N_EXPERTS)]
    rows_s = [scores[e:e + 1] for e in range(N_EXPERTS)]

    gscore = []
    for g in range(N_GROUPS):
        b0, b1, b2, b3 = rows_b[4 * g:4 * g + 4]
        hi1, lo1 = jnp.maximum(b0, b1), jnp.minimum(b0, b1)
        hi2, lo2 = jnp.maximum(b2, b3), jnp.minimum(b2, b3)
        top1 = jnp.maximum(hi1, hi2)
        second = jnp.maximum(jnp.minimum(hi1, hi2), jnp.where(hi1 >= hi2, lo1, lo2))
        gscore.append(top1 + second)

    best = gscore[0]
    bgrp = jnp.zeros_like(best, dtype=I32)
    for g in range(1, N_GROUPS):
        better = gscore[g] > best
        bgrp = jnp.where(better, g, bgrp)
        best = jnp.where(better, gscore[g], best)

    vb, vs = [], []
    for j in range(EXPERTS_PER_GROUP):
        b = rows_b[j]
        s = rows_s[j]
        for g in range(1, N_GROUPS):
            sel = bgrp == g
            b = jnp.where(sel, rows_b[4 * g + j], b)
            s = jnp.where(sel, rows_s[4 * g + j], s)
        vb.append(b)
        vs.append(s)

    m1 = vb[0]
    i1 = jnp.zeros_like(bgrp)
    for j in range(1, EXPERTS_PER_GROUP):
        gt = vb[j] > m1
        i1 = jnp.where(gt, j, i1)
        m1 = jnp.where(gt, vb[j], m1)
    cands = [jnp.where(i1 == j, -jnp.inf, vb[j]) for j in range(EXPERTS_PER_GROUP)]
    m2 = cands[0]
    i2 = jnp.zeros_like(i1)
    for j in range(1, EXPERTS_PER_GROUP):
        gt = cands[j] > m2
        i2 = jnp.where(gt, j, i2)
        m2 = jnp.where(gt, cands[j], m2)

    w1 = vs[0]
    w2 = vs[0]
    for j in range(1, EXPERTS_PER_GROUP):
        w1 = jnp.where(i1 == j, vs[j], w1)
        w2 = jnp.where(i2 == j, vs[j], w2)
    tot = w1 + w2
    w1n = w1 / tot
    w2n = w2 / tot
    gates = [jnp.where(i1 == j, w1n, 0.0) + jnp.where(i2 == j, w2n, 0.0)
             for j in range(EXPERTS_PER_GROUP)]
    return bgrp, gates


def _tail(i, n_tiles, cap, h1, mod, nrm_ref, wrt_ref, br_ref, tri_ref,
          tokmeta_ref, tmeta_ref, xs_hbm, stage_ref, zeros_ref, run_ref, prev_ref, ssem, zsem):
    sh2, sc2 = mod[3:4], mod[4:5]
    u2 = _rms_mod(h1, nrm_ref[1:2], sc2, sh2)
    bgrp, gates = _route(u2, wrt_ref, br_ref)

    sel = [bgrp == g for g in range(N_GROUPS)]
    onehot = jnp.concatenate([jnp.where(s, 1.0, 0.0) for s in sel]
                             + [jnp.zeros((8 - N_GROUPS, TB), F32)], axis=0).astype(BF16)
    rank_incl = jnp.dot(onehot, tri_ref[...], preferred_element_type=F32)
    cnt = lax.dot_general(jnp.ones((8, TB), BF16), onehot, (((1,), (1,)), ((), ())),
                          preferred_element_type=F32)
    npad_row = jnp.floor((cnt[0:1, :] + (ALIGN - 1.0)) * (1.0 / ALIGN)) * ALIGN
    lp = jnp.zeros((1, TB), F32)
    acc = jnp.zeros((1, 1), F32)
    for g in range(N_GROUPS):
        lp = jnp.where(sel[g], acc + rank_incl[g:g + 1] - 1.0, lp)
        acc = acc + npad_row[:, g:g + 1]

    hi = [x.astype(BF16).astype(F32) for x in gates]
    lo = [(x - h).astype(BF16).astype(F32) for x, h in zip(gates, hi)]
    meta_src = jnp.concatenate(hi + lo + [lp, jnp.zeros((LANES - 9, TB), F32)], axis=0)
    meta_t = meta_src.T
    tokmeta_ref[...] = meta_t

    rowdata = jnp.concatenate([u2.astype(BF16), meta_t.astype(BF16)], axis=1)
    r_iota = lax.broadcasted_iota(I32, (STG_P, TB), 0)
    sort = jnp.where(r_iota == lp.astype(I32), 1.0, 0.0).astype(BF16)
    slot = i % 2
    stage_ref[slot] = jnp.dot(sort, rowdata, preferred_element_type=F32).astype(BF16)

    n_s = [cnt[0, g].astype(I32) for g in range(N_GROUPS)]
    npad_s = [((n + (ALIGN - 1)) >> 4) << 4 for n in n_s]
    off_s = [jnp.int32(0)]
    for g in range(1, N_GROUPS):
        off_s.append(off_s[-1] + npad_s[g - 1])

    @pl.when(i == 0)
    def _():
        for g in range(N_GROUPS):
            run_ref[g] = jnp.int32(0)

    dst_s = [g * cap + run_ref[g] for g in range(N_GROUPS)]

    def copies(slot_, npad, off, dst):
        return _tile_copies(xs_hbm, stage_ref.at[slot_], ssem.at[slot_], npad, off, dst, True)

    cur = copies(slot, npad_s, off_s, dst_s)
    for cond, cp in cur:
        pl.when(cond)(cp.start)

    @pl.when(i > 0)
    def _():
        pn = [prev_ref[g] for g in range(N_GROUPS)]
        po = [prev_ref[N_GROUPS + g] for g in range(N_GROUPS)]
        pd = [prev_ref[2 * N_GROUPS + g] for g in range(N_GROUPS)]
        for cond, cp in copies(1 - slot, pn, po, pd):
            pl.when(cond)(cp.wait)

    base = i * META_W
    for g in range(N_GROUPS):
        prev_ref[g] = npad_s[g]
        prev_ref[N_GROUPS + g] = off_s[g]
        prev_ref[2 * N_GROUPS + g] = dst_s[g]
        tmeta_ref[base + g] = npad_s[g]
        tmeta_ref[base + N_GROUPS + g] = off_s[g]
        tmeta_ref[base + 2 * N_GROUPS + g] = dst_s[g]
        tmeta_ref[base + 3 * N_GROUPS + g] = jnp.int32(0)
        run_ref[g] = run_ref[g] + npad_s[g]

    @pl.when(i == n_tiles - 1)
    def _():
        for cond, cp in cur:
            pl.when(cond)(cp.wait)
        zeros_ref[...] = jnp.zeros_like(zeros_ref)
        tails = []
        for g in range(N_GROUPS):
            total = run_ref[g]
            tmeta_ref[n_tiles * META_W + g] = total
            start = pl.multiple_of(g * cap + total, ALIGN)
            tails.append(pltpu.make_async_copy(zeros_ref, xs_hbm.at[pl.ds(start, TBM), :], zsem))
        for g in range(N_GROUPS, META_W):
            tmeta_ref[n_tiles * META_W + g] = jnp.int32(0)
        for cp in tails:
            cp.start()
        for cp in tails:
            cp.wait()


def _attn_conv_mix(x, t, mod, nrm_ref, win_ref, wout_ref, bias_ref, sink_ref, convw_ref,
                   kvprev_ref, cprev_ref):
    tb = TB

    @pl.when(t == 0)
    def _():
        kvprev_ref[...] = jnp.zeros_like(kvprev_ref)
        cprev_ref[...] = jnp.zeros_like(cprev_ref)

    sh1, sc1, g1 = mod[0:1], mod[1:2], mod[2:3]
    u = _rms_mod(x, nrm_ref[0:1], sc1, sh1)
    proj = jnp.dot(u.astype(BF16), win_ref[...], preferred_element_type=F32)

    q = proj[:, 0:ATTN_WIDTH].astype(BF16)
    kf = proj[:, 512:640]
    vf = proj[:, 640:768]
    bgate = proj[:, 768:1280]
    cgate = proj[:, 1280:1792]
    xv = proj[:, 1792:2304]

    kv_prev = kvprev_ref[...]
    kext = jnp.concatenate([kv_prev[:, 0:KV_WIDTH], kf], axis=0)
    vext = jnp.concatenate([kv_prev[:, KV_WIDTH:], vf], axis=0)
    kvprev_ref[:, 0:KV_WIDTH] = kf[tb - BLOCK:tb]
    kvprev_ref[:, KV_WIDTH:] = vf[tb - BLOCK:tb]

    lane = lax.broadcasted_iota(I32, kext.shape, 1)
    lo = lane < HEAD_DIM
    krol = pltpu.roll(kext, HEAD_DIM, axis=1)
    vrol = pltpu.roll(vext, HEAD_DIM, axis=1)
    zero = jnp.zeros_like(kext)
    one_at_64 = jnp.where(lane == HEAD_DIM, 1.0, 0.0)
    one_at_0 = jnp.where(lane == 0, 1.0, 0.0)
    k_ops = [(jnp.where(lo, kext, zero).astype(BF16), jnp.where(lo, zero, krol).astype(BF16)),
             (jnp.where(lo, krol, zero).astype(BF16), jnp.where(lo, zero, kext).astype(BF16))]
    v_ops = [(jnp.where(lo, vext, one_at_64).astype(BF16), jnp.where(lo, one_at_0, vrol).astype(BF16)),
             (jnp.where(lo, vrol, one_at_64).astype(BF16), jnp.where(lo, one_at_0, vext).astype(BF16))]

    col = lax.broadcasted_iota(I32, (BLOCK, 4 * BLOCK), 1)
    prev_cols = (col % (2 * BLOCK)) < BLOCK
    lane_o = lax.broadcasted_iota(I32, (BLOCK, LANES), 1)
    lo_o = lane_o < HEAD_DIM

    attn_rows = []
    for bi in range(tb // BLOCK):
        r0 = bi * BLOCK
        pair_out = []
        for pr in range(N_HEADS // 2):
            kvh = pr // 2
            ka, kb = k_ops[kvh]
            va, vb = v_ops[kvh]
            rhs = jnp.concatenate([ka[r0:r0 + 2 * BLOCK], kb[r0:r0 + 2 * BLOCK]], axis=0)
            qp = q[r0:r0 + BLOCK, pr * LANES:(pr + 1) * LANES]
            s = lax.dot_general(qp, rhs, (((1,), (1,)), ((), ())),
                                preferred_element_type=F32)
            bias = bias_ref[pr]
            if bi == 0:
                bias = jnp.where(prev_cols & (t == 0), NEG_BIG, bias)
            s = s + bias
            outs = []
            for hh in range(2):
                sh = s[:, hh * 2 * BLOCK:(hh + 1) * 2 * BLOCK]
                sink = sink_ref[2 * pr + hh]
                m = jnp.maximum(jnp.max(sh, axis=-1, keepdims=True), sink)
                p = jnp.exp(sh - m).astype(BF16)
                vop = (va, vb)[hh][r0:r0 + 2 * BLOCK]
                o = jnp.dot(p, vop, preferred_element_type=F32)
                den_col = HEAD_DIM if hh == 0 else 0
                den = o[:, den_col:den_col + 1] + jnp.exp(sink - m)
                outs.append(o / den)
            pair_out.append(jnp.where(lo_o, outs[0], outs[1]))
        attn_rows.append(jnp.concatenate(pair_out, axis=1))
    attn = jnp.concatenate(attn_rows, axis=0)

    uc = cgate * xv
    cprev = cprev_ref[...]
    row = lax.broadcasted_iota(I32, uc.shape, 0)
    r1 = jnp.where(row == 0, cprev[7:8], pltpu.roll(uc, 1, axis=0))
    r2 = jnp.where(row == 0, cprev[6:7], jnp.where(row == 1, cprev[7:8], pltpu.roll(uc, 2, axis=0)))
    cprev_ref[...] = uc[tb - 8:tb]
    cw = convw_ref[...]
    conv = bgate * (cw[0:1] * r2 + cw[1:2] * r1 + cw[2:3] * uc)

    mix = (jnp.dot(attn.astype(BF16), wout_ref[0:ATTN_WIDTH, :], preferred_element_type=F32)
           + jnp.dot(conv.astype(BF16), wout_ref[ATTN_WIDTH:, :], preferred_element_type=F32))
    return x + g1 * mix


def _pool_mix(x, t, mod, nrm_ref, wpool_ref, pscale_ref, uprev_ref):
    tb = TB

    @pl.when(t == 0)
    def _():
        uprev_ref[...] = jnp.zeros_like(uprev_ref)

    sh1, sc1, g1 = mod[0:1], mod[1:2], mod[2:3]
    u = _rms_mod(x, nrm_ref[0:1], sc1, sh1)
    ext = jnp.concatenate([uprev_ref[...], u], axis=0)
    uprev_ref[...] = u[tb - POOL_HALO:tb]

    pos = (t * tb + 1 + lax.broadcasted_iota(I32, (tb, 1), 0)).astype(F32)
    mixed = []
    for gi, w in enumerate(POOL_SIZES):
        sl = slice(gi * POOL_GROUP, (gi + 1) * POOL_GROUP)
        acc = ext[:, sl]
        shift = 1
        while shift < w:
            acc = acc + pltpu.roll(acc, shift, axis=0)
            shift *= 2
        mean = acc[POOL_HALO:] / jnp.minimum(pos, float(w))
        pooled = mean - u[:, sl]
        mixed.append(jnp.dot(pooled.astype(BF16), wpool_ref[gi], preferred_element_type=F32))
    mix = jnp.concatenate(mixed, axis=1) * pscale_ref[0:1]
    return x + g1 * mix


N_HEAD_IN = 4
N_EVEN_IN = 6
N_ODD_IN = 2


def _mixer_kernel(*refs, kind, has_head, n_tiles, tiles_per_seq, cap):
    refs = list(refs)
    i = pl.program_id(0)
    t = i % tiles_per_seq
    if has_head:
        pm_ref = refs.pop(0)
        hprev_ref, modprev_ref, tokprev_ref, ys_hbm = refs[:4]
        refs = refs[4:]
    else:
        h_ref = refs.pop(0)
    mod_ref, nrm_ref = refs[:2]
    refs = refs[2:]
    if kind == "even":
        win_ref, wout_ref, bias_ref, sink_ref, convw_ref = refs[:5]
        refs = refs[5:]
    else:
        wpool_ref, pscale_ref = refs[:2]
        refs = refs[2:]
    wrt_ref, br_ref, tri_ref = refs[:3]
    refs = refs[3:]
    hout_ref, tokmeta_ref, tmeta_ref, xs_hbm = refs[:4]
    refs = refs[4:]
    if kind == "even":
        kvprev_ref, cprev_ref = refs[:2]
        refs = refs[2:]
    else:
        uprev_ref = refs.pop(0)
    stage_ref, zeros_ref, run_ref, prev_ref, ssem, zsem = refs[:6]
    refs = refs[6:]

    if has_head:
        ystage_ref, gsem = refs
        x = _head(i, n_tiles, pm_ref, hprev_ref, modprev_ref, tokprev_ref, ys_hbm, ystage_ref, gsem)
    else:
        x = h_ref[...]
    mod = mod_ref[0]
    if kind == "even":
        h1 = _attn_conv_mix(x, t, mod, nrm_ref, win_ref, wout_ref, bias_ref, sink_ref, convw_ref,
                            kvprev_ref, cprev_ref)
    else:
        h1 = _pool_mix(x, t, mod, nrm_ref, wpool_ref, pscale_ref, uprev_ref)
    hout_ref[...] = h1
    _tail(i, n_tiles, cap, h1, mod, nrm_ref, wrt_ref, br_ref, tri_ref, tokmeta_ref, tmeta_ref,
          xs_hbm, stage_ref, zeros_ref, run_ref, prev_ref, ssem, zsem)


def _nat(cols):
    return pl.BlockSpec((TB, cols), lambda i, *_: (i, 0))


def _full(shape):
    nd = len(shape)
    return pl.BlockSpec(shape, lambda i, *_: (0,) * nd)


def _mod_spec(tiles_per_seq):
    return pl.BlockSpec((1, 8, D_MODEL), lambda i, *_: (i // tiles_per_seq, 0, 0))


def _mixer_call(kind, head, h, mod8, nrm8, weights, wrt, br, tri, tiles_per_seq):
    n_tok = (h if head is None else head[1]).shape[0]
    n_tiles, cap, _ = _sizes(n_tok)
    has_head = head is not None
    any_spec = pl.BlockSpec(memory_space=pl.ANY)
    smem_spec = pl.BlockSpec(memory_space=pltpu.SMEM)

    if has_head:
        args = list(head[1:])
        in_specs = [_nat(D_MODEL), _mod_spec(tiles_per_seq), _nat(LANES), any_spec]
    else:
        args = [h]
        in_specs = [_nat(D_MODEL)]
    args += [mod8, nrm8]
    in_specs += [_mod_spec(tiles_per_seq), _full((8, D_MODEL))]
    if kind == "even":
        win, wout, bias_tab, sinks, convw8 = weights
        args += [win, wout, bias_tab, sinks, convw8]
        in_specs += [_full((D_MODEL, IN_WIDTH)), _full((D_MODEL, D_MODEL)),
                     _full((N_HEADS // 2, BLOCK, 4 * BLOCK)), smem_spec, _full((8, CONV_WIDTH))]
        mix_scratch = [pltpu.VMEM((BLOCK, 2 * KV_WIDTH), F32), pltpu.VMEM((8, CONV_WIDTH), F32)]
    else:
        wpool, pscale8 = weights
        args += [wpool, pscale8]
        in_specs += [_full((len(POOL_SIZES), POOL_GROUP, POOL_GROUP)), _full((8, D_MODEL))]
        mix_scratch = [pltpu.VMEM((POOL_HALO, D_MODEL), F32)]
    args += [wrt, br, tri]
    in_specs += [_full((N_EXPERTS, D_MODEL)), _full((N_EXPERTS, 1)), _full((TB, TB))]

    out_shape = (jax.ShapeDtypeStruct((n_tok, D_MODEL), F32),
                 jax.ShapeDtypeStruct((n_tok, LANES), F32),
                 jax.ShapeDtypeStruct(((n_tiles + 1) * META_W,), I32),
                 jax.ShapeDtypeStruct((N_GROUPS * cap, ROW_W), BF16))
    out_specs = (_nat(D_MODEL), _nat(LANES), smem_spec, any_spec)
    scratch = mix_scratch + [
        pltpu.VMEM((2, STG_P, ROW_W), BF16),
        pltpu.VMEM((TBM, ROW_W), BF16),
        pltpu.SMEM((N_GROUPS,), I32),
        pltpu.SMEM((3 * N_GROUPS,), I32),
        pltpu.SemaphoreType.DMA((2,)),
        pltpu.SemaphoreType.DMA(()),
    ]
    if has_head:
        scratch += [pltpu.VMEM((2, STG_C, D_MODEL), BF16), pltpu.SemaphoreType.DMA((2,))]

    body = functools.partial(_mixer_kernel, kind=kind, has_head=has_head, n_tiles=n_tiles,
                             tiles_per_seq=tiles_per_seq, cap=cap)
    grid_spec = pltpu.PrefetchScalarGridSpec(
        num_scalar_prefetch=1 if has_head else 0, grid=(n_tiles,),
        in_specs=in_specs, out_specs=out_specs, scratch_shapes=scratch)
    call = pl.pallas_call(
        body, grid_spec=grid_spec, out_shape=out_shape,
        compiler_params=pltpu.CompilerParams(
            dimension_semantics=("arbitrary",), vmem_limit_bytes=VMEM_LIMIT),
        name=kind + "_mixer")
    if has_head:
        return call(head[0], *args)
    return call(*args)


def _moe_kernel(blk_ref, grp_ref, val_ref, xs_ref, wg_ref, wu_ref, wd_ref, ys_ref):
    i = pl.program_id(0)

    @pl.when(val_ref[i] == 1)
    def _():
        xb = xs_ref[:, 0:D_MODEL]
        meta = xs_ref[:, D_MODEL:ROW_W].astype(F32)
        gates = meta[:, 0:EXPERTS_PER_GROUP] + meta[:, EXPERTS_PER_GROUP:2 * EXPERTS_PER_GROUP]
        parts = []
        for e in range(EXPERTS_PER_GROUP):
            a = jnp.dot(xb, wg_ref[e], preferred_element_type=F32)
            bu = jnp.dot(xb, wu_ref[e], preferred_element_type=F32)
            hid = (a * _sigmoid(a)) * bu
            parts.append((hid * gates[:, e:e + 1]).astype(BF16))
        hid_all = jnp.concatenate(parts, axis=1)
        ys_ref[...] = jnp.dot(hid_all, wd_ref[0], preferred_element_type=F32).astype(BF16)


def _work_tables(totals, cap, n_work):
    ntile = (totals + (TBM - 1)) // TBM
    ends = jnp.cumsum(ntile)
    starts = ends - ntile
    nvalid = ends[-1]
    idx = jnp.arange(n_work, dtype=I32)
    idc = jnp.minimum(idx, nvalid - 1)
    grp = jnp.sum((idc[:, None] >= ends[None, :]).astype(I32), axis=1)
    blk = grp * (cap // TBM) + idc - starts[grp]
    return blk.astype(I32), grp.astype(I32), (idx < nvalid).astype(I32)


def _moe_call(xs, blk, grp, val, wg, wu, wd, n_work):
    grid_spec = pltpu.PrefetchScalarGridSpec(
        num_scalar_prefetch=3, grid=(n_work,),
        in_specs=[
            pl.BlockSpec((TBM, ROW_W), lambda i, blk, grp, val: (blk[i], 0)),
            pl.BlockSpec((EXPERTS_PER_GROUP, D_MODEL, EXPERT_FF), lambda i, blk, grp, val: (grp[i], 0, 0)),
            pl.BlockSpec((EXPERTS_PER_GROUP, D_MODEL, EXPERT_FF), lambda i, blk, grp, val: (grp[i], 0, 0)),
            pl.BlockSpec((1, GROUP_FF, D_MODEL), lambda i, blk, grp, val: (grp[i], 0, 0)),
        ],
        out_specs=pl.BlockSpec((TBM, D_MODEL), lambda i, blk, grp, val: (blk[i], 0)),
    )
    return pl.pallas_call(
        _moe_kernel, grid_spec=grid_spec,
        out_shape=jax.ShapeDtypeStruct((xs.shape[0], D_MODEL), BF16),
        compiler_params=pltpu.CompilerParams(
            dimension_semantics=("arbitrary",), vmem_limit_bytes=VMEM_LIMIT),
        name="moe_experts",
    )(blk, grp, val, xs, wg, wu, wd)


def _final_kernel(pm_ref, hprev_ref, modprev_ref, tokprev_ref, ys_hbm, nf_ref, o_ref,
                  ystage_ref, gsem, *, n_tiles):
    i = pl.program_id(0)
    h = _head(i, n_tiles, pm_ref, hprev_ref, modprev_ref, tokprev_ref, ys_hbm, ystage_ref, gsem)
    ms = jnp.mean(h * h, axis=-1, keepdims=True)
    o_ref[...] = (h * lax.rsqrt(ms + EPS)) * nf_ref[0:1]


def _final_call(head, nf8, tiles_per_seq):
    pm, hprev, modprev, tokprev, ys = head
    n_tok = hprev.shape[0]
    n_tiles, _, _ = _sizes(n_tok)
    grid_spec = pltpu.PrefetchScalarGridSpec(
        num_scalar_prefetch=1, grid=(n_tiles,),
        in_specs=[_nat(D_MODEL), _mod_spec(tiles_per_seq), _nat(LANES),
                  pl.BlockSpec(memory_space=pl.ANY), _full((8, D_MODEL))],
        out_specs=_nat(D_MODEL),
        scratch_shapes=[pltpu.VMEM((2, STG_C, D_MODEL), BF16), pltpu.SemaphoreType.DMA((2,))])
    return pl.pallas_call(
        functools.partial(_final_kernel, n_tiles=n_tiles), grid_spec=grid_spec,
        out_shape=jax.ShapeDtypeStruct((n_tok, D_MODEL), F32),
        compiler_params=pltpu.CompilerParams(
            dimension_semantics=("arbitrary",), vmem_limit_bytes=VMEM_LIMIT),
        name="final_norm",
    )(pm, hprev, modprev, tokprev, ys, nf8)


def _pad_rows(a, rows=8):
    return jnp.pad(a, ((0, rows - a.shape[0]), (0, 0)))


def _attn_bias_table():
    slopes = 2.0 ** (-8.0 * (jnp.arange(N_HEADS, dtype=F32) + 1.0) / N_HEADS)
    dist = (jnp.arange(BLOCK)[:, None] + BLOCK) - jnp.arange(2 * BLOCK)[None, :]
    ok = (dist >= 0) & (dist < WINDOW)
    per_head = jnp.where(ok[None], -slopes[:, None, None] * dist.astype(F32)[None], NEG_BIG)
    return per_head.reshape(N_HEADS // 2, 2, BLOCK, 2 * BLOCK).transpose(0, 2, 1, 3).reshape(
        N_HEADS // 2, BLOCK, 4 * BLOCK)


def kernel(x, c, w_ada, b_ada, norm_mix, norm_ffn, w_in, w_out, sinks, conv_w, w_pool,
           pool_scale, w_router, b_router, w_gate, w_up, w_down, norm_final):
    b, s, _ = x.shape
    n_tok = b * s
    tiles_per_seq = s // TB
    n_tiles, cap, n_work = _sizes(n_tok)

    mod_all = _mod_call(_pad_rows(c), w_ada, b_ada.reshape(DEPTH, 1, N_MOD * D_MODEL))
    bias_tab = _attn_bias_table()
    wrt = w_router.T
    br = b_router.reshape(N_EXPERTS, 1)
    nf8 = _pad_rows(norm_final[None, :])
    tri = jnp.triu(jnp.ones((TB, TB), F32)).astype(BF16)
    qscale = jnp.concatenate([jnp.full((ATTN_WIDTH,), HEAD_DIM ** -0.5, F32),
                              jnp.ones((IN_WIDTH - ATTN_WIDTH,), F32)])

    h = x.reshape(n_tok, D_MODEL)
    head = None
    for layer in range(DEPTH):
        j = layer // 2
        mod8 = jnp.pad(mod_all[layer, :b].reshape(b, N_MOD, D_MODEL), ((0, 0), (0, 2), (0, 0)))
        nrm8 = _pad_rows(jnp.stack([norm_mix[layer], norm_ffn[layer]]))
        if layer % 2 == 0:
            kind = "even"
            weights = ((w_in[j] * qscale).astype(BF16), w_out[j].astype(BF16), bias_tab,
                       sinks[j], _pad_rows(conv_w[j]))
        else:
            kind = "odd"
            weights = (w_pool[j].astype(BF16), _pad_rows(pool_scale[j][None, :]))
        h1, tokmeta, tmeta, xs = _mixer_call(kind, head, h, mod8, nrm8, weights, wrt, br, tri,
                                             tiles_per_seq)
        totals = tmeta[n_tiles * META_W:n_tiles * META_W + N_GROUPS]
        blk, grp, val = _work_tables(totals, cap, n_work)
        ys = _moe_call(xs, blk, grp, val, w_gate[layer].astype(BF16), w_up[layer].astype(BF16),
                       w_down[layer].astype(BF16).reshape(N_GROUPS, GROUP_FF, D_MODEL), n_work)
        head = (tmeta, h1, mod8, tokmeta, ys)
    out = _final_call(head, nf8, tiles_per_seq)
    return out.reshape(b, s, D_MODEL)
```

```python
import functools

import jax
import jax.numpy as jnp
from jax import lax
from jax.experimental import pallas as pl
from jax.experimental.pallas import tpu as pltpu

F32 = jnp.float32
BF16 = jnp.bfloat16
I32 = jnp.int32

D_MODEL = 1024
DEPTH = 4
EPS = 1e-6
N_MOD = 6

ATTN_WIDTH = 512
HEAD_DIM = 64
N_HEADS = 8
KV_WIDTH = 128
WINDOW = 128
BLOCK = 128
CONV_WIDTH = 512
IN_WIDTH = 2304

POOL_SIZES = (2, 4, 8, 16)
POOL_GROUP = 256
POOL_HALO = 16

N_EXPERTS = 16
N_GROUPS = 4
EXPERTS_PER_GROUP = 4
EXPERT_FF = 256
GROUP_FF = EXPERTS_PER_GROUP * EXPERT_FF

LANES = 128
NEG_BIG = -1e30

TB = 256
ALIGN = 16
SLOT = 96
SLOT_W = TB
STG = N_GROUPS * SLOT
STG_W = N_GROUPS * SLOT_W
TBM = 512
TAIL = TBM + SLOT_W
ROW_W = D_MODEL + LANES
META_W = 16
LP_LANE = 8
LPW_LANE = 9
MOD_COLS = 1536
VMEM_LIMIT = 56 * 1024 * 1024
NT_DIMS = (((1,), (1,)), ((), ()))

assert SLOT % ALIGN == 0 and STG % LANES == 0


def _sizes(n_tok):
    n_tiles = n_tok // TB
    cap = -(-(n_tok + n_tiles * (ALIGN - 1) + TAIL) // TBM) * TBM
    n_work = (n_tok + n_tiles * N_GROUPS * (ALIGN - 1) + N_GROUPS * SLOT_W) // TBM + N_GROUPS
    return n_tiles, cap, n_work


def _sigmoid(x):
    return 1.0 / (1.0 + jnp.exp(-x))


def _rms_mod(x, g, sc, sh):
    ms = jnp.mean(x * x, axis=-1, keepdims=True)
    return (x * lax.rsqrt(ms + EPS)) * (g * (1.0 + sc)) + sh


def _onehot(cond):
    return jnp.where(cond, 1.0, 0.0).astype(BF16)


def _mod_kernel(c_ref, w_ref, b_ref, o_ref):
    c = c_ref[...]
    cond = c * _sigmoid(c)
    o_ref[0] = jnp.dot(cond, w_ref[0], precision=lax.Precision.HIGHEST,
                       preferred_element_type=F32) + b_ref[0]


def _mod_call(c8, w_ada, b_ada3):
    n_col = (N_MOD * D_MODEL) // MOD_COLS
    return pl.pallas_call(
        _mod_kernel,
        grid=(DEPTH, n_col),
        in_specs=[
            pl.BlockSpec((8, D_MODEL), lambda l, j: (0, 0)),
            pl.BlockSpec((1, D_MODEL, MOD_COLS), lambda l, j: (l, 0, j)),
            pl.BlockSpec((1, 1, MOD_COLS), lambda l, j: (l, 0, j)),
        ],
        out_specs=pl.BlockSpec((1, 8, MOD_COLS), lambda l, j: (l, 0, j)),
        out_shape=jax.ShapeDtypeStruct((DEPTH, 8, N_MOD * D_MODEL), F32),
        compiler_params=pltpu.CompilerParams(
            dimension_semantics=("arbitrary", "arbitrary"), vmem_limit_bytes=VMEM_LIMIT),
        name="adaln_mod",
    )(c8, w_ada, b_ada3)


def _group_copies(hbm_ref, stage_ref, sem, dst, rows, to_hbm):
    out = []
    for g in range(N_GROUPS):
        s_view = stage_ref.at[pl.ds(g * rows, rows), :]
        h_view = hbm_ref.at[pl.ds(pl.multiple_of(dst[g], ALIGN), rows), :]
        out.append(pltpu.make_async_copy(s_view, h_view, sem) if to_hbm
                   else pltpu.make_async_copy(h_view, s_view, sem))
    return out


def _tile_dst(meta_ref, tile):
    return [meta_ref[tile * META_W + g] for g in range(N_GROUPS)]


def _head(i, n_tiles, pm_ref, hprev_ref, modprev_ref, tokmeta_ref, ys_hbm,
          ystage_ref, ystage_w_ref, ywide_ref, gsem, wsem):
    slot = i % 2
    wide = pm_ref[i * META_W + N_GROUPS] == 1

    def narrow(tile, slot_):
        return _group_copies(ys_hbm, ystage_ref.at[slot_], gsem.at[slot_],
                             _tile_dst(pm_ref, tile), SLOT, False)

    @pl.when(i == 0)
    def _():
        ywide_ref[...] = jnp.zeros_like(ywide_ref)
        for cp in narrow(0, 0):
            cp.start()

    @pl.when(i + 1 < n_tiles)
    def _():
        for cp in narrow(i + 1, 1 - slot):
            cp.start()

    for cp in narrow(i, slot):
        cp.wait()

    @pl.when(wide)
    def _():
        cps = _group_copies(ys_hbm, ystage_w_ref, wsem, _tile_dst(pm_ref, i), SLOT_W, False)
        for cp in cps:
            cp.start()
        for cp in cps:
            cp.wait()
        lpw = tokmeta_ref[:, LPW_LANE:LPW_LANE + 1].astype(I32)
        unsort_w = _onehot(lax.broadcasted_iota(I32, (TB, STG_W), 1) == lpw)
        ywide_ref[...] = jnp.dot(unsort_w, ystage_w_ref[...], preferred_element_type=F32)

    lp = tokmeta_ref[:, LP_LANE:LP_LANE + 1].astype(I32)
    unsort = _onehot(lax.broadcasted_iota(I32, (TB, STG), 1) == lp)
    y = jnp.dot(unsort, ystage_ref[slot], preferred_element_type=F32)
    y = jnp.where(wide, ywide_ref[...], y)
    return hprev_ref[...] + modprev_ref[0, 5:6] * y


def _route(u2, wrt_ref, br_ref):
    uh = u2.astype(BF16)
    ul = (u2 - uh.astype(F32)).astype(BF16)
    w = wrt_ref[...]
    r1 = lax.dot_general(w, uh, NT_DIMS, preferred_element_type=F32)
    r2 = lax.dot_general(w[0:N_EXPERTS], ul, NT_DIMS, preferred_element_type=F32)
    scores = _sigmoid(r1[0:N_EXPERTS] + r1[N_EXPERTS:] + r2)
    biased = scores + br_ref[...]
    s = [scores[4 * j:4 * j + 4] for j in range(EXPERTS_PER_GROUP)]
    b = [biased[4 * j:4 * j + 4] for j in range(EXPERTS_PER_GROUP)]

    hi1, lo1 = jnp.maximum(b[0], b[1]), jnp.minimum(b[0], b[1])
    hi2, lo2 = jnp.maximum(b[2], b[3]), jnp.minimum(b[2], b[3])
    second = jnp.maximum(jnp.minimum(hi1, hi2), jnp.where(hi1 >= hi2, lo1, lo2))
    gscore = jnp.maximum(hi1, hi2) + second

    best = gscore[0:1]
    bgrp = jnp.zeros((1, TB), I32)
    for g in range(1, N_GROUPS):
        better = gscore[g:g + 1] > best
        bgrp = jnp.where(better, g, bgrp)
        best = jnp.where(better, gscore[g:g + 1], best)
    selmask = lax.broadcasted_iota(I32, (N_GROUPS, TB), 0) == bgrp

    m1 = b[0]
    i1 = jnp.zeros((N_GROUPS, TB), I32)
    for j in range(1, EXPERTS_PER_GROUP):
        gt = b[j] > m1
        i1 = jnp.where(gt, j, i1)
        m1 = jnp.where(gt, b[j], m1)
    cands = [jnp.where(i1 == j, -jnp.inf, b[j]) for j in range(EXPERTS_PER_GROUP)]
    m2 = cands[0]
    i2 = jnp.zeros_like(i1)
    for j in range(1, EXPERTS_PER_GROUP):
        gt = cands[j] > m2
        i2 = jnp.where(gt, j, i2)
        m2 = jnp.where(gt, cands[j], m2)
    w1 = s[0]
    w2 = s[0]
    for j in range(1, EXPERTS_PER_GROUP):
        w1 = jnp.where(i1 == j, s[j], w1)
        w2 = jnp.where(i2 == j, s[j], w2)
    tot = w1 + w2
    w1n = w1 / tot
    w2n = w2 / tot
    gates = []
    for j in range(EXPERTS_PER_GROUP):
        gj = jnp.where(i1 == j, w1n, 0.0) + jnp.where(i2 == j, w2n, 0.0)
        gates.append(jnp.sum(jnp.where(selmask, gj, 0.0), axis=0, keepdims=True))
    return bgrp, gates, selmask


def _tail(i, n_tiles, cap, h1, mod, nrm_ref, wrt_ref, br_ref, tri_ref,
          tokmeta_ref, tmeta_ref, xs_hbm, stage_ref, stage_w_ref, zeros_ref,
          run_ref, prev_ref, ssem, wsem):
    sh2, sc2 = mod[3:4], mod[4:5]
    u2 = _rms_mod(h1, nrm_ref[1:2], sc2, sh2)
    bgrp, gates, selmask = _route(u2, wrt_ref, br_ref)

    onehot = jnp.concatenate([jnp.where(selmask, 1.0, 0.0), jnp.zeros((8 - N_GROUPS, TB), F32)],
                             axis=0).astype(BF16)
    rank_incl = jnp.dot(onehot, tri_ref[...], preferred_element_type=F32)
    rank = jnp.sum(jnp.where(selmask, rank_incl[0:N_GROUPS], 0.0), axis=0, keepdims=True) - 1.0
    cnt = lax.dot_general(jnp.ones((8, TB), BF16), onehot, NT_DIMS,
                          preferred_element_type=F32)
    bgrp_f = bgrp.astype(F32)
    lp = bgrp_f * SLOT + rank
    lp_w = bgrp_f * SLOT_W + rank

    hi = [x.astype(BF16).astype(F32) for x in gates]
    lo = [(x - h).astype(BF16).astype(F32) for x, h in zip(gates, hi)]
    meta_src = jnp.concatenate(hi + lo + [lp, lp_w, jnp.zeros((LANES - 10, TB), F32)], axis=0)
    meta_t = meta_src.T
    tokmeta_ref[...] = meta_t

    rowdata = jnp.concatenate([u2.astype(BF16), meta_t.astype(BF16)], axis=1)
    sort = _onehot(lax.broadcasted_iota(I32, (STG, TB), 0) == lp.astype(I32))
    slot = i % 2
    stage_ref[slot] = jnp.dot(sort, rowdata, preferred_element_type=F32).astype(BF16)

    n_s = [cnt[0, g].astype(I32) for g in range(N_GROUPS)]
    npad_s = [((n + (ALIGN - 1)) // ALIGN) * ALIGN for n in n_s]
    wide = n_s[0] > SLOT
    for g in range(1, N_GROUPS):
        wide = jnp.logical_or(wide, n_s[g] > SLOT)
    dst_s = [g * cap + run_ref[g] for g in range(N_GROUPS)]
    prev_dst = [prev_ref[g] for g in range(N_GROUPS)]
    prev_wide = prev_ref[N_GROUPS] == 1

    def narrow(slot_, dst):
        return _group_copies(xs_hbm, stage_ref.at[slot_], ssem.at[slot_], dst, SLOT, True)

    @pl.when(jnp.logical_and(i > 0, jnp.logical_not(prev_wide)))
    def _():
        for cp in narrow(1 - slot, prev_dst):
            cp.wait()

    @pl.when(jnp.logical_not(wide))
    def _():
        for cp in narrow(slot, dst_s):
            cp.start()

    @pl.when(wide)
    def _():
        sort_w = _onehot(lax.broadcasted_iota(I32, (STG_W, TB), 0) == lp_w.astype(I32))
        stage_w_ref[...] = jnp.dot(sort_w, rowdata, preferred_element_type=F32).astype(BF16)
        cps = _group_copies(xs_hbm, stage_w_ref, wsem, dst_s, SLOT_W, True)
        for cp in cps:
            cp.start()
        for cp in cps:
            cp.wait()

    base = i * META_W
    wide_i = wide.astype(I32)
    for g in range(N_GROUPS):
        prev_ref[g] = dst_s[g]
        tmeta_ref[base + g] = dst_s[g]
        run_ref[g] = run_ref[g] + npad_s[g]
    prev_ref[N_GROUPS] = wide_i
    tmeta_ref[base + N_GROUPS] = wide_i
    for k in range(N_GROUPS + 1, META_W):
        tmeta_ref[base + k] = jnp.int32(0)
    run_ref[N_GROUPS] = jnp.maximum(run_ref[N_GROUPS], wide_i)

    @pl.when(i == n_tiles - 1)
    def _():
        @pl.when(jnp.logical_not(wide))
        def _():
            for cp in narrow(slot, dst_s):
                cp.wait()
        zeros_ref[...] = jnp.zeros_like(zeros_ref)
        tails = []
        for g in range(N_GROUPS):
            total = run_ref[g]
            tmeta_ref[n_tiles * META_W + g] = total
            start = pl.multiple_of(g * cap + total, ALIGN)
            tails.append(pltpu.make_async_copy(zeros_ref, xs_hbm.at[pl.ds(start, TAIL), :], wsem))
        tmeta_ref[n_tiles * META_W + N_GROUPS] = run_ref[N_GROUPS]
        for k in range(N_GROUPS + 1, META_W):
            tmeta_ref[n_tiles * META_W + k] = jnp.int32(0)
        for cp in tails:
            cp.start()
        for cp in tails:
            cp.wait()


def _attn_conv_mix(x, t, mod, nrm_ref, win_ref, wout_ref, bias_ref, sink_ref, convw_ref,
                   kvprev_ref, cprev_ref):
    tb = TB
    first = t == 0
    sh1, sc1, g1 = mod[0:1], mod[1:2], mod[2:3]
    u = _rms_mod(x, nrm_ref[0:1], sc1, sh1)
    proj = jnp.dot(u.astype(BF16), win_ref[...], preferred_element_type=F32)

    q = proj[:, 0:ATTN_WIDTH].astype(BF16)
    kf = proj[:, 512:640]
    vf = proj[:, 640:768]
    bgate = proj[:, 768:1280]
    cgate = proj[:, 1280:1792]
    xv = proj[:, 1792:2304]

    kv_prev = kvprev_ref[...]
    kext = jnp.concatenate([kv_prev[:, 0:KV_WIDTH], kf], axis=0)
    vext = jnp.concatenate([kv_prev[:, KV_WIDTH:], vf], axis=0)
    kvprev_ref[:, 0:KV_WIDTH] = kf[tb - BLOCK:tb]
    kvprev_ref[:, KV_WIDTH:] = vf[tb - BLOCK:tb]

    lane = lax.broadcasted_iota(I32, kext.shape, 1)
    lo = lane < HEAD_DIM
    krol = pltpu.roll(kext, HEAD_DIM, axis=1)
    vrol = pltpu.roll(vext, HEAD_DIM, axis=1)
    zero = jnp.zeros_like(kext)
    one_at_64 = jnp.where(lane == HEAD_DIM, 1.0, 0.0)
    one_at_0 = jnp.where(lane == 0, 1.0, 0.0)
    k_ops = [(jnp.where(lo, kext, zero).astype(BF16), jnp.where(lo, zero, krol).astype(BF16)),
             (jnp.where(lo, krol, zero).astype(BF16), jnp.where(lo, zero, kext).astype(BF16))]
    v_ops = [(jnp.where(lo, vext, one_at_64).astype(BF16), jnp.where(lo, one_at_0, vrol).astype(BF16)),
             (jnp.where(lo, vrol, one_at_64).astype(BF16), jnp.where(lo, one_at_0, vext).astype(BF16))]

    col = lax.broadcasted_iota(I32, (BLOCK, 4 * BLOCK), 1)
    prev_cols = (col % (2 * BLOCK)) < BLOCK
    lane_o = lax.broadcasted_iota(I32, (BLOCK, LANES), 1)
    lo_o = lane_o < HEAD_DIM

    attn_rows = []
    for bi in range(tb // BLOCK):
        r0 = bi * BLOCK
        pair_out = []
        for pr in range(N_HEADS // 2):
            kvh = pr // 2
            ka, kb = k_ops[kvh]
            va, vb = v_ops[kvh]
            rhs = jnp.concatenate([ka[r0:r0 + 2 * BLOCK], kb[r0:r0 + 2 * BLOCK]], axis=0)
            qp = q[r0:r0 + BLOCK, pr * LANES:(pr + 1) * LANES]
            s = lax.dot_general(qp, rhs, NT_DIMS, preferred_element_type=F32)
            bias = bias_ref[pr]
            if bi == 0:
                bias = jnp.where(prev_cols & first, NEG_BIG, bias)
            s = s + bias
            outs = []
            for hh in range(2):
                sh = s[:, hh * 2 * BLOCK:(hh + 1) * 2 * BLOCK]
                sink = sink_ref[2 * pr + hh]
                m = jnp.maximum(jnp.max(sh, axis=-1, keepdims=True), sink)
                p = jnp.exp(sh - m).astype(BF16)
                vop = (va, vb)[hh][r0:r0 + 2 * BLOCK]
                o = jnp.dot(p, vop, preferred_element_type=F32)
                den_col = HEAD_DIM if hh == 0 else 0
                den = o[:, den_col:den_col + 1] + jnp.exp(sink - m)
                outs.append(o / den)
            pair_out.append(jnp.where(lo_o, outs[0], outs[1]))
        attn_rows.append(jnp.concatenate(pair_out, axis=1))
    attn = jnp.concatenate(attn_rows, axis=0)

    uc = cgate * xv
    cprev = jnp.where(first, 0.0, cprev_ref[...])
    row = lax.broadcasted_iota(I32, uc.shape, 0)
    r1 = jnp.where(row == 0, cprev[7:8], pltpu.roll(uc, 1, axis=0))
    r2 = jnp.where(row == 0, cprev[6:7], jnp.where(row == 1, cprev[7:8], pltpu.roll(uc, 2, axis=0)))
    cprev_ref[...] = uc[tb - 8:tb]
    cw = convw_ref[...]
    conv = bgate * (cw[0:1] * r2 + cw[1:2] * r1 + cw[2:3] * uc)

    mix = (jnp.dot(attn.astype(BF16), wout_ref[0:ATTN_WIDTH, :], preferred_element_type=F32)
           + jnp.dot(conv.astype(BF16), wout_ref[ATTN_WIDTH:, :], preferred_element_type=F32))
    return x + g1 * mix


def _pool_mix(x, t, mod, nrm_ref, wpool_ref, pscale_ref, uprev_ref):
    tb = TB
    sh1, sc1, g1 = mod[0:1], mod[1:2], mod[2:3]
    u = _rms_mod(x, nrm_ref[0:1], sc1, sh1)
    ext = jnp.concatenate([jnp.where(t == 0, 0.0, uprev_ref[...]), u], axis=0)
    uprev_ref[...] = u[tb - POOL_HALO:tb]

    pos = (t * tb + 1 + lax.broadcasted_iota(I32, (tb, 1), 0)).astype(F32)
    mixed = []
    for gi, w in enumerate(POOL_SIZES):
        sl = slice(gi * POOL_GROUP, (gi + 1) * POOL_GROUP)
        acc = ext[:, sl]
        shift = 1
        while shift < w:
            acc = acc + pltpu.roll(acc, shift, axis=0)
            shift *= 2
        mean = acc[POOL_HALO:] / jnp.minimum(pos, float(w))
        pooled = mean - u[:, sl]
        mixed.append(jnp.dot(pooled.astype(BF16), wpool_ref[gi], preferred_element_type=F32))
    mix = jnp.concatenate(mixed, axis=1) * pscale_ref[0:1]
    return x + g1 * mix


def _mixer_kernel(*refs, kind, has_head, n_tiles, tiles_per_seq, cap):
    refs = list(refs)
    i = pl.program_id(0)
    t = i % tiles_per_seq
    if has_head:
        pm_ref = refs.pop(0)
        hprev_ref, modprev_ref, tokprev_ref, ys_hbm = refs[:4]
        refs = refs[4:]
    else:
        h_ref = refs.pop(0)
    mod_ref, nrm_ref = refs[:2]
    refs = refs[2:]
    if kind == "even":
        win_ref, wout_ref, bias_ref, sink_ref, convw_ref = refs[:5]
        refs = refs[5:]
    else:
        wpool_ref, pscale_ref = refs[:2]
        refs = refs[2:]
    wrt_ref, br_ref, tri_ref = refs[:3]
    refs = refs[3:]
    hout_ref, tokmeta_ref, tmeta_ref, xs_hbm = refs[:4]
    refs = refs[4:]
    if kind == "even":
        carry_refs = refs[:2]
        refs = refs[2:]
    else:
        carry_refs = refs[:1]
        refs = refs[1:]
    stage_ref, stage_w_ref, zeros_ref, run_ref, prev_ref, ssem, wsem = refs[:7]
    refs = refs[7:]

    @pl.when(i == 0)
    def _():
        for r in carry_refs:
            r[...] = jnp.zeros_like(r)
        for k in range(N_GROUPS + 1):
            run_ref[k] = jnp.int32(0)
            prev_ref[k] = jnp.int32(0)

    if has_head:
        ystage_ref, ystage_w_ref, ywide_ref, gsem, gwsem = refs
        x = _head(i, n_tiles, pm_ref, hprev_ref, modprev_ref, tokprev_ref, ys_hbm,
                  ystage_ref, ystage_w_ref, ywide_ref, gsem, gwsem)
    else:
        x = h_ref[...]
    mod = mod_ref[0]
    if kind == "even":
        h1 = _attn_conv_mix(x, t, mod, nrm_ref, win_ref, wout_ref, bias_ref, sink_ref, convw_ref,
                            *carry_refs)
    else:
        h1 = _pool_mix(x, t, mod, nrm_ref, wpool_ref, pscale_ref, *carry_refs)
    hout_ref[...] = h1
    _tail(i, n_tiles, cap, h1, mod, nrm_ref, wrt_ref, br_ref, tri_ref, tokmeta_ref, tmeta_ref,
          xs_hbm, stage_ref, stage_w_ref, zeros_ref, run_ref, prev_ref, ssem, wsem)


def _nat(cols):
    return pl.BlockSpec((TB, cols), lambda i, *_: (i, 0))


def _full(shape):
    nd = len(shape)
    return pl.BlockSpec(shape, lambda i, *_: (0,) * nd)


def _mod_spec(tiles_per_seq):
    return pl.BlockSpec((1, 8, D_MODEL), lambda i, *_: (i // tiles_per_seq, 0, 0))


def _head_scratch():
    return [pltpu.VMEM((2, STG, D_MODEL), BF16), pltpu.VMEM((STG_W, D_MODEL), BF16),
            pltpu.VMEM((TB, D_MODEL), F32), pltpu.SemaphoreType.DMA((2,)),
            pltpu.SemaphoreType.DMA(())]


def _mixer_call(kind, head, h, mod8, nrm8, weights, wrt, br, tri, tiles_per_seq):
    n_tok = (h if head is None else head[1]).shape[0]
    n_tiles, cap, _ = _sizes(n_tok)
    has_head = head is not None
    any_spec = pl.BlockSpec(memory_space=pl.ANY)
    smem_spec = pl.BlockSpec(memory_space=pltpu.SMEM)

    if has_head:
        args = list(head[1:])
        in_specs = [_nat(D_MODEL), _mod_spec(tiles_per_seq), _nat(LANES), any_spec]
    else:
        args = [h]
        in_specs = [_nat(D_MODEL)]
    args += [mod8, nrm8]
    in_specs += [_mod_spec(tiles_per_seq), _full((8, D_MODEL))]
    if kind == "even":
        win, wout, bias_tab, sinks, convw8 = weights
        args += [win, wout, bias_tab, sinks, convw8]
        in_specs += [_full((D_MODEL, IN_WIDTH)), _full((D_MODEL, D_MODEL)),
                     _full((N_HEADS // 2, BLOCK, 4 * BLOCK)), smem_spec, _full((8, CONV_WIDTH))]
        mix_scratch = [pltpu.VMEM((BLOCK, 2 * KV_WIDTH), F32), pltpu.VMEM((8, CONV_WIDTH), F32)]
    else:
        wpool, pscale8 = weights
        args += [wpool, pscale8]
        in_specs += [_full((len(POOL_SIZES), POOL_GROUP, POOL_GROUP)), _full((8, D_MODEL))]
        mix_scratch = [pltpu.VMEM((POOL_HALO, D_MODEL), F32)]
    args += [wrt, br, tri]
    in_specs += [_full((2 * N_EXPERTS, D_MODEL)), _full((N_EXPERTS, 1)), _full((TB, TB))]

    out_shape = (jax.ShapeDtypeStruct((n_tok, D_MODEL), F32),
                 jax.ShapeDtypeStruct((n_tok, LANES), F32),
                 jax.ShapeDtypeStruct(((n_tiles + 1) * META_W,), I32),
                 jax.ShapeDtypeStruct((N_GROUPS * cap, ROW_W), BF16))
    out_specs = (_nat(D_MODEL), _nat(LANES), smem_spec, any_spec)
    scratch = mix_scratch + [
        pltpu.VMEM((2, STG, ROW_W), BF16),
        pltpu.VMEM((STG_W, ROW_W), BF16),
        pltpu.VMEM((TAIL, ROW_W), BF16),
        pltpu.SMEM((N_GROUPS + 1,), I32),
        pltpu.SMEM((N_GROUPS + 1,), I32),
        pltpu.SemaphoreType.DMA((2,)),
        pltpu.SemaphoreType.DMA(()),
    ]
    if has_head:
        scratch += _head_scratch()

    body = functools.partial(_mixer_kernel, kind=kind, has_head=has_head, n_tiles=n_tiles,
                             tiles_per_seq=tiles_per_seq, cap=cap)
    grid_spec = pltpu.PrefetchScalarGridSpec(
        num_scalar_prefetch=1 if has_head else 0, grid=(n_tiles,),
        in_specs=in_specs, out_specs=out_specs, scratch_shapes=scratch)
    call = pl.pallas_call(
        body, grid_spec=grid_spec, out_shape=out_shape,
        compiler_params=pltpu.CompilerParams(
            dimension_semantics=("arbitrary",), vmem_limit_bytes=VMEM_LIMIT),
        name=kind + "_mixer")
    if has_head:
        return call(head[0], *args)
    return call(*args)


def _moe_kernel(blk_ref, grp_ref, val_ref, xs_ref, wg_ref, wu_ref, wd_ref, ys_ref):
    i = pl.program_id(0)

    @pl.when(val_ref[i] == 1)
    def _():
        xb = xs_ref[:, 0:D_MODEL]
        meta = xs_ref[:, D_MODEL:ROW_W].astype(F32)
        gates = meta[:, 0:EXPERTS_PER_GROUP] + meta[:, EXPERTS_PER_GROUP:2 * EXPERTS_PER_GROUP]
        parts = []
        for e in range(EXPERTS_PER_GROUP):
            a = jnp.dot(xb, wg_ref[e], preferred_element_type=F32)
            bu = jnp.dot(xb, wu_ref[e], preferred_element_type=F32)
            hid = (a * _sigmoid(a)) * bu
            parts.append((hid * gates[:, e:e + 1]).astype(BF16))
        hid_all = jnp.concatenate(parts, axis=1)
        ys_ref[...] = jnp.dot(hid_all, wd_ref[0], preferred_element_type=F32).astype(BF16)


def _work_tables(totals, any_wide, cap, n_work):
    slack = jnp.where(any_wide > 0, SLOT_W, SLOT)
    ntile = (totals + slack + (TBM - 1)) // TBM
    ends = jnp.cumsum(ntile)
    starts = ends - ntile
    nvalid = ends[-1]
    idx = jnp.arange(n_work, dtype=I32)
    idc = jnp.maximum(jnp.minimum(idx, nvalid - 1), 0)
    grp = jnp.minimum(jnp.sum((idc[:, None] >= ends[None, :]).astype(I32), axis=1), N_GROUPS - 1)
    blk = grp * (cap // TBM) + idc - starts[grp]
    return blk.astype(I32), grp.astype(I32), (idx < nvalid).astype(I32)


def _moe_call(xs, blk, grp, val, wg, wu, wd, n_work):
    grid_spec = pltpu.PrefetchScalarGridSpec(
        num_scalar_prefetch=3, grid=(n_work,),
        in_specs=[
            pl.BlockSpec((TBM, ROW_W), lambda i, blk, grp, val: (blk[i], 0)),
            pl.BlockSpec((EXPERTS_PER_GROUP, D_MODEL, EXPERT_FF), lambda i, blk, grp, val: (grp[i], 0, 0)),
            pl.BlockSpec((EXPERTS_PER_GROUP, D_MODEL, EXPERT_FF), lambda i, blk, grp, val: (grp[i], 0, 0)),
            pl.BlockSpec((1, GROUP_FF, D_MODEL), lambda i, blk, grp, val: (grp[i], 0, 0)),
        ],
        out_specs=pl.BlockSpec((TBM, D_MODEL), lambda i, blk, grp, val: (blk[i], 0)),
    )
    return pl.pallas_call(
        _moe_kernel, grid_spec=grid_spec,
        out_shape=jax.ShapeDtypeStruct((xs.shape[0], D_MODEL), BF16),
        compiler_params=pltpu.CompilerParams(
            dimension_semantics=("arbitrary",), vmem_limit_bytes=VMEM_LIMIT),
        name="moe_experts",
    )(blk, grp, val, xs, wg, wu, wd)


def _final_kernel(pm_ref, hprev_ref, modprev_ref, tokprev_ref, ys_hbm, nf_ref, o_ref,
                  ystage_ref, ystage_w_ref, ywide_ref, gsem, gwsem, *, n_tiles):
    i = pl.program_id(0)
    h = _head(i, n_tiles, pm_ref, hprev_ref, modprev_ref, tokprev_ref, ys_hbm,
              ystage_ref, ystage_w_ref, ywide_ref, gsem, gwsem)
    ms = jnp.mean(h * h, axis=-1, keepdims=True)
    o_ref[...] = (h * lax.rsqrt(ms + EPS)) * nf_ref[0:1]


def _final_call(head, nf8, tiles_per_seq):
    pm, hprev, modprev, tokprev, ys = head
    n_tok = hprev.shape[0]
    n_tiles, _, _ = _sizes(n_tok)
    grid_spec = pltpu.PrefetchScalarGridSpec(
        num_scalar_prefetch=1, grid=(n_tiles,),
        in_specs=[_nat(D_MODEL), _mod_spec(tiles_per_seq), _nat(LANES),
                  pl.BlockSpec(memory_space=pl.ANY), _full((8, D_MODEL))],
        out_specs=_nat(D_MODEL),
        scratch_shapes=_head_scratch())
    return pl.pallas_call(
        functools.partial(_final_kernel, n_tiles=n_tiles), grid_spec=grid_spec,
        out_shape=jax.ShapeDtypeStruct((n_tok, D_MODEL), F32),
        compiler_params=pltpu.CompilerParams(
            dimension_semantics=("arbitrary",), vmem_limit_bytes=VMEM_LIMIT),
        name="final_norm",
    )(pm, hprev, modprev, tokprev, ys, nf8)


def _pad_rows(a, rows=8):
    return jnp.pad(a, ((0, rows - a.shape[0]), (0, 0)))


def _attn_bias_table():
    slopes = 2.0 ** (-8.0 * (jnp.arange(N_HEADS, dtype=F32) + 1.0) / N_HEADS)
    dist = (jnp.arange(BLOCK)[:, None] + BLOCK) - jnp.arange(2 * BLOCK)[None, :]
    ok = (dist >= 0) & (dist < WINDOW)
    per_head = jnp.where(ok[None], -slopes[:, None, None] * dist.astype(F32)[None], NEG_BIG)
    return per_head.reshape(N_HEADS // 2, 2, BLOCK, 2 * BLOCK).transpose(0, 2, 1, 3).reshape(
        N_HEADS // 2, BLOCK, 4 * BLOCK)


def _router_operands(w_router, b_router):
    def reorder(a):
        return a.reshape(N_GROUPS, EXPERTS_PER_GROUP, -1).transpose(1, 0, 2).reshape(N_EXPERTS, -1)
    w = reorder(w_router.T)
    wh = w.astype(BF16)
    wl = (w - wh.astype(F32)).astype(BF16)
    return jnp.concatenate([wh, wl], axis=0), reorder(b_router[:, None])


def kernel(x, c, w_ada, b_ada, norm_mix, norm_ffn, w_in, w_out, sinks, conv_w, w_pool,
           pool_scale, w_router, b_router, w_gate, w_up, w_down, norm_final):
    b, s, _ = x.shape
    n_tok = b * s
    tiles_per_seq = s // TB
    n_tiles, cap, n_work = _sizes(n_tok)

    mod_all = _mod_call(_pad_rows(c), w_ada, b_ada.reshape(DEPTH, 1, N_MOD * D_MODEL))
    bias_tab = _attn_bias_table()
    wrt, br = _router_operands(w_router, b_router)
    nf8 = _pad_rows(norm_final[None, :])
    tri = jnp.triu(jnp.ones((TB, TB), F32)).astype(BF16)
    qscale = jnp.concatenate([jnp.full((ATTN_WIDTH,), HEAD_DIM ** -0.5, F32),
                              jnp.ones((IN_WIDTH - ATTN_WIDTH,), F32)])

    h = x.reshape(n_tok, D_MODEL)
    head = None
    for layer in range(DEPTH):
        j = layer // 2
        mod8 = jnp.pad(mod_all[layer, :b].reshape(b, N_MOD, D_MODEL), ((0, 0), (0, 2), (0, 0)))
        nrm8 = _pad_rows(jnp.stack([norm_mix[layer], norm_ffn[layer]]))
        if layer % 2 == 0:
            kind = "even"
            weights = ((w_in[j] * qscale).astype(BF16), w_out[j].astype(BF16), bias_tab,
                       sinks[j], _pad_rows(conv_w[j]))
        else:
            kind = "odd"
            weights = (w_pool[j].astype(BF16), _pad_rows(pool_scale[j][None, :]))
        h1, tokmeta, tmeta, xs = _mixer_call(kind, head, h, mod8, nrm8, weights, wrt, br, tri,
                                             tiles_per_seq)
        totals = tmeta[n_tiles * META_W:n_tiles * META_W + N_GROUPS]
        blk, grp, val = _work_tables(totals, tmeta[n_tiles * META_W + N_GROUPS], cap, n_work)
        ys = _moe_call(xs, blk, grp, val, w_gate[layer].astype(BF16), w_up[layer].astype(BF16),
                       w_down[layer].astype(BF16).reshape(N_GROUPS, GROUP_FF, D_MODEL), n_work)
        head = (tmeta, h1, mod8, tokmeta, ys)
    out = _final_call(head, nf8, tiles_per_seq)
    return out.reshape(b, s, D_MODEL)
```

```python
import functools

import jax
import jax.numpy as jnp
from jax import lax
from jax.experimental import pallas as pl
from jax.experimental.pallas import tpu as pltpu

F32 = jnp.float32
BF16 = jnp.bfloat16
I32 = jnp.int32

D_MODEL = 1024
DEPTH = 4
EPS = 1e-6
N_MOD = 6

ATTN_WIDTH = 512
HEAD_DIM = 64
N_HEADS = 8
KV_WIDTH = 128
WINDOW = 128
BLOCK = 128
CONV_WIDTH = 512
IN_WIDTH = 2304

POOL_SIZES = (2, 4, 8, 16)
POOL_GROUP = 256
POOL_HALO = 16

N_EXPERTS = 16
N_GROUPS = 4
EXPERTS_PER_GROUP = 4
EXPERT_FF = 256
GROUP_FF = EXPERTS_PER_GROUP * EXPERT_FF

LANES = 128
NEG_BIG = -1e30

TB = 256
ALIGN = 16
SUB = 32
N_SUB = 12
STG = N_SUB * SUB
TBM = 512
TAIL = TBM + SUB
ROW_W = D_MODEL + LANES
META_W = 16
LP_LANE = 8
MOD_COLS = 1536
VMEM_LIMIT = 56 * 1024 * 1024
NT_DIMS = (((1,), (1,)), ((), ()))

assert SUB % ALIGN == 0 and STG % LANES == 0 and N_SUB <= META_W
assert N_SUB >= TB // SUB + N_GROUPS - 1


def _sizes(n_tok):
    n_tiles = n_tok // TB
    cap = -(-(n_tok + n_tiles * (ALIGN - 1) + TAIL) // TBM) * TBM
    n_work = (n_tok + n_tiles * N_GROUPS * (ALIGN - 1) + N_GROUPS * SUB) // TBM + N_GROUPS
    return n_tiles, cap, n_work


def _sigmoid(x):
    return 1.0 / (1.0 + jnp.exp(-x))


def _rms_mod(x, g, sc, sh):
    ms = jnp.mean(x * x, axis=-1, keepdims=True)
    return (x * lax.rsqrt(ms + EPS)) * (g * (1.0 + sc)) + sh


def _onehot(cond):
    return jnp.where(cond, 1.0, 0.0).astype(BF16)


def _mod_kernel(c_ref, w_ref, b_ref, o_ref):
    c = c_ref[...]
    cond = c * _sigmoid(c)
    o_ref[0] = jnp.dot(cond, w_ref[0], precision=lax.Precision.HIGHEST,
                       preferred_element_type=F32) + b_ref[0]


def _mod_call(c8, w_ada, b_ada3):
    n_col = (N_MOD * D_MODEL) // MOD_COLS
    return pl.pallas_call(
        _mod_kernel,
        grid=(DEPTH, n_col),
        in_specs=[
            pl.BlockSpec((8, D_MODEL), lambda l, j: (0, 0)),
            pl.BlockSpec((1, D_MODEL, MOD_COLS), lambda l, j: (l, 0, j)),
            pl.BlockSpec((1, 1, MOD_COLS), lambda l, j: (l, 0, j)),
        ],
        out_specs=pl.BlockSpec((1, 8, MOD_COLS), lambda l, j: (l, 0, j)),
        out_shape=jax.ShapeDtypeStruct((DEPTH, 8, N_MOD * D_MODEL), F32),
        compiler_params=pltpu.CompilerParams(
            dimension_semantics=("arbitrary", "arbitrary"), vmem_limit_bytes=VMEM_LIMIT),
        name="adaln_mod",
    )(c8, w_ada, b_ada3)


def _sub_copies(hbm_ref, stage_ref, sem, rows, to_hbm):
    out = []
    for s in range(N_SUB):
        s_view = stage_ref.at[pl.ds(s * SUB, SUB), :]
        h_view = hbm_ref.at[pl.ds(pl.multiple_of(rows[s], ALIGN), SUB), :]
        out.append(pltpu.make_async_copy(s_view, h_view, sem) if to_hbm
                   else pltpu.make_async_copy(h_view, s_view, sem))
    return out


def _head(i, n_tiles, pm_ref, hprev_ref, modprev_ref, tokmeta_ref, ys_hbm, ystage_ref, gsem):
    slot = i % 2

    def copies(tile, slot_):
        rows = [pm_ref[tile * META_W + s] for s in range(N_SUB)]
        return _sub_copies(ys_hbm, ystage_ref.at[slot_], gsem.at[slot_], rows, False)

    @pl.when(i == 0)
    def _():
        for cp in copies(0, 0):
            cp.start()

    @pl.when(i + 1 < n_tiles)
    def _():
        for cp in copies(i + 1, 1 - slot):
            cp.start()

    for cp in copies(i, slot):
        cp.wait()

    lp = tokmeta_ref[:, LP_LANE:LP_LANE + 1].astype(I32)
    unsort = _onehot(lax.broadcasted_iota(I32, (TB, STG), 1) == lp)
    y = jnp.dot(unsort, ystage_ref[slot], preferred_element_type=F32)
    return hprev_ref[...] + modprev_ref[0, 5:6] * y


def _route(u2, wrt_ref, br_ref):
    uh = u2.astype(BF16)
    ul = (u2 - uh.astype(F32)).astype(BF16)
    w = wrt_ref[...]
    r1 = lax.dot_general(w, uh, NT_DIMS, preferred_element_type=F32)
    r2 = lax.dot_general(w[0:N_EXPERTS], ul, NT_DIMS, preferred_element_type=F32)
    scores = _sigmoid(r1[0:N_EXPERTS] + r1[N_EXPERTS:] + r2)
    biased = scores + br_ref[...]
    s = [scores[4 * j:4 * j + 4] for j in range(EXPERTS_PER_GROUP)]
    b = [biased[4 * j:4 * j + 4] for j in range(EXPERTS_PER_GROUP)]

    hi1, lo1 = jnp.maximum(b[0], b[1]), jnp.minimum(b[0], b[1])
    hi2, lo2 = jnp.maximum(b[2], b[3]), jnp.minimum(b[2], b[3])
    second = jnp.maximum(jnp.minimum(hi1, hi2), jnp.where(hi1 >= hi2, lo1, lo2))
    gscore = jnp.maximum(hi1, hi2) + second

    best = gscore[0:1]
    bgrp = jnp.zeros((1, TB), I32)
    for g in range(1, N_GROUPS):
        better = gscore[g:g + 1] > best
        bgrp = jnp.where(better, g, bgrp)
        best = jnp.where(better, gscore[g:g + 1], best)
    selmask = lax.broadcasted_iota(I32, (N_GROUPS, TB), 0) == bgrp

    m1 = b[0]
    i1 = jnp.zeros((N_GROUPS, TB), I32)
    for j in range(1, EXPERTS_PER_GROUP):
        gt = b[j] > m1
        i1 = jnp.where(gt, j, i1)
        m1 = jnp.where(gt, b[j], m1)
    cands = [jnp.where(i1 == j, -jnp.inf, b[j]) for j in range(EXPERTS_PER_GROUP)]
    m2 = cands[0]
    i2 = jnp.zeros_like(i1)
    for j in range(1, EXPERTS_PER_GROUP):
        gt = cands[j] > m2
        i2 = jnp.where(gt, j, i2)
        m2 = jnp.where(gt, cands[j], m2)
    w1 = s[0]
    w2 = s[0]
    for j in range(1, EXPERTS_PER_GROUP):
        w1 = jnp.where(i1 == j, s[j], w1)
        w2 = jnp.where(i2 == j, s[j], w2)
    tot = w1 + w2
    w1n = w1 / tot
    w2n = w2 / tot
    gates = []
    for j in range(EXPERTS_PER_GROUP):
        gj = jnp.where(i1 == j, w1n, 0.0) + jnp.where(i2 == j, w2n, 0.0)
        gates.append(jnp.sum(jnp.where(selmask, gj, 0.0), axis=0, keepdims=True))
    return bgrp, gates, selmask


def _tail(i, n_tiles, cap, h1, mod, nrm_ref, wrt_ref, br_ref, tri_ref,
          tokmeta_ref, tmeta_ref, xs_hbm, stage_ref, zeros_ref, run_ref, prev_ref, ssem, zsem):
    sh2, sc2 = mod[3:4], mod[4:5]
    u2 = _rms_mod(h1, nrm_ref[1:2], sc2, sh2)
    bgrp, gates, selmask = _route(u2, wrt_ref, br_ref)

    onehot = jnp.concatenate([jnp.where(selmask, 1.0, 0.0), jnp.zeros((8 - N_GROUPS, TB), F32)],
                             axis=0).astype(BF16)
    rank_incl = jnp.dot(onehot, tri_ref[...], preferred_element_type=F32)
    rank = jnp.sum(jnp.where(selmask, rank_incl[0:N_GROUPS], 0.0), axis=0, keepdims=True) - 1.0
    cnt = lax.dot_general(jnp.ones((8, TB), BF16), onehot, NT_DIMS,
                          preferred_element_type=F32)
    sub_rows = jnp.floor((cnt[0:1, :] + (SUB - 1.0)) * (1.0 / SUB)) * SUB
    lp = rank
    first_row = jnp.zeros((1, 1), F32)
    for g in range(1, N_GROUPS):
        first_row = first_row + sub_rows[:, g - 1:g]
        lp = lp + jnp.where(bgrp == g, first_row, 0.0)

    hi = [x.astype(BF16).astype(F32) for x in gates]
    lo = [(x - h).astype(BF16).astype(F32) for x, h in zip(gates, hi)]
    meta_src = jnp.concatenate(hi + lo + [lp, jnp.zeros((LANES - 9, TB), F32)], axis=0)
    meta_t = meta_src.T
    tokmeta_ref[...] = meta_t

    rowdata = jnp.concatenate([u2.astype(BF16), meta_t.astype(BF16)], axis=1)
    sort = _onehot(lax.broadcasted_iota(I32, (STG, TB), 0) == lp.astype(I32))
    slot = i % 2
    stage_ref[slot] = jnp.dot(sort, rowdata, preferred_element_type=F32).astype(BF16)

    n_s = [cnt[0, g].astype(I32) for g in range(N_GROUPS)]
    npad_s = [((n + (ALIGN - 1)) // ALIGN) * ALIGN for n in n_s]
    first_sub = [jnp.int32(0)]
    for g in range(N_GROUPS):
        first_sub.append(first_sub[-1] + (n_s[g] + (SUB - 1)) // SUB)
    dst_s = [g * cap + run_ref[g] for g in range(N_GROUPS)]
    junk = N_GROUPS * cap
    scatter_rows, gather_rows = [], []
    for s in range(N_SUB):
        grp_first, grp_dst = first_sub[0], dst_s[0]
        for g in range(1, N_GROUPS):
            later = s >= first_sub[g]
            grp_first = jnp.where(later, first_sub[g], grp_first)
            grp_dst = jnp.where(later, dst_s[g], grp_dst)
        row = grp_dst + (s - grp_first) * SUB
        used = s < first_sub[N_GROUPS]
        scatter_rows.append(jnp.where(used, row, junk + s * SUB))
        gather_rows.append(jnp.where(used, row, gather_rows[0] if s else row))

    def copies(slot_, rows):
        return _sub_copies(xs_hbm, stage_ref.at[slot_], ssem.at[slot_], rows, True)

    @pl.when(i > 0)
    def _():
        for cp in copies(1 - slot, [prev_ref[s] for s in range(N_SUB)]):
            cp.wait()

    for cp in copies(slot, scatter_rows):
        cp.start()

    base = i * META_W
    for s in range(N_SUB):
        prev_ref[s] = scatter_rows[s]
        tmeta_ref[base + s] = gather_rows[s]
    for k in range(N_SUB, META_W):
        tmeta_ref[base + k] = jnp.int32(0)
    for g in range(N_GROUPS):
        run_ref[g] = run_ref[g] + npad_s[g]

    @pl.when(i == n_tiles - 1)
    def _():
        for cp in copies(slot, scatter_rows):
            cp.wait()
        zeros_ref[...] = jnp.zeros_like(zeros_ref)
        tails = []
        for g in range(N_GROUPS):
            total = run_ref[g]
            tmeta_ref[n_tiles * META_W + g] = total
            start = pl.multiple_of(g * cap + total, ALIGN)
            tails.append(pltpu.make_async_copy(zeros_ref, xs_hbm.at[pl.ds(start, TAIL), :], zsem))
        for k in range(N_GROUPS, META_W):
            tmeta_ref[n_tiles * META_W + k] = jnp.int32(0)
        for cp in tails:
            cp.start()
        for cp in tails:
            cp.wait()


def _attn_conv_mix(x, t, mod, nrm_ref, win_ref, wout_ref, bias_ref, sink_ref, convw_ref,
                   kvprev_ref, cprev_ref):
    tb = TB
    first = t == 0
    sh1, sc1, g1 = mod[0:1], mod[1:2], mod[2:3]
    u = _rms_mod(x, nrm_ref[0:1], sc1, sh1)
    proj = jnp.dot(u.astype(BF16), win_ref[...], preferred_element_type=F32)

    q = proj[:, 0:ATTN_WIDTH].astype(BF16)
    kf = proj[:, 512:640]
    vf = proj[:, 640:768]
    bgate = proj[:, 768:1280]
    cgate = proj[:, 1280:1792]
    xv = proj[:, 1792:2304]

    kv_prev = kvprev_ref[...]
    kext = jnp.concatenate([kv_prev[:, 0:KV_WIDTH], kf], axis=0)
    vext = jnp.concatenate([kv_prev[:, KV_WIDTH:], vf], axis=0)
    kvprev_ref[:, 0:KV_WIDTH] = kf[tb - BLOCK:tb]
    kvprev_ref[:, KV_WIDTH:] = vf[tb - BLOCK:tb]

    lane = lax.broadcasted_iota(I32, kext.shape, 1)
    lo = lane < HEAD_DIM
    krol = pltpu.roll(kext, HEAD_DIM, axis=1)
    vrol = pltpu.roll(vext, HEAD_DIM, axis=1)
    zero = jnp.zeros_like(kext)
    one_at_64 = jnp.where(lane == HEAD_DIM, 1.0, 0.0)
    one_at_0 = jnp.where(lane == 0, 1.0, 0.0)
    k_ops = [(jnp.where(lo, kext, zero).astype(BF16), jnp.where(lo, zero, krol).astype(BF16)),
             (jnp.where(lo, krol, zero).astype(BF16), jnp.where(lo, zero, kext).astype(BF16))]
    v_ops = [(jnp.where(lo, vext, one_at_64).astype(BF16), jnp.where(lo, one_at_0, vrol).astype(BF16)),
             (jnp.where(lo, vrol, one_at_64).astype(BF16), jnp.where(lo, one_at_0, vext).astype(BF16))]

    col = lax.broadcasted_iota(I32, (BLOCK, 4 * BLOCK), 1)
    prev_cols = (col % (2 * BLOCK)) < BLOCK
    lane_o = lax.broadcasted_iota(I32, (BLOCK, LANES), 1)
    lo_o = lane_o < HEAD_DIM

    attn_rows = []
    for bi in range(tb // BLOCK):
        r0 = bi * BLOCK
        pair_out = []
        for pr in range(N_HEADS // 2):
            kvh = pr // 2
            ka, kb = k_ops[kvh]
            va, vb = v_ops[kvh]
            rhs = jnp.concatenate([ka[r0:r0 + 2 * BLOCK], kb[r0:r0 + 2 * BLOCK]], axis=0)
            qp = q[r0:r0 + BLOCK, pr * LANES:(pr + 1) * LANES]
            s = lax.dot_general(qp, rhs, NT_DIMS, preferred_element_type=F32)
            bias = bias_ref[pr]
            if bi == 0:
                bias = jnp.where(prev_cols & first, NEG_BIG, bias)
            s = s + bias
            outs = []
            for hh in range(2):
                sh = s[:, hh * 2 * BLOCK:(hh + 1) * 2 * BLOCK]
                sink = sink_ref[2 * pr + hh]
                m = jnp.maximum(jnp.max(sh, axis=-1, keepdims=True), sink)
                p = jnp.exp(sh - m).astype(BF16)
                vop = (va, vb)[hh][r0:r0 + 2 * BLOCK]
                o = jnp.dot(p, vop, preferred_element_type=F32)
                den_col = HEAD_DIM if hh == 0 else 0
                den = o[:, den_col:den_col + 1] + jnp.exp(sink - m)
                outs.append(o / den)
            pair_out.append(jnp.where(lo_o, outs[0], outs[1]))
        attn_rows.append(jnp.concatenate(pair_out, axis=1))
    attn = jnp.concatenate(attn_rows, axis=0)

    uc = cgate * xv
    cprev = jnp.where(first, 0.0, cprev_ref[...])
    row = lax.broadcasted_iota(I32, uc.shape, 0)
    r1 = jnp.where(row == 0, cprev[7:8], pltpu.roll(uc, 1, axis=0))
    r2 = jnp.where(row == 0, cprev[6:7], jnp.where(row == 1, cprev[7:8], pltpu.roll(uc, 2, axis=0)))
    cprev_ref[...] = uc[tb - 8:tb]
    cw = convw_ref[...]
    conv = bgate * (cw[0:1] * r2 + cw[1:2] * r1 + cw[2:3] * uc)

    mix = (jnp.dot(attn.astype(BF16), wout_ref[0:ATTN_WIDTH, :], preferred_element_type=F32)
           + jnp.dot(conv.astype(BF16), wout_ref[ATTN_WIDTH:, :], preferred_element_type=F32))
    return x + g1 * mix


def _pool_mix(x, t, mod, nrm_ref, wpool_ref, pscale_ref, uprev_ref):
    tb = TB
    sh1, sc1, g1 = mod[0:1], mod[1:2], mod[2:3]
    u = _rms_mod(x, nrm_ref[0:1], sc1, sh1)
    ext = jnp.concatenate([jnp.where(t == 0, 0.0, uprev_ref[...]), u], axis=0)
    uprev_ref[...] = u[tb - POOL_HALO:tb]

    pos = (t * tb + 1 + lax.broadcasted_iota(I32, (tb, 1), 0)).astype(F32)
    mixed = []
    for gi, w in enumerate(POOL_SIZES):
        sl = slice(gi * POOL_GROUP, (gi + 1) * POOL_GROUP)
        acc = ext[:, sl]
        shift = 1
        while shift < w:
            acc = acc + pltpu.roll(acc, shift, axis=0)
            shift *= 2
        mean = acc[POOL_HALO:] / jnp.minimum(pos, float(w))
        pooled = mean - u[:, sl]
        mixed.append(jnp.dot(pooled.astype(BF16), wpool_ref[gi], preferred_element_type=F32))
    mix = jnp.concatenate(mixed, axis=1) * pscale_ref[0:1]
    return x + g1 * mix


def _mixer_kernel(*refs, kind, has_head, n_tiles, tiles_per_seq, cap):
    refs = list(refs)
    i = pl.program_id(0)
    t = i % tiles_per_seq
    if has_head:
        pm_ref = refs.pop(0)
        hprev_ref, modprev_ref, tokprev_ref, ys_hbm = refs[:4]
        refs = refs[4:]
    else:
        h_ref = refs.pop(0)
    mod_ref, nrm_ref = refs[:2]
    refs = refs[2:]
    if kind == "even":
        win_ref, wout_ref, bias_ref, sink_ref, convw_ref = refs[:5]
        refs = refs[5:]
    else:
        wpool_ref, pscale_ref = refs[:2]
        refs = refs[2:]
    wrt_ref, br_ref, tri_ref = refs[:3]
    refs = refs[3:]
    hout_ref, tokmeta_ref, tmeta_ref, xs_hbm = refs[:4]
    refs = refs[4:]
    if kind == "even":
        carry_refs = refs[:2]
        refs = refs[2:]
    else:
        carry_refs = refs[:1]
        refs = refs[1:]
    stage_ref, zeros_ref, run_ref, prev_ref, ssem, zsem = refs[:6]
    refs = refs[6:]

    @pl.when(i == 0)
    def _():
        for r in carry_refs:
            r[...] = jnp.zeros_like(r)
        for g in range(N_GROUPS):
            run_ref[g] = jnp.int32(0)

    if has_head:
        ystage_ref, gsem = refs
        x = _head(i, n_tiles, pm_ref, hprev_ref, modprev_ref, tokprev_ref, ys_hbm,
                  ystage_ref, gsem)
    else:
        x = h_ref[...]
    mod = mod_ref[0]
    if kind == "even":
        h1 = _attn_conv_mix(x, t, mod, nrm_ref, win_ref, wout_ref, bias_ref, sink_ref, convw_ref,
                            *carry_refs)
    else:
        h1 = _pool_mix(x, t, mod, nrm_ref, wpool_ref, pscale_ref, *carry_refs)
    hout_ref[...] = h1
    _tail(i, n_tiles, cap, h1, mod, nrm_ref, wrt_ref, br_ref, tri_ref, tokmeta_ref, tmeta_ref,
          xs_hbm, stage_ref, zeros_ref, run_ref, prev_ref, ssem, zsem)


def _nat(cols):
    return pl.BlockSpec((TB, cols), lambda i, *_: (i, 0))


def _full(shape):
    nd = len(shape)
    return pl.BlockSpec(shape, lambda i, *_: (0,) * nd)


def _mod_spec(tiles_per_seq):
    return pl.BlockSpec((1, 8, D_MODEL), lambda i, *_: (i // tiles_per_seq, 0, 0))


def _head_scratch():
    return [pltpu.VMEM((2, STG, D_MODEL), BF16), pltpu.SemaphoreType.DMA((2,))]


def _mixer_call(kind, head, h, mod8, nrm8, weights, wrt, br, tri, tiles_per_seq):
    n_tok = (h if head is None else head[1]).shape[0]
    n_tiles, cap, _ = _sizes(n_tok)
    has_head = head is not None
    any_spec = pl.BlockSpec(memory_space=pl.ANY)
    smem_spec = pl.BlockSpec(memory_space=pltpu.SMEM)

    if has_head:
        args = list(head[1:])
        in_specs = [_nat(D_MODEL), _mod_spec(tiles_per_seq), _nat(LANES), any_spec]
    else:
        args = [h]
        in_specs = [_nat(D_MODEL)]
    args += [mod8, nrm8]
    in_specs += [_mod_spec(tiles_per_seq), _full((8, D_MODEL))]
    if kind == "even":
        win, wout, bias_tab, sinks, convw8 = weights
        args += [win, wout, bias_tab, sinks, convw8]
        in_specs += [_full((D_MODEL, IN_WIDTH)), _full((D_MODEL, D_MODEL)),
                     _full((N_HEADS // 2, BLOCK, 4 * BLOCK)), smem_spec, _full((8, CONV_WIDTH))]
        mix_scratch = [pltpu.VMEM((BLOCK, 2 * KV_WIDTH), F32), pltpu.VMEM((8, CONV_WIDTH), F32)]
    else:
        wpool, pscale8 = weights
        args += [wpool, pscale8]
        in_specs += [_full((len(POOL_SIZES), POOL_GROUP, POOL_GROUP)), _full((8, D_MODEL))]
        mix_scratch = [pltpu.VMEM((POOL_HALO, D_MODEL), F32)]
    args += [wrt, br, tri]
    in_specs += [_full((2 * N_EXPERTS, D_MODEL)), _full((N_EXPERTS, 1)), _full((TB, TB))]

    out_shape = (jax.ShapeDtypeStruct((n_tok, D_MODEL), F32),
                 jax.ShapeDtypeStruct((n_tok, LANES), F32),
                 jax.ShapeDtypeStruct(((n_tiles + 1) * META_W,), I32),
                 jax.ShapeDtypeStruct((N_GROUPS * cap + STG, ROW_W), BF16))
    out_specs = (_nat(D_MODEL), _nat(LANES), smem_spec, any_spec)
    scratch = mix_scratch + [
        pltpu.VMEM((2, STG, ROW_W), BF16),
        pltpu.VMEM((TAIL, ROW_W), BF16),
        pltpu.SMEM((N_GROUPS,), I32),
        pltpu.SMEM((N_SUB,), I32),
        pltpu.SemaphoreType.DMA((2,)),
        pltpu.SemaphoreType.DMA(()),
    ]
    if has_head:
        scratch += _head_scratch()

    body = functools.partial(_mixer_kernel, kind=kind, has_head=has_head, n_tiles=n_tiles,
                             tiles_per_seq=tiles_per_seq, cap=cap)
    grid_spec = pltpu.PrefetchScalarGridSpec(
        num_scalar_prefetch=1 if has_head else 0, grid=(n_tiles,),
        in_specs=in_specs, out_specs=out_specs, scratch_shapes=scratch)
    call = pl.pallas_call(
        body, grid_spec=grid_spec, out_shape=out_shape,
        compiler_params=pltpu.CompilerParams(
            dimension_semantics=("arbitrary",), vmem_limit_bytes=VMEM_LIMIT),
        name=kind + "_mixer")
    if has_head:
        return call(head[0], *args)
    return call(*args)


def _moe_kernel(blk_ref, grp_ref, val_ref, xs_ref, wg_ref, wu_ref, wd_ref, ys_ref):
    i = pl.program_id(0)

    @pl.when(val_ref[i] == 1)
    def _():
        xb = xs_ref[:, 0:D_MODEL]
        meta = xs_ref[:, D_MODEL:ROW_W].astype(F32)
        gates = meta[:, 0:EXPERTS_PER_GROUP] + meta[:, EXPERTS_PER_GROUP:2 * EXPERTS_PER_GROUP]
        parts = []
        for e in range(EXPERTS_PER_GROUP):
            a = jnp.dot(xb, wg_ref[e], preferred_element_type=F32)
            bu = jnp.dot(xb, wu_ref[e], preferred_element_type=F32)
            hid = (a * _sigmoid(a)) * bu
            parts.append((hid * gates[:, e:e + 1]).astype(BF16))
        hid_all = jnp.concatenate(parts, axis=1)
        ys_ref[...] = jnp.dot(hid_all, wd_ref[0], preferred_element_type=F32).astype(BF16)


def _work_tables(totals, cap, n_work):
    ntile = (totals + SUB + (TBM - 1)) // TBM
    ends = jnp.cumsum(ntile)
    starts = ends - ntile
    nvalid = ends[-1]
    idx = jnp.arange(n_work, dtype=I32)
    idc = jnp.maximum(jnp.minimum(idx, nvalid - 1), 0)
    grp = jnp.minimum(jnp.sum((idc[:, None] >= ends[None, :]).astype(I32), axis=1), N_GROUPS - 1)
    blk = grp * (cap // TBM) + idc - starts[grp]
    return blk.astype(I32), grp.astype(I32), (idx < nvalid).astype(I32)


def _moe_call(xs, blk, grp, val, wg, wu, wd, n_work):
    grid_spec = pltpu.PrefetchScalarGridSpec(
        num_scalar_prefetch=3, grid=(n_work,),
        in_specs=[
            pl.BlockSpec((TBM, ROW_W), lambda i, blk, grp, val: (blk[i], 0)),
            pl.BlockSpec((EXPERTS_PER_GROUP, D_MODEL, EXPERT_FF), lambda i, blk, grp, val: (grp[i], 0, 0)),
            pl.BlockSpec((EXPERTS_PER_GROUP, D_MODEL, EXPERT_FF), lambda i, blk, grp, val: (grp[i], 0, 0)),
            pl.BlockSpec((1, GROUP_FF, D_MODEL), lambda i, blk, grp, val: (grp[i], 0, 0)),
        ],
        out_specs=pl.BlockSpec((TBM, D_MODEL), lambda i, blk, grp, val: (blk[i], 0)),
    )
    return pl.pallas_call(
        _moe_kernel, grid_spec=grid_spec,
        out_shape=jax.ShapeDtypeStruct((xs.shape[0], D_MODEL), BF16),
        compiler_params=pltpu.CompilerParams(
            dimension_semantics=("arbitrary",), vmem_limit_bytes=VMEM_LIMIT),
        name="moe_experts",
    )(blk, grp, val, xs, wg, wu, wd)


def _final_kernel(pm_ref, hprev_ref, modprev_ref, tokprev_ref, ys_hbm, nf_ref, o_ref,
                  ystage_ref, gsem, *, n_tiles):
    i = pl.program_id(0)
    h = _head(i, n_tiles, pm_ref, hprev_ref, modprev_ref, tokprev_ref, ys_hbm, ystage_ref, gsem)
    ms = jnp.mean(h * h, axis=-1, keepdims=True)
    o_ref[...] = (h * lax.rsqrt(ms + EPS)) * nf_ref[0:1]


def _final_call(head, nf8, tiles_per_seq):
    pm, hprev, modprev, tokprev, ys = head
    n_tok = hprev.shape[0]
    n_tiles, _, _ = _sizes(n_tok)
    grid_spec = pltpu.PrefetchScalarGridSpec(
        num_scalar_prefetch=1, grid=(n_tiles,),
        in_specs=[_nat(D_MODEL), _mod_spec(tiles_per_seq), _nat(LANES),
                  pl.BlockSpec(memory_space=pl.ANY), _full((8, D_MODEL))],
        out_specs=_nat(D_MODEL),
        scratch_shapes=_head_scratch())
    return pl.pallas_call(
        functools.partial(_final_kernel, n_tiles=n_tiles), grid_spec=grid_spec,
        out_shape=jax.ShapeDtypeStruct((n_tok, D_MODEL), F32),
        compiler_params=pltpu.CompilerParams(
            dimension_semantics=("arbitrary",), vmem_limit_bytes=VMEM_LIMIT),
        name="final_norm",
    )(pm, hprev, modprev, tokprev, ys, nf8)


def _pad_rows(a, rows=8):
    return jnp.pad(a, ((0, rows - a.shape[0]), (0, 0)))


def _attn_bias_table():
    slopes = 2.0 ** (-8.0 * (jnp.arange(N_HEADS, dtype=F32) + 1.0) / N_HEADS)
    dist = (jnp.arange(BLOCK)[:, None] + BLOCK) - jnp.arange(2 * BLOCK)[None, :]
    ok = (dist >= 0) & (dist < WINDOW)
    per_head = jnp.where(ok[None], -slopes[:, None, None] * dist.astype(F32)[None], NEG_BIG)
    return per_head.reshape(N_HEADS // 2, 2, BLOCK, 2 * BLOCK).transpose(0, 2, 1, 3).reshape(
        N_HEADS // 2, BLOCK, 4 * BLOCK)


def _router_operands(w_router, b_router):
    def reorder(a):
        return a.reshape(N_GROUPS, EXPERTS_PER_GROUP, -1).transpose(1, 0, 2).reshape(N_EXPERTS, -1)
    w = reorder(w_router.T)
    wh = w.astype(BF16)
    wl = (w - wh.astype(F32)).astype(BF16)
    return jnp.concatenate([wh, wl], axis=0), reorder(b_router[:, None])


def kernel(x, c, w_ada, b_ada, norm_mix, norm_ffn, w_in, w_out, sinks, conv_w, w_pool,
           pool_scale, w_router, b_router, w_gate, w_up, w_down, norm_final):
    b, s, _ = x.shape
    n_tok = b * s
    tiles_per_seq = s // TB
    n_tiles, cap, n_work = _sizes(n_tok)

    mod_all = _mod_call(_pad_rows(c), w_ada, b_ada.reshape(DEPTH, 1, N_MOD * D_MODEL))
    bias_tab = _attn_bias_table()
    wrt, br = _router_operands(w_router, b_router)
    nf8 = _pad_rows(norm_final[None, :])
    tri = jnp.triu(jnp.ones((TB, TB), F32)).astype(BF16)
    qscale = jnp.concatenate([jnp.full((ATTN_WIDTH,), HEAD_DIM ** -0.5, F32),
                              jnp.ones((IN_WIDTH - ATTN_WIDTH,), F32)])

    h = x.reshape(n_tok, D_MODEL)
    head = None
    for layer in range(DEPTH):
        j = layer // 2
        mod8 = jnp.pad(mod_all[layer, :b].reshape(b, N_MOD, D_MODEL), ((0, 0), (0, 2), (0, 0)))
        nrm8 = _pad_rows(jnp.stack([norm_mix[layer], norm_ffn[layer]]))
        if layer % 2 == 0:
            kind = "even"
            weights = ((w_in[j] * qscale).astype(BF16), w_out[j].astype(BF16), bias_tab,
                       sinks[j], _pad_rows(conv_w[j]))
        else:
            kind = "odd"
            weights = (w_pool[j].astype(BF16), _pad_rows(pool_scale[j][None, :]))
        h1, tokmeta, tmeta, xs = _mixer_call(kind, head, h, mod8, nrm8, weights, wrt, br, tri,
                                             tiles_per_seq)
        totals = tmeta[n_tiles * META_W:n_tiles * META_W + N_GROUPS]
        blk, grp, val = _work_tables(totals, cap, n_work)
        ys = _moe_call(xs, blk, grp, val, w_gate[layer].astype(BF16), w_up[layer].astype(BF16),
                       w_down[layer].astype(BF16).reshape(N_GROUPS, GROUP_FF, D_MODEL), n_work)
        head = (tmeta, h1, mod8, tokmeta, ys)
    out = _final_call(head, nf8, tiles_per_seq)
    return out.reshape(b, s, D_MODEL)
```

```python
import functools

import jax
import jax.numpy as jnp
from jax import lax
from jax.experimental import pallas as pl
from jax.experimental.pallas import tpu as pltpu

F32 = jnp.float32
BF16 = jnp.bfloat16
I32 = jnp.int32

D_MODEL = 1024
DEPTH = 4
EPS = 1e-6
N_MOD = 6

ATTN_WIDTH = 512
HEAD_DIM = 64
N_HEADS = 8
KV_WIDTH = 128
WINDOW = 128
BLOCK = 128
CONV_WIDTH = 512
IN_WIDTH = 2304

POOL_SIZES = (2, 4, 8, 16)
POOL_GROUP = 256
POOL_HALO = 16

N_EXPERTS = 16
N_GROUPS = 4
EXPERTS_PER_GROUP = 4
EXPERT_FF = 256
GROUP_FF = EXPERTS_PER_GROUP * EXPERT_FF

LANES = 128
NEG_BIG = -1e30

TB = 256
SUB = 16
N_SUB = TB // SUB + N_GROUPS - 1
STG_P = 320
STG_C = 384
TBM = 512
ROW_W = D_MODEL + LANES
META_W = 32
LP_LANE = 8
CHAIN_LAG = 0
MOD_COLS = 1536
VMEM_LIMIT = 56 * 1024 * 1024
NT_DIMS = (((1,), (1,)), ((), ()))

assert N_SUB * SUB <= STG_P <= STG_C and STG_C % LANES == 0 and N_SUB <= META_W


def _sizes(n_tok):
    n_tiles = n_tok // TB
    cap = -(-(n_tok + n_tiles * (SUB - 1) + TBM) // TBM) * TBM
    n_work = (n_tok + n_tiles * N_GROUPS * (SUB - 1)) // TBM + N_GROUPS
    return n_tiles, cap, n_work


def _sigmoid(x):
    return 1.0 / (1.0 + jnp.exp(-x))


def _rms_mod(x, g, sc, sh):
    ms = jnp.mean(x * x, axis=-1, keepdims=True)
    return (x * lax.rsqrt(ms + EPS)) * (g * (1.0 + sc)) + sh


def _onehot(cond):
    return jnp.where(cond, 1.0, 0.0).astype(BF16)


def _mod_kernel(c_ref, w_ref, b_ref, o_ref):
    c = c_ref[...]
    cond = c * _sigmoid(c)
    o_ref[0] = jnp.dot(cond, w_ref[0], precision=lax.Precision.HIGHEST,
                       preferred_element_type=F32) + b_ref[0]


def _mod_call(c8, w_ada, b_ada3):
    n_col = (N_MOD * D_MODEL) // MOD_COLS
    return pl.pallas_call(
        _mod_kernel,
        grid=(DEPTH, n_col),
        in_specs=[
            pl.BlockSpec((8, D_MODEL), lambda l, j: (0, 0)),
            pl.BlockSpec((1, D_MODEL, MOD_COLS), lambda l, j: (l, 0, j)),
            pl.BlockSpec((1, 1, MOD_COLS), lambda l, j: (l, 0, j)),
        ],
        out_specs=pl.BlockSpec((1, 8, MOD_COLS), lambda l, j: (l, 0, j)),
        out_shape=jax.ShapeDtypeStruct((DEPTH, 8, N_MOD * D_MODEL), F32),
        compiler_params=pltpu.CompilerParams(
            dimension_semantics=("arbitrary", "arbitrary"), vmem_limit_bytes=VMEM_LIMIT),
        name="adaln_mod",
    )(c8, w_ada, b_ada3)


def _sub_copies(hbm_ref, stage_ref, sem, rows, to_hbm):
    out = []
    for s in range(N_SUB):
        s_view = stage_ref.at[pl.ds(s * SUB, SUB), :]
        h_view = hbm_ref.at[pl.ds(pl.multiple_of(rows[s], SUB), SUB), :]
        out.append(pltpu.make_async_copy(s_view, h_view, sem) if to_hbm
                   else pltpu.make_async_copy(h_view, s_view, sem))
    return out


def _head_dma(t, n_steps, nb, pm_ref, ys_hbm, ystage_ref, gsem):
    slot = t % 2

    def copies(step, b, slot_):
        base = (step * nb + b) * META_W
        rows = [pm_ref[base + s] for s in range(N_SUB)]
        return _sub_copies(ys_hbm, ystage_ref.at[b, slot_], gsem.at[b, slot_], rows, False)

    @pl.when(t == 0)
    def _():
        ystage_ref[...] = jnp.zeros_like(ystage_ref)
        for b in range(nb):
            for cp in copies(0, b, 0):
                cp.start()

    @pl.when(t + 1 < n_steps)
    def _():
        for b in range(nb):
            for cp in copies(t + 1, b, 1 - slot):
                cp.start()

    for b in range(nb):
        for cp in copies(t, b, slot):
            cp.wait()


def _head_combine(b, slot, hprev_ref, modprev_ref, tokmeta_ref, ystage_ref):
    lp = tokmeta_ref[b, :, LP_LANE:LP_LANE + 1].astype(I32)
    unsort = _onehot(lax.broadcasted_iota(I32, (TB, STG_C), 1) == lp)
    y = jnp.dot(unsort, ystage_ref[b, slot], preferred_element_type=F32)
    return hprev_ref[b] + modprev_ref[b, 5:6] * y


def _route(u2, wrt_ref, br_ref):
    uh = u2.astype(BF16)
    ul = (u2 - uh.astype(F32)).astype(BF16)
    w = wrt_ref[...]
    r1 = lax.dot_general(w, uh, NT_DIMS, preferred_element_type=F32)
    r2 = lax.dot_general(w[0:N_EXPERTS], ul, NT_DIMS, preferred_element_type=F32)
    yield
    scores = _sigmoid(r1[0:N_EXPERTS] + r1[N_EXPERTS:] + r2)
    biased = scores + br_ref[...]
    s = [scores[4 * j:4 * j + 4] for j in range(EXPERTS_PER_GROUP)]
    b = [biased[4 * j:4 * j + 4] for j in range(EXPERTS_PER_GROUP)]

    hi1, lo1 = jnp.maximum(b[0], b[1]), jnp.minimum(b[0], b[1])
    hi2, lo2 = jnp.maximum(b[2], b[3]), jnp.minimum(b[2], b[3])
    second = jnp.maximum(jnp.minimum(hi1, hi2), jnp.where(hi1 >= hi2, lo1, lo2))
    gscore = jnp.maximum(hi1, hi2) + second

    best = gscore[0:1]
    bgrp = jnp.zeros((1, TB), I32)
    for g in range(1, N_GROUPS):
        better = gscore[g:g + 1] > best
        bgrp = jnp.where(better, g, bgrp)
        best = jnp.where(better, gscore[g:g + 1], best)
    selmask = lax.broadcasted_iota(I32, (N_GROUPS, TB), 0) == bgrp
    yield

    m1 = b[0]
    i1 = jnp.zeros((N_GROUPS, TB), I32)
    for j in range(1, EXPERTS_PER_GROUP):
        gt = b[j] > m1
        i1 = jnp.where(gt, j, i1)
        m1 = jnp.where(gt, b[j], m1)
    cands = [jnp.where(i1 == j, -jnp.inf, b[j]) for j in range(EXPERTS_PER_GROUP)]
    m2 = cands[0]
    i2 = jnp.zeros_like(i1)
    for j in range(1, EXPERTS_PER_GROUP):
        gt = cands[j] > m2
        i2 = jnp.where(gt, j, i2)
        m2 = jnp.where(gt, cands[j], m2)
    w1 = s[0]
    w2 = s[0]
    for j in range(1, EXPERTS_PER_GROUP):
        w1 = jnp.where(i1 == j, s[j], w1)
        w2 = jnp.where(i2 == j, s[j], w2)
    yield
    tot = w1 + w2
    w1n = w1 / tot
    w2n = w2 / tot
    gates = []
    for j in range(EXPERTS_PER_GROUP):
        gj = jnp.where(i1 == j, w1n, 0.0) + jnp.where(i2 == j, w2n, 0.0)
        gates.append(jnp.sum(jnp.where(selmask, gj, 0.0), axis=0, keepdims=True))
    return bgrp, gates, selmask


def _tail_compute(b, slot, h1, mod, nrm_ref, wrt_ref, br_ref, tri_ref, tokmeta_ref, stage_ref):
    sh2, sc2 = mod[3:4], mod[4:5]
    u2 = _rms_mod(h1, nrm_ref[1:2], sc2, sh2)
    bgrp, gates, selmask = yield from _route(u2, wrt_ref, br_ref)
    yield

    onehot = jnp.concatenate([jnp.where(selmask, 1.0, 0.0), jnp.zeros((8 - N_GROUPS, TB), F32)],
                             axis=0).astype(BF16)
    rank_incl = jnp.dot(onehot, tri_ref[...], preferred_element_type=F32)
    rank = jnp.sum(jnp.where(selmask, rank_incl[0:N_GROUPS], 0.0), axis=0, keepdims=True) - 1.0
    cnt = lax.dot_general(jnp.ones((8, TB), BF16), onehot, NT_DIMS,
                          preferred_element_type=F32)
    sub_rows = jnp.floor((cnt[0:1, :] + (SUB - 1.0)) * (1.0 / SUB)) * SUB
    lp = rank
    first_row = jnp.zeros((1, 1), F32)
    for g in range(1, N_GROUPS):
        first_row = first_row + sub_rows[:, g - 1:g]
        lp = lp + jnp.where(bgrp == g, first_row, 0.0)

    hi = [x.astype(BF16).astype(F32) for x in gates]
    lo = [(x - h).astype(BF16).astype(F32) for x, h in zip(gates, hi)]
    meta_src = jnp.concatenate(hi + lo + [lp, jnp.zeros((LANES - 9, TB), F32)], axis=0)
    meta_t = meta_src.T
    tokmeta_ref[b] = meta_t
    yield

    rowdata = jnp.concatenate([u2.astype(BF16), meta_t.astype(BF16)], axis=1)
    sort = _onehot(lax.broadcasted_iota(I32, (STG_P, TB), 0) == lp.astype(I32))
    stage_ref[b, slot] = jnp.dot(sort, rowdata, preferred_element_type=F32).astype(BF16)
    return [cnt[0, g].astype(I32) for g in range(N_GROUPS)]


def _tail_dma(t, n_steps, nb, cap, counts, tmeta_ref, xs_hbm, stage_ref, zeros_ref,
              run_ref, prev_ref, ssem, zsem):
    slot = t % 2
    run = [run_ref[g] for g in range(N_GROUPS)]
    junk = N_GROUPS * cap
    scatter_rows = []
    for b in range(nb):
        first_sub = [jnp.int32(0)]
        for g in range(N_GROUPS):
            first_sub.append(first_sub[-1] + (counts[b][g] + (SUB - 1)) // SUB)
        dst = [g * cap + run[g] for g in range(N_GROUPS)]
        rows_b, base = [], (t * nb + b) * META_W
        for s in range(N_SUB):
            grp_first, grp_dst = first_sub[0], dst[0]
            for g in range(1, N_GROUPS):
                later = s >= first_sub[g]
                grp_first = jnp.where(later, first_sub[g], grp_first)
                grp_dst = jnp.where(later, dst[g], grp_dst)
            row = grp_dst + (s - grp_first) * SUB
            used = s < first_sub[N_GROUPS]
            rows_b.append(jnp.where(used, row, junk + (b * N_SUB + s) * SUB))
            tmeta_ref[base + s] = jnp.where(used, row, row if s == 0 else first_row)
            if s == 0:
                first_row = row
        for k in range(N_SUB, META_W):
            tmeta_ref[base + k] = jnp.int32(0)
        for g in range(N_GROUPS):
            run[g] = run[g] + (first_sub[g + 1] - first_sub[g]) * SUB
        scatter_rows.append(rows_b)
    for g in range(N_GROUPS):
        run_ref[g] = run[g]

    def copies(b, slot_, rows):
        return _sub_copies(xs_hbm, stage_ref.at[b, slot_], ssem.at[b, slot_], rows, True)

    @pl.when(t > 0)
    def _():
        for b in range(nb):
            for cp in copies(b, 1 - slot, [prev_ref[b * N_SUB + s] for s in range(N_SUB)]):
                cp.wait()

    for b in range(nb):
        for cp in copies(b, slot, scatter_rows[b]):
            cp.start()
        for s in range(N_SUB):
            prev_ref[b * N_SUB + s] = scatter_rows[b][s]

    @pl.when(t == n_steps - 1)
    def _():
        for b in range(nb):
            for cp in copies(b, slot, scatter_rows[b]):
                cp.wait()
        zeros_ref[...] = jnp.zeros_like(zeros_ref)
        tails = []
        base = n_steps * nb * META_W
        for g in range(N_GROUPS):
            tmeta_ref[base + g] = run[g]
            start = pl.multiple_of(g * cap + run[g], SUB)
            tails.append(pltpu.make_async_copy(zeros_ref, xs_hbm.at[pl.ds(start, TBM), :], zsem))
        for k in range(N_GROUPS, META_W):
            tmeta_ref[base + k] = jnp.int32(0)
        for cp in tails:
            cp.start()
        for cp in tails:
            cp.wait()


def _attn_conv_mix(x, t, mod, nrm_ref, win_ref, wout_ref, bias_ref, sink_ref, convw_ref,
                   kvprev_ref, cprev_ref):
    tb = TB
    first = t == 0
    sh1, sc1, g1 = mod[0:1], mod[1:2], mod[2:3]
    u = _rms_mod(x, nrm_ref[0:1], sc1, sh1)
    proj = jnp.dot(u.astype(BF16), win_ref[...], preferred_element_type=F32)

    yield
    q = proj[:, 0:ATTN_WIDTH].astype(BF16)
    kf = proj[:, 512:640]
    vf = proj[:, 640:768]
    bgate = proj[:, 768:1280]
    cgate = proj[:, 1280:1792]
    xv = proj[:, 1792:2304]

    kv_prev = kvprev_ref[...]
    kext = jnp.concatenate([kv_prev[:, 0:KV_WIDTH], kf], axis=0)
    vext = jnp.concatenate([kv_prev[:, KV_WIDTH:], vf], axis=0)
    kvprev_ref[:, 0:KV_WIDTH] = kf[tb - BLOCK:tb]
    kvprev_ref[:, KV_WIDTH:] = vf[tb - BLOCK:tb]

    lane = lax.broadcasted_iota(I32, kext.shape, 1)
    lo = lane < HEAD_DIM
    krol = pltpu.roll(kext, HEAD_DIM, axis=1)
    vrol = pltpu.roll(vext, HEAD_DIM, axis=1)
    zero = jnp.zeros_like(kext)
    one_at_64 = jnp.where(lane == HEAD_DIM, 1.0, 0.0)
    one_at_0 = jnp.where(lane == 0, 1.0, 0.0)
    k_ops = [(jnp.where(lo, kext, zero).astype(BF16), jnp.where(lo, zero, krol).astype(BF16)),
             (jnp.where(lo, krol, zero).astype(BF16), jnp.where(lo, zero, kext).astype(BF16))]
    v_ops = [(jnp.where(lo, vext, one_at_64).astype(BF16), jnp.where(lo, one_at_0, vrol).astype(BF16)),
             (jnp.where(lo, vrol, one_at_64).astype(BF16), jnp.where(lo, one_at_0, vext).astype(BF16))]

    col = lax.broadcasted_iota(I32, (2 * BLOCK, 4 * BLOCK), 1)
    prev_cols = (col % (2 * BLOCK)) < BLOCK
    upper = lax.broadcasted_iota(I32, (2 * BLOCK, 1), 0) < BLOCK
    lane_o = lax.broadcasted_iota(I32, (2 * BLOCK, LANES), 1)
    lo_o = lane_o < HEAD_DIM

    yield
    attn_rows = []
    for bi in range(tb // BLOCK):
        r0 = bi * BLOCK
        pair_out = []
        for kvh in range(N_HEADS // 4):
            ka, kb = k_ops[kvh]
            rhs = jnp.concatenate([ka[r0:r0 + 2 * BLOCK], kb[r0:r0 + 2 * BLOCK]], axis=0)
            qrows = q[r0:r0 + BLOCK]
            qp = jnp.concatenate([qrows[:, (2 * kvh) * LANES:(2 * kvh + 1) * LANES],
                                  qrows[:, (2 * kvh + 1) * LANES:(2 * kvh + 2) * LANES]], axis=0)
            s = lax.dot_general(qp, rhs, NT_DIMS, preferred_element_type=F32)
            bias = bias_ref[kvh]
            if bi == 0:
                bias = jnp.where(prev_cols & first, NEG_BIG, bias)
            s = s + bias
            outs = []
            for hh in range(2):
                sh = s[:, hh * 2 * BLOCK:(hh + 1) * 2 * BLOCK]
                sink = jnp.where(upper, sink_ref[4 * kvh + hh], sink_ref[4 * kvh + 2 + hh])
                m = jnp.maximum(jnp.max(sh, axis=-1, keepdims=True), sink)
                p = jnp.exp(sh - m).astype(BF16)
                vop = v_ops[kvh][hh][r0:r0 + 2 * BLOCK]
                o = jnp.dot(p, vop, preferred_element_type=F32)
                den_col = HEAD_DIM if hh == 0 else 0
                den = o[:, den_col:den_col + 1] + jnp.exp(sink - m)
                outs.append(o / den)
            both = jnp.where(lo_o, outs[0], outs[1])
            pair_out += [both[0:BLOCK], both[BLOCK:]]
            yield
        attn_rows.append(jnp.concatenate(pair_out, axis=1))
    attn = jnp.concatenate(attn_rows, axis=0)

    uc = cgate * xv
    cprev = jnp.where(first, 0.0, cprev_ref[...])
    row = lax.broadcasted_iota(I32, uc.shape, 0)
    r1 = jnp.where(row == 0, cprev[7:8], pltpu.roll(uc, 1, axis=0))
    r2 = jnp.where(row == 0, cprev[6:7], jnp.where(row == 1, cprev[7:8], pltpu.roll(uc, 2, axis=0)))
    cprev_ref[...] = uc[tb - 8:tb]
    cw = convw_ref[...]
    conv = bgate * (cw[0:1] * r2 + cw[1:2] * r1 + cw[2:3] * uc)
    yield

    mix = (jnp.dot(attn.astype(BF16), wout_ref[0:ATTN_WIDTH, :], preferred_element_type=F32)
           + jnp.dot(conv.astype(BF16), wout_ref[ATTN_WIDTH:, :], preferred_element_type=F32))
    return x + g1 * mix


def _pool_mix(x, t, mod, nrm_ref, wpool_ref, pscale_ref, uprev_ref):
    tb = TB
    sh1, sc1, g1 = mod[0:1], mod[1:2], mod[2:3]
    u = _rms_mod(x, nrm_ref[0:1], sc1, sh1)
    ext = jnp.concatenate([jnp.where(t == 0, 0.0, uprev_ref[...]), u], axis=0)
    uprev_ref[...] = u[tb - POOL_HALO:tb]

    yield
    pos = (t * tb + 1 + lax.broadcasted_iota(I32, (tb, 1), 0)).astype(F32)
    mixed = []
    for gi, w in enumerate(POOL_SIZES):
        sl = slice(gi * POOL_GROUP, (gi + 1) * POOL_GROUP)
        acc = ext[:, sl]
        shift = 1
        while shift < w:
            acc = acc + pltpu.roll(acc, shift, axis=0)
            shift *= 2
        mean = acc[POOL_HALO:] / jnp.minimum(pos, float(w))
        pooled = mean - u[:, sl]
        mixed.append(jnp.dot(pooled.astype(BF16), wpool_ref[gi], preferred_element_type=F32))
        yield
    mix = jnp.concatenate(mixed, axis=1) * pscale_ref[0:1]
    return x + g1 * mix


def _interleave(chains, lag):
    results = [None] * len(chains)
    live = list(range(len(chains)))
    rnd = 0
    while live:
        for k in list(live):
            if rnd < k * lag:
                continue
            try:
                next(chains[k])
            except StopIteration as done:
                results[k] = done.value
                live.remove(k)
        rnd += 1
    return results


def _mixer_kernel(*refs, kind, has_head, nb, n_steps, cap):
    refs = list(refs)
    t = pl.program_id(0)
    slot = t % 2
    if has_head:
        pm_ref = refs.pop(0)
        hprev_ref, modprev_ref, tokprev_ref, ys_hbm = refs[:4]
        refs = refs[4:]
    else:
        h_ref = refs.pop(0)
    mod_ref, nrm_ref = refs[:2]
    refs = refs[2:]
    n_w = 5 if kind == "even" else 2
    weight_refs = refs[:n_w]
    refs = refs[n_w:]
    wrt_ref, br_ref, tri_ref = refs[:3]
    refs = refs[3:]
    hout_ref, tokmeta_ref, tmeta_ref, xs_hbm = refs[:4]
    refs = refs[4:]
    n_c = 2 if kind == "even" else 1
    carry_refs = refs[:n_c]
    refs = refs[n_c:]
    stage_ref, zeros_ref, run_ref, prev_ref, ssem, zsem = refs[:6]
    refs = refs[6:]

    @pl.when(t == 0)
    def _():
        for r in carry_refs:
            r[...] = jnp.zeros_like(r)
        for g in range(N_GROUPS):
            run_ref[g] = jnp.int32(0)

    if has_head:
        ystage_ref, gsem = refs
        _head_dma(t, n_steps, nb, pm_ref, ys_hbm, ystage_ref, gsem)

    def chain(b):
        if has_head:
            x = _head_combine(b, slot, hprev_ref, modprev_ref, tokprev_ref, ystage_ref)
            yield
        else:
            x = h_ref[b]
        mod = mod_ref[b]
        carries = [r.at[b] for r in carry_refs]
        mix = _attn_conv_mix if kind == "even" else _pool_mix
        h1 = yield from mix(x, t, mod, nrm_ref, *weight_refs, *carries)
        hout_ref[b] = h1
        yield
        return (yield from _tail_compute(b, slot, h1, mod, nrm_ref, wrt_ref, br_ref, tri_ref,
                                         tokmeta_ref, stage_ref))

    counts = _interleave([chain(b) for b in range(nb)], CHAIN_LAG)
    _tail_dma(t, n_steps, nb, cap, counts, tmeta_ref, xs_hbm, stage_ref, zeros_ref,
              run_ref, prev_ref, ssem, zsem)


def _nat(nb, cols):
    return pl.BlockSpec((nb, TB, cols), lambda t, *_: (0, t, 0))


def _full(shape):
    nd = len(shape)
    return pl.BlockSpec(shape, lambda t, *_: (0,) * nd)


def _head_scratch(nb):
    return [pltpu.VMEM((nb, 2, STG_C, D_MODEL), BF16), pltpu.SemaphoreType.DMA((nb, 2))]


def _mixer_call(kind, head, h, mod8, nrm8, weights, wrt, br, tri):
    nb, seq, _ = (h if head is None else head[1]).shape
    n_steps = seq // TB
    n_tiles, cap, _ = _sizes(nb * seq)
    has_head = head is not None
    any_spec = pl.BlockSpec(memory_space=pl.ANY)
    smem_spec = pl.BlockSpec(memory_space=pltpu.SMEM)

    if has_head:
        args = list(head[1:])
        in_specs = [_nat(nb, D_MODEL), _full((nb, 8, D_MODEL)), _nat(nb, LANES), any_spec]
    else:
        args = [h]
        in_specs = [_nat(nb, D_MODEL)]
    args += [mod8, nrm8]
    in_specs += [_full((nb, 8, D_MODEL)), _full((8, D_MODEL))]
    if kind == "even":
        args += list(weights)
        in_specs += [_full((D_MODEL, IN_WIDTH)), _full((D_MODEL, D_MODEL)),
                     _full((N_HEADS // 4, 2 * BLOCK, 4 * BLOCK)), smem_spec, _full((8, CONV_WIDTH))]
        mix_scratch = [pltpu.VMEM((nb, BLOCK, 2 * KV_WIDTH), F32),
                       pltpu.VMEM((nb, 8, CONV_WIDTH), F32)]
    else:
        args += list(weights)
        in_specs += [_full((len(POOL_SIZES), POOL_GROUP, POOL_GROUP)), _full((8, D_MODEL))]
        mix_scratch = [pltpu.VMEM((nb, POOL_HALO, D_MODEL), F32)]
    args += [wrt, br, tri]
    in_specs += [_full((2 * N_EXPERTS, D_MODEL)), _full((N_EXPERTS, 1)), _full((TB, TB))]

    out_shape = (jax.ShapeDtypeStruct((nb, seq, D_MODEL), F32),
                 jax.ShapeDtypeStruct((nb, seq, LANES), F32),
                 jax.ShapeDtypeStruct(((n_tiles + 1) * META_W,), I32),
                 jax.ShapeDtypeStruct((N_GROUPS * cap + nb * N_SUB * SUB, ROW_W), BF16))
    out_specs = (_nat(nb, D_MODEL), _nat(nb, LANES), smem_spec, any_spec)
    scratch = mix_scratch + [
        pltpu.VMEM((nb, 2, STG_P, ROW_W), BF16),
        pltpu.VMEM((TBM, ROW_W), BF16),
        pltpu.SMEM((N_GROUPS,), I32),
        pltpu.SMEM((nb * N_SUB,), I32),
        pltpu.SemaphoreType.DMA((nb, 2)),
        pltpu.SemaphoreType.DMA(()),
    ]
    if has_head:
        scratch += _head_scratch(nb)

    body = functools.partial(_mixer_kernel, kind=kind, has_head=has_head, nb=nb,
                             n_steps=n_steps, cap=cap)
    grid_spec = pltpu.PrefetchScalarGridSpec(
        num_scalar_prefetch=1 if has_head else 0, grid=(n_steps,),
        in_specs=in_specs, out_specs=out_specs, scratch_shapes=scratch)
    call = pl.pallas_call(
        body, grid_spec=grid_spec, out_shape=out_shape,
        compiler_params=pltpu.CompilerParams(
            dimension_semantics=("arbitrary",), vmem_limit_bytes=VMEM_LIMIT),
        name=kind + "_mixer")
    if has_head:
        return call(head[0], *args)
    return call(*args)


def _moe_kernel(blk_ref, grp_ref, val_ref, xs_ref, wg_ref, wu_ref, wd_ref, ys_ref):
    i = pl.program_id(0)

    @pl.when(val_ref[i] == 1)
    def _():
        xb = xs_ref[:, 0:D_MODEL]
        meta = xs_ref[:, D_MODEL:ROW_W].astype(F32)
        gates = meta[:, 0:EXPERTS_PER_GROUP] + meta[:, EXPERTS_PER_GROUP:2 * EXPERTS_PER_GROUP]
        parts = []
        for e in range(EXPERTS_PER_GROUP):
            a = jnp.dot(xb, wg_ref[e], preferred_element_type=F32)
            bu = jnp.dot(xb, wu_ref[e], preferred_element_type=F32)
            hid = (a * _sigmoid(a)) * bu
            parts.append((hid * gates[:, e:e + 1]).astype(BF16))
        hid_all = jnp.concatenate(parts, axis=1)
        ys_ref[...] = jnp.dot(hid_all, wd_ref[0], preferred_element_type=F32).astype(BF16)


def _work_tables(totals, cap, n_work):
    ntile = (totals + (TBM - 1)) // TBM
    ends = jnp.cumsum(ntile)
    starts = ends - ntile
    nvalid = ends[-1]
    idx = jnp.arange(n_work, dtype=I32)
    idc = jnp.maximum(jnp.minimum(idx, nvalid - 1), 0)
    grp = jnp.minimum(jnp.sum((idc[:, None] >= ends[None, :]).astype(I32), axis=1), N_GROUPS - 1)
    blk = grp * (cap // TBM) + idc - starts[grp]
    return blk.astype(I32), grp.astype(I32), (idx < nvalid).astype(I32)


def _moe_call(xs, blk, grp, val, wg, wu, wd, n_work):
    grid_spec = pltpu.PrefetchScalarGridSpec(
        num_scalar_prefetch=3, grid=(n_work,),
        in_specs=[
            pl.BlockSpec((TBM, ROW_W), lambda i, blk, grp, val: (blk[i], 0)),
            pl.BlockSpec((EXPERTS_PER_GROUP, D_MODEL, EXPERT_FF), lambda i, blk, grp, val: (grp[i], 0, 0)),
            pl.BlockSpec((EXPERTS_PER_GROUP, D_MODEL, EXPERT_FF), lambda i, blk, grp, val: (grp[i], 0, 0)),
            pl.BlockSpec((1, GROUP_FF, D_MODEL), lambda i, blk, grp, val: (grp[i], 0, 0)),
        ],
        out_specs=pl.BlockSpec((TBM, D_MODEL), lambda i, blk, grp, val: (blk[i], 0)),
    )
    return pl.pallas_call(
        _moe_kernel, grid_spec=grid_spec,
        out_shape=jax.ShapeDtypeStruct((xs.shape[0], D_MODEL), BF16),
        compiler_params=pltpu.CompilerParams(
            dimension_semantics=("arbitrary",), vmem_limit_bytes=VMEM_LIMIT),
        name="moe_experts",
    )(blk, grp, val, xs, wg, wu, wd)


def _final_kernel(pm_ref, hprev_ref, modprev_ref, tokprev_ref, ys_hbm, nf_ref, o_ref,
                  ystage_ref, gsem, *, nb, n_steps):
    t = pl.program_id(0)
    _head_dma(t, n_steps, nb, pm_ref, ys_hbm, ystage_ref, gsem)
    for b in range(nb):
        h = _head_combine(b, t % 2, hprev_ref, modprev_ref, tokprev_ref, ystage_ref)
        ms = jnp.mean(h * h, axis=-1, keepdims=True)
        o_ref[b] = (h * lax.rsqrt(ms + EPS)) * nf_ref[0:1]


def _final_call(head, nf8):
    pm, hprev, modprev, tokprev, ys = head
    nb, seq, _ = hprev.shape
    n_steps = seq // TB
    grid_spec = pltpu.PrefetchScalarGridSpec(
        num_scalar_prefetch=1, grid=(n_steps,),
        in_specs=[_nat(nb, D_MODEL), _full((nb, 8, D_MODEL)), _nat(nb, LANES),
                  pl.BlockSpec(memory_space=pl.ANY), _full((8, D_MODEL))],
        out_specs=_nat(nb, D_MODEL),
        scratch_shapes=_head_scratch(nb))
    return pl.pallas_call(
        functools.partial(_final_kernel, nb=nb, n_steps=n_steps), grid_spec=grid_spec,
        out_shape=jax.ShapeDtypeStruct((nb, seq, D_MODEL), F32),
        compiler_params=pltpu.CompilerParams(
            dimension_semantics=("arbitrary",), vmem_limit_bytes=VMEM_LIMIT),
        name="final_norm",
    )(pm, hprev, modprev, tokprev, ys, nf8)


def _pad_rows(a, rows=8):
    return jnp.pad(a, ((0, rows - a.shape[0]), (0, 0)))


def _attn_bias_table():
    slopes = 2.0 ** (-8.0 * (jnp.arange(N_HEADS, dtype=F32) + 1.0) / N_HEADS)
    dist = (jnp.arange(BLOCK)[:, None] + BLOCK) - jnp.arange(2 * BLOCK)[None, :]
    ok = (dist >= 0) & (dist < WINDOW)
    per_head = jnp.where(ok[None], -slopes[:, None, None] * dist.astype(F32)[None], NEG_BIG)
    return per_head.reshape(N_HEADS // 2, 2, BLOCK, 2 * BLOCK).transpose(0, 2, 1, 3).reshape(
        N_HEADS // 4, 2 * BLOCK, 4 * BLOCK)


def _router_operands(w_router, b_router):
    def reorder(a):
        return a.reshape(N_GROUPS, EXPERTS_PER_GROUP, -1).transpose(1, 0, 2).reshape(N_EXPERTS, -1)
    w = reorder(w_router.T)
    wh = w.astype(BF16)
    wl = (w - wh.astype(F32)).astype(BF16)
    return jnp.concatenate([wh, wl], axis=0), reorder(b_router[:, None])


def kernel(x, c, w_ada, b_ada, norm_mix, norm_ffn, w_in, w_out, sinks, conv_w, w_pool,
           pool_scale, w_router, b_router, w_gate, w_up, w_down, norm_final):
    b, s, _ = x.shape
    n_tiles, cap, n_work = _sizes(b * s)

    mod_all = _mod_call(_pad_rows(c), w_ada, b_ada.reshape(DEPTH, 1, N_MOD * D_MODEL))
    bias_tab = _attn_bias_table()
    wrt, br = _router_operands(w_router, b_router)
    nf8 = _pad_rows(norm_final[None, :])
    tri = jnp.triu(jnp.ones((TB, TB), F32)).astype(BF16)
    qscale = jnp.concatenate([jnp.full((ATTN_WIDTH,), HEAD_DIM ** -0.5, F32),
                              jnp.ones((IN_WIDTH - ATTN_WIDTH,), F32)])

    h = x
    head = None
    for layer in range(DEPTH):
        j = layer // 2
        mod8 = jnp.pad(mod_all[layer, :b].reshape(b, N_MOD, D_MODEL), ((0, 0), (0, 2), (0, 0)))
        nrm8 = _pad_rows(jnp.stack([norm_mix[layer], norm_ffn[layer]]))
        if layer % 2 == 0:
            kind = "even"
            weights = ((w_in[j] * qscale).astype(BF16), w_out[j].astype(BF16), bias_tab,
                       sinks[j], _pad_rows(conv_w[j]))
        else:
            kind = "odd"
            weights = (w_pool[j].astype(BF16), _pad_rows(pool_scale[j][None, :]))
        h1, tokmeta, tmeta, xs = _mixer_call(kind, head, h, mod8, nrm8, weights, wrt, br, tri)
        totals = tmeta[n_tiles * META_W:n_tiles * META_W + N_GROUPS]
        blk, grp, val = _work_tables(totals, cap, n_work)
        ys = _moe_call(xs, blk, grp, val, w_gate[layer].astype(BF16), w_up[layer].astype(BF16),
                       w_down[layer].astype(BF16).reshape(N_GROUPS, GROUP_FF, D_MODEL), n_work)
        head = (tmeta, h1, mod8, tokmeta, ys)
    return _final_call(head, nf8)
```

```python
import functools

import jax
import jax.numpy as jnp
from jax import lax
from jax.experimental import pallas as pl
from jax.experimental.pallas import tpu as pltpu

F32 = jnp.float32
BF16 = jnp.bfloat16
I32 = jnp.int32

D_MODEL = 1024
DEPTH = 4
EPS = 1e-6
N_MOD = 6

ATTN_WIDTH = 512
HEAD_DIM = 64
N_HEADS = 8
KV_WIDTH = 128
WINDOW = 128
BLOCK = 128
CONV_WIDTH = 512
IN_WIDTH = 2304

POOL_SIZES = (2, 4, 8, 16)
POOL_GROUP = 256
POOL_HALO = 16

N_EXPERTS = 16
N_GROUPS = 4
EXPERTS_PER_GROUP = 4
EXPERT_FF = 256
GROUP_FF = EXPERTS_PER_GROUP * EXPERT_FF

LANES = 128
NEG_BIG = -1e30

TB = 256
SUB = 16
N_SUB = TB // SUB + N_GROUPS - 1
STG_P = 320
STG_C = 384
TBM = 512
ROW_W = D_MODEL + LANES
META_W = 32
LP_LANE = 8
CHAIN_LAG = 0
MOD_COLS = 1536
VMEM_LIMIT = 56 * 1024 * 1024
NT_DIMS = (((1,), (1,)), ((), ()))

assert N_SUB * SUB <= STG_P <= STG_C and STG_C % LANES == 0 and N_SUB <= META_W


def _sizes(n_tok):
    n_tiles = n_tok // TB
    cap = -(-(n_tok + n_tiles * (SUB - 1) + TBM) // TBM) * TBM
    n_work = (n_tok + n_tiles * N_GROUPS * (SUB - 1)) // TBM + N_GROUPS
    return n_tiles, cap, n_work


def _sigmoid(x):
    return 1.0 / (1.0 + jnp.exp(-x))


def _rms_mod(x, g, sc, sh):
    ms = jnp.mean(x * x, axis=-1, keepdims=True)
    return (x * lax.rsqrt(ms + EPS)) * (g * (1.0 + sc)) + sh


def _onehot(cond):
    return jnp.where(cond, 1.0, 0.0).astype(BF16)


def _mod_kernel(c_ref, w_ref, b_ref, o_ref):
    c = c_ref[...]
    cond = c * _sigmoid(c)
    o_ref[0] = jnp.dot(cond, w_ref[0], precision=lax.Precision.HIGHEST,
                       preferred_element_type=F32) + b_ref[0]


def _mod_call(c8, w_ada, b_ada3):
    n_col = (N_MOD * D_MODEL) // MOD_COLS
    return pl.pallas_call(
        _mod_kernel,
        grid=(DEPTH, n_col),
        in_specs=[
            pl.BlockSpec((8, D_MODEL), lambda l, j: (0, 0)),
            pl.BlockSpec((1, D_MODEL, MOD_COLS), lambda l, j: (l, 0, j)),
            pl.BlockSpec((1, 1, MOD_COLS), lambda l, j: (l, 0, j)),
        ],
        out_specs=pl.BlockSpec((1, 8, MOD_COLS), lambda l, j: (l, 0, j)),
        out_shape=jax.ShapeDtypeStruct((DEPTH, 8, N_MOD * D_MODEL), F32),
        compiler_params=pltpu.CompilerParams(
            dimension_semantics=("arbitrary", "arbitrary"), vmem_limit_bytes=VMEM_LIMIT),
        name="adaln_mod",
    )(c8, w_ada, b_ada3)


def _sub_copies(hbm_ref, stage_ref, sem, rows, to_hbm):
    out = []
    for s in range(N_SUB):
        s_view = stage_ref.at[pl.ds(s * SUB, SUB), :]
        h_view = hbm_ref.at[pl.ds(pl.multiple_of(rows[s], SUB), SUB), :]
        out.append(pltpu.make_async_copy(s_view, h_view, sem) if to_hbm
                   else pltpu.make_async_copy(h_view, s_view, sem))
    return out


def _head_dma(t, n_steps, nb, pm_ref, ys_hbm, ystage_ref, gsem):
    slot = t % 2

    def copies(step, b, slot_):
        base = (step * nb + b) * META_W
        rows = [pm_ref[base + s] for s in range(N_SUB)]
        return _sub_copies(ys_hbm, ystage_ref.at[b, slot_], gsem.at[b, slot_], rows, False)

    @pl.when(t == 0)
    def _():
        ystage_ref[...] = jnp.zeros_like(ystage_ref)
        for b in range(nb):
            for cp in copies(0, b, 0):
                cp.start()

    @pl.when(t + 1 < n_steps)
    def _():
        for b in range(nb):
            for cp in copies(t + 1, b, 1 - slot):
                cp.start()

    for b in range(nb):
        for cp in copies(t, b, slot):
            cp.wait()


def _head_combine(b, slot, hprev_ref, modprev_ref, tokmeta_ref, ystage_ref):
    lp = tokmeta_ref[b, :, LP_LANE:LP_LANE + 1].astype(I32)
    unsort = _onehot(lax.broadcasted_iota(I32, (TB, STG_C), 1) == lp)
    y = jnp.dot(unsort, ystage_ref[b, slot], preferred_element_type=F32)
    return hprev_ref[b] + modprev_ref[b, 5:6] * y


def _route(u2, wrt_ref, br_ref):
    uh = u2.astype(BF16)
    ul = (u2 - uh.astype(F32)).astype(BF16)
    w = wrt_ref[...]
    r1 = lax.dot_general(w, uh, NT_DIMS, preferred_element_type=F32)
    r2 = lax.dot_general(w[0:N_EXPERTS], ul, NT_DIMS, preferred_element_type=F32)
    yield
    scores = _sigmoid(r1[0:N_EXPERTS] + r1[N_EXPERTS:] + r2)
    biased = scores + br_ref[...]
    s = [scores[4 * j:4 * j + 4] for j in range(EXPERTS_PER_GROUP)]
    b = [biased[4 * j:4 * j + 4] for j in range(EXPERTS_PER_GROUP)]

    hi1, lo1 = jnp.maximum(b[0], b[1]), jnp.minimum(b[0], b[1])
    hi2, lo2 = jnp.maximum(b[2], b[3]), jnp.minimum(b[2], b[3])
    second = jnp.maximum(jnp.minimum(hi1, hi2), jnp.where(hi1 >= hi2, lo1, lo2))
    gscore = jnp.maximum(hi1, hi2) + second

    best = gscore[0:1]
    bgrp = jnp.zeros((1, TB), I32)
    for g in range(1, N_GROUPS):
        better = gscore[g:g + 1] > best
        bgrp = jnp.where(better, g, bgrp)
        best = jnp.where(better, gscore[g:g + 1], best)
    selmask = lax.broadcasted_iota(I32, (N_GROUPS, TB), 0) == bgrp
    yield

    m1 = b[0]
    i1 = jnp.zeros((N_GROUPS, TB), I32)
    for j in range(1, EXPERTS_PER_GROUP):
        gt = b[j] > m1
        i1 = jnp.where(gt, j, i1)
        m1 = jnp.where(gt, b[j], m1)
    cands = [jnp.where(i1 == j, -jnp.inf, b[j]) for j in range(EXPERTS_PER_GROUP)]
    m2 = cands[0]
    i2 = jnp.zeros_like(i1)
    for j in range(1, EXPERTS_PER_GROUP):
        gt = cands[j] > m2
        i2 = jnp.where(gt, j, i2)
        m2 = jnp.where(gt, cands[j], m2)
    w1 = s[0]
    w2 = s[0]
    for j in range(1, EXPERTS_PER_GROUP):
        w1 = jnp.where(i1 == j, s[j], w1)
        w2 = jnp.where(i2 == j, s[j], w2)
    yield
    tot = w1 + w2
    w1n = w1 / tot
    w2n = w2 / tot
    gates = []
    for j in range(EXPERTS_PER_GROUP):
        gj = jnp.where(i1 == j, w1n, 0.0) + jnp.where(i2 == j, w2n, 0.0)
        gates.append(jnp.sum(jnp.where(selmask, gj, 0.0), axis=0, keepdims=True))
    return bgrp, gates, selmask


def _tail_compute(b, slot, h1, mod, nrm_ref, wrt_ref, br_ref, tri_ref, tokmeta_ref, stage_ref):
    sh2, sc2 = mod[3:4], mod[4:5]
    u2 = _rms_mod(h1, nrm_ref[1:2], sc2, sh2)
    bgrp, gates, selmask = yield from _route(u2, wrt_ref, br_ref)
    yield

    onehot = jnp.concatenate([jnp.where(selmask, 1.0, 0.0), jnp.zeros((8 - N_GROUPS, TB), F32)],
                             axis=0).astype(BF16)
    rank_incl = jnp.dot(onehot, tri_ref[...], preferred_element_type=F32)
    rank = jnp.sum(jnp.where(selmask, rank_incl[0:N_GROUPS], 0.0), axis=0, keepdims=True) - 1.0
    cnt = lax.dot_general(jnp.ones((8, TB), BF16), onehot, NT_DIMS,
                          preferred_element_type=F32)
    sub_rows = jnp.floor((cnt[0:1, :] + (SUB - 1.0)) * (1.0 / SUB)) * SUB
    lp = rank
    first_row = jnp.zeros((1, 1), F32)
    for g in range(1, N_GROUPS):
        first_row = first_row + sub_rows[:, g - 1:g]
        lp = lp + jnp.where(bgrp == g, first_row, 0.0)

    hi = [x.astype(BF16).astype(F32) for x in gates]
    lo = [(x - h).astype(BF16).astype(F32) for x, h in zip(gates, hi)]
    meta_src = jnp.concatenate(hi + lo + [lp, jnp.zeros((LANES - 9, TB), F32)], axis=0)
    meta_t = meta_src.T
    tokmeta_ref[b] = meta_t
    yield

    rowdata = jnp.concatenate([u2.astype(BF16), meta_t.astype(BF16)], axis=1)
    sort = _onehot(lax.broadcasted_iota(I32, (STG_P, TB), 0) == lp.astype(I32))
    stage_ref[b, slot] = jnp.dot(sort, rowdata, preferred_element_type=F32).astype(BF16)
    return [cnt[0, g].astype(I32) for g in range(N_GROUPS)]


def _tail_dma(t, n_steps, nb, cap, counts, tmeta_ref, xs_hbm, stage_ref, zeros_ref,
              run_ref, prev_ref, ssem, zsem):
    slot = t % 2
    run = [run_ref[g] for g in range(N_GROUPS)]
    junk = N_GROUPS * cap
    scatter_rows = []
    for b in range(nb):
        first_sub = [jnp.int32(0)]
        for g in range(N_GROUPS):
            first_sub.append(first_sub[-1] + (counts[b][g] + (SUB - 1)) // SUB)
        dst = [g * cap + run[g] for g in range(N_GROUPS)]
        rows_b, base = [], (t * nb + b) * META_W
        for s in range(N_SUB):
            grp_first, grp_dst = first_sub[0], dst[0]
            for g in range(1, N_GROUPS):
                later = s >= first_sub[g]
                grp_first = jnp.where(later, first_sub[g], grp_first)
                grp_dst = jnp.where(later, dst[g], grp_dst)
            row = grp_dst + (s - grp_first) * SUB
            used = s < first_sub[N_GROUPS]
            rows_b.append(jnp.where(used, row, junk + (b * N_SUB + s) * SUB))
            tmeta_ref[base + s] = jnp.where(used, row, row if s == 0 else first_row)
            if s == 0:
                first_row = row
        for k in range(N_SUB, META_W):
            tmeta_ref[base + k] = jnp.int32(0)
        for g in range(N_GROUPS):
            run[g] = run[g] + (first_sub[g + 1] - first_sub[g]) * SUB
        scatter_rows.append(rows_b)
    for g in range(N_GROUPS):
        run_ref[g] = run[g]

    def copies(b, slot_, rows):
        return _sub_copies(xs_hbm, stage_ref.at[b, slot_], ssem.at[b, slot_], rows, True)

    @pl.when(t > 0)
    def _():
        for b in range(nb):
            for cp in copies(b, 1 - slot, [prev_ref[b * N_SUB + s] for s in range(N_SUB)]):
                cp.wait()

    for b in range(nb):
        for cp in copies(b, slot, scatter_rows[b]):
            cp.start()
        for s in range(N_SUB):
            prev_ref[b * N_SUB + s] = scatter_rows[b][s]

    @pl.when(t == n_steps - 1)
    def _():
        for b in range(nb):
            for cp in copies(b, slot, scatter_rows[b]):
                cp.wait()
        zeros_ref[...] = jnp.zeros_like(zeros_ref)
        tails = []
        base = n_steps * nb * META_W
        for g in range(N_GROUPS):
            tmeta_ref[base + g] = run[g]
            start = pl.multiple_of(g * cap + run[g], SUB)
            tails.append(pltpu.make_async_copy(zeros_ref, xs_hbm.at[pl.ds(start, TBM), :], zsem))
        for k in range(N_GROUPS, META_W):
            tmeta_ref[base + k] = jnp.int32(0)
        for cp in tails:
            cp.start()
        for cp in tails:
            cp.wait()


def _attn_conv_mix(x, t, mod, nrm_ref, win_ref, wout_ref, bias_ref, sink_ref, convw_ref,
                   kvprev_ref, cprev_ref):
    tb = TB
    first = t == 0
    sh1, sc1, g1 = mod[0:1], mod[1:2], mod[2:3]
    u = _rms_mod(x, nrm_ref[0:1], sc1, sh1)
    proj = jnp.dot(u.astype(BF16), win_ref[...], preferred_element_type=F32)

    yield
    q = (proj[:, 0:ATTN_WIDTH] * HEAD_DIM ** -0.5).astype(BF16)
    kf = proj[:, 512:640]
    vf = proj[:, 640:768]
    bgate = proj[:, 768:1280]
    cgate = proj[:, 1280:1792]
    xv = proj[:, 1792:2304]

    kv_prev = kvprev_ref[...]
    kext = jnp.concatenate([kv_prev[:, 0:KV_WIDTH], kf], axis=0)
    vext = jnp.concatenate([kv_prev[:, KV_WIDTH:], vf], axis=0)
    kvprev_ref[:, 0:KV_WIDTH] = kf[tb - BLOCK:tb]
    kvprev_ref[:, KV_WIDTH:] = vf[tb - BLOCK:tb]

    lane = lax.broadcasted_iota(I32, kext.shape, 1)
    lo = lane < HEAD_DIM
    krol = pltpu.roll(kext, HEAD_DIM, axis=1)
    vrol = pltpu.roll(vext, HEAD_DIM, axis=1)
    zero = jnp.zeros_like(kext)
    one_at_64 = jnp.where(lane == HEAD_DIM, 1.0, 0.0)
    one_at_0 = jnp.where(lane == 0, 1.0, 0.0)
    k_ops = [(jnp.where(lo, kext, zero).astype(BF16), jnp.where(lo, zero, krol).astype(BF16)),
             (jnp.where(lo, krol, zero).astype(BF16), jnp.where(lo, zero, kext).astype(BF16))]
    v_ops = [(jnp.where(lo, vext, one_at_64).astype(BF16), jnp.where(lo, one_at_0, vrol).astype(BF16)),
             (jnp.where(lo, vrol, one_at_64).astype(BF16), jnp.where(lo, one_at_0, vext).astype(BF16))]

    col = lax.broadcasted_iota(I32, (2 * BLOCK, 4 * BLOCK), 1)
    prev_cols = (col % (2 * BLOCK)) < BLOCK
    upper = lax.broadcasted_iota(I32, (2 * BLOCK, 1), 0) < BLOCK
    lane_o = lax.broadcasted_iota(I32, (2 * BLOCK, LANES), 1)
    lo_o = lane_o < HEAD_DIM

    yield
    attn_rows = []
    for bi in range(tb // BLOCK):
        r0 = bi * BLOCK
        pair_out = []
        for kvh in range(N_HEADS // 4):
            ka, kb = k_ops[kvh]
            rhs = jnp.concatenate([ka[r0:r0 + 2 * BLOCK], kb[r0:r0 + 2 * BLOCK]], axis=0)
            qrows = q[r0:r0 + BLOCK]
            qp = jnp.concatenate([qrows[:, (2 * kvh) * LANES:(2 * kvh + 1) * LANES],
                                  qrows[:, (2 * kvh + 1) * LANES:(2 * kvh + 2) * LANES]], axis=0)
            s = lax.dot_general(qp, rhs, NT_DIMS, preferred_element_type=F32)
            bias = bias_ref[kvh]
            if bi == 0:
                bias = jnp.where(prev_cols & first, NEG_BIG, bias)
            s = s + bias
            outs = []
            for hh in range(2):
                sh = s[:, hh * 2 * BLOCK:(hh + 1) * 2 * BLOCK]
                sink = jnp.where(upper, sink_ref[4 * kvh + hh], sink_ref[4 * kvh + 2 + hh])
                m = jnp.maximum(jnp.max(sh, axis=-1, keepdims=True), sink)
                p = jnp.exp(sh - m).astype(BF16)
                vop = v_ops[kvh][hh][r0:r0 + 2 * BLOCK]
                o = jnp.dot(p, vop, preferred_element_type=F32)
                den_col = HEAD_DIM if hh == 0 else 0
                den = o[:, den_col:den_col + 1] + jnp.exp(sink - m)
                outs.append(o / den)
            both = jnp.where(lo_o, outs[0], outs[1])
            pair_out += [both[0:BLOCK], both[BLOCK:]]
            yield
        attn_rows.append(jnp.concatenate(pair_out, axis=1))
    attn = jnp.concatenate(attn_rows, axis=0)

    uc = cgate * xv
    cprev = jnp.where(first, 0.0, cprev_ref[...])
    row = lax.broadcasted_iota(I32, uc.shape, 0)
    r1 = jnp.where(row == 0, cprev[7:8], pltpu.roll(uc, 1, axis=0))
    r2 = jnp.where(row == 0, cprev[6:7], jnp.where(row == 1, cprev[7:8], pltpu.roll(uc, 2, axis=0)))
    cprev_ref[...] = uc[tb - 8:tb]
    cw = convw_ref[...]
    conv = bgate * (cw[0:1] * r2 + cw[1:2] * r1 + cw[2:3] * uc)
    yield

    mix = (jnp.dot(attn.astype(BF16), wout_ref[0:ATTN_WIDTH, :], preferred_element_type=F32)
           + jnp.dot(conv.astype(BF16), wout_ref[ATTN_WIDTH:, :], preferred_element_type=F32))
    return x + g1 * mix


def _pool_mix(x, t, mod, nrm_ref, wpool_ref, pscale_ref, uprev_ref):
    tb = TB
    sh1, sc1, g1 = mod[0:1], mod[1:2], mod[2:3]
    u = _rms_mod(x, nrm_ref[0:1], sc1, sh1)
    ext = jnp.concatenate([jnp.where(t == 0, 0.0, uprev_ref[...]), u], axis=0)
    uprev_ref[...] = u[tb - POOL_HALO:tb]

    yield
    pos = (t * tb + 1 + lax.broadcasted_iota(I32, (tb, 1), 0)).astype(F32)
    mixed = []
    for gi, w in enumerate(POOL_SIZES):
        sl = slice(gi * POOL_GROUP, (gi + 1) * POOL_GROUP)
        acc = ext[:, sl]
        shift = 1
        while shift < w:
            acc = acc + pltpu.roll(acc, shift, axis=0)
            shift *= 2
        mean = acc[POOL_HALO:] / jnp.minimum(pos, float(w))
        pooled = mean - u[:, sl]
        mixed.append(jnp.dot(pooled.astype(BF16), wpool_ref[gi].astype(BF16),
                             preferred_element_type=F32))
        yield
    mix = jnp.concatenate(mixed, axis=1) * pscale_ref[0:1]
    return x + g1 * mix


def _interleave(chains, lag):
    results = [None] * len(chains)
    live = list(range(len(chains)))
    rnd = 0
    while live:
        for k in list(live):
            if rnd < k * lag:
                continue
            try:
                next(chains[k])
            except StopIteration as done:
                results[k] = done.value
                live.remove(k)
        rnd += 1
    return results


def _mixer_kernel(*refs, kind, has_head, nb, n_steps, cap):
    refs = list(refs)
    t = pl.program_id(0)
    slot = t % 2
    if has_head:
        pm_ref = refs.pop(0)
        hprev_ref, modprev_ref, tokprev_ref, ys_hbm = refs[:4]
        refs = refs[4:]
    else:
        h_ref = refs.pop(0)
    mod_ref, nrm_ref = refs[:2]
    refs = refs[2:]
    n_w = 5 if kind == "even" else 2
    weight_refs = refs[:n_w]
    refs = refs[n_w:]
    wrt_ref, br_ref, tri_ref = refs[:3]
    refs = refs[3:]
    hout_ref, tokmeta_ref, tmeta_ref, xs_hbm = refs[:4]
    refs = refs[4:]
    n_c = 2 if kind == "even" else 1
    carry_refs = refs[:n_c]
    refs = refs[n_c:]
    if kind == "even":
        cast_refs = refs[:2]
        refs = refs[2:]
    stage_ref, zeros_ref, run_ref, prev_ref, ssem, zsem = refs[:6]
    refs = refs[6:]

    @pl.when(t == 0)
    def _():
        for r in carry_refs:
            r[...] = jnp.zeros_like(r)
        for g in range(N_GROUPS):
            run_ref[g] = jnp.int32(0)
        if kind == "even":
            for src, dst in zip(weight_refs[:2], cast_refs):
                for c0 in range(0, src.shape[1], 2 * LANES):
                    dst[:, c0:c0 + 2 * LANES] = src[:, c0:c0 + 2 * LANES].astype(BF16)

    if kind == "even":
        weight_refs = list(cast_refs) + list(weight_refs[2:])

    if has_head:
        ystage_ref, gsem = refs
        _head_dma(t, n_steps, nb, pm_ref, ys_hbm, ystage_ref, gsem)

    def chain(b):
        if has_head:
            x = _head_combine(b, slot, hprev_ref, modprev_ref, tokprev_ref, ystage_ref)
            yield
        else:
            x = h_ref[b]
        mod = mod_ref[b]
        carries = [r.at[b] for r in carry_refs]
        mix = _attn_conv_mix if kind == "even" else _pool_mix
        h1 = yield from mix(x, t, mod, nrm_ref, *weight_refs, *carries)
        hout_ref[b] = h1
        yield
        return (yield from _tail_compute(b, slot, h1, mod, nrm_ref, wrt_ref, br_ref, tri_ref,
                                         tokmeta_ref, stage_ref))

    counts = _interleave([chain(b) for b in range(nb)], CHAIN_LAG)
    _tail_dma(t, n_steps, nb, cap, counts, tmeta_ref, xs_hbm, stage_ref, zeros_ref,
              run_ref, prev_ref, ssem, zsem)


def _nat(nb, cols):
    return pl.BlockSpec((nb, TB, cols), lambda t, *_: (0, t, 0))


def _full(shape):
    nd = len(shape)
    return pl.BlockSpec(shape, lambda t, *_: (0,) * nd)


def _head_scratch(nb):
    return [pltpu.VMEM((nb, 2, STG_C, D_MODEL), BF16), pltpu.SemaphoreType.DMA((nb, 2))]


def _mixer_call(kind, head, h, mod8, nrm8, weights, wrt, br, tri):
    nb, seq, _ = (h if head is None else head[1]).shape
    n_steps = seq // TB
    n_tiles, cap, _ = _sizes(nb * seq)
    has_head = head is not None
    any_spec = pl.BlockSpec(memory_space=pl.ANY)
    smem_spec = pl.BlockSpec(memory_space=pltpu.SMEM)

    if has_head:
        args = list(head[1:])
        in_specs = [_nat(nb, D_MODEL), _full((nb, 8, D_MODEL)), _nat(nb, LANES), any_spec]
    else:
        args = [h]
        in_specs = [_nat(nb, D_MODEL)]
    args += [mod8, nrm8]
    in_specs += [_full((nb, 8, D_MODEL)), _full((8, D_MODEL))]
    if kind == "even":
        args += list(weights)
        once = dict(pipeline_mode=pl.Buffered(1))
        in_specs += [pl.BlockSpec((D_MODEL, IN_WIDTH), lambda t, *_: (0, 0), **once),
                     pl.BlockSpec((D_MODEL, D_MODEL), lambda t, *_: (0, 0), **once),
                     _full((N_HEADS // 4, 2 * BLOCK, 4 * BLOCK)), smem_spec, _full((8, CONV_WIDTH))]
        mix_scratch = [pltpu.VMEM((nb, BLOCK, 2 * KV_WIDTH), F32),
                       pltpu.VMEM((nb, 8, CONV_WIDTH), F32),
                       pltpu.VMEM((D_MODEL, IN_WIDTH), BF16),
                       pltpu.VMEM((D_MODEL, D_MODEL), BF16)]
    else:
        args += list(weights)
        in_specs += [_full((len(POOL_SIZES), POOL_GROUP, POOL_GROUP)), _full((8, D_MODEL))]
        mix_scratch = [pltpu.VMEM((nb, POOL_HALO, D_MODEL), F32)]
    args += [wrt, br, tri]
    in_specs += [_full((2 * N_EXPERTS, D_MODEL)), _full((N_EXPERTS, 1)), _full((TB, TB))]

    out_shape = (jax.ShapeDtypeStruct((nb, seq, D_MODEL), F32),
                 jax.ShapeDtypeStruct((nb, seq, LANES), F32),
                 jax.ShapeDtypeStruct(((n_tiles + 1) * META_W,), I32),
                 jax.ShapeDtypeStruct((N_GROUPS * cap + nb * N_SUB * SUB, ROW_W), BF16))
    out_specs = (_nat(nb, D_MODEL), _nat(nb, LANES), smem_spec, any_spec)
    scratch = mix_scratch + [
        pltpu.VMEM((nb, 2, STG_P, ROW_W), BF16),
        pltpu.VMEM((TBM, ROW_W), BF16),
        pltpu.SMEM((N_GROUPS,), I32),
        pltpu.SMEM((nb * N_SUB,), I32),
        pltpu.SemaphoreType.DMA((nb, 2)),
        pltpu.SemaphoreType.DMA(()),
    ]
    if has_head:
        scratch += _head_scratch(nb)

    body = functools.partial(_mixer_kernel, kind=kind, has_head=has_head, nb=nb,
                             n_steps=n_steps, cap=cap)
    grid_spec = pltpu.PrefetchScalarGridSpec(
        num_scalar_prefetch=1 if has_head else 0, grid=(n_steps,),
        in_specs=in_specs, out_specs=out_specs, scratch_shapes=scratch)
    call = pl.pallas_call(
        body, grid_spec=grid_spec, out_shape=out_shape,
        compiler_params=pltpu.CompilerParams(
            dimension_semantics=("arbitrary",), vmem_limit_bytes=VMEM_LIMIT),
        name=kind + "_mixer")
    if has_head:
        return call(head[0], *args)
    return call(*args)


def _moe_kernel(blk_ref, grp_ref, val_ref, new_ref, xs_ref, wg_ref, wu_ref, wd_ref, ys_ref,
                wg_s, wu_s, wd_s):
    i = pl.program_id(0)

    @pl.when(new_ref[i] == 1)
    def _():
        for e in range(EXPERTS_PER_GROUP):
            wg_s[e] = wg_ref[e].astype(BF16)
            wu_s[e] = wu_ref[e].astype(BF16)
        wd_s[...] = wd_ref[0].astype(BF16)

    @pl.when(val_ref[i] == 1)
    def _():
        xb = xs_ref[:, 0:D_MODEL]
        meta = xs_ref[:, D_MODEL:ROW_W].astype(F32)
        gates = meta[:, 0:EXPERTS_PER_GROUP] + meta[:, EXPERTS_PER_GROUP:2 * EXPERTS_PER_GROUP]
        parts = []
        for e in range(EXPERTS_PER_GROUP):
            a = jnp.dot(xb, wg_s[e], preferred_element_type=F32)
            bu = jnp.dot(xb, wu_s[e], preferred_element_type=F32)
            hid = (a * _sigmoid(a)) * bu
            parts.append((hid * gates[:, e:e + 1]).astype(BF16))
        hid_all = jnp.concatenate(parts, axis=1)
        ys_ref[...] = jnp.dot(hid_all, wd_s[...], preferred_element_type=F32).astype(BF16)


def _work_tables(totals, cap, n_work):
    ntile = (totals + (TBM - 1)) // TBM
    ends = jnp.cumsum(ntile)
    starts = ends - ntile
    nvalid = ends[-1]
    idx = jnp.arange(n_work, dtype=I32)
    idc = jnp.maximum(jnp.minimum(idx, nvalid - 1), 0)
    grp = jnp.minimum(jnp.sum((idc[:, None] >= ends[None, :]).astype(I32), axis=1), N_GROUPS - 1)
    blk = grp * (cap // TBM) + idc - starts[grp]
    valid = idx < nvalid
    new = valid & ((idx == 0) | (grp != jnp.roll(grp, 1)))
    return blk.astype(I32), grp.astype(I32), valid.astype(I32), new.astype(I32)


def _moe_call(xs, tables, wg, wu, wd, n_work):
    row_map = lambda i, blk, grp, val, new: (blk[i], 0)
    grp_map = lambda i, blk, grp, val, new: (grp[i], 0, 0)
    grid_spec = pltpu.PrefetchScalarGridSpec(
        num_scalar_prefetch=4, grid=(n_work,),
        in_specs=[
            pl.BlockSpec((TBM, ROW_W), row_map),
            pl.BlockSpec((EXPERTS_PER_GROUP, D_MODEL, EXPERT_FF), grp_map),
            pl.BlockSpec((EXPERTS_PER_GROUP, D_MODEL, EXPERT_FF), grp_map),
            pl.BlockSpec((1, GROUP_FF, D_MODEL), grp_map),
        ],
        out_specs=pl.BlockSpec((TBM, D_MODEL), row_map),
        scratch_shapes=[pltpu.VMEM((EXPERTS_PER_GROUP, D_MODEL, EXPERT_FF), BF16),
                        pltpu.VMEM((EXPERTS_PER_GROUP, D_MODEL, EXPERT_FF), BF16),
                        pltpu.VMEM((GROUP_FF, D_MODEL), BF16)],
    )
    return pl.pallas_call(
        _moe_kernel, grid_spec=grid_spec,
        out_shape=jax.ShapeDtypeStruct((xs.shape[0], D_MODEL), BF16),
        compiler_params=pltpu.CompilerParams(
            dimension_semantics=("arbitrary",), vmem_limit_bytes=VMEM_LIMIT),
        name="moe_experts",
    )(*tables, xs, wg, wu, wd)


def _final_kernel(pm_ref, hprev_ref, modprev_ref, tokprev_ref, ys_hbm, nf_ref, o_ref,
                  ystage_ref, gsem, *, nb, n_steps):
    t = pl.program_id(0)
    _head_dma(t, n_steps, nb, pm_ref, ys_hbm, ystage_ref, gsem)
    for b in range(nb):
        h = _head_combine(b, t % 2, hprev_ref, modprev_ref, tokprev_ref, ystage_ref)
        ms = jnp.mean(h * h, axis=-1, keepdims=True)
        o_ref[b] = (h * lax.rsqrt(ms + EPS)) * nf_ref[0:1]


def _final_call(head, nf8):
    pm, hprev, modprev, tokprev, ys = head
    nb, seq, _ = hprev.shape
    n_steps = seq // TB
    grid_spec = pltpu.PrefetchScalarGridSpec(
        num_scalar_prefetch=1, grid=(n_steps,),
        in_specs=[_nat(nb, D_MODEL), _full((nb, 8, D_MODEL)), _nat(nb, LANES),
                  pl.BlockSpec(memory_space=pl.ANY), _full((8, D_MODEL))],
        out_specs=_nat(nb, D_MODEL),
        scratch_shapes=_head_scratch(nb))
    return pl.pallas_call(
        functools.partial(_final_kernel, nb=nb, n_steps=n_steps), grid_spec=grid_spec,
        out_shape=jax.ShapeDtypeStruct((nb, seq, D_MODEL), F32),
        compiler_params=pltpu.CompilerParams(
            dimension_semantics=("arbitrary",), vmem_limit_bytes=VMEM_LIMIT),
        name="final_norm",
    )(pm, hprev, modprev, tokprev, ys, nf8)


def _pad_rows(a, rows=8):
    return jnp.pad(a, ((0, rows - a.shape[0]), (0, 0)))


def _attn_bias_table():
    slopes = 2.0 ** (-8.0 * (jnp.arange(N_HEADS, dtype=F32) + 1.0) / N_HEADS)
    dist = (jnp.arange(BLOCK)[:, None] + BLOCK) - jnp.arange(2 * BLOCK)[None, :]
    ok = (dist >= 0) & (dist < WINDOW)
    per_head = jnp.where(ok[None], -slopes[:, None, None] * dist.astype(F32)[None], NEG_BIG)
    return per_head.reshape(N_HEADS // 2, 2, BLOCK, 2 * BLOCK).transpose(0, 2, 1, 3).reshape(
        N_HEADS // 4, 2 * BLOCK, 4 * BLOCK)


def _router_operands(w_router, b_router):
    def reorder(a):
        return a.reshape(N_GROUPS, EXPERTS_PER_GROUP, -1).transpose(1, 0, 2).reshape(N_EXPERTS, -1)
    w = reorder(w_router.T)
    wh = w.astype(BF16)
    wl = (w - wh.astype(F32)).astype(BF16)
    return jnp.concatenate([wh, wl], axis=0), reorder(b_router[:, None])


def kernel(x, c, w_ada, b_ada, norm_mix, norm_ffn, w_in, w_out, sinks, conv_w, w_pool,
           pool_scale, w_router, b_router, w_gate, w_up, w_down, norm_final):
    b, s, _ = x.shape
    n_tiles, cap, n_work = _sizes(b * s)

    mod_all = _mod_call(_pad_rows(c), w_ada, b_ada.reshape(DEPTH, 1, N_MOD * D_MODEL))
    bias_tab = _attn_bias_table()
    wrt, br = _router_operands(w_router, b_router)
    nf8 = _pad_rows(norm_final[None, :])
    tri = jnp.triu(jnp.ones((TB, TB), F32)).astype(BF16)

    h = x
    head = None
    for layer in range(DEPTH):
        j = layer // 2
        mod8 = jnp.pad(mod_all[layer, :b].reshape(b, N_MOD, D_MODEL), ((0, 0), (0, 2), (0, 0)))
        nrm8 = _pad_rows(jnp.stack([norm_mix[layer], norm_ffn[layer]]))
        if layer % 2 == 0:
            kind = "even"
            weights = (w_in[j], w_out[j], bias_tab, sinks[j], _pad_rows(conv_w[j]))
        else:
            kind = "odd"
            weights = (w_pool[j], _pad_rows(pool_scale[j][None, :]))
        h1, tokmeta, tmeta, xs = _mixer_call(kind, head, h, mod8, nrm8, weights, wrt, br, tri)
        totals = tmeta[n_tiles * META_W:n_tiles * META_W + N_GROUPS]
        ys = _moe_call(xs, _work_tables(totals, cap, n_work), w_gate[layer], w_up[layer],
                       w_down[layer].reshape(N_GROUPS, GROUP_FF, D_MODEL), n_work)
        head = (tmeta, h1, mod8, tokmeta, ys)
    return _final_call(head, nf8)
```

```python
import functools

import jax
import jax.numpy as jnp
from jax import lax
from jax.experimental import pallas as pl
from jax.experimental.pallas import tpu as pltpu

F32 = jnp.float32
BF16 = jnp.bfloat16
I32 = jnp.int32

D_MODEL = 1024
DEPTH = 4
EPS = 1e-6
N_MOD = 6

ATTN_WIDTH = 512
HEAD_DIM = 64
N_HEADS = 8
KV_WIDTH = 128
WINDOW = 128
BLOCK = 128
CONV_WIDTH = 512
IN_WIDTH = 2304

POOL_SIZES = (2, 4, 8, 16)
POOL_GROUP = 256
POOL_HALO = 16

N_EXPERTS = 16
N_GROUPS = 4
EXPERTS_PER_GROUP = 4
EXPERT_FF = 256
GROUP_FF = EXPERTS_PER_GROUP * EXPERT_FF

LANES = 128
NEG_BIG = -1e30

TB = 256
SUB = 16
N_SUB = TB // SUB + N_GROUPS - 1
STG_P = 320
STG_C = 384
TBM = 512
ROW_W = D_MODEL + LANES
META_W = 32
LP_LANE = 8
CHAIN_LAG = 0
MOD_COLS = 1536
VMEM_LIMIT = 56 * 1024 * 1024
NT_DIMS = (((1,), (1,)), ((), ()))

assert N_SUB * SUB <= STG_P <= STG_C and STG_C % LANES == 0 and N_SUB <= META_W


def _sizes(n_tok):
    n_tiles = n_tok // TB
    cap = -(-(n_tok + n_tiles * (SUB - 1) + TBM) // TBM) * TBM
    n_work = (n_tok + n_tiles * N_GROUPS * (SUB - 1)) // TBM + N_GROUPS
    return n_tiles, cap, n_work


def _sigmoid(x):
    return 1.0 / (1.0 + jnp.exp(-x))


def _rms_mod(x, g, sc, sh):
    ms = jnp.mean(x * x, axis=-1, keepdims=True)
    return (x * lax.rsqrt(ms + EPS)) * (g * (1.0 + sc)) + sh


def _onehot(cond):
    return jnp.where(cond, 1.0, 0.0).astype(BF16)


def _mod_kernel(c_ref, w_ref, b_ref, o_ref):
    c = c_ref[...]
    cond = c * _sigmoid(c)
    o_ref[0] = jnp.dot(cond, w_ref[0], precision=lax.Precision.HIGHEST,
                       preferred_element_type=F32) + b_ref[0]


def _mod_call(c8, w_ada, b_ada3):
    n_col = (N_MOD * D_MODEL) // MOD_COLS
    return pl.pallas_call(
        _mod_kernel,
        grid=(DEPTH, n_col),
        in_specs=[
            pl.BlockSpec((8, D_MODEL), lambda l, j: (0, 0)),
            pl.BlockSpec((1, D_MODEL, MOD_COLS), lambda l, j: (l, 0, j)),
            pl.BlockSpec((1, 1, MOD_COLS), lambda l, j: (l, 0, j)),
        ],
        out_specs=pl.BlockSpec((1, 8, MOD_COLS), lambda l, j: (l, 0, j)),
        out_shape=jax.ShapeDtypeStruct((DEPTH, 8, N_MOD * D_MODEL), F32),
        compiler_params=pltpu.CompilerParams(
            dimension_semantics=("arbitrary", "arbitrary"), vmem_limit_bytes=VMEM_LIMIT),
        name="adaln_mod",
    )(c8, w_ada, b_ada3)


def _sub_copies(hbm_ref, stage_ref, sem, rows, to_hbm):
    out = []
    for s in range(N_SUB):
        s_view = stage_ref.at[pl.ds(s * SUB, SUB), :]
        h_view = hbm_ref.at[pl.ds(pl.multiple_of(rows[s], SUB), SUB), :]
        out.append(pltpu.make_async_copy(s_view, h_view, sem) if to_hbm
                   else pltpu.make_async_copy(h_view, s_view, sem))
    return out


def _head_dma(t, n_steps, nb, pm_ref, ys_hbm, ystage_ref, gsem):
    slot = t % 2

    def copies(step, b, slot_):
        base = (step * nb + b) * META_W
        rows = [pm_ref[base + s] for s in range(N_SUB)]
        return _sub_copies(ys_hbm, ystage_ref.at[b, slot_], gsem.at[b, slot_], rows, False)

    @pl.when(t == 0)
    def _():
        ystage_ref[...] = jnp.zeros_like(ystage_ref)
        for b in range(nb):
            for cp in copies(0, b, 0):
                cp.start()

    @pl.when(t + 1 < n_steps)
    def _():
        for b in range(nb):
            for cp in copies(t + 1, b, 1 - slot):
                cp.start()

    for b in range(nb):
        for cp in copies(t, b, slot):
            cp.wait()


def _head_combine(b, slot, hprev_ref, modprev_ref, tokmeta_ref, ystage_ref):
    lp = tokmeta_ref[b, :, LP_LANE:LP_LANE + 1].astype(I32)
    unsort = _onehot(lax.broadcasted_iota(I32, (TB, STG_C), 1) == lp)
    y = jnp.dot(unsort, ystage_ref[b, slot], preferred_element_type=F32)
    return hprev_ref[b] + modprev_ref[b, 5:6] * y


def _route(u2, wrt_ref, br_ref):
    uh = u2.astype(BF16)
    ul = (u2 - uh.astype(F32)).astype(BF16)
    w = wrt_ref[...]
    r1 = lax.dot_general(w, uh, NT_DIMS, preferred_element_type=F32)
    r2 = lax.dot_general(w[0:N_EXPERTS], ul, NT_DIMS, preferred_element_type=F32)
    yield
    scores = _sigmoid(r1[0:N_EXPERTS] + r1[N_EXPERTS:] + r2)
    biased = scores + br_ref[...]
    s = [scores[4 * j:4 * j + 4] for j in range(EXPERTS_PER_GROUP)]
    b = [biased[4 * j:4 * j + 4] for j in range(EXPERTS_PER_GROUP)]

    hi1, lo1 = jnp.maximum(b[0], b[1]), jnp.minimum(b[0], b[1])
    hi2, lo2 = jnp.maximum(b[2], b[3]), jnp.minimum(b[2], b[3])
    second = jnp.maximum(jnp.minimum(hi1, hi2), jnp.where(hi1 >= hi2, lo1, lo2))
    gscore = jnp.maximum(hi1, hi2) + second

    best = gscore[0:1]
    bgrp = jnp.zeros((1, TB), I32)
    for g in range(1, N_GROUPS):
        better = gscore[g:g + 1] > best
        bgrp = jnp.where(better, g, bgrp)
        best = jnp.where(better, gscore[g:g + 1], best)
    selmask = lax.broadcasted_iota(I32, (N_GROUPS, TB), 0) == bgrp
    yield

    m1 = b[0]
    i1 = jnp.zeros((N_GROUPS, TB), I32)
    for j in range(1, EXPERTS_PER_GROUP):
        gt = b[j] > m1
        i1 = jnp.where(gt, j, i1)
        m1 = jnp.where(gt, b[j], m1)
    cands = [jnp.where(i1 == j, -jnp.inf, b[j]) for j in range(EXPERTS_PER_GROUP)]
    m2 = cands[0]
    i2 = jnp.zeros_like(i1)
    for j in range(1, EXPERTS_PER_GROUP):
        gt = cands[j] > m2
        i2 = jnp.where(gt, j, i2)
        m2 = jnp.where(gt, cands[j], m2)
    w1 = s[0]
    w2 = s[0]
    for j in range(1, EXPERTS_PER_GROUP):
        w1 = jnp.where(i1 == j, s[j], w1)
        w2 = jnp.where(i2 == j, s[j], w2)
    yield
    tot = w1 + w2
    w1n = w1 / tot
    w2n = w2 / tot
    gates = []
    for j in range(EXPERTS_PER_GROUP):
        gj = jnp.where(i1 == j, w1n, 0.0) + jnp.where(i2 == j, w2n, 0.0)
        gates.append(jnp.sum(jnp.where(selmask, gj, 0.0), axis=0, keepdims=True))
    return bgrp, gates, selmask


def _tail_compute(b, slot, h1, mod, nrm_ref, wrt_ref, br_ref, tri_ref, tokmeta_ref, stage_ref):
    sh2, sc2 = mod[3:4], mod[4:5]
    u2 = _rms_mod(h1, nrm_ref[1:2], sc2, sh2)
    bgrp, gates, selmask = yield from _route(u2, wrt_ref, br_ref)
    yield

    onehot = jnp.concatenate([jnp.where(selmask, 1.0, 0.0), jnp.zeros((8 - N_GROUPS, TB), F32)],
                             axis=0).astype(BF16)
    rank_incl = jnp.dot(onehot, tri_ref[...], preferred_element_type=F32)
    rank = jnp.sum(jnp.where(selmask, rank_incl[0:N_GROUPS], 0.0), axis=0, keepdims=True) - 1.0
    cnt = lax.dot_general(jnp.ones((8, TB), BF16), onehot, NT_DIMS,
                          preferred_element_type=F32)
    sub_rows = jnp.floor((cnt[0:1, :] + (SUB - 1.0)) * (1.0 / SUB)) * SUB
    lp = rank
    first_row = jnp.zeros((1, 1), F32)
    for g in range(1, N_GROUPS):
        first_row = first_row + sub_rows[:, g - 1:g]
        lp = lp + jnp.where(bgrp == g, first_row, 0.0)

    hi = [x.astype(BF16).astype(F32) for x in gates]
    lo = [(x - h).astype(BF16).astype(F32) for x, h in zip(gates, hi)]
    meta_src = jnp.concatenate(hi + lo + [lp, jnp.zeros((LANES - 9, TB), F32)], axis=0)
    meta_t = meta_src.T
    tokmeta_ref[b] = meta_t
    yield

    rowdata = jnp.concatenate([u2.astype(BF16), meta_t.astype(BF16)], axis=1)
    sort = _onehot(lax.broadcasted_iota(I32, (STG_P, TB), 0) == lp.astype(I32))
    stage_ref[b, slot] = jnp.dot(sort, rowdata, preferred_element_type=F32).astype(BF16)
    return [cnt[0, g].astype(I32) for g in range(N_GROUPS)]


def _tail_dma(t, n_steps, nb, cap, counts, tmeta_ref, xs_hbm, stage_ref, zeros_ref,
              run_ref, prev_ref, ssem, zsem):
    slot = t % 2
    run = [run_ref[g] for g in range(N_GROUPS)]
    junk = N_GROUPS * cap
    scatter_rows = []
    for b in range(nb):
        first_sub = [jnp.int32(0)]
        for g in range(N_GROUPS):
            first_sub.append(first_sub[-1] + (counts[b][g] + (SUB - 1)) // SUB)
        dst = [g * cap + run[g] for g in range(N_GROUPS)]
        rows_b, base = [], (t * nb + b) * META_W
        for s in range(N_SUB):
            grp_first, grp_dst = first_sub[0], dst[0]
            for g in range(1, N_GROUPS):
                later = s >= first_sub[g]
                grp_first = jnp.where(later, first_sub[g], grp_first)
                grp_dst = jnp.where(later, dst[g], grp_dst)
            row = grp_dst + (s - grp_first) * SUB
            used = s < first_sub[N_GROUPS]
            rows_b.append(jnp.where(used, row, junk + (b * N_SUB + s) * SUB))
            tmeta_ref[base + s] = jnp.where(used, row, row if s == 0 else first_row)
            if s == 0:
                first_row = row
        for k in range(N_SUB, META_W):
            tmeta_ref[base + k] = jnp.int32(0)
        for g in range(N_GROUPS):
            run[g] = run[g] + (first_sub[g + 1] - first_sub[g]) * SUB
        scatter_rows.append(rows_b)
    for g in range(N_GROUPS):
        run_ref[g] = run[g]

    def copies(b, slot_, rows):
        return _sub_copies(xs_hbm, stage_ref.at[b, slot_], ssem.at[b, slot_], rows, True)

    @pl.when(t > 0)
    def _():
        for b in range(nb):
            for cp in copies(b, 1 - slot, [prev_ref[b * N_SUB + s] for s in range(N_SUB)]):
                cp.wait()

    for b in range(nb):
        for cp in copies(b, slot, scatter_rows[b]):
            cp.start()
        for s in range(N_SUB):
            prev_ref[b * N_SUB + s] = scatter_rows[b][s]

    @pl.when(t == n_steps - 1)
    def _():
        for b in range(nb):
            for cp in copies(b, slot, scatter_rows[b]):
                cp.wait()
        zeros_ref[...] = jnp.zeros_like(zeros_ref)
        tails = []
        base = n_steps * nb * META_W
        for g in range(N_GROUPS):
            tmeta_ref[base + g] = run[g]
            start = pl.multiple_of(g * cap + run[g], SUB)
            tails.append(pltpu.make_async_copy(zeros_ref, xs_hbm.at[pl.ds(start, TBM), :], zsem))
        for k in range(N_GROUPS, META_W):
            tmeta_ref[base + k] = jnp.int32(0)
        for cp in tails:
            cp.start()
        for cp in tails:
            cp.wait()


def _attn_conv_mix(x, t, mod, nrm_ref, win_ref, wout_ref, bias_ref, sink_ref, convw_ref,
                   kvprev_ref, cprev_ref):
    tb = TB
    first = t == 0
    sh1, sc1, g1 = mod[0:1], mod[1:2], mod[2:3]
    u = _rms_mod(x, nrm_ref[0:1], sc1, sh1)
    proj = jnp.dot(u.astype(BF16), win_ref[...], preferred_element_type=F32)

    yield
    q = (proj[:, 0:ATTN_WIDTH] * HEAD_DIM ** -0.5).astype(BF16)
    kf = proj[:, 512:640]
    vf = proj[:, 640:768]
    bgate = proj[:, 768:1280]
    cgate = proj[:, 1280:1792]
    xv = proj[:, 1792:2304]

    kv_prev = kvprev_ref[...]
    kext = jnp.concatenate([kv_prev[:, 0:KV_WIDTH], kf], axis=0)
    vext = jnp.concatenate([kv_prev[:, KV_WIDTH:], vf], axis=0)
    kvprev_ref[:, 0:KV_WIDTH] = kf[tb - BLOCK:tb]
    kvprev_ref[:, KV_WIDTH:] = vf[tb - BLOCK:tb]

    lane = lax.broadcasted_iota(I32, kext.shape, 1)
    lo = lane < HEAD_DIM
    krol = pltpu.roll(kext, HEAD_DIM, axis=1)
    vrol = pltpu.roll(vext, HEAD_DIM, axis=1)
    zero = jnp.zeros_like(kext)
    one_at_64 = jnp.where(lane == HEAD_DIM, 1.0, 0.0)
    one_at_0 = jnp.where(lane == 0, 1.0, 0.0)
    k_ops = [(jnp.where(lo, kext, zero).astype(BF16), jnp.where(lo, zero, krol).astype(BF16)),
             (jnp.where(lo, krol, zero).astype(BF16), jnp.where(lo, zero, kext).astype(BF16))]
    v_ops = [(jnp.where(lo, vext, one_at_64).astype(BF16), jnp.where(lo, one_at_0, vrol).astype(BF16)),
             (jnp.where(lo, vrol, one_at_64).astype(BF16), jnp.where(lo, one_at_0, vext).astype(BF16))]

    col = lax.broadcasted_iota(I32, (2 * BLOCK, 4 * BLOCK), 1)
    prev_cols = (col % (2 * BLOCK)) < BLOCK
    upper = lax.broadcasted_iota(I32, (2 * BLOCK, 1), 0) < BLOCK
    lane_o = lax.broadcasted_iota(I32, (2 * BLOCK, LANES), 1)
    lo_o = lane_o < HEAD_DIM

    yield
    attn_rows = []
    for bi in range(tb // BLOCK):
        r0 = bi * BLOCK
        pair_out = []
        for kvh in range(N_HEADS // 4):
            ka, kb = k_ops[kvh]
            rhs = jnp.concatenate([ka[r0:r0 + 2 * BLOCK], kb[r0:r0 + 2 * BLOCK]], axis=0)
            qrows = q[r0:r0 + BLOCK]
            qp = jnp.concatenate([qrows[:, (2 * kvh) * LANES:(2 * kvh + 1) * LANES],
                                  qrows[:, (2 * kvh + 1) * LANES:(2 * kvh + 2) * LANES]], axis=0)
            s = lax.dot_general(qp, rhs, NT_DIMS, preferred_element_type=F32)
            bias = bias_ref[kvh]
            if bi == 0:
                bias = jnp.where(prev_cols & first, NEG_BIG, bias)
            s = s + bias
            outs = []
            for hh in range(2):
                sh = s[:, hh * 2 * BLOCK:(hh + 1) * 2 * BLOCK]
                sink = jnp.where(upper, sink_ref[4 * kvh + hh], sink_ref[4 * kvh + 2 + hh])
                m = jnp.maximum(jnp.max(sh, axis=-1, keepdims=True), sink)
                p = jnp.exp(sh - m).astype(BF16)
                vop = v_ops[kvh][hh][r0:r0 + 2 * BLOCK]
                o = jnp.dot(p, vop, preferred_element_type=F32)
                den_col = HEAD_DIM if hh == 0 else 0
                den = o[:, den_col:den_col + 1] + jnp.exp(sink - m)
                outs.append(o / den)
            both = jnp.where(lo_o, outs[0], outs[1])
            pair_out += [both[0:BLOCK], both[BLOCK:]]
            yield
        attn_rows.append(jnp.concatenate(pair_out, axis=1))
    attn = jnp.concatenate(attn_rows, axis=0)

    uc = cgate * xv
    cprev = jnp.where(first, 0.0, cprev_ref[...])
    row = lax.broadcasted_iota(I32, uc.shape, 0)
    r1 = jnp.where(row == 0, cprev[7:8], pltpu.roll(uc, 1, axis=0))
    r2 = jnp.where(row == 0, cprev[6:7], jnp.where(row == 1, cprev[7:8], pltpu.roll(uc, 2, axis=0)))
    cprev_ref[...] = uc[tb - 8:tb]
    cw = convw_ref[...]
    conv = bgate * (cw[0:1] * r2 + cw[1:2] * r1 + cw[2:3] * uc)
    yield

    mix = (jnp.dot(attn.astype(BF16), wout_ref[0:ATTN_WIDTH, :], preferred_element_type=F32)
           + jnp.dot(conv.astype(BF16), wout_ref[ATTN_WIDTH:, :], preferred_element_type=F32))
    return x + g1 * mix


def _pool_mix(x, t, mod, nrm_ref, wpool_ref, pscale_ref, uprev_ref):
    tb = TB
    sh1, sc1, g1 = mod[0:1], mod[1:2], mod[2:3]
    u = _rms_mod(x, nrm_ref[0:1], sc1, sh1)
    ext = jnp.concatenate([jnp.where(t == 0, 0.0, uprev_ref[...]), u], axis=0)
    uprev_ref[...] = u[tb - POOL_HALO:tb]

    yield
    pos = (t * tb + 1 + lax.broadcasted_iota(I32, (tb, 1), 0)).astype(F32)
    mixed = []
    for gi, w in enumerate(POOL_SIZES):
        sl = slice(gi * POOL_GROUP, (gi + 1) * POOL_GROUP)
        acc = ext[:, sl]
        shift = 1
        while shift < w:
            acc = acc + pltpu.roll(acc, shift, axis=0)
            shift *= 2
        mean = acc[POOL_HALO:] / jnp.minimum(pos, float(w))
        pooled = mean - u[:, sl]
        mixed.append(jnp.dot(pooled.astype(BF16), wpool_ref[gi].astype(BF16),
                             preferred_element_type=F32))
        yield
    mix = jnp.concatenate(mixed, axis=1) * pscale_ref[0:1]
    return x + g1 * mix


def _interleave(chains, lag):
    results = [None] * len(chains)
    live = list(range(len(chains)))
    rnd = 0
    while live:
        for k in list(live):
            if rnd < k * lag:
                continue
            try:
                next(chains[k])
            except StopIteration as done:
                results[k] = done.value
                live.remove(k)
        rnd += 1
    return results


def _mixer_kernel(*refs, kind, has_head, nb, n_steps, cap):
    refs = list(refs)
    t = pl.program_id(0)
    slot = t % 2
    if has_head:
        pm_ref = refs.pop(0)
        hprev_ref, modprev_ref, tokprev_ref, ys_hbm = refs[:4]
        refs = refs[4:]
    else:
        h_ref = refs.pop(0)
    mod_ref, nrm_ref = refs[:2]
    refs = refs[2:]
    n_w = 5 if kind == "even" else 2
    weight_refs = refs[:n_w]
    refs = refs[n_w:]
    wrt_ref, br_ref, tri_ref = refs[:3]
    refs = refs[3:]
    hout_ref, tokmeta_ref, tmeta_ref, xs_hbm = refs[:4]
    refs = refs[4:]
    n_c = 2 if kind == "even" else 1
    carry_refs = refs[:n_c]
    refs = refs[n_c:]
    if kind == "even":
        cast_refs = refs[:2]
        refs = refs[2:]
    stage_ref, zeros_ref, run_ref, prev_ref, ssem, zsem = refs[:6]
    refs = refs[6:]

    @pl.when(t == 0)
    def _():
        for r in carry_refs:
            r[...] = jnp.zeros_like(r)
        for g in range(N_GROUPS):
            run_ref[g] = jnp.int32(0)
        if kind == "even":
            for src, dst in zip(weight_refs[:2], cast_refs):
                for c0 in range(0, src.shape[1], 2 * LANES):
                    dst[:, c0:c0 + 2 * LANES] = src[:, c0:c0 + 2 * LANES].astype(BF16)

    if kind == "even":
        weight_refs = list(cast_refs) + list(weight_refs[2:])

    if has_head:
        ystage_ref, gsem = refs
        _head_dma(t, n_steps, nb, pm_ref, ys_hbm, ystage_ref, gsem)

    def chain(b):
        if has_head:
            x = _head_combine(b, slot, hprev_ref, modprev_ref, tokprev_ref, ystage_ref)
            yield
        else:
            x = h_ref[b]
        mod = mod_ref[b]
        carries = [r.at[b] for r in carry_refs]
        mix = _attn_conv_mix if kind == "even" else _pool_mix
        h1 = yield from mix(x, t, mod, nrm_ref, *weight_refs, *carries)
        hout_ref[b] = h1
        yield
        return (yield from _tail_compute(b, slot, h1, mod, nrm_ref, wrt_ref, br_ref, tri_ref,
                                         tokmeta_ref, stage_ref))

    counts = _interleave([chain(b) for b in range(nb)], CHAIN_LAG)
    _tail_dma(t, n_steps, nb, cap, counts, tmeta_ref, xs_hbm, stage_ref, zeros_ref,
              run_ref, prev_ref, ssem, zsem)


def _nat(nb, cols):
    return pl.BlockSpec((nb, TB, cols), lambda t, *_: (0, t, 0))


def _full(shape):
    nd = len(shape)
    return pl.BlockSpec(shape, lambda t, *_: (0,) * nd)


def _head_scratch(nb):
    return [pltpu.VMEM((nb, 2, STG_C, D_MODEL), BF16), pltpu.SemaphoreType.DMA((nb, 2))]


def _mixer_call(kind, j, head, h, mod8, nrm8, weights, wrt, br, tri):
    nb, seq, _ = (h if head is None else head[1]).shape
    n_steps = seq // TB
    n_tiles, cap, _ = _sizes(nb * seq)
    has_head = head is not None
    any_spec = pl.BlockSpec(memory_space=pl.ANY)
    smem_spec = pl.BlockSpec(memory_space=pltpu.SMEM)

    if has_head:
        args = list(head[1:])
        in_specs = [_nat(nb, D_MODEL), _full((nb, 8, D_MODEL)), _nat(nb, LANES), any_spec]
    else:
        args = [h]
        in_specs = [_nat(nb, D_MODEL)]
    args += [mod8, nrm8]
    in_specs += [_full((nb, 8, D_MODEL)), _full((8, D_MODEL))]
    if kind == "even":
        args += list(weights)
        once = dict(pipeline_mode=pl.Buffered(1))
        in_specs += [pl.BlockSpec((None, D_MODEL, IN_WIDTH), lambda t, *_: (j, 0, 0), **once),
                     pl.BlockSpec((None, D_MODEL, D_MODEL), lambda t, *_: (j, 0, 0), **once),
                     _full((N_HEADS // 4, 2 * BLOCK, 4 * BLOCK)), smem_spec, _full((8, CONV_WIDTH))]
        mix_scratch = [pltpu.VMEM((nb, BLOCK, 2 * KV_WIDTH), F32),
                       pltpu.VMEM((nb, 8, CONV_WIDTH), F32),
                       pltpu.VMEM((D_MODEL, IN_WIDTH), BF16),
                       pltpu.VMEM((D_MODEL, D_MODEL), BF16)]
    else:
        args += list(weights)
        in_specs += [pl.BlockSpec((None, len(POOL_SIZES), POOL_GROUP, POOL_GROUP),
                                  lambda t, *_: (j, 0, 0, 0)), _full((8, D_MODEL))]
        mix_scratch = [pltpu.VMEM((nb, POOL_HALO, D_MODEL), F32)]
    args += [wrt, br, tri]
    in_specs += [_full((2 * N_EXPERTS, D_MODEL)), _full((N_EXPERTS, 1)), _full((TB, TB))]

    out_shape = (jax.ShapeDtypeStruct((nb, seq, D_MODEL), F32),
                 jax.ShapeDtypeStruct((nb, seq, LANES), F32),
                 jax.ShapeDtypeStruct(((n_tiles + 1) * META_W,), I32),
                 jax.ShapeDtypeStruct((N_GROUPS * cap + nb * N_SUB * SUB, ROW_W), BF16))
    out_specs = (_nat(nb, D_MODEL), _nat(nb, LANES), smem_spec, any_spec)
    scratch = mix_scratch + [
        pltpu.VMEM((nb, 2, STG_P, ROW_W), BF16),
        pltpu.VMEM((TBM, ROW_W), BF16),
        pltpu.SMEM((N_GROUPS,), I32),
        pltpu.SMEM((nb * N_SUB,), I32),
        pltpu.SemaphoreType.DMA((nb, 2)),
        pltpu.SemaphoreType.DMA(()),
    ]
    if has_head:
        scratch += _head_scratch(nb)

    body = functools.partial(_mixer_kernel, kind=kind, has_head=has_head, nb=nb,
                             n_steps=n_steps, cap=cap)
    grid_spec = pltpu.PrefetchScalarGridSpec(
        num_scalar_prefetch=1 if has_head else 0, grid=(n_steps,),
        in_specs=in_specs, out_specs=out_specs, scratch_shapes=scratch)
    call = pl.pallas_call(
        body, grid_spec=grid_spec, out_shape=out_shape,
        compiler_params=pltpu.CompilerParams(
            dimension_semantics=("arbitrary",), vmem_limit_bytes=VMEM_LIMIT),
        name=kind + "_mixer")
    if has_head:
        return call(head[0], *args)
    return call(*args)


def _moe_kernel(blk_ref, grp_ref, val_ref, new_ref, xs_ref, wg_ref, wu_ref, wd_ref, ys_ref,
                wg_s, wu_s, wd_s):
    i = pl.program_id(0)

    @pl.when(new_ref[i] == 1)
    def _():
        for e in range(EXPERTS_PER_GROUP):
            wg_s[e] = wg_ref[e].astype(BF16)
            wu_s[e] = wu_ref[e].astype(BF16)
        wd_s[...] = wd_ref[0].astype(BF16)

    @pl.when(val_ref[i] == 1)
    def _():
        xb = xs_ref[:, 0:D_MODEL]
        meta = xs_ref[:, D_MODEL:ROW_W].astype(F32)
        gates = meta[:, 0:EXPERTS_PER_GROUP] + meta[:, EXPERTS_PER_GROUP:2 * EXPERTS_PER_GROUP]
        parts = []
        for e in range(EXPERTS_PER_GROUP):
            a = jnp.dot(xb, wg_s[e], preferred_element_type=F32)
            bu = jnp.dot(xb, wu_s[e], preferred_element_type=F32)
            hid = (a * _sigmoid(a)) * bu
            parts.append((hid * gates[:, e:e + 1]).astype(BF16))
        hid_all = jnp.concatenate(parts, axis=1)
        ys_ref[...] = jnp.dot(hid_all, wd_s[...], preferred_element_type=F32).astype(BF16)


def _work_tables(totals, cap, n_work):
    ntile = (totals + (TBM - 1)) // TBM
    ends = jnp.cumsum(ntile)
    starts = ends - ntile
    nvalid = ends[-1]
    idx = jnp.arange(n_work, dtype=I32)
    idc = jnp.maximum(jnp.minimum(idx, nvalid - 1), 0)
    grp = jnp.minimum(jnp.sum((idc[:, None] >= ends[None, :]).astype(I32), axis=1), N_GROUPS - 1)
    blk = grp * (cap // TBM) + idc - starts[grp]
    valid = idx < nvalid
    new = valid & ((idx == 0) | (grp != jnp.roll(grp, 1)))
    return blk.astype(I32), grp.astype(I32), valid.astype(I32), new.astype(I32)


def _moe_call(xs, tables, layer, wg, wu, wd, n_work):
    row_map = lambda i, blk, grp, val, new: (blk[i], 0)
    grp_map = lambda i, blk, grp, val, new: (layer, grp[i], 0, 0)
    grid_spec = pltpu.PrefetchScalarGridSpec(
        num_scalar_prefetch=4, grid=(n_work,),
        in_specs=[
            pl.BlockSpec((TBM, ROW_W), row_map),
            pl.BlockSpec((None, EXPERTS_PER_GROUP, D_MODEL, EXPERT_FF), grp_map),
            pl.BlockSpec((None, EXPERTS_PER_GROUP, D_MODEL, EXPERT_FF), grp_map),
            pl.BlockSpec((None, 1, GROUP_FF, D_MODEL), grp_map),
        ],
        out_specs=pl.BlockSpec((TBM, D_MODEL), row_map),
        scratch_shapes=[pltpu.VMEM((EXPERTS_PER_GROUP, D_MODEL, EXPERT_FF), BF16),
                        pltpu.VMEM((EXPERTS_PER_GROUP, D_MODEL, EXPERT_FF), BF16),
                        pltpu.VMEM((GROUP_FF, D_MODEL), BF16)],
    )
    return pl.pallas_call(
        _moe_kernel, grid_spec=grid_spec,
        out_shape=jax.ShapeDtypeStruct((xs.shape[0], D_MODEL), BF16),
        compiler_params=pltpu.CompilerParams(
            dimension_semantics=("arbitrary",), vmem_limit_bytes=VMEM_LIMIT),
        name="moe_experts",
    )(*tables, xs, wg, wu, wd)


def _final_kernel(pm_ref, hprev_ref, modprev_ref, tokprev_ref, ys_hbm, nf_ref, o_ref,
                  ystage_ref, gsem, *, nb, n_steps):
    t = pl.program_id(0)
    _head_dma(t, n_steps, nb, pm_ref, ys_hbm, ystage_ref, gsem)
    for b in range(nb):
        h = _head_combine(b, t % 2, hprev_ref, modprev_ref, tokprev_ref, ystage_ref)
        ms = jnp.mean(h * h, axis=-1, keepdims=True)
        o_ref[b] = (h * lax.rsqrt(ms + EPS)) * nf_ref[0:1]


def _final_call(head, nf8):
    pm, hprev, modprev, tokprev, ys = head
    nb, seq, _ = hprev.shape
    n_steps = seq // TB
    grid_spec = pltpu.PrefetchScalarGridSpec(
        num_scalar_prefetch=1, grid=(n_steps,),
        in_specs=[_nat(nb, D_MODEL), _full((nb, 8, D_MODEL)), _nat(nb, LANES),
                  pl.BlockSpec(memory_space=pl.ANY), _full((8, D_MODEL))],
        out_specs=_nat(nb, D_MODEL),
        scratch_shapes=_head_scratch(nb))
    return pl.pallas_call(
        functools.partial(_final_kernel, nb=nb, n_steps=n_steps), grid_spec=grid_spec,
        out_shape=jax.ShapeDtypeStruct((nb, seq, D_MODEL), F32),
        compiler_params=pltpu.CompilerParams(
            dimension_semantics=("arbitrary",), vmem_limit_bytes=VMEM_LIMIT),
        name="final_norm",
    )(pm, hprev, modprev, tokprev, ys, nf8)


def _pad_rows(a, rows=8):
    return jnp.pad(a, ((0, rows - a.shape[0]), (0, 0)))


def _attn_bias_table():
    slopes = 2.0 ** (-8.0 * (jnp.arange(N_HEADS, dtype=F32) + 1.0) / N_HEADS)
    dist = (jnp.arange(BLOCK)[:, None] + BLOCK) - jnp.arange(2 * BLOCK)[None, :]
    ok = (dist >= 0) & (dist < WINDOW)
    per_head = jnp.where(ok[None], -slopes[:, None, None] * dist.astype(F32)[None], NEG_BIG)
    return per_head.reshape(N_HEADS // 2, 2, BLOCK, 2 * BLOCK).transpose(0, 2, 1, 3).reshape(
        N_HEADS // 4, 2 * BLOCK, 4 * BLOCK)


def _router_operands(w_router, b_router):
    def reorder(a):
        return a.reshape(N_GROUPS, EXPERTS_PER_GROUP, -1).transpose(1, 0, 2).reshape(N_EXPERTS, -1)
    w = reorder(w_router.T)
    wh = w.astype(BF16)
    wl = (w - wh.astype(F32)).astype(BF16)
    return jnp.concatenate([wh, wl], axis=0), reorder(b_router[:, None])


def kernel(x, c, w_ada, b_ada, norm_mix, norm_ffn, w_in, w_out, sinks, conv_w, w_pool,
           pool_scale, w_router, b_router, w_gate, w_up, w_down, norm_final):
    b, s, _ = x.shape
    n_tiles, cap, n_work = _sizes(b * s)

    mod_all = _mod_call(_pad_rows(c), w_ada, b_ada.reshape(DEPTH, 1, N_MOD * D_MODEL))
    bias_tab = _attn_bias_table()
    wrt, br = _router_operands(w_router, b_router)
    nf8 = _pad_rows(norm_final[None, :])
    tri = jnp.triu(jnp.ones((TB, TB), F32)).astype(BF16)

    h = x
    head = None
    for layer in range(DEPTH):
        j = layer // 2
        mod8 = jnp.pad(mod_all[layer, :b].reshape(b, N_MOD, D_MODEL), ((0, 0), (0, 2), (0, 0)))
        nrm8 = _pad_rows(jnp.stack([norm_mix[layer], norm_ffn[layer]]))
        if layer % 2 == 0:
            kind = "even"
            weights = (w_in, w_out, bias_tab, sinks[j], _pad_rows(conv_w[j]))
        else:
            kind = "odd"
            weights = (w_pool, _pad_rows(pool_scale[j][None, :]))
        h1, tokmeta, tmeta, xs = _mixer_call(kind, j, head, h, mod8, nrm8, weights, wrt, br, tri)
        totals = tmeta[n_tiles * META_W:n_tiles * META_W + N_GROUPS]
        ys = _moe_call(xs, _work_tables(totals, cap, n_work), layer, w_gate, w_up,
                       w_down.reshape(DEPTH, N_GROUPS, GROUP_FF, D_MODEL), n_work)
        head = (tmeta, h1, mod8, tokmeta, ys)
    return _final_call(head, nf8)
```

```python
import functools

import jax
import jax.numpy as jnp
from jax import lax
from jax.experimental import pallas as pl
from jax.experimental.pallas import tpu as pltpu

F32 = jnp.float32
BF16 = jnp.bfloat16
I32 = jnp.int32

D_MODEL = 1024
DEPTH = 4
EPS = 1e-6
N_MOD = 6

ATTN_WIDTH = 512
HEAD_DIM = 64
N_HEADS = 8
KV_WIDTH = 128
WINDOW = 128
BLOCK = 128
CONV_WIDTH = 512
IN_WIDTH = 2304

POOL_SIZES = (2, 4, 8, 16)
POOL_GROUP = 256
POOL_HALO = 16

N_EXPERTS = 16
N_GROUPS = 4
EXPERTS_PER_GROUP = 4
EXPERT_FF = 256
GROUP_FF = EXPERTS_PER_GROUP * EXPERT_FF

LANES = 128
NEG_BIG = -1e30

TB = 256
SUB = 16
N_SUB = TB // SUB + N_GROUPS - 1
STG_P = 320
STG_C = 384
TBM = 512
ROW_W = D_MODEL + LANES
META_W = 32
LP_LANE = 8
CHAIN_LAG = 0
MOD_COLS = 3072
VMEM_LIMIT = 56 * 1024 * 1024
NT_DIMS = (((1,), (1,)), ((), ()))

assert N_SUB * SUB <= STG_P <= STG_C and STG_C % LANES == 0 and N_SUB <= META_W


def _sizes(n_tok):
    n_tiles = n_tok // TB
    cap = -(-(n_tok + n_tiles * (SUB - 1) + TBM) // TBM) * TBM
    n_work = (n_tok + n_tiles * N_GROUPS * (SUB - 1)) // TBM + N_GROUPS
    return n_tiles, cap, n_work


def _sigmoid(x):
    return 1.0 / (1.0 + jnp.exp(-x))


def _rms_mod(x, g, sc, sh):
    ms = jnp.mean(x * x, axis=-1, keepdims=True)
    return (x * lax.rsqrt(ms + EPS)) * (g * (1.0 + sc)) + sh


def _onehot(cond):
    return jnp.where(cond, 1.0, 0.0).astype(BF16)


def _mod_kernel(c_ref, w_ref, b_ref, o_ref):
    c = c_ref[...]
    cond = c * _sigmoid(c)
    o_ref[0] = jnp.dot(cond, w_ref[0], precision=lax.Precision.HIGHEST,
                       preferred_element_type=F32) + b_ref[0]


def _mod_call(c8, w_ada, b_ada3):
    n_col = (N_MOD * D_MODEL) // MOD_COLS
    return pl.pallas_call(
        _mod_kernel,
        grid=(DEPTH, n_col),
        in_specs=[
            pl.BlockSpec((8, D_MODEL), lambda l, j: (0, 0)),
            pl.BlockSpec((1, D_MODEL, MOD_COLS), lambda l, j: (l, 0, j)),
            pl.BlockSpec((1, 1, MOD_COLS), lambda l, j: (l, 0, j)),
        ],
        out_specs=pl.BlockSpec((1, 8, MOD_COLS), lambda l, j: (l, 0, j)),
        out_shape=jax.ShapeDtypeStruct((DEPTH, 8, N_MOD * D_MODEL), F32),
        compiler_params=pltpu.CompilerParams(
            dimension_semantics=("arbitrary", "arbitrary"), vmem_limit_bytes=VMEM_LIMIT),
        name="adaln_mod",
    )(c8, w_ada, b_ada3)


def _sub_copies(hbm_ref, stage_ref, sem, rows, to_hbm):
    out = []
    for s in range(N_SUB):
        s_view = stage_ref.at[pl.ds(s * SUB, SUB), :]
        h_view = hbm_ref.at[pl.ds(pl.multiple_of(rows[s], SUB), SUB), :]
        out.append(pltpu.make_async_copy(s_view, h_view, sem) if to_hbm
                   else pltpu.make_async_copy(h_view, s_view, sem))
    return out


def _head_dma(t, n_steps, nb, pm_ref, ys_hbm, ystage_ref, gsem):
    slot = t % 2

    def copies(step, b, slot_):
        base = (step * nb + b) * META_W
        rows = [pm_ref[base + s] for s in range(N_SUB)]
        return _sub_copies(ys_hbm, ystage_ref.at[b, slot_], gsem.at[b, slot_], rows, False)

    @pl.when(t == 0)
    def _():
        ystage_ref[...] = jnp.zeros_like(ystage_ref)
        for b in range(nb):
            for cp in copies(0, b, 0):
                cp.start()

    nxt = jnp.minimum(t + 1, n_steps - 1)
    for b in range(nb):
        for cp in copies(nxt, b, 1 - slot):
            cp.start()

    for b in range(nb):
        for cp in copies(t, b, slot):
            cp.wait()

    def drain():
        for b in range(nb):
            for cp in copies(nxt, b, 1 - slot):
                cp.wait()
    return drain


def _head_combine(b, slot, hprev_ref, modprev_ref, tokmeta_ref, ystage_ref):
    lp = tokmeta_ref[b, :, LP_LANE:LP_LANE + 1].astype(I32)
    unsort = _onehot(lax.broadcasted_iota(I32, (TB, STG_C), 1) == lp)
    y = jnp.dot(unsort, ystage_ref[b, slot], preferred_element_type=F32)
    return hprev_ref[b] + modprev_ref[b, 5:6] * y


def _route(u2, wrt_ref, br_ref):
    uh = u2.astype(BF16)
    ul = (u2 - uh.astype(F32)).astype(BF16)
    w = wrt_ref[...]
    r1 = lax.dot_general(w, uh, NT_DIMS, preferred_element_type=F32)
    r2 = lax.dot_general(w[0:N_EXPERTS], ul, NT_DIMS, preferred_element_type=F32)
    yield
    scores = _sigmoid(r1[0:N_EXPERTS] + r1[N_EXPERTS:] + r2)
    biased = scores + br_ref[...]
    s = [scores[4 * j:4 * j + 4] for j in range(EXPERTS_PER_GROUP)]
    b = [biased[4 * j:4 * j + 4] for j in range(EXPERTS_PER_GROUP)]

    hi1, lo1 = jnp.maximum(b[0], b[1]), jnp.minimum(b[0], b[1])
    hi2, lo2 = jnp.maximum(b[2], b[3]), jnp.minimum(b[2], b[3])
    second = jnp.maximum(jnp.minimum(hi1, hi2), jnp.where(hi1 >= hi2, lo1, lo2))
    gscore = jnp.maximum(hi1, hi2) + second

    best = gscore[0:1]
    bgrp = jnp.zeros((1, TB), I32)
    for g in range(1, N_GROUPS):
        better = gscore[g:g + 1] > best
        bgrp = jnp.where(better, g, bgrp)
        best = jnp.where(better, gscore[g:g + 1], best)
    selmask = lax.broadcasted_iota(I32, (N_GROUPS, TB), 0) == bgrp
    yield

    m1 = b[0]
    i1 = jnp.zeros((N_GROUPS, TB), I32)
    for j in range(1, EXPERTS_PER_GROUP):
        gt = b[j] > m1
        i1 = jnp.where(gt, j, i1)
        m1 = jnp.where(gt, b[j], m1)
    cands = [jnp.where(i1 == j, -jnp.inf, b[j]) for j in range(EXPERTS_PER_GROUP)]
    m2 = cands[0]
    i2 = jnp.zeros_like(i1)
    for j in range(1, EXPERTS_PER_GROUP):
        gt = cands[j] > m2
        i2 = jnp.where(gt, j, i2)
        m2 = jnp.where(gt, cands[j], m2)
    w1 = s[0]
    w2 = s[0]
    for j in range(1, EXPERTS_PER_GROUP):
        w1 = jnp.where(i1 == j, s[j], w1)
        w2 = jnp.where(i2 == j, s[j], w2)
    yield
    tot = w1 + w2
    w1n = w1 / tot
    w2n = w2 / tot
    gates = []
    for j in range(EXPERTS_PER_GROUP):
        gj = jnp.where(i1 == j, w1n, 0.0) + jnp.where(i2 == j, w2n, 0.0)
        gates.append(jnp.sum(jnp.where(selmask, gj, 0.0), axis=0, keepdims=True))
    return bgrp, gates, selmask


def _tail_compute(b, slot, h1, mod, nrm_ref, wrt_ref, br_ref, tri_ref, tokmeta_ref, stage_ref):
    sh2, sc2 = mod[3:4], mod[4:5]
    u2 = _rms_mod(h1, nrm_ref[1:2], sc2, sh2)
    bgrp, gates, selmask = yield from _route(u2, wrt_ref, br_ref)
    yield

    onehot = jnp.concatenate([jnp.where(selmask, 1.0, 0.0), jnp.zeros((8 - N_GROUPS, TB), F32)],
                             axis=0).astype(BF16)
    rank_incl = jnp.dot(onehot, tri_ref[...], preferred_element_type=F32)
    rank = jnp.sum(jnp.where(selmask, rank_incl[0:N_GROUPS], 0.0), axis=0, keepdims=True) - 1.0
    cnt = lax.dot_general(jnp.ones((8, TB), BF16), onehot, NT_DIMS,
                          preferred_element_type=F32)
    sub_rows = jnp.floor((cnt[0:1, :] + (SUB - 1.0)) * (1.0 / SUB)) * SUB
    lp = rank
    first_row = jnp.zeros((1, 1), F32)
    for g in range(1, N_GROUPS):
        first_row = first_row + sub_rows[:, g - 1:g]
        lp = lp + jnp.where(bgrp == g, first_row, 0.0)

    hi = [x.astype(BF16).astype(F32) for x in gates]
    lo = [(x - h).astype(BF16).astype(F32) for x, h in zip(gates, hi)]
    meta_src = jnp.concatenate(hi + lo + [lp, jnp.zeros((LANES - 9, TB), F32)], axis=0)
    meta_t = meta_src.T
    tokmeta_ref[b] = meta_t
    yield

    rowdata = jnp.concatenate([u2.astype(BF16), meta_t.astype(BF16)], axis=1)
    sort = _onehot(lax.broadcasted_iota(I32, (STG_P, TB), 0) == lp.astype(I32))
    stage_ref[b, slot] = jnp.dot(sort, rowdata, preferred_element_type=F32).astype(BF16)
    return [cnt[0, g].astype(I32) for g in range(N_GROUPS)]


def _prime_scatter(nb, cap, xs_hbm, stage_ref, prev_ref, ssem):
    stage_ref[:, 1] = jnp.zeros((nb,) + stage_ref.shape[2:], BF16)
    for b in range(nb):
        rows = [jnp.int32(N_GROUPS * cap + (b * N_SUB + s) * SUB) for s in range(N_SUB)]
        for s in range(N_SUB):
            prev_ref[b * N_SUB + s] = rows[s]
        for cp in _sub_copies(xs_hbm, stage_ref.at[b, 1], ssem.at[b, 1], rows, True):
            cp.start()


def _tail_dma(t, n_steps, nb, cap, counts, tmeta_ref, xs_hbm, stage_ref, zeros_ref,
              run_ref, prev_ref, ssem, zsem, drain_head):
    slot = t % 2
    run = [run_ref[g] for g in range(N_GROUPS)]
    junk = N_GROUPS * cap
    scatter_rows = []
    for b in range(nb):
        first_sub = [jnp.int32(0)]
        for g in range(N_GROUPS):
            first_sub.append(first_sub[-1] + (counts[b][g] + (SUB - 1)) // SUB)
        dst = [g * cap + run[g] for g in range(N_GROUPS)]
        rows_b, base = [], (t * nb + b) * META_W
        for s in range(N_SUB):
            grp_first, grp_dst = first_sub[0], dst[0]
            for g in range(1, N_GROUPS):
                later = s >= first_sub[g]
                grp_first = jnp.where(later, first_sub[g], grp_first)
                grp_dst = jnp.where(later, dst[g], grp_dst)
            row = grp_dst + (s - grp_first) * SUB
            used = s < first_sub[N_GROUPS]
            rows_b.append(jnp.where(used, row, junk + (b * N_SUB + s) * SUB))
            tmeta_ref[base + s] = jnp.where(used, row, row if s == 0 else first_row)
            if s == 0:
                first_row = row
        for k in range(N_SUB, META_W):
            tmeta_ref[base + k] = jnp.int32(0)
        for g in range(N_GROUPS):
            run[g] = run[g] + (first_sub[g + 1] - first_sub[g]) * SUB
        scatter_rows.append(rows_b)
    for g in range(N_GROUPS):
        run_ref[g] = run[g]

    def copies(b, slot_, rows):
        return _sub_copies(xs_hbm, stage_ref.at[b, slot_], ssem.at[b, slot_], rows, True)

    for b in range(nb):
        for cp in copies(b, 1 - slot, [prev_ref[b * N_SUB + s] for s in range(N_SUB)]):
            cp.wait()

    for b in range(nb):
        for cp in copies(b, slot, scatter_rows[b]):
            cp.start()
        for s in range(N_SUB):
            prev_ref[b * N_SUB + s] = scatter_rows[b][s]

    @pl.when(t == n_steps - 1)
    def _():
        for b in range(nb):
            for cp in copies(b, slot, scatter_rows[b]):
                cp.wait()
        if drain_head is not None:
            drain_head()
        zeros_ref[...] = jnp.zeros_like(zeros_ref)
        tails = []
        base = n_steps * nb * META_W
        for g in range(N_GROUPS):
            tmeta_ref[base + g] = run[g]
            start = pl.multiple_of(g * cap + run[g], SUB)
            tails.append(pltpu.make_async_copy(zeros_ref, xs_hbm.at[pl.ds(start, TBM), :], zsem))
        for k in range(N_GROUPS, META_W):
            tmeta_ref[base + k] = jnp.int32(0)
        for cp in tails:
            cp.start()
        for cp in tails:
            cp.wait()


def _attn_conv_mix(x, t, mod, nrm_ref, win_ref, wout_ref, bias_ref, sink_ref, convw_ref,
                   kvprev_ref, cprev_ref):
    tb = TB
    first = t == 0
    sh1, sc1, g1 = mod[0:1], mod[1:2], mod[2:3]
    u = _rms_mod(x, nrm_ref[0:1], sc1, sh1)
    proj = jnp.dot(u.astype(BF16), win_ref[...], preferred_element_type=F32)

    yield
    q = (proj[:, 0:ATTN_WIDTH] * HEAD_DIM ** -0.5).astype(BF16)
    kf = proj[:, 512:640]
    vf = proj[:, 640:768]
    bgate = proj[:, 768:1280]
    cgate = proj[:, 1280:1792]
    xv = proj[:, 1792:2304]

    kv_prev = kvprev_ref[...]
    kext = jnp.concatenate([kv_prev[:, 0:KV_WIDTH], kf], axis=0)
    vext = jnp.concatenate([kv_prev[:, KV_WIDTH:], vf], axis=0)
    kvprev_ref[:, 0:KV_WIDTH] = kf[tb - BLOCK:tb]
    kvprev_ref[:, KV_WIDTH:] = vf[tb - BLOCK:tb]

    lane = lax.broadcasted_iota(I32, kext.shape, 1)
    lo = lane < HEAD_DIM
    krol = pltpu.roll(kext, HEAD_DIM, axis=1)
    vrol = pltpu.roll(vext, HEAD_DIM, axis=1)
    zero = jnp.zeros_like(kext)
    one_at_64 = jnp.where(lane == HEAD_DIM, 1.0, 0.0)
    one_at_0 = jnp.where(lane == 0, 1.0, 0.0)
    k_ops = [(jnp.where(lo, kext, zero).astype(BF16), jnp.where(lo, zero, krol).astype(BF16)),
             (jnp.where(lo, krol, zero).astype(BF16), jnp.where(lo, zero, kext).astype(BF16))]
    v_ops = [(jnp.where(lo, vext, one_at_64).astype(BF16), jnp.where(lo, one_at_0, vrol).astype(BF16)),
             (jnp.where(lo, vrol, one_at_64).astype(BF16), jnp.where(lo, one_at_0, vext).astype(BF16))]

    col = lax.broadcasted_iota(I32, (2 * BLOCK, 4 * BLOCK), 1)
    prev_cols = (col % (2 * BLOCK)) < BLOCK
    upper = lax.broadcasted_iota(I32, (2 * BLOCK, 1), 0) < BLOCK
    lane_o = lax.broadcasted_iota(I32, (2 * BLOCK, LANES), 1)
    lo_o = lane_o < HEAD_DIM

    yield
    attn_rows = []
    for bi in range(tb // BLOCK):
        r0 = bi * BLOCK
        pair_out = []
        for kvh in range(N_HEADS // 4):
            ka, kb = k_ops[kvh]
            rhs = jnp.concatenate([ka[r0:r0 + 2 * BLOCK], kb[r0:r0 + 2 * BLOCK]], axis=0)
            qrows = q[r0:r0 + BLOCK]
            qp = jnp.concatenate([qrows[:, (2 * kvh) * LANES:(2 * kvh + 1) * LANES],
                                  qrows[:, (2 * kvh + 1) * LANES:(2 * kvh + 2) * LANES]], axis=0)
            s = lax.dot_general(qp, rhs, NT_DIMS, preferred_element_type=F32)
            bias = bias_ref[kvh]
            if bi == 0:
                bias = jnp.where(prev_cols & first, NEG_BIG, bias)
            s = s + bias
            outs = []
            for hh in range(2):
                sh = s[:, hh * 2 * BLOCK:(hh + 1) * 2 * BLOCK]
                sink = jnp.where(upper, sink_ref[4 * kvh + hh], sink_ref[4 * kvh + 2 + hh])
                m = jnp.maximum(jnp.max(sh, axis=-1, keepdims=True), sink)
                p = jnp.exp(sh - m).astype(BF16)
                vop = v_ops[kvh][hh][r0:r0 + 2 * BLOCK]
                o = jnp.dot(p, vop, preferred_element_type=F32)
                den_col = HEAD_DIM if hh == 0 else 0
                den = o[:, den_col:den_col + 1] + jnp.exp(sink - m)
                outs.append(o / den)
            both = jnp.where(lo_o, outs[0], outs[1])
            pair_out += [both[0:BLOCK], both[BLOCK:]]
            yield
        attn_rows.append(jnp.concatenate(pair_out, axis=1))
    attn = jnp.concatenate(attn_rows, axis=0)

    uc = cgate * xv
    cprev = jnp.where(first, 0.0, cprev_ref[...])
    row = lax.broadcasted_iota(I32, uc.shape, 0)
    r1 = jnp.where(row == 0, cprev[7:8], pltpu.roll(uc, 1, axis=0))
    r2 = jnp.where(row == 0, cprev[6:7], jnp.where(row == 1, cprev[7:8], pltpu.roll(uc, 2, axis=0)))
    cprev_ref[...] = uc[tb - 8:tb]
    cw = convw_ref[...]
    conv = bgate * (cw[0:1] * r2 + cw[1:2] * r1 + cw[2:3] * uc)
    yield

    mix = (jnp.dot(attn.astype(BF16), wout_ref[0:ATTN_WIDTH, :], preferred_element_type=F32)
           + jnp.dot(conv.astype(BF16), wout_ref[ATTN_WIDTH:, :], preferred_element_type=F32))
    return x + g1 * mix


def _pool_mix(x, t, mod, nrm_ref, wpool_ref, pscale_ref, uprev_ref):
    tb = TB
    sh1, sc1, g1 = mod[0:1], mod[1:2], mod[2:3]
    u = _rms_mod(x, nrm_ref[0:1], sc1, sh1)
    ext = jnp.concatenate([jnp.where(t == 0, 0.0, uprev_ref[...]), u], axis=0)
    uprev_ref[...] = u[tb - POOL_HALO:tb]

    yield
    pos = (t * tb + 1 + lax.broadcasted_iota(I32, (tb, 1), 0)).astype(F32)
    mixed = []
    for gi, w in enumerate(POOL_SIZES):
        sl = slice(gi * POOL_GROUP, (gi + 1) * POOL_GROUP)
        acc = ext[:, sl]
        shift = 1
        while shift < w:
            acc = acc + pltpu.roll(acc, shift, axis=0)
            shift *= 2
        mean = acc[POOL_HALO:] / jnp.minimum(pos, float(w))
        pooled = mean - u[:, sl]
        mixed.append(jnp.dot(pooled.astype(BF16), wpool_ref[gi].astype(BF16),
                             preferred_element_type=F32))
        yield
    mix = jnp.concatenate(mixed, axis=1) * pscale_ref[0:1]
    return x + g1 * mix


def _interleave(chains, lag):
    results = [None] * len(chains)
    live = list(range(len(chains)))
    rnd = 0
    while live:
        for k in list(live):
            if rnd < k * lag:
                continue
            try:
                next(chains[k])
            except StopIteration as done:
                results[k] = done.value
                live.remove(k)
        rnd += 1
    return results


def _mixer_kernel(*refs, kind, has_head, nb, n_steps, cap):
    refs = list(refs)
    t = pl.program_id(0)
    slot = t % 2
    if has_head:
        pm_ref = refs.pop(0)
        hprev_ref, modprev_ref, tokprev_ref, ys_hbm = refs[:4]
        refs = refs[4:]
    else:
        h_ref = refs.pop(0)
    mod_ref, nrm_ref = refs[:2]
    refs = refs[2:]
    n_w = 5 if kind == "even" else 2
    weight_refs = refs[:n_w]
    refs = refs[n_w:]
    wrt_ref, br_ref, tri_ref = refs[:3]
    refs = refs[3:]
    hout_ref, tokmeta_ref, tmeta_ref, xs_hbm = refs[:4]
    refs = refs[4:]
    n_c = 2 if kind == "even" else 1
    carry_refs = refs[:n_c]
    refs = refs[n_c:]
    if kind == "even":
        cast_refs = refs[:2]
        refs = refs[2:]
    stage_ref, zeros_ref, run_ref, prev_ref, ssem, zsem = refs[:6]
    refs = refs[6:]

    @pl.when(t == 0)
    def _():
        for r in carry_refs:
            r[...] = jnp.zeros_like(r)
        for g in range(N_GROUPS):
            run_ref[g] = jnp.int32(0)
        if kind == "even":
            for src, dst in zip(weight_refs[:2], cast_refs):
                for c0 in range(0, src.shape[1], 2 * LANES):
                    dst[:, c0:c0 + 2 * LANES] = src[:, c0:c0 + 2 * LANES].astype(BF16)
        _prime_scatter(nb, cap, xs_hbm, stage_ref, prev_ref, ssem)

    if kind == "even":
        weight_refs = list(cast_refs) + list(weight_refs[2:])

    drain_head = None
    if has_head:
        ystage_ref, gsem = refs
        drain_head = _head_dma(t, n_steps, nb, pm_ref, ys_hbm, ystage_ref, gsem)

    def chain(b):
        if has_head:
            x = _head_combine(b, slot, hprev_ref, modprev_ref, tokprev_ref, ystage_ref)
            yield
        else:
            x = h_ref[b]
        mod = mod_ref[b]
        carries = [r.at[b] for r in carry_refs]
        mix = _attn_conv_mix if kind == "even" else _pool_mix
        h1 = yield from mix(x, t, mod, nrm_ref, *weight_refs, *carries)
        hout_ref[b] = h1
        yield
        return (yield from _tail_compute(b, slot, h1, mod, nrm_ref, wrt_ref, br_ref, tri_ref,
                                         tokmeta_ref, stage_ref))

    counts = _interleave([chain(b) for b in range(nb)], CHAIN_LAG)
    _tail_dma(t, n_steps, nb, cap, counts, tmeta_ref, xs_hbm, stage_ref, zeros_ref,
              run_ref, prev_ref, ssem, zsem, drain_head)


def _nat(nb, cols):
    return pl.BlockSpec((nb, TB, cols), lambda t, *_: (0, t, 0))


def _full(shape):
    nd = len(shape)
    return pl.BlockSpec(shape, lambda t, *_: (0,) * nd)


def _head_scratch(nb):
    return [pltpu.VMEM((nb, 2, STG_C, D_MODEL), BF16), pltpu.SemaphoreType.DMA((nb, 2))]


def _mixer_call(kind, j, head, h, mod8, nrm8, weights, wrt, br, tri):
    nb, seq, _ = (h if head is None else head[1]).shape
    n_steps = seq // TB
    n_tiles, cap, _ = _sizes(nb * seq)
    has_head = head is not None
    any_spec = pl.BlockSpec(memory_space=pl.ANY)
    smem_spec = pl.BlockSpec(memory_space=pltpu.SMEM)

    if has_head:
        args = list(head[1:])
        in_specs = [_nat(nb, D_MODEL), _full((nb, 8, D_MODEL)), _nat(nb, LANES), any_spec]
    else:
        args = [h]
        in_specs = [_nat(nb, D_MODEL)]
    args += [mod8, nrm8]
    in_specs += [_full((nb, 8, D_MODEL)), _full((8, D_MODEL))]
    if kind == "even":
        args += list(weights)
        once = dict(pipeline_mode=pl.Buffered(1))
        in_specs += [pl.BlockSpec((None, D_MODEL, IN_WIDTH), lambda t, *_: (j, 0, 0), **once),
                     pl.BlockSpec((None, D_MODEL, D_MODEL), lambda t, *_: (j, 0, 0), **once),
                     _full((N_HEADS // 4, 2 * BLOCK, 4 * BLOCK)), smem_spec, _full((8, CONV_WIDTH))]
        mix_scratch = [pltpu.VMEM((nb, BLOCK, 2 * KV_WIDTH), F32),
                       pltpu.VMEM((nb, 8, CONV_WIDTH), F32),
                       pltpu.VMEM((D_MODEL, IN_WIDTH), BF16),
                       pltpu.VMEM((D_MODEL, D_MODEL), BF16)]
    else:
        args += list(weights)
        in_specs += [pl.BlockSpec((None, len(POOL_SIZES), POOL_GROUP, POOL_GROUP),
                                  lambda t, *_: (j, 0, 0, 0)), _full((8, D_MODEL))]
        mix_scratch = [pltpu.VMEM((nb, POOL_HALO, D_MODEL), F32)]
    args += [wrt, br, tri]
    in_specs += [_full((2 * N_EXPERTS, D_MODEL)), _full((N_EXPERTS, 1)), _full((TB, TB))]

    out_shape = (jax.ShapeDtypeStruct((nb, seq, D_MODEL), F32),
                 jax.ShapeDtypeStruct((nb, seq, LANES), F32),
                 jax.ShapeDtypeStruct(((n_tiles + 1) * META_W,), I32),
                 jax.ShapeDtypeStruct((N_GROUPS * cap + nb * N_SUB * SUB, ROW_W), BF16))
    out_specs = (_nat(nb, D_MODEL), _nat(nb, LANES), smem_spec, any_spec)
    scratch = mix_scratch + [
        pltpu.VMEM((nb, 2, STG_P, ROW_W), BF16),
        pltpu.VMEM((TBM, ROW_W), BF16),
        pltpu.SMEM((N_GROUPS,), I32),
        pltpu.SMEM((nb * N_SUB,), I32),
        pltpu.SemaphoreType.DMA((nb, 2)),
        pltpu.SemaphoreType.DMA(()),
    ]
    if has_head:
        scratch += _head_scratch(nb)

    body = functools.partial(_mixer_kernel, kind=kind, has_head=has_head, nb=nb,
                             n_steps=n_steps, cap=cap)
    grid_spec = pltpu.PrefetchScalarGridSpec(
        num_scalar_prefetch=1 if has_head else 0, grid=(n_steps,),
        in_specs=in_specs, out_specs=out_specs, scratch_shapes=scratch)
    call = pl.pallas_call(
        body, grid_spec=grid_spec, out_shape=out_shape,
        compiler_params=pltpu.CompilerParams(
            dimension_semantics=("arbitrary",), vmem_limit_bytes=VMEM_LIMIT),
        name=kind + "_mixer")
    if has_head:
        return call(head[0], *args)
    return call(*args)


def _moe_kernel(blk_ref, grp_ref, val_ref, new_ref, xs_ref, wg_ref, wu_ref, wd_ref, ys_ref,
                wg_s, wu_s, wd_s):
    i = pl.program_id(0)

    @pl.when(new_ref[i] == 1)
    def _():
        for e in range(EXPERTS_PER_GROUP):
            wg_s[e] = wg_ref[e].astype(BF16)
            wu_s[e] = wu_ref[e].astype(BF16)
        wd_s[...] = wd_ref[0].astype(BF16)

    @pl.when(val_ref[i] == 1)
    def _():
        xb = xs_ref[:, 0:D_MODEL]
        meta = xs_ref[:, D_MODEL:ROW_W].astype(F32)
        gates = meta[:, 0:EXPERTS_PER_GROUP] + meta[:, EXPERTS_PER_GROUP:2 * EXPERTS_PER_GROUP]
        parts = []
        for e in range(EXPERTS_PER_GROUP):
            a = jnp.dot(xb, wg_s[e], preferred_element_type=F32)
            bu = jnp.dot(xb, wu_s[e], preferred_element_type=F32)
            hid = (a * _sigmoid(a)) * bu
            parts.append((hid * gates[:, e:e + 1]).astype(BF16))
        hid_all = jnp.concatenate(parts, axis=1)
        ys_ref[...] = jnp.dot(hid_all, wd_s[...], preferred_element_type=F32).astype(BF16)


def _work_tables(totals, cap, n_work):
    ntile = (totals + (TBM - 1)) // TBM
    ends = jnp.cumsum(ntile)
    starts = ends - ntile
    nvalid = ends[-1]
    idx = jnp.arange(n_work, dtype=I32)
    idc = jnp.maximum(jnp.minimum(idx, nvalid - 1), 0)
    grp = jnp.minimum(jnp.sum((idc[:, None] >= ends[None, :]).astype(I32), axis=1), N_GROUPS - 1)
    blk = grp * (cap // TBM) + idc - starts[grp]
    valid = idx < nvalid
    new = valid & ((idx == 0) | (grp != jnp.roll(grp, 1)))
    return blk.astype(I32), grp.astype(I32), valid.astype(I32), new.astype(I32)


def _moe_call(xs, tables, layer, wg, wu, wd, n_work):
    row_map = lambda i, blk, grp, val, new: (blk[i], 0)
    grp_map = lambda i, blk, grp, val, new: (layer, grp[i], 0, 0)
    grid_spec = pltpu.PrefetchScalarGridSpec(
        num_scalar_prefetch=4, grid=(n_work,),
        in_specs=[
            pl.BlockSpec((TBM, ROW_W), row_map),
            pl.BlockSpec((None, EXPERTS_PER_GROUP, D_MODEL, EXPERT_FF), grp_map),
            pl.BlockSpec((None, EXPERTS_PER_GROUP, D_MODEL, EXPERT_FF), grp_map),
            pl.BlockSpec((None, 1, GROUP_FF, D_MODEL), grp_map),
        ],
        out_specs=pl.BlockSpec((TBM, D_MODEL), row_map),
        scratch_shapes=[pltpu.VMEM((EXPERTS_PER_GROUP, D_MODEL, EXPERT_FF), BF16),
                        pltpu.VMEM((EXPERTS_PER_GROUP, D_MODEL, EXPERT_FF), BF16),
                        pltpu.VMEM((GROUP_FF, D_MODEL), BF16)],
    )
    return pl.pallas_call(
        _moe_kernel, grid_spec=grid_spec,
        out_shape=jax.ShapeDtypeStruct((xs.shape[0], D_MODEL), BF16),
        compiler_params=pltpu.CompilerParams(
            dimension_semantics=("arbitrary",), vmem_limit_bytes=VMEM_LIMIT),
        name="moe_experts",
    )(*tables, xs, wg, wu, wd)


def _final_kernel(pm_ref, hprev_ref, modprev_ref, tokprev_ref, ys_hbm, nf_ref, o_ref,
                  ystage_ref, gsem, *, nb, n_steps):
    t = pl.program_id(0)
    drain = _head_dma(t, n_steps, nb, pm_ref, ys_hbm, ystage_ref, gsem)
    for b in range(nb):
        h = _head_combine(b, t % 2, hprev_ref, modprev_ref, tokprev_ref, ystage_ref)
        ms = jnp.mean(h * h, axis=-1, keepdims=True)
        o_ref[b] = (h * lax.rsqrt(ms + EPS)) * nf_ref[0:1]
    pl.when(t == n_steps - 1)(drain)


def _final_call(head, nf8):
    pm, hprev, modprev, tokprev, ys = head
    nb, seq, _ = hprev.shape
    n_steps = seq // TB
    grid_spec = pltpu.PrefetchScalarGridSpec(
        num_scalar_prefetch=1, grid=(n_steps,),
        in_specs=[_nat(nb, D_MODEL), _full((nb, 8, D_MODEL)), _nat(nb, LANES),
                  pl.BlockSpec(memory_space=pl.ANY), _full((8, D_MODEL))],
        out_specs=_nat(nb, D_MODEL),
        scratch_shapes=_head_scratch(nb))
    return pl.pallas_call(
        functools.partial(_final_kernel, nb=nb, n_steps=n_steps), grid_spec=grid_spec,
        out_shape=jax.ShapeDtypeStruct((nb, seq, D_MODEL), F32),
        compiler_params=pltpu.CompilerParams(
            dimension_semantics=("arbitrary",), vmem_limit_bytes=VMEM_LIMIT),
        name="final_norm",
    )(pm, hprev, modprev, tokprev, ys, nf8)


def _pad_rows(a, rows=8):
    return jnp.pad(a, ((0, rows - a.shape[0]), (0, 0)))


def _attn_bias_table():
    slopes = 2.0 ** (-8.0 * (jnp.arange(N_HEADS, dtype=F32) + 1.0) / N_HEADS)
    dist = (jnp.arange(BLOCK)[:, None] + BLOCK) - jnp.arange(2 * BLOCK)[None, :]
    ok = (dist >= 0) & (dist < WINDOW)
    per_head = jnp.where(ok[None], -slopes[:, None, None] * dist.astype(F32)[None], NEG_BIG)
    return per_head.reshape(N_HEADS // 2, 2, BLOCK, 2 * BLOCK).transpose(0, 2, 1, 3).reshape(
        N_HEADS // 4, 2 * BLOCK, 4 * BLOCK)


def _router_operands(w_router, b_router):
    def reorder(a):
        return a.reshape(N_GROUPS, EXPERTS_PER_GROUP, -1).transpose(1, 0, 2).reshape(N_EXPERTS, -1)
    w = reorder(w_router.T)
    wh = w.astype(BF16)
    wl = (w - wh.astype(F32)).astype(BF16)
    return jnp.concatenate([wh, wl], axis=0), reorder(b_router[:, None])


def kernel(x, c, w_ada, b_ada, norm_mix, norm_ffn, w_in, w_out, sinks, conv_w, w_pool,
           pool_scale, w_router, b_router, w_gate, w_up, w_down, norm_final):
    b, s, _ = x.shape
    n_tiles, cap, n_work = _sizes(b * s)

    mod_all = _mod_call(_pad_rows(c), w_ada, b_ada.reshape(DEPTH, 1, N_MOD * D_MODEL))
    bias_tab = _attn_bias_table()
    wrt, br = _router_operands(w_router, b_router)
    nf8 = _pad_rows(norm_final[None, :])
    tri = jnp.triu(jnp.ones((TB, TB), F32)).astype(BF16)

    h = x
    head = None
    for layer in range(DEPTH):
        j = layer // 2
        mod8 = jnp.pad(mod_all[layer, :b].reshape(b, N_MOD, D_MODEL), ((0, 0), (0, 2), (0, 0)))
        nrm8 = _pad_rows(jnp.stack([norm_mix[layer], norm_ffn[layer]]))
        if layer % 2 == 0:
            kind = "even"
            weights = (w_in, w_out, bias_tab, sinks[j], _pad_rows(conv_w[j]))
        else:
            kind = "odd"
            weights = (w_pool, _pad_rows(pool_scale[j][None, :]))
        h1, tokmeta, tmeta, xs = _mixer_call(kind, j, head, h, mod8, nrm8, weights, wrt, br, tri)
        totals = tmeta[n_tiles * META_W:n_tiles * META_W + N_GROUPS]
        ys = _moe_call(xs, _work_tables(totals, cap, n_work), layer, w_gate, w_up,
                       w_down.reshape(DEPTH, N_GROUPS, GROUP_FF, D_MODEL), n_work)
        head = (tmeta, h1, mod8, tokmeta, ys)
    return _final_call(head, nf8)
```

```python
import functools

import jax
import jax.numpy as jnp
from jax import lax
from jax.experimental import pallas as pl
from jax.experimental.pallas import tpu as pltpu

F32 = jnp.float32
BF16 = jnp.bfloat16
I32 = jnp.int32

D_MODEL = 1024
DEPTH = 4
EPS = 1e-6
N_MOD = 6

ATTN_WIDTH = 512
HEAD_DIM = 64
N_HEADS = 8
KV_WIDTH = 128
WINDOW = 128
BLOCK = 128
CONV_WIDTH = 512
IN_WIDTH = 2304

POOL_SIZES = (2, 4, 8, 16)
POOL_GROUP = 256
POOL_HALO = 16

N_EXPERTS = 16
N_GROUPS = 4
EXPERTS_PER_GROUP = 4
EXPERT_FF = 256
GROUP_FF = EXPERTS_PER_GROUP * EXPERT_FF

LANES = 128
NEG_BIG = -1e30

TB = 256
SUB = 16
N_SUB = TB // SUB + N_GROUPS - 1
STG_P = 320
STG_C = 384
TBM = 512
ROW_W = D_MODEL + LANES
META_W = 32
LP_LANE = 8
CHAIN_LAG = 0
MOD_COLS = 3072
VMEM_LIMIT = 56 * 1024 * 1024
NT_DIMS = (((1,), (1,)), ((), ()))

assert N_SUB * SUB <= STG_P <= STG_C and STG_C % LANES == 0 and N_SUB <= META_W


def _sizes(n_tok):
    n_tiles = n_tok // TB
    cap = -(-(n_tok + n_tiles * (SUB - 1) + TBM) // TBM) * TBM
    n_work = (n_tok + n_tiles * N_GROUPS * (SUB - 1)) // TBM + N_GROUPS
    return n_tiles, cap, n_work


def _sigmoid(x):
    return 1.0 / (1.0 + jnp.exp(-x))


def _rms_mod(x, g, sc, sh):
    ms = jnp.mean(x * x, axis=-1, keepdims=True)
    return (x * lax.rsqrt(ms + EPS)) * (g * (1.0 + sc)) + sh


def _onehot(cond):
    return jnp.where(cond, 1.0, 0.0).astype(BF16)


def _mod_kernel(c_ref, w_ref, b_ref, o_ref):
    c = c_ref[...]
    cond = c * _sigmoid(c)
    o_ref[0] = jnp.dot(cond, w_ref[0], precision=lax.Precision.HIGHEST,
                       preferred_element_type=F32) + b_ref[0]


def _mod_call(c8, w_ada, b_ada3):
    n_col = (N_MOD * D_MODEL) // MOD_COLS
    return pl.pallas_call(
        _mod_kernel,
        grid=(DEPTH, n_col),
        in_specs=[
            pl.BlockSpec((8, D_MODEL), lambda l, j: (0, 0)),
            pl.BlockSpec((1, D_MODEL, MOD_COLS), lambda l, j: (l, 0, j)),
            pl.BlockSpec((1, 1, MOD_COLS), lambda l, j: (l, 0, j)),
        ],
        out_specs=pl.BlockSpec((1, 8, MOD_COLS), lambda l, j: (l, 0, j)),
        out_shape=jax.ShapeDtypeStruct((DEPTH, 8, N_MOD * D_MODEL), F32),
        compiler_params=pltpu.CompilerParams(
            dimension_semantics=("arbitrary", "arbitrary"), vmem_limit_bytes=VMEM_LIMIT),
        name="adaln_mod",
    )(c8, w_ada, b_ada3)


def _sub_copies(hbm_ref, stage_ref, sem, rows, to_hbm):
    out = []
    for s in range(N_SUB):
        s_view = stage_ref.at[pl.ds(s * SUB, SUB), :]
        h_view = hbm_ref.at[pl.ds(pl.multiple_of(rows[s], SUB), SUB), :]
        out.append(pltpu.make_async_copy(s_view, h_view, sem) if to_hbm
                   else pltpu.make_async_copy(h_view, s_view, sem))
    return out


def _head_dma(t, n_steps, nb, pm_ref, ys_hbm, ystage_ref, gsem):
    slot = t % 2

    def copies(step, b, slot_):
        base = (step * nb + b) * META_W
        rows = [pm_ref[base + s] for s in range(N_SUB)]
        return _sub_copies(ys_hbm, ystage_ref.at[b, slot_], gsem.at[b, slot_], rows, False)

    @pl.when(t == 0)
    def _():
        ystage_ref[...] = jnp.zeros_like(ystage_ref)
        for b in range(nb):
            for cp in copies(0, b, 0):
                cp.start()

    nxt = jnp.minimum(t + 1, n_steps - 1)
    for b in range(nb):
        for cp in copies(nxt, b, 1 - slot):
            cp.start()

    for b in range(nb):
        for cp in copies(t, b, slot):
            cp.wait()

    def drain():
        for b in range(nb):
            for cp in copies(nxt, b, 1 - slot):
                cp.wait()
    return drain


def _head_combine(b, slot, hprev_ref, modprev_ref, tokmeta_ref, ystage_ref):
    lp = tokmeta_ref[b, :, LP_LANE:LP_LANE + 1].astype(I32)
    unsort = _onehot(lax.broadcasted_iota(I32, (TB, STG_C), 1) == lp)
    y = jnp.dot(unsort, ystage_ref[b, slot], preferred_element_type=F32)
    return hprev_ref[b] + modprev_ref[b, 5:6] * y


def _route(u2, wrt_ref, br_ref):
    uh = u2.astype(BF16)
    ul = (u2 - uh.astype(F32)).astype(BF16)
    w = wrt_ref[...]
    r1 = lax.dot_general(w, uh, NT_DIMS, preferred_element_type=F32)
    r2 = lax.dot_general(w[0:N_EXPERTS], ul, NT_DIMS, preferred_element_type=F32)
    yield
    scores = _sigmoid(r1[0:N_EXPERTS] + r1[N_EXPERTS:] + r2)
    biased = scores + br_ref[...]
    s = [scores[4 * j:4 * j + 4] for j in range(EXPERTS_PER_GROUP)]
    b = [biased[4 * j:4 * j + 4] for j in range(EXPERTS_PER_GROUP)]

    hi1, lo1 = jnp.maximum(b[0], b[1]), jnp.minimum(b[0], b[1])
    hi2, lo2 = jnp.maximum(b[2], b[3]), jnp.minimum(b[2], b[3])
    second = jnp.maximum(jnp.minimum(hi1, hi2), jnp.where(hi1 >= hi2, lo1, lo2))
    gscore = jnp.maximum(hi1, hi2) + second

    best = gscore[0:1]
    bgrp = jnp.zeros((1, TB), I32)
    for g in range(1, N_GROUPS):
        better = gscore[g:g + 1] > best
        bgrp = jnp.where(better, g, bgrp)
        best = jnp.where(better, gscore[g:g + 1], best)
    selmask = lax.broadcasted_iota(I32, (N_GROUPS, TB), 0) == bgrp
    yield

    m1 = b[0]
    i1 = jnp.zeros((N_GROUPS, TB), I32)
    for j in range(1, EXPERTS_PER_GROUP):
        gt = b[j] > m1
        i1 = jnp.where(gt, j, i1)
        m1 = jnp.where(gt, b[j], m1)
    cands = [jnp.where(i1 == j, -jnp.inf, b[j]) for j in range(EXPERTS_PER_GROUP)]
    m2 = cands[0]
    i2 = jnp.zeros_like(i1)
    for j in range(1, EXPERTS_PER_GROUP):
        gt = cands[j] > m2
        i2 = jnp.where(gt, j, i2)
        m2 = jnp.where(gt, cands[j], m2)
    w1 = s[0]
    w2 = s[0]
    for j in range(1, EXPERTS_PER_GROUP):
        w1 = jnp.where(i1 == j, s[j], w1)
        w2 = jnp.where(i2 == j, s[j], w2)
    yield
    tot = w1 + w2
    w1n = w1 / tot
    w2n = w2 / tot
    gates = []
    for j in range(EXPERTS_PER_GROUP):
        gj = jnp.where(i1 == j, w1n, 0.0) + jnp.where(i2 == j, w2n, 0.0)
        gates.append(jnp.sum(jnp.where(selmask, gj, 0.0), axis=0, keepdims=True))
    return bgrp, gates, selmask


def _tail_compute(b, slot, h1, mod, nrm_ref, wrt_ref, br_ref, tri_ref, tokmeta_ref, stage_ref):
    sh2, sc2 = mod[3:4], mod[4:5]
    u2 = _rms_mod(h1, nrm_ref[1:2], sc2, sh2)
    bgrp, gates, selmask = yield from _route(u2, wrt_ref, br_ref)
    yield

    onehot = jnp.concatenate([jnp.where(selmask, 1.0, 0.0), jnp.zeros((8 - N_GROUPS, TB), F32)],
                             axis=0).astype(BF16)
    rank_incl = jnp.dot(onehot, tri_ref[...], preferred_element_type=F32)
    rank = jnp.sum(jnp.where(selmask, rank_incl[0:N_GROUPS], 0.0), axis=0, keepdims=True) - 1.0
    cnt = lax.dot_general(jnp.ones((8, TB), BF16), onehot, NT_DIMS,
                          preferred_element_type=F32)
    sub_rows = jnp.floor((cnt[0:1, :] + (SUB - 1.0)) * (1.0 / SUB)) * SUB
    lp = rank
    first_row = jnp.zeros((1, 1), F32)
    for g in range(1, N_GROUPS):
        first_row = first_row + sub_rows[:, g - 1:g]
        lp = lp + jnp.where(bgrp == g, first_row, 0.0)

    hi = [x.astype(BF16).astype(F32) for x in gates]
    lo = [(x - h).astype(BF16).astype(F32) for x, h in zip(gates, hi)]
    meta_src = jnp.concatenate(hi + lo + [lp, jnp.zeros((LANES - 9, TB), F32)], axis=0)
    meta_t = meta_src.T
    tokmeta_ref[b] = meta_t
    yield

    rowdata = jnp.concatenate([u2.astype(BF16), meta_t.astype(BF16)], axis=1)
    sort = _onehot(lax.broadcasted_iota(I32, (STG_P, TB), 0) == lp.astype(I32))
    stage_ref[b, slot] = jnp.dot(sort, rowdata, preferred_element_type=F32).astype(BF16)
    return [cnt[0, g].astype(I32) for g in range(N_GROUPS)]


def _prime_scatter(nb, cap, xs_hbm, stage_ref, prev_ref, ssem):
    stage_ref[:, 1] = jnp.zeros((nb,) + stage_ref.shape[2:], BF16)
    for b in range(nb):
        rows = [jnp.int32(N_GROUPS * cap + (b * N_SUB + s) * SUB) for s in range(N_SUB)]
        for s in range(N_SUB):
            prev_ref[b * N_SUB + s] = rows[s]
        for cp in _sub_copies(xs_hbm, stage_ref.at[b, 1], ssem.at[b, 1], rows, True):
            cp.start()


def _tail_dma(t, n_steps, nb, cap, counts, tmeta_ref, xs_hbm, stage_ref, zeros_ref,
              run_ref, prev_ref, ssem, zsem, drain_head):
    slot = t % 2
    run = [run_ref[g] for g in range(N_GROUPS)]
    junk = N_GROUPS * cap
    scatter_rows = []
    for b in range(nb):
        first_sub = [jnp.int32(0)]
        for g in range(N_GROUPS):
            first_sub.append(first_sub[-1] + (counts[b][g] + (SUB - 1)) // SUB)
        dst = [g * cap + run[g] for g in range(N_GROUPS)]
        rows_b, base = [], (t * nb + b) * META_W
        for s in range(N_SUB):
            grp_first, grp_dst = first_sub[0], dst[0]
            for g in range(1, N_GROUPS):
                later = s >= first_sub[g]
                grp_first = jnp.where(later, first_sub[g], grp_first)
                grp_dst = jnp.where(later, dst[g], grp_dst)
            row = grp_dst + (s - grp_first) * SUB
            used = s < first_sub[N_GROUPS]
            rows_b.append(jnp.where(used, row, junk + (b * N_SUB + s) * SUB))
            tmeta_ref[base + s] = jnp.where(used, row, row if s == 0 else first_row)
            if s == 0:
                first_row = row
        for k in range(N_SUB, META_W):
            tmeta_ref[base + k] = jnp.int32(0)
        for g in range(N_GROUPS):
            run[g] = run[g] + (first_sub[g + 1] - first_sub[g]) * SUB
        scatter_rows.append(rows_b)
    for g in range(N_GROUPS):
        run_ref[g] = run[g]

    def copies(b, slot_, rows):
        return _sub_copies(xs_hbm, stage_ref.at[b, slot_], ssem.at[b, slot_], rows, True)

    for b in range(nb):
        for cp in copies(b, 1 - slot, [prev_ref[b * N_SUB + s] for s in range(N_SUB)]):
            cp.wait()

    for b in range(nb):
        for cp in copies(b, slot, scatter_rows[b]):
            cp.start()
        for s in range(N_SUB):
            prev_ref[b * N_SUB + s] = scatter_rows[b][s]

    @pl.when(t == n_steps - 1)
    def _():
        for b in range(nb):
            for cp in copies(b, slot, scatter_rows[b]):
                cp.wait()
        if drain_head is not None:
            drain_head()
        zeros_ref[...] = jnp.zeros_like(zeros_ref)
        tails = []
        base = n_steps * nb * META_W
        for g in range(N_GROUPS):
            tmeta_ref[base + g] = run[g]
            start = pl.multiple_of(g * cap + run[g], SUB)
            tails.append(pltpu.make_async_copy(zeros_ref, xs_hbm.at[pl.ds(start, TBM), :], zsem))
        for k in range(N_GROUPS, META_W):
            tmeta_ref[base + k] = jnp.int32(0)
        for cp in tails:
            cp.start()
        for cp in tails:
            cp.wait()


def _attn_conv_mix(x, t, mod, nrm_ref, win_ref, wout_ref, bias_ref, sink_ref, convw_ref,
                   kvprev_ref, cprev_ref):
    tb = TB
    first = t == 0
    sh1, sc1, g1 = mod[0:1], mod[1:2], mod[2:3]
    u = _rms_mod(x, nrm_ref[0:1], sc1, sh1)
    proj = jnp.dot(u.astype(BF16), win_ref[...], preferred_element_type=F32)

    yield
    q = (proj[:, 0:ATTN_WIDTH] * HEAD_DIM ** -0.5).astype(BF16)
    kf = proj[:, 512:640]
    vf = proj[:, 640:768]
    bgate = proj[:, 768:1280]
    cgate = proj[:, 1280:1792]
    xv = proj[:, 1792:2304]

    kv_prev = kvprev_ref[...]
    kext = jnp.concatenate([kv_prev[:, 0:KV_WIDTH], kf], axis=0)
    vext = jnp.concatenate([kv_prev[:, KV_WIDTH:], vf], axis=0)
    kvprev_ref[:, 0:KV_WIDTH] = kf[tb - BLOCK:tb]
    kvprev_ref[:, KV_WIDTH:] = vf[tb - BLOCK:tb]

    lane = lax.broadcasted_iota(I32, kext.shape, 1)
    lo = lane < HEAD_DIM
    krol = pltpu.roll(kext, HEAD_DIM, axis=1)
    vrol = pltpu.roll(vext, HEAD_DIM, axis=1)
    zero = jnp.zeros_like(kext)
    one_at_64 = jnp.where(lane == HEAD_DIM, 1.0, 0.0)
    one_at_0 = jnp.where(lane == 0, 1.0, 0.0)
    k_ops = [(jnp.where(lo, kext, zero).astype(BF16), jnp.where(lo, zero, krol).astype(BF16)),
             (jnp.where(lo, krol, zero).astype(BF16), jnp.where(lo, zero, kext).astype(BF16))]
    v_ops = [(jnp.where(lo, vext, one_at_64).astype(BF16), jnp.where(lo, one_at_0, vrol).astype(BF16)),
             (jnp.where(lo, vrol, one_at_64).astype(BF16), jnp.where(lo, one_at_0, vext).astype(BF16))]

    col = lax.broadcasted_iota(I32, (2 * BLOCK, 4 * BLOCK), 1)
    prev_cols = (col % (2 * BLOCK)) < BLOCK
    upper = lax.broadcasted_iota(I32, (2 * BLOCK, 1), 0) < BLOCK
    lane_o = lax.broadcasted_iota(I32, (2 * BLOCK, LANES), 1)
    lo_o = lane_o < HEAD_DIM

    yield
    attn_rows = []
    for bi in range(tb // BLOCK):
        r0 = bi * BLOCK
        pair_out = []
        for kvh in range(N_HEADS // 4):
            ka, kb = k_ops[kvh]
            rhs = jnp.concatenate([ka[r0:r0 + 2 * BLOCK], kb[r0:r0 + 2 * BLOCK]], axis=0)
            qrows = q[r0:r0 + BLOCK]
            qp = jnp.concatenate([qrows[:, (2 * kvh) * LANES:(2 * kvh + 1) * LANES],
                                  qrows[:, (2 * kvh + 1) * LANES:(2 * kvh + 2) * LANES]], axis=0)
            s = lax.dot_general(qp, rhs, NT_DIMS, preferred_element_type=F32)
            bias = bias_ref[kvh]
            if bi == 0:
                bias = jnp.where(prev_cols & first, NEG_BIG, bias)
            s = s + bias
            outs = []
            for hh in range(2):
                sh = s[:, hh * 2 * BLOCK:(hh + 1) * 2 * BLOCK]
                sink = jnp.where(upper, sink_ref[4 * kvh + hh], sink_ref[4 * kvh + 2 + hh])
                m = jnp.maximum(jnp.max(sh, axis=-1, keepdims=True), sink)
                p = jnp.exp(sh - m).astype(BF16)
                vop = v_ops[kvh][hh][r0:r0 + 2 * BLOCK]
                o = jnp.dot(p, vop, preferred_element_type=F32)
                den_col = HEAD_DIM if hh == 0 else 0
                den = o[:, den_col:den_col + 1] + jnp.exp(sink - m)
                outs.append(o / den)
            both = jnp.where(lo_o, outs[0], outs[1])
            pair_out += [both[0:BLOCK], both[BLOCK:]]
            yield
        attn_rows.append(jnp.concatenate(pair_out, axis=1))
    attn = jnp.concatenate(attn_rows, axis=0)

    uc = cgate * xv
    cprev = jnp.where(first, 0.0, cprev_ref[...])
    row = lax.broadcasted_iota(I32, uc.shape, 0)
    r1 = jnp.where(row == 0, cprev[7:8], pltpu.roll(uc, 1, axis=0))
    r2 = jnp.where(row == 0, cprev[6:7], jnp.where(row == 1, cprev[7:8], pltpu.roll(uc, 2, axis=0)))
    cprev_ref[...] = uc[tb - 8:tb]
    cw = convw_ref[...]
    conv = bgate * (cw[0:1] * r2 + cw[1:2] * r1 + cw[2:3] * uc)
    yield

    mix = (jnp.dot(attn.astype(BF16), wout_ref[0:ATTN_WIDTH, :], preferred_element_type=F32)
           + jnp.dot(conv.astype(BF16), wout_ref[ATTN_WIDTH:, :], preferred_element_type=F32))
    return x + g1 * mix


def _pool_mix(x, t, mod, nrm_ref, wpool_ref, band_ref, pscale_ref, uprev_ref):
    tb = TB
    sh1, sc1, g1 = mod[0:1], mod[1:2], mod[2:3]
    u = _rms_mod(x, nrm_ref[0:1], sc1, sh1)
    halo = jnp.where(t == 0, 0.0, uprev_ref[...]).astype(BF16)
    ext = jnp.concatenate([jnp.zeros((LANES - POOL_HALO, D_MODEL), BF16), halo, u.astype(BF16)],
                          axis=0)
    uprev_ref[...] = u[tb - POOL_HALO:tb]

    yield
    pos = (t * tb + 1 + lax.broadcasted_iota(I32, (tb, 1), 0)).astype(F32)
    cols = [slice(gi * POOL_GROUP, (gi + 1) * POOL_GROUP) for gi in range(len(POOL_SIZES))]
    wsums = [jnp.dot(band_ref[gi], ext[:, sl], preferred_element_type=F32)
             for gi, sl in enumerate(cols)]
    yield
    mixed = []
    for gi, w in enumerate(POOL_SIZES):
        mean = wsums[gi] / jnp.minimum(pos, float(w))
        pooled = mean - u[:, cols[gi]]
        mixed.append(jnp.dot(pooled.astype(BF16), wpool_ref[gi].astype(BF16),
                             preferred_element_type=F32))
    yield
    mix = jnp.concatenate(mixed, axis=1) * pscale_ref[0:1]
    return x + g1 * mix


def _interleave(chains, lag):
    results = [None] * len(chains)
    live = list(range(len(chains)))
    rnd = 0
    while live:
        for k in list(live):
            if rnd < k * lag:
                continue
            try:
                next(chains[k])
            except StopIteration as done:
                results[k] = done.value
                live.remove(k)
        rnd += 1
    return results


def _mixer_kernel(*refs, kind, has_head, nb, n_steps, cap):
    refs = list(refs)
    t = pl.program_id(0)
    slot = t % 2
    if has_head:
        pm_ref = refs.pop(0)
        hprev_ref, modprev_ref, tokprev_ref, ys_hbm = refs[:4]
        refs = refs[4:]
    else:
        h_ref = refs.pop(0)
    mod_ref, nrm_ref = refs[:2]
    refs = refs[2:]
    n_w = 5 if kind == "even" else 3
    weight_refs = refs[:n_w]
    refs = refs[n_w:]
    wrt_ref, br_ref, tri_ref = refs[:3]
    refs = refs[3:]
    hout_ref, tokmeta_ref, tmeta_ref, xs_hbm = refs[:4]
    refs = refs[4:]
    n_c = 2 if kind == "even" else 1
    carry_refs = refs[:n_c]
    refs = refs[n_c:]
    if kind == "even":
        cast_refs = refs[:2]
        refs = refs[2:]
    stage_ref, zeros_ref, run_ref, prev_ref, ssem, zsem = refs[:6]
    refs = refs[6:]

    @pl.when(t == 0)
    def _():
        for r in carry_refs:
            r[...] = jnp.zeros_like(r)
        for g in range(N_GROUPS):
            run_ref[g] = jnp.int32(0)
        if kind == "even":
            for src, dst in zip(weight_refs[:2], cast_refs):
                for c0 in range(0, src.shape[1], 2 * LANES):
                    dst[:, c0:c0 + 2 * LANES] = src[:, c0:c0 + 2 * LANES].astype(BF16)
        _prime_scatter(nb, cap, xs_hbm, stage_ref, prev_ref, ssem)

    if kind == "even":
        weight_refs = list(cast_refs) + list(weight_refs[2:])

    drain_head = None
    if has_head:
        ystage_ref, gsem = refs
        drain_head = _head_dma(t, n_steps, nb, pm_ref, ys_hbm, ystage_ref, gsem)

    def chain(b):
        if has_head:
            x = _head_combine(b, slot, hprev_ref, modprev_ref, tokprev_ref, ystage_ref)
            yield
        else:
            x = h_ref[b]
        mod = mod_ref[b]
        carries = [r.at[b] for r in carry_refs]
        mix = _attn_conv_mix if kind == "even" else _pool_mix
        h1 = yield from mix(x, t, mod, nrm_ref, *weight_refs, *carries)
        hout_ref[b] = h1
        yield
        return (yield from _tail_compute(b, slot, h1, mod, nrm_ref, wrt_ref, br_ref, tri_ref,
                                         tokmeta_ref, stage_ref))

    counts = _interleave([chain(b) for b in range(nb)], CHAIN_LAG)
    _tail_dma(t, n_steps, nb, cap, counts, tmeta_ref, xs_hbm, stage_ref, zeros_ref,
              run_ref, prev_ref, ssem, zsem, drain_head)


def _nat(nb, cols):
    return pl.BlockSpec((nb, TB, cols), lambda t, *_: (0, t, 0))


def _full(shape):
    nd = len(shape)
    return pl.BlockSpec(shape, lambda t, *_: (0,) * nd)


def _head_scratch(nb):
    return [pltpu.VMEM((nb, 2, STG_C, D_MODEL), BF16), pltpu.SemaphoreType.DMA((nb, 2))]


def _mixer_call(kind, j, head, h, mod8, nrm8, weights, wrt, br, tri):
    nb, seq, _ = (h if head is None else head[1]).shape
    n_steps = seq // TB
    n_tiles, cap, _ = _sizes(nb * seq)
    has_head = head is not None
    any_spec = pl.BlockSpec(memory_space=pl.ANY)
    smem_spec = pl.BlockSpec(memory_space=pltpu.SMEM)

    if has_head:
        args = list(head[1:])
        in_specs = [_nat(nb, D_MODEL), _full((nb, 8, D_MODEL)), _nat(nb, LANES), any_spec]
    else:
        args = [h]
        in_specs = [_nat(nb, D_MODEL)]
    args += [mod8, nrm8]
    in_specs += [_full((nb, 8, D_MODEL)), _full((8, D_MODEL))]
    if kind == "even":
        args += list(weights)
        once = dict(pipeline_mode=pl.Buffered(1))
        in_specs += [pl.BlockSpec((None, D_MODEL, IN_WIDTH), lambda t, *_: (j, 0, 0), **once),
                     pl.BlockSpec((None, D_MODEL, D_MODEL), lambda t, *_: (j, 0, 0), **once),
                     _full((N_HEADS // 4, 2 * BLOCK, 4 * BLOCK)), smem_spec, _full((8, CONV_WIDTH))]
        mix_scratch = [pltpu.VMEM((nb, BLOCK, 2 * KV_WIDTH), F32),
                       pltpu.VMEM((nb, 8, CONV_WIDTH), F32),
                       pltpu.VMEM((D_MODEL, IN_WIDTH), BF16),
                       pltpu.VMEM((D_MODEL, D_MODEL), BF16)]
    else:
        args += list(weights)
        in_specs += [pl.BlockSpec((None, len(POOL_SIZES), POOL_GROUP, POOL_GROUP),
                                  lambda t, *_: (j, 0, 0, 0)),
                     _full((len(POOL_SIZES), TB, TB + LANES)), _full((8, D_MODEL))]
        mix_scratch = [pltpu.VMEM((nb, POOL_HALO, D_MODEL), F32)]
    args += [wrt, br, tri]
    in_specs += [_full((2 * N_EXPERTS, D_MODEL)), _full((N_EXPERTS, 1)), _full((TB, TB))]

    out_shape = (jax.ShapeDtypeStruct((nb, seq, D_MODEL), F32),
                 jax.ShapeDtypeStruct((nb, seq, LANES), F32),
                 jax.ShapeDtypeStruct(((n_tiles + 1) * META_W,), I32),
                 jax.ShapeDtypeStruct((N_GROUPS * cap + nb * N_SUB * SUB, ROW_W), BF16))
    out_specs = (_nat(nb, D_MODEL), _nat(nb, LANES), smem_spec, any_spec)
    scratch = mix_scratch + [
        pltpu.VMEM((nb, 2, STG_P, ROW_W), BF16),
        pltpu.VMEM((TBM, ROW_W), BF16),
        pltpu.SMEM((N_GROUPS,), I32),
        pltpu.SMEM((nb * N_SUB,), I32),
        pltpu.SemaphoreType.DMA((nb, 2)),
        pltpu.SemaphoreType.DMA(()),
    ]
    if has_head:
        scratch += _head_scratch(nb)

    body = functools.partial(_mixer_kernel, kind=kind, has_head=has_head, nb=nb,
                             n_steps=n_steps, cap=cap)
    grid_spec = pltpu.PrefetchScalarGridSpec(
        num_scalar_prefetch=1 if has_head else 0, grid=(n_steps,),
        in_specs=in_specs, out_specs=out_specs, scratch_shapes=scratch)
    call = pl.pallas_call(
        body, grid_spec=grid_spec, out_shape=out_shape,
        compiler_params=pltpu.CompilerParams(
            dimension_semantics=("arbitrary",), vmem_limit_bytes=VMEM_LIMIT),
        name=kind + "_mixer")
    if has_head:
        return call(head[0], *args)
    return call(*args)


def _moe_kernel(blk_ref, grp_ref, val_ref, new_ref, xs_ref, wg_ref, wu_ref, wd_ref, ys_ref,
                wg_s, wu_s, wd_s):
    i = pl.program_id(0)

    @pl.when(new_ref[i] == 1)
    def _():
        for e in range(EXPERTS_PER_GROUP):
            wg_s[e] = wg_ref[e].astype(BF16)
            wu_s[e] = wu_ref[e].astype(BF16)
        wd_s[...] = wd_ref[0].astype(BF16)

    @pl.when(val_ref[i] == 1)
    def _():
        xb = xs_ref[:, 0:D_MODEL]
        meta = xs_ref[:, D_MODEL:ROW_W].astype(F32)
        gates = meta[:, 0:EXPERTS_PER_GROUP] + meta[:, EXPERTS_PER_GROUP:2 * EXPERTS_PER_GROUP]
        parts = []
        for e in range(EXPERTS_PER_GROUP):
            a = jnp.dot(xb, wg_s[e], preferred_element_type=F32)
            bu = jnp.dot(xb, wu_s[e], preferred_element_type=F32)
            hid = (a * _sigmoid(a)) * bu
            parts.append((hid * gates[:, e:e + 1]).astype(BF16))
        hid_all = jnp.concatenate(parts, axis=1)
        ys_ref[...] = jnp.dot(hid_all, wd_s[...], preferred_element_type=F32).astype(BF16)


def _work_tables(totals, cap, n_work):
    ntile = (totals + (TBM - 1)) // TBM
    ends = jnp.cumsum(ntile)
    starts = ends - ntile
    nvalid = ends[-1]
    idx = jnp.arange(n_work, dtype=I32)
    idc = jnp.maximum(jnp.minimum(idx, nvalid - 1), 0)
    grp = jnp.minimum(jnp.sum((idc[:, None] >= ends[None, :]).astype(I32), axis=1), N_GROUPS - 1)
    blk = grp * (cap // TBM) + idc - starts[grp]
    valid = idx < nvalid
    new = valid & ((idx == 0) | (grp != jnp.roll(grp, 1)))
    return blk.astype(I32), grp.astype(I32), valid.astype(I32), new.astype(I32)


def _moe_call(xs, tables, layer, wg, wu, wd, n_work):
    row_map = lambda i, blk, grp, val, new: (blk[i], 0)
    grp_map = lambda i, blk, grp, val, new: (layer, grp[i], 0, 0)
    grid_spec = pltpu.PrefetchScalarGridSpec(
        num_scalar_prefetch=4, grid=(n_work,),
        in_specs=[
            pl.BlockSpec((TBM, ROW_W), row_map),
            pl.BlockSpec((None, EXPERTS_PER_GROUP, D_MODEL, EXPERT_FF), grp_map),
            pl.BlockSpec((None, EXPERTS_PER_GROUP, D_MODEL, EXPERT_FF), grp_map),
            pl.BlockSpec((None, 1, GROUP_FF, D_MODEL), grp_map),
        ],
        out_specs=pl.BlockSpec((TBM, D_MODEL), row_map),
        scratch_shapes=[pltpu.VMEM((EXPERTS_PER_GROUP, D_MODEL, EXPERT_FF), BF16),
                        pltpu.VMEM((EXPERTS_PER_GROUP, D_MODEL, EXPERT_FF), BF16),
                        pltpu.VMEM((GROUP_FF, D_MODEL), BF16)],
    )
    return pl.pallas_call(
        _moe_kernel, grid_spec=grid_spec,
        out_shape=jax.ShapeDtypeStruct((xs.shape[0], D_MODEL), BF16),
        compiler_params=pltpu.CompilerParams(
            dimension_semantics=("arbitrary",), vmem_limit_bytes=VMEM_LIMIT),
        name="moe_experts",
    )(*tables, xs, wg, wu, wd)


def _final_kernel(pm_ref, hprev_ref, modprev_ref, tokprev_ref, ys_hbm, nf_ref, o_ref,
                  ystage_ref, gsem, *, nb, n_steps):
    t = pl.program_id(0)
    drain = _head_dma(t, n_steps, nb, pm_ref, ys_hbm, ystage_ref, gsem)
    for b in range(nb):
        h = _head_combine(b, t % 2, hprev_ref, modprev_ref, tokprev_ref, ystage_ref)
        ms = jnp.mean(h * h, axis=-1, keepdims=True)
        o_ref[b] = (h * lax.rsqrt(ms + EPS)) * nf_ref[0:1]
    pl.when(t == n_steps - 1)(drain)


def _final_call(head, nf8):
    pm, hprev, modprev, tokprev, ys = head
    nb, seq, _ = hprev.shape
    n_steps = seq // TB
    grid_spec = pltpu.PrefetchScalarGridSpec(
        num_scalar_prefetch=1, grid=(n_steps,),
        in_specs=[_nat(nb, D_MODEL), _full((nb, 8, D_MODEL)), _nat(nb, LANES),
                  pl.BlockSpec(memory_space=pl.ANY), _full((8, D_MODEL))],
        out_specs=_nat(nb, D_MODEL),
        scratch_shapes=_head_scratch(nb))
    return pl.pallas_call(
        functools.partial(_final_kernel, nb=nb, n_steps=n_steps), grid_spec=grid_spec,
        out_shape=jax.ShapeDtypeStruct((nb, seq, D_MODEL), F32),
        compiler_params=pltpu.CompilerParams(
            dimension_semantics=("arbitrary",), vmem_limit_bytes=VMEM_LIMIT),
        name="final_norm",
    )(pm, hprev, modprev, tokprev, ys, nf8)


def _pad_rows(a, rows=8):
    return jnp.pad(a, ((0, rows - a.shape[0]), (0, 0)))


def _pool_band_table():
    r = jnp.arange(TB)[:, None]
    c = jnp.arange(TB + LANES)[None, :]
    return jnp.stack([(c > r + LANES - w) & (c <= r + LANES) for w in POOL_SIZES]).astype(BF16)


def _attn_bias_table():
    slopes = 2.0 ** (-8.0 * (jnp.arange(N_HEADS, dtype=F32) + 1.0) / N_HEADS)
    dist = (jnp.arange(BLOCK)[:, None] + BLOCK) - jnp.arange(2 * BLOCK)[None, :]
    ok = (dist >= 0) & (dist < WINDOW)
    per_head = jnp.where(ok[None], -slopes[:, None, None] * dist.astype(F32)[None], NEG_BIG)
    return per_head.reshape(N_HEADS // 2, 2, BLOCK, 2 * BLOCK).transpose(0, 2, 1, 3).reshape(
        N_HEADS // 4, 2 * BLOCK, 4 * BLOCK)


def _router_operands(w_router, b_router):
    def reorder(a):
        return a.reshape(N_GROUPS, EXPERTS_PER_GROUP, -1).transpose(1, 0, 2).reshape(N_EXPERTS, -1)
    w = reorder(w_router.T)
    wh = w.astype(BF16)
    wl = (w - wh.astype(F32)).astype(BF16)
    return jnp.concatenate([wh, wl], axis=0), reorder(b_router[:, None])


def kernel(x, c, w_ada, b_ada, norm_mix, norm_ffn, w_in, w_out, sinks, conv_w, w_pool,
           pool_scale, w_router, b_router, w_gate, w_up, w_down, norm_final):
    b, s, _ = x.shape
    n_tiles, cap, n_work = _sizes(b * s)

    mod_all = _mod_call(_pad_rows(c), w_ada, b_ada.reshape(DEPTH, 1, N_MOD * D_MODEL))
    bias_tab = _attn_bias_table()
    band_tab = _pool_band_table()
    wrt, br = _router_operands(w_router, b_router)
    nf8 = _pad_rows(norm_final[None, :])
    tri = jnp.triu(jnp.ones((TB, TB), F32)).astype(BF16)

    h = x
    head = None
    for layer in range(DEPTH):
        j = layer // 2
        mod8 = jnp.pad(mod_all[layer, :b].reshape(b, N_MOD, D_MODEL), ((0, 0), (0, 2), (0, 0)))
        nrm8 = _pad_rows(jnp.stack([norm_mix[layer], norm_ffn[layer]]))
        if layer % 2 == 0:
            kind = "even"
            weights = (w_in, w_out, bias_tab, sinks[j], _pad_rows(conv_w[j]))
        else:
            kind = "odd"
            weights = (w_pool, band_tab, _pad_rows(pool_scale[j][None, :]))
        h1, tokmeta, tmeta, xs = _mixer_call(kind, j, head, h, mod8, nrm8, weights, wrt, br, tri)
        totals = tmeta[n_tiles * META_W:n_tiles * META_W + N_GROUPS]
        ys = _moe_call(xs, _work_tables(totals, cap, n_work), layer, w_gate, w_up,
                       w_down.reshape(DEPTH, N_GROUPS, GROUP_FF, D_MODEL), n_work)
        head = (tmeta, h1, mod8, tokmeta, ys)
    return _final_call(head, nf8)
```

```python
import functools

import jax
import jax.numpy as jnp
from jax import lax
from jax.experimental import pallas as pl
from jax.experimental.pallas import tpu as pltpu

F32 = jnp.float32
BF16 = jnp.bfloat16
I32 = jnp.int32

D_MODEL = 1024
DEPTH = 4
EPS = 1e-6
N_MOD = 6

ATTN_WIDTH = 512
HEAD_DIM = 64
N_HEADS = 8
KV_WIDTH = 128
WINDOW = 128
BLOCK = 128
CONV_WIDTH = 512
IN_WIDTH = 2304

POOL_SIZES = (2, 4, 8, 16)
POOL_GROUP = 256
POOL_HALO = 16

N_EXPERTS = 16
N_GROUPS = 4
EXPERTS_PER_GROUP = 4
EXPERT_FF = 256
GROUP_FF = EXPERTS_PER_GROUP * EXPERT_FF

LANES = 128
NEG_BIG = -1e30

TB = 256
SUB = 16
N_SUB = TB // SUB + N_GROUPS - 1
STG_P = 320
STG_C = 384
TBM = 512
ROW_W = D_MODEL + LANES
META_W = 32
LP_LANE = 8
CHAIN_LAG = 0
MOD_COLS = 3072
VMEM_LIMIT = 56 * 1024 * 1024
NT_DIMS = (((1,), (1,)), ((), ()))

assert N_SUB * SUB <= STG_P <= STG_C and STG_C % LANES == 0 and N_SUB <= META_W


def _sizes(n_tok):
    n_tiles = n_tok // TB
    cap = -(-(n_tok + n_tiles * (SUB - 1) + TBM) // TBM) * TBM
    n_work = (n_tok + n_tiles * N_GROUPS * (SUB - 1)) // TBM + N_GROUPS
    return n_tiles, cap, n_work


def _sigmoid(x):
    return 1.0 / (1.0 + jnp.exp(-x))


def _rms_mod(x, g, sc, sh):
    ms = jnp.mean(x * x, axis=-1, keepdims=True)
    return (x * lax.rsqrt(ms + EPS)) * (g * (1.0 + sc)) + sh


def _onehot(cond):
    return jnp.where(cond, 1.0, 0.0).astype(BF16)


def _mod_kernel(ct_ref, w_ref, b_ref, o_ref, *, nb):
    j = pl.program_id(1)
    ct = ct_ref[...]
    cond = ct * _sigmoid(ct)
    vec_per_step = MOD_COLS // D_MODEL
    for b in range(nb):
        col = cond[:, b:b + 1]
        acc = jnp.zeros((8, MOD_COLS), F32)
        for r in range(D_MODEL // 8):
            acc = acc + w_ref[0, 8 * r:8 * r + 8, :] * col[8 * r:8 * r + 8]
        res = jnp.sum(acc, axis=0, keepdims=True) + b_ref[0]
        for jj in range(N_MOD // vec_per_step):
            @pl.when(j == jj)
            def _():
                for k in range(vec_per_step):
                    row = jj * vec_per_step + k
                    o_ref[0, b, row:row + 1, :] = res[:, k * D_MODEL:(k + 1) * D_MODEL]

        @pl.when(j == 0)
        def _():
            o_ref[0, b, N_MOD:8, :] = jnp.zeros((8 - N_MOD, D_MODEL), F32)


def _mod_call(c, w_ada, b_ada):
    nb = c.shape[0]
    ct = jnp.pad(c.T, ((0, 0), (0, LANES - nb)))
    n_col = (N_MOD * D_MODEL) // MOD_COLS
    return pl.pallas_call(
        functools.partial(_mod_kernel, nb=nb),
        grid=(DEPTH, n_col),
        in_specs=[
            pl.BlockSpec((D_MODEL, LANES), lambda l, j: (0, 0)),
            pl.BlockSpec((1, D_MODEL, MOD_COLS), lambda l, j: (l, 0, j)),
            pl.BlockSpec((1, 1, MOD_COLS), lambda l, j: (l, 0, j)),
        ],
        out_specs=pl.BlockSpec((1, nb, 8, D_MODEL), lambda l, j: (l, 0, 0, 0)),
        out_shape=jax.ShapeDtypeStruct((DEPTH, nb, 8, D_MODEL), F32),
        compiler_params=pltpu.CompilerParams(
            dimension_semantics=("arbitrary", "arbitrary"), vmem_limit_bytes=VMEM_LIMIT),
        name="adaln_mod",
    )(ct, w_ada, b_ada.reshape(DEPTH, 1, N_MOD * D_MODEL))


def _sub_copies(hbm_ref, stage_ref, sem, rows, to_hbm):
    out = []
    for s in range(N_SUB):
        s_view = stage_ref.at[pl.ds(s * SUB, SUB), :]
        h_view = hbm_ref.at[pl.ds(pl.multiple_of(rows[s], SUB), SUB), :]
        out.append(pltpu.make_async_copy(s_view, h_view, sem) if to_hbm
                   else pltpu.make_async_copy(h_view, s_view, sem))
    return out


def _head_dma(t, n_steps, nb, pm_ref, ys_hbm, ystage_ref, gsem):
    slot = t % 2

    def copies(step, b, slot_):
        base = (step * nb + b) * META_W
        rows = [pm_ref[base + s] for s in range(N_SUB)]
        return _sub_copies(ys_hbm, ystage_ref.at[b, slot_], gsem.at[b, slot_], rows, False)

    @pl.when(t == 0)
    def _():
        ystage_ref[...] = jnp.zeros_like(ystage_ref)
        for b in range(nb):
            for cp in copies(0, b, 0):
                cp.start()

    nxt = jnp.minimum(t + 1, n_steps - 1)
    for b in range(nb):
        for cp in copies(nxt, b, 1 - slot):
            cp.start()

    for b in range(nb):
        for cp in copies(t, b, slot):
            cp.wait()

    def drain():
        for b in range(nb):
            for cp in copies(nxt, b, 1 - slot):
                cp.wait()
    return drain


def _head_combine(b, slot, hprev_ref, modprev_ref, tokmeta_ref, ystage_ref):
    lp = tokmeta_ref[b, :, LP_LANE:LP_LANE + 1].astype(I32)
    unsort = _onehot(lax.broadcasted_iota(I32, (TB, STG_C), 1) == lp)
    y = jnp.dot(unsort, ystage_ref[b, slot], preferred_element_type=F32)
    return hprev_ref[b] + modprev_ref[b, 5:6] * y


def _route(u2, wrt_ref, br_ref):
    uh = u2.astype(BF16)
    ul = (u2 - uh.astype(F32)).astype(BF16)
    w = wrt_ref[...]
    r1 = lax.dot_general(w, uh, NT_DIMS, preferred_element_type=F32)
    r2 = lax.dot_general(w[0:N_EXPERTS], ul, NT_DIMS, preferred_element_type=F32)
    yield
    scores = _sigmoid(r1[0:N_EXPERTS] + r1[N_EXPERTS:] + r2)
    biased = scores + br_ref[...]
    s = [scores[4 * j:4 * j + 4] for j in range(EXPERTS_PER_GROUP)]
    b = [biased[4 * j:4 * j + 4] for j in range(EXPERTS_PER_GROUP)]

    hi1, lo1 = jnp.maximum(b[0], b[1]), jnp.minimum(b[0], b[1])
    hi2, lo2 = jnp.maximum(b[2], b[3]), jnp.minimum(b[2], b[3])
    second = jnp.maximum(jnp.minimum(hi1, hi2), jnp.where(hi1 >= hi2, lo1, lo2))
    gscore = jnp.maximum(hi1, hi2) + second

    best = gscore[0:1]
    bgrp = jnp.zeros((1, TB), I32)
    for g in range(1, N_GROUPS):
        better = gscore[g:g + 1] > best
        bgrp = jnp.where(better, g, bgrp)
        best = jnp.where(better, gscore[g:g + 1], best)
    selmask = lax.broadcasted_iota(I32, (N_GROUPS, TB), 0) == bgrp
    yield

    m1 = b[0]
    i1 = jnp.zeros((N_GROUPS, TB), I32)
    for j in range(1, EXPERTS_PER_GROUP):
        gt = b[j] > m1
        i1 = jnp.where(gt, j, i1)
        m1 = jnp.where(gt, b[j], m1)
    cands = [jnp.where(i1 == j, -jnp.inf, b[j]) for j in range(EXPERTS_PER_GROUP)]
    m2 = cands[0]
    i2 = jnp.zeros_like(i1)
    for j in range(1, EXPERTS_PER_GROUP):
        gt = cands[j] > m2
        i2 = jnp.where(gt, j, i2)
        m2 = jnp.where(gt, cands[j], m2)
    w1 = s[0]
    w2 = s[0]
    for j in range(1, EXPERTS_PER_GROUP):
        w1 = jnp.where(i1 == j, s[j], w1)
        w2 = jnp.where(i2 == j, s[j], w2)
    yield
    tot = w1 + w2
    w1n = w1 / tot
    w2n = w2 / tot
    gates = []
    for j in range(EXPERTS_PER_GROUP):
        gj = jnp.where(i1 == j, w1n, 0.0) + jnp.where(i2 == j, w2n, 0.0)
        gates.append(jnp.sum(jnp.where(selmask, gj, 0.0), axis=0, keepdims=True))
    return bgrp, gates, selmask


def _tail_compute(b, slot, h1, mod, g_ffn, wrt_ref, br_ref, tri_ref, tokmeta_ref, stage_ref):
    sh2, sc2 = mod[3:4], mod[4:5]
    u2 = _rms_mod(h1, g_ffn, sc2, sh2)
    bgrp, gates, selmask = yield from _route(u2, wrt_ref, br_ref)
    yield

    onehot = jnp.concatenate([jnp.where(selmask, 1.0, 0.0), jnp.zeros((8 - N_GROUPS, TB), F32)],
                             axis=0).astype(BF16)
    rank_incl = jnp.dot(onehot, tri_ref[...], preferred_element_type=F32)
    rank = jnp.sum(jnp.where(selmask, rank_incl[0:N_GROUPS], 0.0), axis=0, keepdims=True) - 1.0
    cnt = lax.dot_general(jnp.ones((8, TB), BF16), onehot, NT_DIMS,
                          preferred_element_type=F32)
    sub_rows = jnp.floor((cnt[0:1, :] + (SUB - 1.0)) * (1.0 / SUB)) * SUB
    lp = rank
    first_row = jnp.zeros((1, 1), F32)
    for g in range(1, N_GROUPS):
        first_row = first_row + sub_rows[:, g - 1:g]
        lp = lp + jnp.where(bgrp == g, first_row, 0.0)

    hi = [x.astype(BF16).astype(F32) for x in gates]
    lo = [(x - h).astype(BF16).astype(F32) for x, h in zip(gates, hi)]
    meta_src = jnp.concatenate(hi + lo + [lp, jnp.zeros((LANES - 9, TB), F32)], axis=0)
    meta_t = meta_src.T
    tokmeta_ref[b] = meta_t
    yield

    rowdata = jnp.concatenate([u2.astype(BF16), meta_t.astype(BF16)], axis=1)
    sort = _onehot(lax.broadcasted_iota(I32, (STG_P, TB), 0) == lp.astype(I32))
    stage_ref[b, slot] = jnp.dot(sort, rowdata, preferred_element_type=F32).astype(BF16)
    return [cnt[0, g].astype(I32) for g in range(N_GROUPS)]


def _prime_scatter(nb, cap, xs_hbm, stage_ref, prev_ref, ssem):
    stage_ref[:, 1] = jnp.zeros((nb,) + stage_ref.shape[2:], BF16)
    for b in range(nb):
        rows = [jnp.int32(N_GROUPS * cap + (b * N_SUB + s) * SUB) for s in range(N_SUB)]
        for s in range(N_SUB):
            prev_ref[b * N_SUB + s] = rows[s]
        for cp in _sub_copies(xs_hbm, stage_ref.at[b, 1], ssem.at[b, 1], rows, True):
            cp.start()


def _tail_dma(t, n_steps, nb, cap, n_work, counts, tmeta_ref, xs_hbm, stage_ref, zeros_ref,
              run_ref, prev_ref, ssem, zsem, drain_head):
    slot = t % 2
    run = [run_ref[g] for g in range(N_GROUPS)]
    junk = N_GROUPS * cap
    scatter_rows = []
    for b in range(nb):
        first_sub = [jnp.int32(0)]
        for g in range(N_GROUPS):
            first_sub.append(first_sub[-1] + (counts[b][g] + (SUB - 1)) // SUB)
        dst = [g * cap + run[g] for g in range(N_GROUPS)]
        rows_b, base = [], (t * nb + b) * META_W
        for s in range(N_SUB):
            grp_first, grp_dst = first_sub[0], dst[0]
            for g in range(1, N_GROUPS):
                later = s >= first_sub[g]
                grp_first = jnp.where(later, first_sub[g], grp_first)
                grp_dst = jnp.where(later, dst[g], grp_dst)
            row = grp_dst + (s - grp_first) * SUB
            used = s < first_sub[N_GROUPS]
            rows_b.append(jnp.where(used, row, junk + (b * N_SUB + s) * SUB))
            tmeta_ref[base + s] = jnp.where(used, row, row if s == 0 else first_row)
            if s == 0:
                first_row = row
        for k in range(N_SUB, META_W):
            tmeta_ref[base + k] = jnp.int32(0)
        for g in range(N_GROUPS):
            run[g] = run[g] + (first_sub[g + 1] - first_sub[g]) * SUB
        scatter_rows.append(rows_b)
    for g in range(N_GROUPS):
        run_ref[g] = run[g]

    def copies(b, slot_, rows):
        return _sub_copies(xs_hbm, stage_ref.at[b, slot_], ssem.at[b, slot_], rows, True)

    for b in range(nb):
        for cp in copies(b, 1 - slot, [prev_ref[b * N_SUB + s] for s in range(N_SUB)]):
            cp.wait()

    for b in range(nb):
        for cp in copies(b, slot, scatter_rows[b]):
            cp.start()
        for s in range(N_SUB):
            prev_ref[b * N_SUB + s] = scatter_rows[b][s]

    @pl.when(t == n_steps - 1)
    def _():
        for b in range(nb):
            for cp in copies(b, slot, scatter_rows[b]):
                cp.wait()
        if drain_head is not None:
            drain_head()
        zeros_ref[...] = jnp.zeros_like(zeros_ref)
        tails = []
        base = n_steps * nb * META_W
        for g in range(N_GROUPS):
            tmeta_ref[base + g] = run[g]
            start = pl.multiple_of(g * cap + run[g], SUB)
            tails.append(pltpu.make_async_copy(zeros_ref, xs_hbm.at[pl.ds(start, TBM), :], zsem))
        for k in range(N_GROUPS, META_W):
            tmeta_ref[base + k] = jnp.int32(0)
        for cp in tails:
            cp.start()
        _work_tables(run, cap, n_work, tmeta_ref, base + META_W)
        for cp in tails:
            cp.wait()


def _attn_conv_mix(x, t, mod, g_mix, win_ref, wout_ref, bias_ref, sink_at, convw_ref,
                   kvprev_ref, cprev_ref):
    tb = TB
    first = t == 0
    sh1, sc1, g1 = mod[0:1], mod[1:2], mod[2:3]
    u = _rms_mod(x, g_mix, sc1, sh1)
    proj = jnp.dot(u.astype(BF16), win_ref[...], preferred_element_type=F32)

    yield
    q = (proj[:, 0:ATTN_WIDTH] * HEAD_DIM ** -0.5).astype(BF16)
    kf = proj[:, 512:640]
    vf = proj[:, 640:768]
    bgate = proj[:, 768:1280]
    cgate = proj[:, 1280:1792]
    xv = proj[:, 1792:2304]

    kv_prev = kvprev_ref[...]
    kext = jnp.concatenate([kv_prev[:, 0:KV_WIDTH], kf], axis=0)
    vext = jnp.concatenate([kv_prev[:, KV_WIDTH:], vf], axis=0)
    kvprev_ref[:, 0:KV_WIDTH] = kf[tb - BLOCK:tb]
    kvprev_ref[:, KV_WIDTH:] = vf[tb - BLOCK:tb]

    lane = lax.broadcasted_iota(I32, kext.shape, 1)
    lo = lane < HEAD_DIM
    krol = pltpu.roll(kext, HEAD_DIM, axis=1)
    vrol = pltpu.roll(vext, HEAD_DIM, axis=1)
    zero = jnp.zeros_like(kext)
    one_at_64 = jnp.where(lane == HEAD_DIM, 1.0, 0.0)
    one_at_0 = jnp.where(lane == 0, 1.0, 0.0)
    k_ops = [(jnp.where(lo, kext, zero).astype(BF16), jnp.where(lo, zero, krol).astype(BF16)),
             (jnp.where(lo, krol, zero).astype(BF16), jnp.where(lo, zero, kext).astype(BF16))]
    v_ops = [(jnp.where(lo, vext, one_at_64).astype(BF16), jnp.where(lo, one_at_0, vrol).astype(BF16)),
             (jnp.where(lo, vrol, one_at_64).astype(BF16), jnp.where(lo, one_at_0, vext).astype(BF16))]

    col = lax.broadcasted_iota(I32, (2 * BLOCK, 4 * BLOCK), 1)
    prev_cols = (col % (2 * BLOCK)) < BLOCK
    upper = lax.broadcasted_iota(I32, (2 * BLOCK, 1), 0) < BLOCK
    lane_o = lax.broadcasted_iota(I32, (2 * BLOCK, LANES), 1)
    lo_o = lane_o < HEAD_DIM

    yield
    attn_rows = []
    for bi in range(tb // BLOCK):
        r0 = bi * BLOCK
        pair_out = []
        for kvh in range(N_HEADS // 4):
            ka, kb = k_ops[kvh]
            rhs = jnp.concatenate([ka[r0:r0 + 2 * BLOCK], kb[r0:r0 + 2 * BLOCK]], axis=0)
            qrows = q[r0:r0 + BLOCK]
            qp = jnp.concatenate([qrows[:, (2 * kvh) * LANES:(2 * kvh + 1) * LANES],
                                  qrows[:, (2 * kvh + 1) * LANES:(2 * kvh + 2) * LANES]], axis=0)
            s = lax.dot_general(qp, rhs, NT_DIMS, preferred_element_type=F32)
            bias = bias_ref[kvh]
            if bi == 0:
                bias = jnp.where(prev_cols & first, NEG_BIG, bias)
            s = s + bias
            outs = []
            for hh in range(2):
                sh = s[:, hh * 2 * BLOCK:(hh + 1) * 2 * BLOCK]
                sink = jnp.where(upper, sink_at(4 * kvh + hh), sink_at(4 * kvh + 2 + hh))
                m = jnp.maximum(jnp.max(sh, axis=-1, keepdims=True), sink)
                p = jnp.exp(sh - m).astype(BF16)
                vop = v_ops[kvh][hh][r0:r0 + 2 * BLOCK]
                o = jnp.dot(p, vop, preferred_element_type=F32)
                den_col = HEAD_DIM if hh == 0 else 0
                den = o[:, den_col:den_col + 1] + jnp.exp(sink - m)
                outs.append(o / den)
            both = jnp.where(lo_o, outs[0], outs[1])
            pair_out += [both[0:BLOCK], both[BLOCK:]]
            yield
        attn_rows.append(jnp.concatenate(pair_out, axis=1))
    attn = jnp.concatenate(attn_rows, axis=0)

    uc = cgate * xv
    cprev = jnp.where(first, 0.0, cprev_ref[...])
    row = lax.broadcasted_iota(I32, uc.shape, 0)
    r1 = jnp.where(row == 0, cprev[7:8], pltpu.roll(uc, 1, axis=0))
    r2 = jnp.where(row == 0, cprev[6:7], jnp.where(row == 1, cprev[7:8], pltpu.roll(uc, 2, axis=0)))
    cprev_ref[...] = uc[tb - 8:tb]
    cw = convw_ref[...]
    conv = bgate * (cw[0:1] * r2 + cw[1:2] * r1 + cw[2:3] * uc)
    yield

    mix = (jnp.dot(attn.astype(BF16), wout_ref[0:ATTN_WIDTH, :], preferred_element_type=F32)
           + jnp.dot(conv.astype(BF16), wout_ref[ATTN_WIDTH:, :], preferred_element_type=F32))
    return x + g1 * mix


def _pool_mix(x, t, mod, g_mix, wpool_ref, band_ref, pscale, uprev_ref):
    tb = TB
    sh1, sc1, g1 = mod[0:1], mod[1:2], mod[2:3]
    u = _rms_mod(x, g_mix, sc1, sh1)
    halo = jnp.where(t == 0, 0.0, uprev_ref[...]).astype(BF16)
    ext = jnp.concatenate([jnp.zeros((LANES - POOL_HALO, D_MODEL), BF16), halo, u.astype(BF16)],
                          axis=0)
    uprev_ref[...] = u[tb - POOL_HALO:tb]

    yield
    pos = (t * tb + 1 + lax.broadcasted_iota(I32, (tb, 1), 0)).astype(F32)
    cols = [slice(gi * POOL_GROUP, (gi + 1) * POOL_GROUP) for gi in range(len(POOL_SIZES))]
    wsums = [jnp.dot(band_ref[gi], ext[:, sl], preferred_element_type=F32)
             for gi, sl in enumerate(cols)]
    yield
    mixed = []
    for gi, w in enumerate(POOL_SIZES):
        mean = wsums[gi] / jnp.minimum(pos, float(w))
        pooled = mean - u[:, cols[gi]]
        mixed.append(jnp.dot(pooled.astype(BF16), wpool_ref[gi].astype(BF16),
                             preferred_element_type=F32))
    yield
    mix = jnp.concatenate(mixed, axis=1) * pscale
    return x + g1 * mix


def _interleave(chains, lag):
    results = [None] * len(chains)
    live = list(range(len(chains)))
    rnd = 0
    while live:
        for k in list(live):
            if rnd < k * lag:
                continue
            try:
                next(chains[k])
            except StopIteration as done:
                results[k] = done.value
                live.remove(k)
        rnd += 1
    return results


def _mixer_kernel(*refs, kind, layer, has_head, nb, n_steps, cap, n_work):
    refs = list(refs)
    t = pl.program_id(0)
    slot = t % 2
    if has_head:
        pm_ref = refs.pop(0)
        hprev_ref, modprev_ref, tokprev_ref, ys_hbm = refs[:4]
        refs = refs[4:]
    else:
        h_ref = refs.pop(0)
    mod_ref, nmix_ref, nffn_ref = refs[:3]
    refs = refs[3:]
    n_w = 5 if kind == "even" else 3
    weight_refs = refs[:n_w]
    refs = refs[n_w:]
    wrt_ref, br_ref, tri_ref = refs[:3]
    refs = refs[3:]
    hout_ref, tokmeta_ref, tmeta_ref, xs_hbm = refs[:4]
    refs = refs[4:]
    n_c = 2 if kind == "even" else 1
    carry_refs = refs[:n_c]
    refs = refs[n_c:]
    if kind == "even":
        cast_refs = refs[:2]
        refs = refs[2:]
    stage_ref, zeros_ref, run_ref, prev_ref, ssem, zsem = refs[:6]
    refs = refs[6:]

    @pl.when(t == 0)
    def _():
        for r in carry_refs:
            r[...] = jnp.zeros_like(r)
        for g in range(N_GROUPS):
            run_ref[g] = jnp.int32(0)
        if kind == "even":
            for src, dst in zip(weight_refs[:2], cast_refs):
                for c0 in range(0, src.shape[1], 2 * LANES):
                    dst[:, c0:c0 + 2 * LANES] = src[:, c0:c0 + 2 * LANES].astype(BF16)
        _prime_scatter(nb, cap, xs_hbm, stage_ref, prev_ref, ssem)

    j = layer // 2
    g_mix = nmix_ref[layer:layer + 1]
    g_ffn = nffn_ref[layer:layer + 1]
    if kind == "even":
        sink_ref = weight_refs[3]
        weight_refs = list(cast_refs) + [weight_refs[2], lambda i: sink_ref[j, i], weight_refs[4]]
    else:
        weight_refs = list(weight_refs[:2]) + [weight_refs[2][j:j + 1]]

    drain_head = None
    if has_head:
        ystage_ref, gsem = refs
        drain_head = _head_dma(t, n_steps, nb, pm_ref, ys_hbm, ystage_ref, gsem)

    def chain(b):
        if has_head:
            x = _head_combine(b, slot, hprev_ref, modprev_ref, tokprev_ref, ystage_ref)
            yield
        else:
            x = h_ref[b]
        mod = mod_ref[b]
        carries = [r.at[b] for r in carry_refs]
        mix = _attn_conv_mix if kind == "even" else _pool_mix
        h1 = yield from mix(x, t, mod, g_mix, *weight_refs, *carries)
        hout_ref[b] = h1
        yield
        return (yield from _tail_compute(b, slot, h1, mod, g_ffn, wrt_ref, br_ref, tri_ref,
                                         tokmeta_ref, stage_ref))

    counts = _interleave([chain(b) for b in range(nb)], CHAIN_LAG)
    _tail_dma(t, n_steps, nb, cap, n_work, counts, tmeta_ref, xs_hbm, stage_ref, zeros_ref,
              run_ref, prev_ref, ssem, zsem, drain_head)


def _nat(nb, cols):
    return pl.BlockSpec((nb, TB, cols), lambda t, *_: (0, t, 0))


def _full(shape):
    nd = len(shape)
    return pl.BlockSpec(shape, lambda t, *_: (0,) * nd)


def _head_scratch(nb):
    return [pltpu.VMEM((nb, 2, STG_C, D_MODEL), BF16), pltpu.SemaphoreType.DMA((nb, 2))]


def _mod_spec(nb, layer):
    return pl.BlockSpec((None, nb, 8, D_MODEL), lambda t, *_: (layer, 0, 0, 0))


def _mixer_call(kind, layer, head, h, mod_all, norm_mix, norm_ffn, weights, wrt, br, tri):
    j = layer // 2
    nb, seq, _ = (h if head is None else head[1]).shape
    n_steps = seq // TB
    n_tiles, cap, n_work = _sizes(nb * seq)
    has_head = head is not None
    any_spec = pl.BlockSpec(memory_space=pl.ANY)
    smem_spec = pl.BlockSpec(memory_space=pltpu.SMEM)

    if has_head:
        pm, hprev, tokprev, ys = head
        args = [hprev, mod_all, tokprev, ys]
        in_specs = [_nat(nb, D_MODEL), _mod_spec(nb, layer - 1), _nat(nb, LANES), any_spec]
    else:
        args = [h]
        in_specs = [_nat(nb, D_MODEL)]
    args += [mod_all, norm_mix, norm_ffn]
    in_specs += [_mod_spec(nb, layer), _full(norm_mix.shape), _full(norm_ffn.shape)]
    if kind == "even":
        args += list(weights)
        once = dict(pipeline_mode=pl.Buffered(1))
        in_specs += [pl.BlockSpec((None, D_MODEL, IN_WIDTH), lambda t, *_: (j, 0, 0), **once),
                     pl.BlockSpec((None, D_MODEL, D_MODEL), lambda t, *_: (j, 0, 0), **once),
                     _full((N_HEADS // 4, 2 * BLOCK, 4 * BLOCK)), smem_spec,
                     pl.BlockSpec((None,) + weights[4].shape[1:], lambda t, *_: (j, 0, 0))]
        mix_scratch = [pltpu.VMEM((nb, BLOCK, 2 * KV_WIDTH), F32),
                       pltpu.VMEM((nb, 8, CONV_WIDTH), F32),
                       pltpu.VMEM((D_MODEL, IN_WIDTH), BF16),
                       pltpu.VMEM((D_MODEL, D_MODEL), BF16)]
    else:
        args += list(weights)
        in_specs += [pl.BlockSpec((None, len(POOL_SIZES), POOL_GROUP, POOL_GROUP),
                                  lambda t, *_: (j, 0, 0, 0)),
                     _full((len(POOL_SIZES), TB, TB + LANES)), _full(weights[2].shape)]
        mix_scratch = [pltpu.VMEM((nb, POOL_HALO, D_MODEL), F32)]
    args += [wrt, br, tri]
    in_specs += [_full((2 * N_EXPERTS, D_MODEL)), _full((N_EXPERTS, 1)), _full((TB, TB))]

    out_shape = (jax.ShapeDtypeStruct((nb, seq, D_MODEL), F32),
                 jax.ShapeDtypeStruct((nb, seq, LANES), F32),
                 jax.ShapeDtypeStruct(((n_tiles + 1) * META_W + 4 * n_work,), I32),
                 jax.ShapeDtypeStruct((N_GROUPS * cap + nb * N_SUB * SUB, ROW_W), BF16))
    out_specs = (_nat(nb, D_MODEL), _nat(nb, LANES), smem_spec, any_spec)
    scratch = mix_scratch + [
        pltpu.VMEM((nb, 2, STG_P, ROW_W), BF16),
        pltpu.VMEM((TBM, ROW_W), BF16),
        pltpu.SMEM((N_GROUPS,), I32),
        pltpu.SMEM((nb * N_SUB,), I32),
        pltpu.SemaphoreType.DMA((nb, 2)),
        pltpu.SemaphoreType.DMA(()),
    ]
    if has_head:
        scratch += _head_scratch(nb)

    body = functools.partial(_mixer_kernel, kind=kind, layer=layer, has_head=has_head, nb=nb,
                             n_steps=n_steps, cap=cap, n_work=n_work)
    grid_spec = pltpu.PrefetchScalarGridSpec(
        num_scalar_prefetch=1 if has_head else 0, grid=(n_steps,),
        in_specs=in_specs, out_specs=out_specs, scratch_shapes=scratch)
    call = pl.pallas_call(
        body, grid_spec=grid_spec, out_shape=out_shape,
        compiler_params=pltpu.CompilerParams(
            dimension_semantics=("arbitrary",), vmem_limit_bytes=VMEM_LIMIT),
        name=kind + "_mixer")
    if has_head:
        return call(pm, *args)
    return call(*args)


def _work_tables(totals, cap, n_work, tmeta_ref, t0):
    ends, starts = [], []
    acc = jnp.int32(0)
    for g in range(N_GROUPS):
        starts.append(acc)
        acc = acc + (totals[g] + (TBM - 1)) // TBM
        ends.append(acc)
    nvalid = ends[-1]
    prev_grp = jnp.int32(-1)
    for i in range(n_work):
        idc = jnp.minimum(jnp.int32(i), nvalid - 1)
        grp, start = jnp.int32(0), starts[0]
        for g in range(1, N_GROUPS):
            later = idc >= ends[g - 1]
            grp = jnp.where(later, g, grp)
            start = jnp.where(later, starts[g], start)
        valid = i < nvalid
        tmeta_ref[t0 + i] = grp * (cap // TBM) + idc - start
        tmeta_ref[t0 + n_work + i] = grp
        tmeta_ref[t0 + 2 * n_work + i] = valid.astype(I32)
        tmeta_ref[t0 + 3 * n_work + i] = jnp.logical_and(valid, grp != prev_grp).astype(I32)
        prev_grp = grp


def _moe_kernel(tm_ref, xs_ref, wg_ref, wu_ref, wd_ref, ys_ref, wg_s, wu_s, wd_s, *, t0, n_work):
    i = pl.program_id(0)
    valid = tm_ref[t0 + 2 * n_work + i] == 1
    first_of_group = tm_ref[t0 + 3 * n_work + i] == 1

    @pl.when(first_of_group)
    def _():
        for e in range(EXPERTS_PER_GROUP):
            wg_s[e] = wg_ref[e].astype(BF16)
            wu_s[e] = wu_ref[e].astype(BF16)
        wd_s[...] = wd_ref[0].astype(BF16)

    @pl.when(valid)
    def _():
        xb = xs_ref[:, 0:D_MODEL]
        meta = xs_ref[:, D_MODEL:ROW_W].astype(F32)
        gates = meta[:, 0:EXPERTS_PER_GROUP] + meta[:, EXPERTS_PER_GROUP:2 * EXPERTS_PER_GROUP]
        parts = []
        for e in range(EXPERTS_PER_GROUP):
            a = jnp.dot(xb, wg_s[e], preferred_element_type=F32)
            bu = jnp.dot(xb, wu_s[e], preferred_element_type=F32)
            hid = (a * _sigmoid(a)) * bu
            parts.append((hid * gates[:, e:e + 1]).astype(BF16))
        hid_all = jnp.concatenate(parts, axis=1)
        ys_ref[...] = jnp.dot(hid_all, wd_s[...], preferred_element_type=F32).astype(BF16)


def _moe_call(xs, tmeta, layer, wg, wu, wd, n_tiles, n_work):
    t0 = (n_tiles + 1) * META_W
    row_map = lambda i, tm: (tm[t0 + i], 0)
    grp_map = lambda i, tm: (layer, tm[t0 + n_work + i], 0, 0)
    grid_spec = pltpu.PrefetchScalarGridSpec(
        num_scalar_prefetch=1, grid=(n_work,),
        in_specs=[
            pl.BlockSpec((TBM, ROW_W), row_map),
            pl.BlockSpec((None, EXPERTS_PER_GROUP, D_MODEL, EXPERT_FF), grp_map),
            pl.BlockSpec((None, EXPERTS_PER_GROUP, D_MODEL, EXPERT_FF), grp_map),
            pl.BlockSpec((None, 1, GROUP_FF, D_MODEL), grp_map),
        ],
        out_specs=pl.BlockSpec((TBM, D_MODEL), row_map),
        scratch_shapes=[pltpu.VMEM((EXPERTS_PER_GROUP, D_MODEL, EXPERT_FF), BF16),
                        pltpu.VMEM((EXPERTS_PER_GROUP, D_MODEL, EXPERT_FF), BF16),
                        pltpu.VMEM((GROUP_FF, D_MODEL), BF16)],
    )
    return pl.pallas_call(
        functools.partial(_moe_kernel, t0=t0, n_work=n_work), grid_spec=grid_spec,
        out_shape=jax.ShapeDtypeStruct((xs.shape[0], D_MODEL), BF16),
        compiler_params=pltpu.CompilerParams(
            dimension_semantics=("arbitrary",), vmem_limit_bytes=VMEM_LIMIT),
        name="moe_experts",
    )(tmeta, xs, wg, wu, wd)


def _final_kernel(pm_ref, hprev_ref, modprev_ref, tokprev_ref, ys_hbm, nf_ref, o_ref,
                  ystage_ref, gsem, *, nb, n_steps):
    t = pl.program_id(0)
    drain = _head_dma(t, n_steps, nb, pm_ref, ys_hbm, ystage_ref, gsem)
    for b in range(nb):
        h = _head_combine(b, t % 2, hprev_ref, modprev_ref, tokprev_ref, ystage_ref)
        ms = jnp.mean(h * h, axis=-1, keepdims=True)
        o_ref[b] = (h * lax.rsqrt(ms + EPS)) * nf_ref[0:1]
    pl.when(t == n_steps - 1)(drain)


def _final_call(head, mod_all, norm_final):
    pm, hprev, tokprev, ys = head
    nb, seq, _ = hprev.shape
    n_steps = seq // TB
    grid_spec = pltpu.PrefetchScalarGridSpec(
        num_scalar_prefetch=1, grid=(n_steps,),
        in_specs=[_nat(nb, D_MODEL), _mod_spec(nb, DEPTH - 1), _nat(nb, LANES),
                  pl.BlockSpec(memory_space=pl.ANY), _full((1, D_MODEL))],
        out_specs=_nat(nb, D_MODEL),
        scratch_shapes=_head_scratch(nb))
    return pl.pallas_call(
        functools.partial(_final_kernel, nb=nb, n_steps=n_steps), grid_spec=grid_spec,
        out_shape=jax.ShapeDtypeStruct((nb, seq, D_MODEL), F32),
        compiler_params=pltpu.CompilerParams(
            dimension_semantics=("arbitrary",), vmem_limit_bytes=VMEM_LIMIT),
        name="final_norm",
    )(pm, hprev, mod_all, tokprev, ys, norm_final.reshape(1, D_MODEL))


def _pad_rows(a, rows=8):
    return jnp.pad(a, ((0, rows - a.shape[0]), (0, 0)))


def _pool_band_table():
    r = jnp.arange(TB)[:, None]
    c = jnp.arange(TB + LANES)[None, :]
    return jnp.stack([(c > r + LANES - w) & (c <= r + LANES) for w in POOL_SIZES]).astype(BF16)


def _attn_bias_table():
    slopes = 2.0 ** (-8.0 * (jnp.arange(N_HEADS, dtype=F32) + 1.0) / N_HEADS)
    dist = (jnp.arange(BLOCK)[:, None] + BLOCK) - jnp.arange(2 * BLOCK)[None, :]
    ok = (dist >= 0) & (dist < WINDOW)
    per_head = jnp.where(ok[None], -slopes[:, None, None] * dist.astype(F32)[None], NEG_BIG)
    return per_head.reshape(N_HEADS // 2, 2, BLOCK, 2 * BLOCK).transpose(0, 2, 1, 3).reshape(
        N_HEADS // 4, 2 * BLOCK, 4 * BLOCK)


def _router_operands(w_router, b_router):
    def reorder(a):
        return a.reshape(N_GROUPS, EXPERTS_PER_GROUP, -1).transpose(1, 0, 2).reshape(N_EXPERTS, -1)
    w = reorder(w_router.T)
    wh = w.astype(BF16)
    wl = (w - wh.astype(F32)).astype(BF16)
    return jnp.concatenate([wh, wl], axis=0), reorder(b_router[:, None])


def kernel(x, c, w_ada, b_ada, norm_mix, norm_ffn, w_in, w_out, sinks, conv_w, w_pool,
           pool_scale, w_router, b_router, w_gate, w_up, w_down, norm_final):
    b, s, _ = x.shape
    n_tiles, cap, n_work = _sizes(b * s)

    mod_all = _mod_call(c, w_ada, b_ada)
    bias_tab = _attn_bias_table()
    band_tab = _pool_band_table()
    wrt, br = _router_operands(w_router, b_router)
    tri = jnp.triu(jnp.ones((TB, TB), F32)).astype(BF16)
    w_down4 = w_down.reshape(DEPTH, N_GROUPS, GROUP_FF, D_MODEL)

    h = x
    head = None
    for layer in range(DEPTH):
        if layer % 2 == 0:
            kind = "even"
            weights = (w_in, w_out, bias_tab, sinks, conv_w)
        else:
            kind = "odd"
            weights = (w_pool, band_tab, pool_scale)
        h1, tokmeta, tmeta, xs = _mixer_call(kind, layer, head, h, mod_all, norm_mix, norm_ffn,
                                             weights, wrt, br, tri)
        ys = _moe_call(xs, tmeta, layer, w_gate, w_up, w_down4, n_tiles, n_work)
        head = (tmeta, h1, tokmeta, ys)
    return _final_call(head, mod_all, norm_final)
```

```python
import functools

import jax
import jax.numpy as jnp
from jax import lax
from jax.experimental import pallas as pl
from jax.experimental.pallas import tpu as pltpu

F32 = jnp.float32
BF16 = jnp.bfloat16
I32 = jnp.int32

D_MODEL = 1024
DEPTH = 4
EPS = 1e-6
N_MOD = 6

ATTN_WIDTH = 512
HEAD_DIM = 64
N_HEADS = 8
KV_WIDTH = 128
WINDOW = 128
BLOCK = 128
CONV_WIDTH = 512
IN_WIDTH = 2304

POOL_SIZES = (2, 4, 8, 16)
POOL_GROUP = 256
POOL_HALO = 16

N_EXPERTS = 16
N_GROUPS = 4
EXPERTS_PER_GROUP = 4
EXPERT_FF = 256
GROUP_FF = EXPERTS_PER_GROUP * EXPERT_FF

LANES = 128
NEG_BIG = -1e30

TB = 256
SUB = 16
N_SUB = TB // SUB + N_GROUPS - 1
STG_P = 320
STG_C = 384
TBM = 1024
TBM_PART = 256
ROW_W = D_MODEL + LANES
META_W = 32
LP_LANE = 8
CHAIN_LAG = 0
MOD_COLS = 3072
VMEM_LIMIT = 56 * 1024 * 1024
NT_DIMS = (((1,), (1,)), ((), ()))

assert N_SUB * SUB <= STG_P <= STG_C and STG_C % LANES == 0 and N_SUB <= META_W


def _sizes(n_tok):
    n_tiles = n_tok // TB
    cap = -(-(n_tok + n_tiles * (SUB - 1) + TBM) // TBM) * TBM
    n_work = (n_tok + n_tiles * N_GROUPS * (SUB - 1)) // TBM + N_GROUPS
    return n_tiles, cap, n_work


def _sigmoid(x):
    return 1.0 / (1.0 + jnp.exp(-x))


def _rms_mod(x, g, sc, sh):
    ms = jnp.mean(x * x, axis=-1, keepdims=True)
    return (x * lax.rsqrt(ms + EPS)) * (g * (1.0 + sc)) + sh


def _onehot(cond):
    return jnp.where(cond, 1.0, 0.0).astype(BF16)


def _mod_kernel(ct_ref, w_ref, b_ref, o_ref, *, nb):
    j = pl.program_id(1)
    ct = ct_ref[...]
    cond = ct * _sigmoid(ct)
    vec_per_step = MOD_COLS // D_MODEL
    for b in range(nb):
        col = cond[:, b:b + 1]
        acc = jnp.zeros((8, MOD_COLS), F32)
        for r in range(D_MODEL // 8):
            acc = acc + w_ref[0, 8 * r:8 * r + 8, :] * col[8 * r:8 * r + 8]
        res = jnp.sum(acc, axis=0, keepdims=True) + b_ref[0]
        for jj in range(N_MOD // vec_per_step):
            @pl.when(j == jj)
            def _():
                for k in range(vec_per_step):
                    row = jj * vec_per_step + k
                    o_ref[0, b, row:row + 1, :] = res[:, k * D_MODEL:(k + 1) * D_MODEL]

        @pl.when(j == 0)
        def _():
            o_ref[0, b, N_MOD:8, :] = jnp.zeros((8 - N_MOD, D_MODEL), F32)


def _mod_call(c, w_ada, b_ada):
    nb = c.shape[0]
    ct = jnp.pad(c.T, ((0, 0), (0, LANES - nb)))
    n_col = (N_MOD * D_MODEL) // MOD_COLS
    return pl.pallas_call(
        functools.partial(_mod_kernel, nb=nb),
        grid=(DEPTH, n_col),
        in_specs=[
            pl.BlockSpec((D_MODEL, LANES), lambda l, j: (0, 0)),
            pl.BlockSpec((1, D_MODEL, MOD_COLS), lambda l, j: (l, 0, j)),
            pl.BlockSpec((1, 1, MOD_COLS), lambda l, j: (l, 0, j)),
        ],
        out_specs=pl.BlockSpec((1, nb, 8, D_MODEL), lambda l, j: (l, 0, 0, 0)),
        out_shape=jax.ShapeDtypeStruct((DEPTH, nb, 8, D_MODEL), F32),
        compiler_params=pltpu.CompilerParams(
            dimension_semantics=("arbitrary", "arbitrary"), vmem_limit_bytes=VMEM_LIMIT),
        name="adaln_mod",
    )(ct, w_ada, b_ada.reshape(DEPTH, 1, N_MOD * D_MODEL))


def _sub_copies(hbm_ref, stage_ref, sem, rows, to_hbm):
    out = []
    for s in range(N_SUB):
        s_view = stage_ref.at[pl.ds(s * SUB, SUB), :]
        h_view = hbm_ref.at[pl.ds(pl.multiple_of(rows[s], SUB), SUB), :]
        out.append(pltpu.make_async_copy(s_view, h_view, sem) if to_hbm
                   else pltpu.make_async_copy(h_view, s_view, sem))
    return out


def _head_dma(t, n_steps, nb, pm_ref, ys_hbm, ystage_ref, gsem):
    slot = t % 2

    def copies(step, b, slot_):
        base = (step * nb + b) * META_W
        rows = [pm_ref[base + s] for s in range(N_SUB)]
        return _sub_copies(ys_hbm, ystage_ref.at[b, slot_], gsem.at[b, slot_], rows, False)

    @pl.when(t == 0)
    def _():
        ystage_ref[...] = jnp.zeros_like(ystage_ref)
        for b in range(nb):
            for cp in copies(0, b, 0):
                cp.start()

    nxt = jnp.minimum(t + 1, n_steps - 1)
    for b in range(nb):
        for cp in copies(nxt, b, 1 - slot):
            cp.start()

    for b in range(nb):
        for cp in copies(t, b, slot):
            cp.wait()

    def drain():
        for b in range(nb):
            for cp in copies(nxt, b, 1 - slot):
                cp.wait()
    return drain


def _head_combine(b, slot, hprev_ref, modprev_ref, tokmeta_ref, ystage_ref):
    lp = tokmeta_ref[b, :, LP_LANE:LP_LANE + 1].astype(I32)
    unsort = _onehot(lax.broadcasted_iota(I32, (TB, STG_C), 1) == lp)
    y = jnp.dot(unsort, ystage_ref[b, slot], preferred_element_type=F32)
    return hprev_ref[b] + modprev_ref[b, 5:6] * y


def _route(u2, wrt_ref, br_ref):
    uh = u2.astype(BF16)
    ul = (u2 - uh.astype(F32)).astype(BF16)
    w = wrt_ref[...]
    r1 = lax.dot_general(w, uh, NT_DIMS, preferred_element_type=F32)
    r2 = lax.dot_general(w[0:N_EXPERTS], ul, NT_DIMS, preferred_element_type=F32)
    yield
    scores = _sigmoid(r1[0:N_EXPERTS] + r1[N_EXPERTS:] + r2)
    biased = scores + br_ref[...]
    s = [scores[4 * j:4 * j + 4] for j in range(EXPERTS_PER_GROUP)]
    b = [biased[4 * j:4 * j + 4] for j in range(EXPERTS_PER_GROUP)]

    hi1, lo1 = jnp.maximum(b[0], b[1]), jnp.minimum(b[0], b[1])
    hi2, lo2 = jnp.maximum(b[2], b[3]), jnp.minimum(b[2], b[3])
    second = jnp.maximum(jnp.minimum(hi1, hi2), jnp.where(hi1 >= hi2, lo1, lo2))
    gscore = jnp.maximum(hi1, hi2) + second

    best = gscore[0:1]
    bgrp = jnp.zeros((1, TB), I32)
    for g in range(1, N_GROUPS):
        better = gscore[g:g + 1] > best
        bgrp = jnp.where(better, g, bgrp)
        best = jnp.where(better, gscore[g:g + 1], best)
    selmask = lax.broadcasted_iota(I32, (N_GROUPS, TB), 0) == bgrp
    yield

    m1 = b[0]
    i1 = jnp.zeros((N_GROUPS, TB), I32)
    for j in range(1, EXPERTS_PER_GROUP):
        gt = b[j] > m1
        i1 = jnp.where(gt, j, i1)
        m1 = jnp.where(gt, b[j], m1)
    cands = [jnp.where(i1 == j, -jnp.inf, b[j]) for j in range(EXPERTS_PER_GROUP)]
    m2 = cands[0]
    i2 = jnp.zeros_like(i1)
    for j in range(1, EXPERTS_PER_GROUP):
        gt = cands[j] > m2
        i2 = jnp.where(gt, j, i2)
        m2 = jnp.where(gt, cands[j], m2)
    w1 = s[0]
    w2 = s[0]
    for j in range(1, EXPERTS_PER_GROUP):
        w1 = jnp.where(i1 == j, s[j], w1)
        w2 = jnp.where(i2 == j, s[j], w2)
    yield
    tot = w1 + w2
    w1n = w1 / tot
    w2n = w2 / tot
    gates = []
    for j in range(EXPERTS_PER_GROUP):
        gj = jnp.where(i1 == j, w1n, 0.0) + jnp.where(i2 == j, w2n, 0.0)
        gates.append(jnp.sum(jnp.where(selmask, gj, 0.0), axis=0, keepdims=True))
    return bgrp, gates, selmask


def _tail_compute(b, slot, h1, mod, g_ffn, wrt_ref, br_ref, tri_ref, tokmeta_ref, stage_ref):
    sh2, sc2 = mod[3:4], mod[4:5]
    u2 = _rms_mod(h1, g_ffn, sc2, sh2)
    bgrp, gates, selmask = yield from _route(u2, wrt_ref, br_ref)
    yield

    onehot = jnp.concatenate([jnp.where(selmask, 1.0, 0.0), jnp.zeros((8 - N_GROUPS, TB), F32)],
                             axis=0).astype(BF16)
    rank_incl = jnp.dot(onehot, tri_ref[...], preferred_element_type=F32)
    rank = jnp.sum(jnp.where(selmask, rank_incl[0:N_GROUPS], 0.0), axis=0, keepdims=True) - 1.0
    cnt = lax.dot_general(jnp.ones((8, TB), BF16), onehot, NT_DIMS,
                          preferred_element_type=F32)
    sub_rows = jnp.floor((cnt[0:1, :] + (SUB - 1.0)) * (1.0 / SUB)) * SUB
    lp = rank
    first_row = jnp.zeros((1, 1), F32)
    for g in range(1, N_GROUPS):
        first_row = first_row + sub_rows[:, g - 1:g]
        lp = lp + jnp.where(bgrp == g, first_row, 0.0)

    hi = [x.astype(BF16).astype(F32) for x in gates]
    lo = [(x - h).astype(BF16).astype(F32) for x, h in zip(gates, hi)]
    meta_src = jnp.concatenate(hi + lo + [lp, jnp.zeros((LANES - 9, TB), F32)], axis=0)
    meta_t = meta_src.T
    tokmeta_ref[b] = meta_t
    yield

    rowdata = jnp.concatenate([u2.astype(BF16), meta_t.astype(BF16)], axis=1)
    sort = _onehot(lax.broadcasted_iota(I32, (STG_P, TB), 0) == lp.astype(I32))
    stage_ref[b, slot] = jnp.dot(sort, rowdata, preferred_element_type=F32).astype(BF16)
    return [cnt[0, g].astype(I32) for g in range(N_GROUPS)]


def _prime_scatter(nb, cap, xs_hbm, stage_ref, prev_ref, ssem):
    stage_ref[:, 1] = jnp.zeros((nb,) + stage_ref.shape[2:], BF16)
    for b in range(nb):
        rows = [jnp.int32(N_GROUPS * cap + (b * N_SUB + s) * SUB) for s in range(N_SUB)]
        for s in range(N_SUB):
            prev_ref[b * N_SUB + s] = rows[s]
        for cp in _sub_copies(xs_hbm, stage_ref.at[b, 1], ssem.at[b, 1], rows, True):
            cp.start()


def _tail_dma(t, n_steps, nb, cap, n_work, counts, tmeta_ref, xs_hbm, stage_ref, zeros_ref,
              run_ref, prev_ref, ssem, zsem, drain_head):
    slot = t % 2
    run = [run_ref[g] for g in range(N_GROUPS)]
    junk = N_GROUPS * cap
    scatter_rows = []
    for b in range(nb):
        first_sub = [jnp.int32(0)]
        for g in range(N_GROUPS):
            first_sub.append(first_sub[-1] + (counts[b][g] + (SUB - 1)) // SUB)
        dst = [g * cap + run[g] for g in range(N_GROUPS)]
        rows_b, base = [], (t * nb + b) * META_W
        for s in range(N_SUB):
            grp_first, grp_dst = first_sub[0], dst[0]
            for g in range(1, N_GROUPS):
                later = s >= first_sub[g]
                grp_first = jnp.where(later, first_sub[g], grp_first)
                grp_dst = jnp.where(later, dst[g], grp_dst)
            row = grp_dst + (s - grp_first) * SUB
            used = s < first_sub[N_GROUPS]
            rows_b.append(jnp.where(used, row, junk + (b * N_SUB + s) * SUB))
            tmeta_ref[base + s] = jnp.where(used, row, row if s == 0 else first_row)
            if s == 0:
                first_row = row
        for k in range(N_SUB, META_W):
            tmeta_ref[base + k] = jnp.int32(0)
        for g in range(N_GROUPS):
            run[g] = run[g] + (first_sub[g + 1] - first_sub[g]) * SUB
        scatter_rows.append(rows_b)
    for g in range(N_GROUPS):
        run_ref[g] = run[g]

    def copies(b, slot_, rows):
        return _sub_copies(xs_hbm, stage_ref.at[b, slot_], ssem.at[b, slot_], rows, True)

    for b in range(nb):
        for cp in copies(b, 1 - slot, [prev_ref[b * N_SUB + s] for s in range(N_SUB)]):
            cp.wait()

    for b in range(nb):
        for cp in copies(b, slot, scatter_rows[b]):
            cp.start()
        for s in range(N_SUB):
            prev_ref[b * N_SUB + s] = scatter_rows[b][s]

    @pl.when(t == n_steps - 1)
    def _():
        for b in range(nb):
            for cp in copies(b, slot, scatter_rows[b]):
                cp.wait()
        if drain_head is not None:
            drain_head()
        zeros_ref[...] = jnp.zeros_like(zeros_ref)
        tails = []
        base = n_steps * nb * META_W
        for g in range(N_GROUPS):
            tmeta_ref[base + g] = run[g]
            start = pl.multiple_of(g * cap + run[g], SUB)
            tails.append(pltpu.make_async_copy(zeros_ref, xs_hbm.at[pl.ds(start, TBM), :], zsem))
        for k in range(N_GROUPS, META_W):
            tmeta_ref[base + k] = jnp.int32(0)
        for cp in tails:
            cp.start()
        _work_tables(run, cap, n_work, tmeta_ref, base + META_W)
        for cp in tails:
            cp.wait()


def _attn_conv_mix(x, t, mod, g_mix, win_ref, wout_ref, bias_ref, sink_at, convw_ref,
                   kvprev_ref, cprev_ref):
    tb = TB
    first = t == 0
    sh1, sc1, g1 = mod[0:1], mod[1:2], mod[2:3]
    u = _rms_mod(x, g_mix, sc1, sh1)
    proj = jnp.dot(u.astype(BF16), win_ref[...], preferred_element_type=F32)

    yield
    q = (proj[:, 0:ATTN_WIDTH] * HEAD_DIM ** -0.5).astype(BF16)
    kf = proj[:, 512:640]
    vf = proj[:, 640:768]
    bgate = proj[:, 768:1280]
    cgate = proj[:, 1280:1792]
    xv = proj[:, 1792:2304]

    kv_prev = kvprev_ref[...]
    kext = jnp.concatenate([kv_prev[:, 0:KV_WIDTH], kf], axis=0)
    vext = jnp.concatenate([kv_prev[:, KV_WIDTH:], vf], axis=0)
    kvprev_ref[:, 0:KV_WIDTH] = kf[tb - BLOCK:tb]
    kvprev_ref[:, KV_WIDTH:] = vf[tb - BLOCK:tb]

    lane = lax.broadcasted_iota(I32, kext.shape, 1)
    lo = lane < HEAD_DIM
    krol = pltpu.roll(kext, HEAD_DIM, axis=1)
    vrol = pltpu.roll(vext, HEAD_DIM, axis=1)
    zero = jnp.zeros_like(kext)
    one_at_64 = jnp.where(lane == HEAD_DIM, 1.0, 0.0)
    one_at_0 = jnp.where(lane == 0, 1.0, 0.0)
    k_ops = [(jnp.where(lo, kext, zero).astype(BF16), jnp.where(lo, zero, krol).astype(BF16)),
             (jnp.where(lo, krol, zero).astype(BF16), jnp.where(lo, zero, kext).astype(BF16))]
    v_ops = [(jnp.where(lo, vext, one_at_64).astype(BF16), jnp.where(lo, one_at_0, vrol).astype(BF16)),
             (jnp.where(lo, vrol, one_at_64).astype(BF16), jnp.where(lo, one_at_0, vext).astype(BF16))]

    col = lax.broadcasted_iota(I32, (2 * BLOCK, 4 * BLOCK), 1)
    prev_cols = (col % (2 * BLOCK)) < BLOCK
    upper = lax.broadcasted_iota(I32, (2 * BLOCK, 1), 0) < BLOCK
    lane_o = lax.broadcasted_iota(I32, (2 * BLOCK, LANES), 1)
    lo_o = lane_o < HEAD_DIM

    yield
    attn_rows = []
    for bi in range(tb // BLOCK):
        r0 = bi * BLOCK
        pair_out = []
        for kvh in range(N_HEADS // 4):
            ka, kb = k_ops[kvh]
            rhs = jnp.concatenate([ka[r0:r0 + 2 * BLOCK], kb[r0:r0 + 2 * BLOCK]], axis=0)
            qrows = q[r0:r0 + BLOCK]
            qp = jnp.concatenate([qrows[:, (2 * kvh) * LANES:(2 * kvh + 1) * LANES],
                                  qrows[:, (2 * kvh + 1) * LANES:(2 * kvh + 2) * LANES]], axis=0)
            s = lax.dot_general(qp, rhs, NT_DIMS, preferred_element_type=F32)
            bias = bias_ref[kvh]
            if bi == 0:
                bias = jnp.where(prev_cols & first, NEG_BIG, bias)
            s = s + bias
            outs = []
            for hh in range(2):
                sh = s[:, hh * 2 * BLOCK:(hh + 1) * 2 * BLOCK]
                sink = jnp.where(upper, sink_at(4 * kvh + hh), sink_at(4 * kvh + 2 + hh))
                m = jnp.maximum(jnp.max(sh, axis=-1, keepdims=True), sink)
                p = jnp.exp(sh - m).astype(BF16)
                vop = v_ops[kvh][hh][r0:r0 + 2 * BLOCK]
                o = jnp.dot(p, vop, preferred_element_type=F32)
                den_col = HEAD_DIM if hh == 0 else 0
                den = o[:, den_col:den_col + 1] + jnp.exp(sink - m)
                outs.append(o / den)
            both = jnp.where(lo_o, outs[0], outs[1])
            pair_out += [both[0:BLOCK], both[BLOCK:]]
            yield
        attn_rows.append(jnp.concatenate(pair_out, axis=1))
    attn = jnp.concatenate(attn_rows, axis=0)

    uc = cgate * xv
    cprev = jnp.where(first, 0.0, cprev_ref[...])
    row = lax.broadcasted_iota(I32, uc.shape, 0)
    r1 = jnp.where(row == 0, cprev[7:8], pltpu.roll(uc, 1, axis=0))
    r2 = jnp.where(row == 0, cprev[6:7], jnp.where(row == 1, cprev[7:8], pltpu.roll(uc, 2, axis=0)))
    cprev_ref[...] = uc[tb - 8:tb]
    cw = convw_ref[...]
    conv = bgate * (cw[0:1] * r2 + cw[1:2] * r1 + cw[2:3] * uc)
    yield

    mix = (jnp.dot(attn.astype(BF16), wout_ref[0:ATTN_WIDTH, :], preferred_element_type=F32)
           + jnp.dot(conv.astype(BF16), wout_ref[ATTN_WIDTH:, :], preferred_element_type=F32))
    return x + g1 * mix


def _pool_mix(x, t, mod, g_mix, wpool_ref, band_ref, pscale, uprev_ref):
    tb = TB
    sh1, sc1, g1 = mod[0:1], mod[1:2], mod[2:3]
    u = _rms_mod(x, g_mix, sc1, sh1)
    halo = jnp.where(t == 0, 0.0, uprev_ref[...]).astype(BF16)
    ext = jnp.concatenate([jnp.zeros((LANES - POOL_HALO, D_MODEL), BF16), halo, u.astype(BF16)],
                          axis=0)
    uprev_ref[...] = u[tb - POOL_HALO:tb]

    yield
    pos = (t * tb + 1 + lax.broadcasted_iota(I32, (tb, 1), 0)).astype(F32)
    cols = [slice(gi * POOL_GROUP, (gi + 1) * POOL_GROUP) for gi in range(len(POOL_SIZES))]
    wsums = [jnp.dot(band_ref[gi], ext[:, sl], preferred_element_type=F32)
             for gi, sl in enumerate(cols)]
    yield
    mixed = []
    for gi, w in enumerate(POOL_SIZES):
        mean = wsums[gi] / jnp.minimum(pos, float(w))
        pooled = mean - u[:, cols[gi]]
        mixed.append(jnp.dot(pooled.astype(BF16), wpool_ref[gi].astype(BF16),
                             preferred_element_type=F32))
    yield
    mix = jnp.concatenate(mixed, axis=1) * pscale
    return x + g1 * mix


def _interleave(chains, lag):
    results = [None] * len(chains)
    live = list(range(len(chains)))
    rnd = 0
    while live:
        for k in list(live):
            if rnd < k * lag:
                continue
            try:
                next(chains[k])
            except StopIteration as done:
                results[k] = done.value
                live.remove(k)
        rnd += 1
    return results


def _mixer_kernel(*refs, kind, layer, has_head, nb, n_steps, cap, n_work):
    refs = list(refs)
    t = pl.program_id(0)
    slot = t % 2
    if has_head:
        pm_ref = refs.pop(0)
        hprev_ref, modprev_ref, tokprev_ref, ys_hbm = refs[:4]
        refs = refs[4:]
    else:
        h_ref = refs.pop(0)
    mod_ref, nmix_ref, nffn_ref = refs[:3]
    refs = refs[3:]
    n_w = 5 if kind == "even" else 3
    weight_refs = refs[:n_w]
    refs = refs[n_w:]
    wrt_ref, br_ref, tri_ref = refs[:3]
    refs = refs[3:]
    hout_ref, tokmeta_ref, tmeta_ref, xs_hbm = refs[:4]
    refs = refs[4:]
    n_c = 2 if kind == "even" else 1
    carry_refs = refs[:n_c]
    refs = refs[n_c:]
    if kind == "even":
        cast_refs = refs[:2]
        refs = refs[2:]
    stage_ref, zeros_ref, run_ref, prev_ref, ssem, zsem = refs[:6]
    refs = refs[6:]

    @pl.when(t == 0)
    def _():
        for r in carry_refs:
            r[...] = jnp.zeros_like(r)
        for g in range(N_GROUPS):
            run_ref[g] = jnp.int32(0)
        if kind == "even":
            for src, dst in zip(weight_refs[:2], cast_refs):
                for c0 in range(0, src.shape[1], 2 * LANES):
                    dst[:, c0:c0 + 2 * LANES] = src[:, c0:c0 + 2 * LANES].astype(BF16)
        _prime_scatter(nb, cap, xs_hbm, stage_ref, prev_ref, ssem)

    j = layer // 2
    g_mix = nmix_ref[layer:layer + 1]
    g_ffn = nffn_ref[layer:layer + 1]
    if kind == "even":
        sink_ref = weight_refs[3]
        weight_refs = list(cast_refs) + [weight_refs[2], lambda i: sink_ref[j, i], weight_refs[4]]
    else:
        weight_refs = list(weight_refs[:2]) + [weight_refs[2][j:j + 1]]

    drain_head = None
    if has_head:
        ystage_ref, gsem = refs
        drain_head = _head_dma(t, n_steps, nb, pm_ref, ys_hbm, ystage_ref, gsem)

    def chain(b):
        if has_head:
            x = _head_combine(b, slot, hprev_ref, modprev_ref, tokprev_ref, ystage_ref)
            yield
        else:
            x = h_ref[b]
        mod = mod_ref[b]
        carries = [r.at[b] for r in carry_refs]
        mix = _attn_conv_mix if kind == "even" else _pool_mix
        h1 = yield from mix(x, t, mod, g_mix, *weight_refs, *carries)
        hout_ref[b] = h1
        yield
        return (yield from _tail_compute(b, slot, h1, mod, g_ffn, wrt_ref, br_ref, tri_ref,
                                         tokmeta_ref, stage_ref))

    counts = _interleave([chain(b) for b in range(nb)], CHAIN_LAG)
    _tail_dma(t, n_steps, nb, cap, n_work, counts, tmeta_ref, xs_hbm, stage_ref, zeros_ref,
              run_ref, prev_ref, ssem, zsem, drain_head)


def _nat(nb, cols):
    return pl.BlockSpec((nb, TB, cols), lambda t, *_: (0, t, 0))


def _full(shape):
    nd = len(shape)
    return pl.BlockSpec(shape, lambda t, *_: (0,) * nd)


def _head_scratch(nb):
    return [pltpu.VMEM((nb, 2, STG_C, D_MODEL), BF16), pltpu.SemaphoreType.DMA((nb, 2))]


def _mod_spec(nb, layer):
    return pl.BlockSpec((None, nb, 8, D_MODEL), lambda t, *_: (layer, 0, 0, 0))


def _mixer_call(kind, layer, head, h, mod_all, norm_mix, norm_ffn, weights, wrt, br, tri):
    j = layer // 2
    nb, seq, _ = (h if head is None else head[1]).shape
    n_steps = seq // TB
    n_tiles, cap, n_work = _sizes(nb * seq)
    has_head = head is not None
    any_spec = pl.BlockSpec(memory_space=pl.ANY)
    smem_spec = pl.BlockSpec(memory_space=pltpu.SMEM)

    if has_head:
        pm, hprev, tokprev, ys = head
        args = [hprev, mod_all, tokprev, ys]
        in_specs = [_nat(nb, D_MODEL), _mod_spec(nb, layer - 1), _nat(nb, LANES), any_spec]
    else:
        args = [h]
        in_specs = [_nat(nb, D_MODEL)]
    args += [mod_all, norm_mix, norm_ffn]
    in_specs += [_mod_spec(nb, layer), _full(norm_mix.shape), _full(norm_ffn.shape)]
    if kind == "even":
        args += list(weights)
        once = dict(pipeline_mode=pl.Buffered(1))
        in_specs += [pl.BlockSpec((None, D_MODEL, IN_WIDTH), lambda t, *_: (j, 0, 0), **once),
                     pl.BlockSpec((None, D_MODEL, D_MODEL), lambda t, *_: (j, 0, 0), **once),
                     _full((N_HEADS // 4, 2 * BLOCK, 4 * BLOCK)), smem_spec,
                     pl.BlockSpec((None,) + weights[4].shape[1:], lambda t, *_: (j, 0, 0))]
        mix_scratch = [pltpu.VMEM((nb, BLOCK, 2 * KV_WIDTH), F32),
                       pltpu.VMEM((nb, 8, CONV_WIDTH), F32),
                       pltpu.VMEM((D_MODEL, IN_WIDTH), BF16),
                       pltpu.VMEM((D_MODEL, D_MODEL), BF16)]
    else:
        args += list(weights)
        in_specs += [pl.BlockSpec((None, len(POOL_SIZES), POOL_GROUP, POOL_GROUP),
                                  lambda t, *_: (j, 0, 0, 0)),
                     _full((len(POOL_SIZES), TB, TB + LANES)), _full(weights[2].shape)]
        mix_scratch = [pltpu.VMEM((nb, POOL_HALO, D_MODEL), F32)]
    args += [wrt, br, tri]
    in_specs += [_full((2 * N_EXPERTS, D_MODEL)), _full((N_EXPERTS, 1)), _full((TB, TB))]

    out_shape = (jax.ShapeDtypeStruct((nb, seq, D_MODEL), F32),
                 jax.ShapeDtypeStruct((nb, seq, LANES), F32),
                 jax.ShapeDtypeStruct(((n_tiles + 1) * META_W + 4 * n_work,), I32),
                 jax.ShapeDtypeStruct((N_GROUPS * cap + nb * N_SUB * SUB, ROW_W), BF16))
    out_specs = (_nat(nb, D_MODEL), _nat(nb, LANES), smem_spec, any_spec)
    scratch = mix_scratch + [
        pltpu.VMEM((nb, 2, STG_P, ROW_W), BF16),
        pltpu.VMEM((TBM, ROW_W), BF16),
        pltpu.SMEM((N_GROUPS,), I32),
        pltpu.SMEM((nb * N_SUB,), I32),
        pltpu.SemaphoreType.DMA((nb, 2)),
        pltpu.SemaphoreType.DMA(()),
    ]
    if has_head:
        scratch += _head_scratch(nb)

    body = functools.partial(_mixer_kernel, kind=kind, layer=layer, has_head=has_head, nb=nb,
                             n_steps=n_steps, cap=cap, n_work=n_work)
    grid_spec = pltpu.PrefetchScalarGridSpec(
        num_scalar_prefetch=1 if has_head else 0, grid=(n_steps,),
        in_specs=in_specs, out_specs=out_specs, scratch_shapes=scratch)
    call = pl.pallas_call(
        body, grid_spec=grid_spec, out_shape=out_shape,
        compiler_params=pltpu.CompilerParams(
            dimension_semantics=("arbitrary",), vmem_limit_bytes=VMEM_LIMIT),
        name=kind + "_mixer")
    if has_head:
        return call(pm, *args)
    return call(*args)


def _work_tables(totals, cap, n_work, tmeta_ref, t0):
    ends, starts = [], []
    acc = jnp.int32(0)
    for g in range(N_GROUPS):
        starts.append(acc)
        acc = acc + (totals[g] + (TBM - 1)) // TBM
        ends.append(acc)
    nvalid = ends[-1]
    prev_grp = jnp.int32(-1)
    for i in range(n_work):
        idc = jnp.minimum(jnp.int32(i), nvalid - 1)
        grp, start, total = jnp.int32(0), starts[0], totals[0]
        for g in range(1, N_GROUPS):
            later = idc >= ends[g - 1]
            grp = jnp.where(later, g, grp)
            start = jnp.where(later, starts[g], start)
            total = jnp.where(later, totals[g], total)
        valid = i < nvalid
        tile = idc - start
        tmeta_ref[t0 + i] = grp * (cap // TBM) + tile
        tmeta_ref[t0 + n_work + i] = grp
        tmeta_ref[t0 + 2 * n_work + i] = jnp.where(valid, jnp.minimum(total - tile * TBM, TBM), 0)
        tmeta_ref[t0 + 3 * n_work + i] = jnp.logical_and(valid, grp != prev_grp).astype(I32)
        prev_grp = grp


def _expert_rows(xs_ref, ys_ref, wg_s, wu_s, wd_s, r0, nrows):
    xb = xs_ref[r0:r0 + nrows, 0:D_MODEL]
    meta = xs_ref[r0:r0 + nrows, D_MODEL:ROW_W].astype(F32)
    gates = meta[:, 0:EXPERTS_PER_GROUP] + meta[:, EXPERTS_PER_GROUP:2 * EXPERTS_PER_GROUP]
    parts = []
    for e in range(EXPERTS_PER_GROUP):
        a = jnp.dot(xb, wg_s[e], preferred_element_type=F32)
        bu = jnp.dot(xb, wu_s[e], preferred_element_type=F32)
        hid = (a * _sigmoid(a)) * bu
        parts.append((hid * gates[:, e:e + 1]).astype(BF16))
    hid_all = jnp.concatenate(parts, axis=1)
    ys_ref[r0:r0 + nrows, :] = jnp.dot(hid_all, wd_s[...],
                                       preferred_element_type=F32).astype(BF16)


def _moe_kernel(tm_ref, xs_ref, wg_ref, wu_ref, wd_ref, ys_ref, wg_s, wu_s, wd_s, *, t0, n_work):
    i = pl.program_id(0)
    rows = tm_ref[t0 + 2 * n_work + i]
    first_of_group = tm_ref[t0 + 3 * n_work + i] == 1

    @pl.when(first_of_group)
    def _():
        for e in range(EXPERTS_PER_GROUP):
            wg_s[e] = wg_ref[e].astype(BF16)
            wu_s[e] = wu_ref[e].astype(BF16)
        wd_s[...] = wd_ref[0].astype(BF16)

    @pl.when(rows == TBM)
    def _():
        _expert_rows(xs_ref, ys_ref, wg_s, wu_s, wd_s, 0, TBM)

    @pl.when(jnp.logical_and(rows > 0, rows < TBM))
    def _():
        for r0 in range(0, TBM, TBM_PART):
            @pl.when(r0 < rows)
            def _():
                _expert_rows(xs_ref, ys_ref, wg_s, wu_s, wd_s, r0, TBM_PART)

            @pl.when(r0 >= rows)
            def _():
                ys_ref[r0:r0 + TBM_PART, :] = jnp.zeros((TBM_PART, D_MODEL), BF16)


def _moe_call(xs, tmeta, layer, wg, wu, wd, n_tiles, n_work):
    t0 = (n_tiles + 1) * META_W
    row_map = lambda i, tm: (tm[t0 + i], 0)
    grp_map = lambda i, tm: (layer, tm[t0 + n_work + i], 0, 0)
    grid_spec = pltpu.PrefetchScalarGridSpec(
        num_scalar_prefetch=1, grid=(n_work,),
        in_specs=[
            pl.BlockSpec((TBM, ROW_W), row_map),
            pl.BlockSpec((None, EXPERTS_PER_GROUP, D_MODEL, EXPERT_FF), grp_map),
            pl.BlockSpec((None, EXPERTS_PER_GROUP, D_MODEL, EXPERT_FF), grp_map),
            pl.BlockSpec((None, 1, GROUP_FF, D_MODEL), grp_map),
        ],
        out_specs=pl.BlockSpec((TBM, D_MODEL), row_map),
        scratch_shapes=[pltpu.VMEM((EXPERTS_PER_GROUP, D_MODEL, EXPERT_FF), BF16),
                        pltpu.VMEM((EXPERTS_PER_GROUP, D_MODEL, EXPERT_FF), BF16),
                        pltpu.VMEM((GROUP_FF, D_MODEL), BF16)],
    )
    return pl.pallas_call(
        functools.partial(_moe_kernel, t0=t0, n_work=n_work), grid_spec=grid_spec,
        out_shape=jax.ShapeDtypeStruct((xs.shape[0], D_MODEL), BF16),
        compiler_params=pltpu.CompilerParams(
            dimension_semantics=("arbitrary",), vmem_limit_bytes=VMEM_LIMIT),
        name="moe_experts",
    )(tmeta, xs, wg, wu, wd)


def _final_kernel(pm_ref, hprev_ref, modprev_ref, tokprev_ref, ys_hbm, nf_ref, o_ref,
                  ystage_ref, gsem, *, nb, n_steps):
    t = pl.program_id(0)
    drain = _head_dma(t, n_steps, nb, pm_ref, ys_hbm, ystage_ref, gsem)
    for b in range(nb):
        h = _head_combine(b, t % 2, hprev_ref, modprev_ref, tokprev_ref, ystage_ref)
        ms = jnp.mean(h * h, axis=-1, keepdims=True)
        o_ref[b] = (h * lax.rsqrt(ms + EPS)) * nf_ref[0:1]
    pl.when(t == n_steps - 1)(drain)


def _final_call(head, mod_all, norm_final):
    pm, hprev, tokprev, ys = head
    nb, seq, _ = hprev.shape
    n_steps = seq // TB
    grid_spec = pltpu.PrefetchScalarGridSpec(
        num_scalar_prefetch=1, grid=(n_steps,),
        in_specs=[_nat(nb, D_MODEL), _mod_spec(nb, DEPTH - 1), _nat(nb, LANES),
                  pl.BlockSpec(memory_space=pl.ANY), _full((1, D_MODEL))],
        out_specs=_nat(nb, D_MODEL),
        scratch_shapes=_head_scratch(nb))
    return pl.pallas_call(
        functools.partial(_final_kernel, nb=nb, n_steps=n_steps), grid_spec=grid_spec,
        out_shape=jax.ShapeDtypeStruct((nb, seq, D_MODEL), F32),
        compiler_params=pltpu.CompilerParams(
            dimension_semantics=("arbitrary",), vmem_limit_bytes=VMEM_LIMIT),
        name="final_norm",
    )(pm, hprev, mod_all, tokprev, ys, norm_final.reshape(1, D_MODEL))


def _pool_band_table():
    r = jnp.arange(TB)[:, None]
    c = jnp.arange(TB + LANES)[None, :]
    return jnp.stack([(c > r + LANES - w) & (c <= r + LANES) for w in POOL_SIZES]).astype(BF16)


def _attn_bias_table():
    slopes = 2.0 ** (-8.0 * (jnp.arange(N_HEADS, dtype=F32) + 1.0) / N_HEADS)
    dist = (jnp.arange(BLOCK)[:, None] + BLOCK) - jnp.arange(2 * BLOCK)[None, :]
    ok = (dist >= 0) & (dist < WINDOW)
    per_head = jnp.where(ok[None], -slopes[:, None, None] * dist.astype(F32)[None], NEG_BIG)
    return per_head.reshape(N_HEADS // 2, 2, BLOCK, 2 * BLOCK).transpose(0, 2, 1, 3).reshape(
        N_HEADS // 4, 2 * BLOCK, 4 * BLOCK)


def _router_operands(w_router, b_router):
    def reorder(a):
        return a.reshape(N_GROUPS, EXPERTS_PER_GROUP, -1).transpose(1, 0, 2).reshape(N_EXPERTS, -1)
    w = reorder(w_router.T)
    wh = w.astype(BF16)
    wl = (w - wh.astype(F32)).astype(BF16)
    return jnp.concatenate([wh, wl], axis=0), reorder(b_router[:, None])


def kernel(x, c, w_ada, b_ada, norm_mix, norm_ffn, w_in, w_out, sinks, conv_w, w_pool,
           pool_scale, w_router, b_router, w_gate, w_up, w_down, norm_final):
    b, s, _ = x.shape
    n_tiles, cap, n_work = _sizes(b * s)

    mod_all = _mod_call(c, w_ada, b_ada)
    bias_tab = _attn_bias_table()
    band_tab = _pool_band_table()
    wrt, br = _router_operands(w_router, b_router)
    tri = jnp.triu(jnp.ones((TB, TB), F32)).astype(BF16)
    w_down4 = w_down.reshape(DEPTH, N_GROUPS, GROUP_FF, D_MODEL)

    h = x
    head = None
    for layer in range(DEPTH):
        if layer % 2 == 0:
            kind = "even"
            weights = (w_in, w_out, bias_tab, sinks, conv_w)
        else:
            kind = "odd"
            weights = (w_pool, band_tab, pool_scale)
        h1, tokmeta, tmeta, xs = _mixer_call(kind, layer, head, h, mod_all, norm_mix, norm_ffn,
                                             weights, wrt, br, tri)
        ys = _moe_call(xs, tmeta, layer, w_gate, w_up, w_down4, n_tiles, n_work)
        head = (tmeta, h1, tokmeta, ys)
    return _final_call(head, mod_all, norm_final)
```

```python
import functools

import jax
import jax.numpy as jnp
from jax import lax
from jax.experimental import pallas as pl
from jax.experimental.pallas import tpu as pltpu

F32 = jnp.float32
BF16 = jnp.bfloat16
I32 = jnp.int32

D_MODEL = 1024
DEPTH = 4
EPS = 1e-6
N_MOD = 6

ATTN_WIDTH = 512
HEAD_DIM = 64
N_HEADS = 8
KV_WIDTH = 128
WINDOW = 128
BLOCK = 128
CONV_WIDTH = 512
IN_WIDTH = 2304

POOL_SIZES = (2, 4, 8, 16)
POOL_GROUP = 256
POOL_HALO = 16

N_EXPERTS = 16
N_GROUPS = 4
EXPERTS_PER_GROUP = 4
EXPERT_FF = 256
GROUP_FF = EXPERTS_PER_GROUP * EXPERT_FF

LANES = 128
NEG_BIG = -1e30

TB = 256
SUB = 16
N_SUB = TB // SUB + N_GROUPS - 1
STG_P = 320
STG_C = 384
TBM = 1024
TBM_PART = 256
ROW_W = D_MODEL + LANES
META_W = 32
LP_LANE = 8
MAIN_START = 2
MOD_COLS = 3072
VMEM_LIMIT = 56 * 1024 * 1024
NT_DIMS = (((1,), (1,)), ((), ()))

assert N_SUB * SUB <= STG_P <= STG_C and STG_C % LANES == 0 and N_SUB <= META_W


def _sizes(n_tok):
    n_tiles = n_tok // TB
    cap = -(-(n_tok + n_tiles * (SUB - 1) + TBM) // TBM) * TBM
    n_work = (n_tok + n_tiles * N_GROUPS * (SUB - 1)) // TBM + N_GROUPS
    return n_tiles, cap, n_work


def _sigmoid(x):
    return 1.0 / (1.0 + jnp.exp(-x))


def _rms_mod(x, g, sc, sh):
    ms = jnp.mean(x * x, axis=-1, keepdims=True)
    return (x * lax.rsqrt(ms + EPS)) * (g * (1.0 + sc)) + sh


def _onehot(cond):
    return jnp.where(cond, 1.0, 0.0).astype(BF16)


def _mod_kernel(ct_ref, w_ref, b_ref, o_ref, *, nb):
    j = pl.program_id(1)
    ct = ct_ref[...]
    cond = ct * _sigmoid(ct)
    vec_per_step = MOD_COLS // D_MODEL
    for b in range(nb):
        col = cond[:, b:b + 1]
        acc = jnp.zeros((8, MOD_COLS), F32)
        for r in range(D_MODEL // 8):
            acc = acc + w_ref[0, 8 * r:8 * r + 8, :] * col[8 * r:8 * r + 8]
        res = jnp.sum(acc, axis=0, keepdims=True) + b_ref[0]
        for jj in range(N_MOD // vec_per_step):
            @pl.when(j == jj)
            def _():
                for k in range(vec_per_step):
                    row = jj * vec_per_step + k
                    o_ref[0, b, row:row + 1, :] = res[:, k * D_MODEL:(k + 1) * D_MODEL]

        @pl.when(j == 0)
        def _():
            o_ref[0, b, N_MOD:8, :] = jnp.zeros((8 - N_MOD, D_MODEL), F32)


def _mod_call(c, w_ada, b_ada):
    nb = c.shape[0]
    ct = jnp.pad(c.T, ((0, 0), (0, LANES - nb)))
    n_col = (N_MOD * D_MODEL) // MOD_COLS
    return pl.pallas_call(
        functools.partial(_mod_kernel, nb=nb),
        grid=(DEPTH, n_col),
        in_specs=[
            pl.BlockSpec((D_MODEL, LANES), lambda l, j: (0, 0)),
            pl.BlockSpec((1, D_MODEL, MOD_COLS), lambda l, j: (l, 0, j)),
            pl.BlockSpec((1, 1, MOD_COLS), lambda l, j: (l, 0, j)),
        ],
        out_specs=pl.BlockSpec((1, nb, 8, D_MODEL), lambda l, j: (l, 0, 0, 0)),
        out_shape=jax.ShapeDtypeStruct((DEPTH, nb, 8, D_MODEL), F32),
        compiler_params=pltpu.CompilerParams(
            dimension_semantics=("arbitrary", "arbitrary"), vmem_limit_bytes=VMEM_LIMIT),
        name="adaln_mod",
    )(ct, w_ada, b_ada.reshape(DEPTH, 1, N_MOD * D_MODEL))


def _sub_copies(hbm_ref, stage_ref, sem, rows, to_hbm):
    out = []
    for s in range(N_SUB):
        s_view = stage_ref.at[pl.ds(s * SUB, SUB), :]
        h_view = hbm_ref.at[pl.ds(pl.multiple_of(rows[s], SUB), SUB), :]
        out.append(pltpu.make_async_copy(s_view, h_view, sem) if to_hbm
                   else pltpu.make_async_copy(h_view, s_view, sem))
    return out


def _head_dma(t, n_steps, nb, pm_ref, ys_hbm, ystage_ref, gsem):
    slot = t % 2

    def copies(step, b, slot_):
        base = (jnp.minimum(step, n_steps - 1) * nb + b) * META_W
        rows = [pm_ref[base + s] for s in range(N_SUB)]
        return _sub_copies(ys_hbm, ystage_ref.at[b, slot_], gsem.at[b, slot_], rows, False)

    @pl.when(t == 0)
    def _():
        ystage_ref[...] = jnp.zeros_like(ystage_ref)
        for b in range(nb):
            for cp in copies(0, b, 0):
                cp.start()

    nxt = t + 1
    for b in range(nb):
        for cp in copies(nxt, b, 1 - slot):
            cp.start()

    for b in range(nb):
        for cp in copies(t, b, slot):
            cp.wait()

    def drain():
        for b in range(nb):
            for cp in copies(nxt, b, 1 - slot):
                cp.wait()
    return drain


def _head_combine(b, slot, hprev_ref, modprev_ref, tokmeta_ref, ystage_ref):
    lp = tokmeta_ref[b, :, LP_LANE:LP_LANE + 1].astype(I32)
    unsort = _onehot(lax.broadcasted_iota(I32, (TB, STG_C), 1) == lp)
    y = jnp.dot(unsort, ystage_ref[b, slot], preferred_element_type=F32)
    return hprev_ref[b] + modprev_ref[b, 5:6] * y


def _route(u2, wrt_ref, br_ref):
    uh = u2.astype(BF16)
    ul = (u2 - uh.astype(F32)).astype(BF16)
    w = wrt_ref[...]
    r1 = lax.dot_general(w, uh, NT_DIMS, preferred_element_type=F32)
    r2 = lax.dot_general(w[0:N_EXPERTS], ul, NT_DIMS, preferred_element_type=F32)
    yield
    scores = _sigmoid(r1[0:N_EXPERTS] + r1[N_EXPERTS:] + r2)
    biased = scores + br_ref[...]
    s = [scores[4 * j:4 * j + 4] for j in range(EXPERTS_PER_GROUP)]
    b = [biased[4 * j:4 * j + 4] for j in range(EXPERTS_PER_GROUP)]

    hi1, lo1 = jnp.maximum(b[0], b[1]), jnp.minimum(b[0], b[1])
    hi2, lo2 = jnp.maximum(b[2], b[3]), jnp.minimum(b[2], b[3])
    second = jnp.maximum(jnp.minimum(hi1, hi2), jnp.where(hi1 >= hi2, lo1, lo2))
    gscore = jnp.maximum(hi1, hi2) + second

    best = gscore[0:1]
    bgrp = jnp.zeros((1, TB), I32)
    for g in range(1, N_GROUPS):
        better = gscore[g:g + 1] > best
        bgrp = jnp.where(better, g, bgrp)
        best = jnp.where(better, gscore[g:g + 1], best)
    selmask = lax.broadcasted_iota(I32, (N_GROUPS, TB), 0) == bgrp
    yield

    m1 = b[0]
    i1 = jnp.zeros((N_GROUPS, TB), I32)
    for j in range(1, EXPERTS_PER_GROUP):
        gt = b[j] > m1
        i1 = jnp.where(gt, j, i1)
        m1 = jnp.where(gt, b[j], m1)
    cands = [jnp.where(i1 == j, -jnp.inf, b[j]) for j in range(EXPERTS_PER_GROUP)]
    m2 = cands[0]
    i2 = jnp.zeros_like(i1)
    for j in range(1, EXPERTS_PER_GROUP):
        gt = cands[j] > m2
        i2 = jnp.where(gt, j, i2)
        m2 = jnp.where(gt, cands[j], m2)
    w1 = s[0]
    w2 = s[0]
    for j in range(1, EXPERTS_PER_GROUP):
        w1 = jnp.where(i1 == j, s[j], w1)
        w2 = jnp.where(i2 == j, s[j], w2)
    yield
    tot = w1 + w2
    w1n = w1 / tot
    w2n = w2 / tot
    gates = []
    for j in range(EXPERTS_PER_GROUP):
        gj = jnp.where(i1 == j, w1n, 0.0) + jnp.where(i2 == j, w2n, 0.0)
        gates.append(jnp.sum(jnp.where(selmask, gj, 0.0), axis=0, keepdims=True))
    return bgrp, gates, selmask


def _tail_compute(b, slot, h1, mod, g_ffn, wrt_ref, br_ref, tri_ref, tokmeta_ref, stage_ref):
    sh2, sc2 = mod[3:4], mod[4:5]
    u2 = _rms_mod(h1, g_ffn, sc2, sh2)
    bgrp, gates, selmask = yield from _route(u2, wrt_ref, br_ref)
    yield

    onehot = jnp.concatenate([jnp.where(selmask, 1.0, 0.0), jnp.zeros((8 - N_GROUPS, TB), F32)],
                             axis=0).astype(BF16)
    rank_incl = jnp.dot(onehot, tri_ref[...], preferred_element_type=F32)
    rank = jnp.sum(jnp.where(selmask, rank_incl[0:N_GROUPS], 0.0), axis=0, keepdims=True) - 1.0
    cnt = lax.dot_general(jnp.ones((8, TB), BF16), onehot, NT_DIMS,
                          preferred_element_type=F32)
    sub_rows = jnp.floor((cnt[0:1, :] + (SUB - 1.0)) * (1.0 / SUB)) * SUB
    lp = rank
    first_row = jnp.zeros((1, 1), F32)
    for g in range(1, N_GROUPS):
        first_row = first_row + sub_rows[:, g - 1:g]
        lp = lp + jnp.where(bgrp == g, first_row, 0.0)

    hi = [x.astype(BF16).astype(F32) for x in gates]
    lo = [(x - h).astype(BF16).astype(F32) for x, h in zip(gates, hi)]
    meta_src = jnp.concatenate(hi + lo + [lp, jnp.zeros((LANES - 9, TB), F32)], axis=0)
    meta_t = meta_src.T
    tokmeta_ref[b] = meta_t
    yield

    rowdata = jnp.concatenate([u2.astype(BF16), meta_t.astype(BF16)], axis=1)
    sort = _onehot(lax.broadcasted_iota(I32, (STG_P, TB), 0) == lp.astype(I32))
    stage_ref[b, slot] = jnp.dot(sort, rowdata, preferred_element_type=F32).astype(BF16)
    return [cnt[0, g].astype(I32) for g in range(N_GROUPS)]


def _prime_scatter(nb, cap, xs_hbm, stage_ref, prev_ref, ssem):
    stage_ref[:, 1] = jnp.zeros((nb,) + stage_ref.shape[2:], BF16)
    for b in range(nb):
        rows = [jnp.int32(N_GROUPS * cap + (b * N_SUB + s) * SUB) for s in range(N_SUB)]
        for s in range(N_SUB):
            prev_ref[b * N_SUB + s] = rows[s]
        for cp in _sub_copies(xs_hbm, stage_ref.at[b, 1], ssem.at[b, 1], rows, True):
            cp.start()


def _tail_dma(t, n_steps, nb, cap, n_work, counts, tmeta_ref, xs_hbm, stage_ref, zeros_ref,
              run_ref, prev_ref, ssem, zsem, drain_head):
    slot = t % 2
    run = [run_ref[g] for g in range(N_GROUPS)]
    junk = N_GROUPS * cap
    scatter_rows = []
    for b in range(nb):
        first_sub = [jnp.int32(0)]
        for g in range(N_GROUPS):
            first_sub.append(first_sub[-1] + (counts[b][g] + (SUB - 1)) // SUB)
        dst = [g * cap + run[g] for g in range(N_GROUPS)]
        rows_b, base = [], (jnp.maximum(t - 1, 0) * nb + b) * META_W
        for s in range(N_SUB):
            grp_first, grp_dst = first_sub[0], dst[0]
            for g in range(1, N_GROUPS):
                later = s >= first_sub[g]
                grp_first = jnp.where(later, first_sub[g], grp_first)
                grp_dst = jnp.where(later, dst[g], grp_dst)
            row = grp_dst + (s - grp_first) * SUB
            used = s < first_sub[N_GROUPS]
            rows_b.append(jnp.where(used, row, junk + (b * N_SUB + s) * SUB))
            tmeta_ref[base + s] = jnp.where(used, row, row if s == 0 else first_row)
            if s == 0:
                first_row = row
        for k in range(N_SUB, META_W):
            tmeta_ref[base + k] = jnp.int32(0)
        for g in range(N_GROUPS):
            run[g] = run[g] + (first_sub[g + 1] - first_sub[g]) * SUB
        scatter_rows.append(rows_b)
    for g in range(N_GROUPS):
        run_ref[g] = run[g]

    def copies(b, slot_, rows):
        return _sub_copies(xs_hbm, stage_ref.at[b, slot_], ssem.at[b, slot_], rows, True)

    for b in range(nb):
        for cp in copies(b, 1 - slot, [prev_ref[b * N_SUB + s] for s in range(N_SUB)]):
            cp.wait()

    for b in range(nb):
        for cp in copies(b, slot, scatter_rows[b]):
            cp.start()
        for s in range(N_SUB):
            prev_ref[b * N_SUB + s] = scatter_rows[b][s]

    @pl.when(t == n_steps)
    def _():
        for b in range(nb):
            for cp in copies(b, slot, scatter_rows[b]):
                cp.wait()
        if drain_head is not None:
            drain_head()
        zeros_ref[...] = jnp.zeros_like(zeros_ref)
        tails = []
        base = n_steps * nb * META_W
        for g in range(N_GROUPS):
            tmeta_ref[base + g] = run[g]
            start = pl.multiple_of(g * cap + run[g], SUB)
            tails.append(pltpu.make_async_copy(zeros_ref, xs_hbm.at[pl.ds(start, TBM), :], zsem))
        for k in range(N_GROUPS, META_W):
            tmeta_ref[base + k] = jnp.int32(0)
        for cp in tails:
            cp.start()
        _work_tables(run, cap, n_work, tmeta_ref, base + META_W)
        for cp in tails:
            cp.wait()


def _attn_conv_mix(x, t, mod, g_mix, win_ref, wout_ref, bias_ref, sink_at, convw_ref,
                   kvprev_ref, cprev_ref):
    tb = TB
    first = t == 0
    sh1, sc1, g1 = mod[0:1], mod[1:2], mod[2:3]
    u = _rms_mod(x, g_mix, sc1, sh1)
    proj = jnp.dot(u.astype(BF16), win_ref[...], preferred_element_type=F32)

    yield
    q = (proj[:, 0:ATTN_WIDTH] * HEAD_DIM ** -0.5).astype(BF16)
    kf = proj[:, 512:640]
    vf = proj[:, 640:768]
    bgate = proj[:, 768:1280]
    cgate = proj[:, 1280:1792]
    xv = proj[:, 1792:2304]

    kv_prev = kvprev_ref[...]
    kext = jnp.concatenate([kv_prev[:, 0:KV_WIDTH], kf], axis=0)
    vext = jnp.concatenate([kv_prev[:, KV_WIDTH:], vf], axis=0)
    kvprev_ref[:, 0:KV_WIDTH] = kf[tb - BLOCK:tb]
    kvprev_ref[:, KV_WIDTH:] = vf[tb - BLOCK:tb]

    lane = lax.broadcasted_iota(I32, kext.shape, 1)
    lo = lane < HEAD_DIM
    krol = pltpu.roll(kext, HEAD_DIM, axis=1)
    vrol = pltpu.roll(vext, HEAD_DIM, axis=1)
    zero = jnp.zeros_like(kext)
    one_at_64 = jnp.where(lane == HEAD_DIM, 1.0, 0.0)
    one_at_0 = jnp.where(lane == 0, 1.0, 0.0)
    k_ops = [(jnp.where(lo, kext, zero).astype(BF16), jnp.where(lo, zero, krol).astype(BF16)),
             (jnp.where(lo, krol, zero).astype(BF16), jnp.where(lo, zero, kext).astype(BF16))]
    v_ops = [(jnp.where(lo, vext, one_at_64).astype(BF16), jnp.where(lo, one_at_0, vrol).astype(BF16)),
             (jnp.where(lo, vrol, one_at_64).astype(BF16), jnp.where(lo, one_at_0, vext).astype(BF16))]

    col = lax.broadcasted_iota(I32, (2 * BLOCK, 4 * BLOCK), 1)
    prev_cols = (col % (2 * BLOCK)) < BLOCK
    upper = lax.broadcasted_iota(I32, (2 * BLOCK, 1), 0) < BLOCK
    lane_o = lax.broadcasted_iota(I32, (2 * BLOCK, LANES), 1)
    lo_o = lane_o < HEAD_DIM

    yield
    attn_rows = []
    for bi in range(tb // BLOCK):
        r0 = bi * BLOCK
        pair_out = []
        for kvh in range(N_HEADS // 4):
            ka, kb = k_ops[kvh]
            rhs = jnp.concatenate([ka[r0:r0 + 2 * BLOCK], kb[r0:r0 + 2 * BLOCK]], axis=0)
            qrows = q[r0:r0 + BLOCK]
            qp = jnp.concatenate([qrows[:, (2 * kvh) * LANES:(2 * kvh + 1) * LANES],
                                  qrows[:, (2 * kvh + 1) * LANES:(2 * kvh + 2) * LANES]], axis=0)
            s = lax.dot_general(qp, rhs, NT_DIMS, preferred_element_type=F32)
            bias = bias_ref[kvh]
            if bi == 0:
                bias = jnp.where(prev_cols & first, NEG_BIG, bias)
            s = s + bias
            outs = []
            for hh in range(2):
                sh = s[:, hh * 2 * BLOCK:(hh + 1) * 2 * BLOCK]
                sink = jnp.where(upper, sink_at(4 * kvh + hh), sink_at(4 * kvh + 2 + hh))
                m = jnp.maximum(jnp.max(sh, axis=-1, keepdims=True), sink)
                p = jnp.exp(sh - m).astype(BF16)
                vop = v_ops[kvh][hh][r0:r0 + 2 * BLOCK]
                o = jnp.dot(p, vop, preferred_element_type=F32)
                den_col = HEAD_DIM if hh == 0 else 0
                den = o[:, den_col:den_col + 1] + jnp.exp(sink - m)
                outs.append(o / den)
            both = jnp.where(lo_o, outs[0], outs[1])
            pair_out += [both[0:BLOCK], both[BLOCK:]]
            yield
        attn_rows.append(jnp.concatenate(pair_out, axis=1))
    attn = jnp.concatenate(attn_rows, axis=0)

    uc = cgate * xv
    cprev = jnp.where(first, 0.0, cprev_ref[...])
    row = lax.broadcasted_iota(I32, uc.shape, 0)
    r1 = jnp.where(row == 0, cprev[7:8], pltpu.roll(uc, 1, axis=0))
    r2 = jnp.where(row == 0, cprev[6:7], jnp.where(row == 1, cprev[7:8], pltpu.roll(uc, 2, axis=0)))
    cprev_ref[...] = uc[tb - 8:tb]
    cw = convw_ref[...]
    conv = bgate * (cw[0:1] * r2 + cw[1:2] * r1 + cw[2:3] * uc)
    yield

    mix = (jnp.dot(attn.astype(BF16), wout_ref[0:ATTN_WIDTH, :], preferred_element_type=F32)
           + jnp.dot(conv.astype(BF16), wout_ref[ATTN_WIDTH:, :], preferred_element_type=F32))
    return x + g1 * mix


def _pool_mix(x, t, mod, g_mix, wpool_ref, band_ref, pscale, uprev_ref):
    tb = TB
    sh1, sc1, g1 = mod[0:1], mod[1:2], mod[2:3]
    u = _rms_mod(x, g_mix, sc1, sh1)
    halo = jnp.where(t == 0, 0.0, uprev_ref[...]).astype(BF16)
    ext = jnp.concatenate([jnp.zeros((LANES - POOL_HALO, D_MODEL), BF16), halo, u.astype(BF16)],
                          axis=0)
    uprev_ref[...] = u[tb - POOL_HALO:tb]

    yield
    pos = (t * tb + 1 + lax.broadcasted_iota(I32, (tb, 1), 0)).astype(F32)
    cols = [slice(gi * POOL_GROUP, (gi + 1) * POOL_GROUP) for gi in range(len(POOL_SIZES))]
    wsums = [jnp.dot(band_ref[gi], ext[:, sl], preferred_element_type=F32)
             for gi, sl in enumerate(cols)]
    yield
    mixed = []
    for gi, w in enumerate(POOL_SIZES):
        mean = wsums[gi] / jnp.minimum(pos, float(w))
        pooled = mean - u[:, cols[gi]]
        mixed.append(jnp.dot(pooled.astype(BF16), wpool_ref[gi].astype(BF16),
                             preferred_element_type=F32))
    yield
    mix = jnp.concatenate(mixed, axis=1) * pscale
    return x + g1 * mix


def _interleave(chains, starts):
    results = [None] * len(chains)
    live = list(range(len(chains)))
    rnd = 0
    while live:
        for k in list(live):
            if rnd < starts[k]:
                continue
            try:
                next(chains[k])
            except StopIteration as done:
                results[k] = done.value
                live.remove(k)
        rnd += 1
    return results


def _mixer_kernel(*refs, kind, layer, has_head, nb, n_steps, cap, n_work):
    refs = list(refs)
    t = pl.program_id(0)
    slot = t % 2
    if has_head:
        pm_ref = refs.pop(0)
        hprev_ref, modprev_ref, tokprev_ref, ys_hbm = refs[:4]
        refs = refs[4:]
    else:
        h_ref = refs.pop(0)
    mod_ref, nmix_ref, nffn_ref = refs[:3]
    refs = refs[3:]
    n_w = 5 if kind == "even" else 3
    weight_refs = refs[:n_w]
    refs = refs[n_w:]
    wrt_ref, br_ref, tri_ref = refs[:3]
    refs = refs[3:]
    hout_ref, tokmeta_ref, tmeta_ref, xs_hbm = refs[:4]
    refs = refs[4:]
    n_c = 2 if kind == "even" else 1
    carry_refs = refs[:n_c]
    refs = refs[n_c:]
    if kind == "even":
        cast_refs = refs[:2]
        refs = refs[2:]
    h1s_ref, stage_ref, zeros_ref, run_ref, prev_ref, ssem, zsem = refs[:7]
    refs = refs[7:]

    tile = jnp.minimum(t, n_steps - 1)
    live_tail = t > 0

    @pl.when(t == 0)
    def _():
        h1s_ref[...] = jnp.zeros_like(h1s_ref)
        for r in carry_refs:
            r[...] = jnp.zeros_like(r)
        for g in range(N_GROUPS):
            run_ref[g] = jnp.int32(0)
        if kind == "even":
            for src, dst in zip(weight_refs[:2], cast_refs):
                for c0 in range(0, src.shape[1], 2 * LANES):
                    dst[:, c0:c0 + 2 * LANES] = src[:, c0:c0 + 2 * LANES].astype(BF16)
        _prime_scatter(nb, cap, xs_hbm, stage_ref, prev_ref, ssem)

    j = layer // 2
    g_mix = nmix_ref[layer:layer + 1]
    g_ffn = nffn_ref[layer:layer + 1]
    if kind == "even":
        sink_ref = weight_refs[3]
        weight_refs = list(cast_refs) + [weight_refs[2], lambda i: sink_ref[j, i], weight_refs[4]]
    else:
        weight_refs = list(weight_refs[:2]) + [weight_refs[2][j:j + 1]]

    drain_head = None
    if has_head:
        ystage_ref, gsem = refs
        drain_head = _head_dma(t, n_steps, nb, pm_ref, ys_hbm, ystage_ref, gsem)

    def main_chain(b):
        if has_head:
            x = _head_combine(b, slot, hprev_ref, modprev_ref, tokprev_ref, ystage_ref)
            yield
        else:
            x = h_ref[b]
        carries = [r.at[b] for r in carry_refs]
        mix = _attn_conv_mix if kind == "even" else _pool_mix
        h1 = yield from mix(x, tile, mod_ref[b], g_mix, *weight_refs, *carries)
        hout_ref[b] = h1
        h1s_ref[b, slot] = h1
        yield

    def tail_chain(b):
        h1 = h1s_ref[b, 1 - slot]
        cnt = yield from _tail_compute(b, slot, h1, mod_ref[b], g_ffn, wrt_ref, br_ref, tri_ref,
                                       tokmeta_ref, stage_ref)
        return [jnp.where(live_tail, c, 0) for c in cnt]

    chains = [main_chain(b) for b in range(nb)] + [tail_chain(b) for b in range(nb)]
    counts = _interleave(chains, [MAIN_START] * nb + [0] * nb)[nb:]
    _tail_dma(t, n_steps, nb, cap, n_work, counts, tmeta_ref, xs_hbm, stage_ref, zeros_ref,
              run_ref, prev_ref, ssem, zsem, drain_head)


def _nat(nb, cols):
    return pl.BlockSpec((nb, TB, cols), lambda t, *_: (0, t, 0))


def _full(shape):
    nd = len(shape)
    return pl.BlockSpec(shape, lambda t, *_: (0,) * nd)


def _head_scratch(nb):
    return [pltpu.VMEM((nb, 2, STG_C, D_MODEL), BF16), pltpu.SemaphoreType.DMA((nb, 2))]


def _mod_spec(nb, layer):
    return pl.BlockSpec((None, nb, 8, D_MODEL), lambda t, *_: (layer, 0, 0, 0))


def _mixer_call(kind, layer, head, h, mod_all, norm_mix, norm_ffn, weights, wrt, br, tri):
    j = layer // 2
    nb, seq = (h if head is None else head[2]).shape[:2]
    n_steps = seq // TB
    n_tiles, cap, n_work = _sizes(nb * seq)
    has_head = head is not None
    any_spec = pl.BlockSpec(memory_space=pl.ANY)
    smem_spec = pl.BlockSpec(memory_space=pltpu.SMEM)

    def main(cols):
        return pl.BlockSpec((nb, TB, cols), lambda t, *_: (0, jnp.minimum(t, n_steps - 1), 0))

    def lagged(cols):
        return pl.BlockSpec((nb, TB, cols), lambda t, *_: (0, jnp.maximum(t - 1, 0), 0))

    if has_head:
        pm, hprev, tokprev, ys = head
        args = [hprev, mod_all, tokprev, ys]
        in_specs = [main(D_MODEL), _mod_spec(nb, layer - 1), main(LANES), any_spec]
    else:
        args = [h]
        in_specs = [main(D_MODEL)]
    args += [mod_all, norm_mix, norm_ffn]
    in_specs += [_mod_spec(nb, layer), _full(norm_mix.shape), _full(norm_ffn.shape)]
    if kind == "even":
        args += list(weights)
        once = dict(pipeline_mode=pl.Buffered(1))
        in_specs += [pl.BlockSpec((None, D_MODEL, IN_WIDTH), lambda t, *_: (j, 0, 0), **once),
                     pl.BlockSpec((None, D_MODEL, D_MODEL), lambda t, *_: (j, 0, 0), **once),
                     _full((N_HEADS // 4, 2 * BLOCK, 4 * BLOCK)), smem_spec,
                     pl.BlockSpec((None,) + weights[4].shape[1:], lambda t, *_: (j, 0, 0))]
        mix_scratch = [pltpu.VMEM((nb, BLOCK, 2 * KV_WIDTH), F32),
                       pltpu.VMEM((nb, 8, CONV_WIDTH), F32),
                       pltpu.VMEM((D_MODEL, IN_WIDTH), BF16),
                       pltpu.VMEM((D_MODEL, D_MODEL), BF16)]
    else:
        args += list(weights)
        in_specs += [pl.BlockSpec((None, len(POOL_SIZES), POOL_GROUP, POOL_GROUP),
                                  lambda t, *_: (j, 0, 0, 0)),
                     _full((len(POOL_SIZES), TB, TB + LANES)), _full(weights[2].shape)]
        mix_scratch = [pltpu.VMEM((nb, POOL_HALO, D_MODEL), F32)]
    args += [wrt, br, tri]
    in_specs += [_full((2 * N_EXPERTS, D_MODEL)), _full((N_EXPERTS, 1)), _full((TB, TB))]

    out_shape = (jax.ShapeDtypeStruct((nb, seq + TB, D_MODEL), F32),
                 jax.ShapeDtypeStruct((nb, seq, LANES), F32),
                 jax.ShapeDtypeStruct(((n_tiles + 1) * META_W + 4 * n_work,), I32),
                 jax.ShapeDtypeStruct((N_GROUPS * cap + nb * N_SUB * SUB, ROW_W), BF16))
    out_specs = (_nat(nb, D_MODEL), lagged(LANES), smem_spec, any_spec)
    scratch = mix_scratch + [
        pltpu.VMEM((nb, 2, TB, D_MODEL), F32),
        pltpu.VMEM((nb, 2, STG_P, ROW_W), BF16),
        pltpu.VMEM((TBM, ROW_W), BF16),
        pltpu.SMEM((N_GROUPS,), I32),
        pltpu.SMEM((nb * N_SUB,), I32),
        pltpu.SemaphoreType.DMA((nb, 2)),
        pltpu.SemaphoreType.DMA(()),
    ]
    if has_head:
        scratch += _head_scratch(nb)

    body = functools.partial(_mixer_kernel, kind=kind, layer=layer, has_head=has_head, nb=nb,
                             n_steps=n_steps, cap=cap, n_work=n_work)
    grid_spec = pltpu.PrefetchScalarGridSpec(
        num_scalar_prefetch=1 if has_head else 0, grid=(n_steps + 1,),
        in_specs=in_specs, out_specs=out_specs, scratch_shapes=scratch)
    call = pl.pallas_call(
        body, grid_spec=grid_spec, out_shape=out_shape,
        compiler_params=pltpu.CompilerParams(
            dimension_semantics=("arbitrary",), vmem_limit_bytes=VMEM_LIMIT),
        name=kind + "_mixer")
    if has_head:
        return call(pm, *args)
    return call(*args)


def _work_tables(totals, cap, n_work, tmeta_ref, t0):
    ends, starts = [], []
    acc = jnp.int32(0)
    for g in range(N_GROUPS):
        starts.append(acc)
        acc = acc + (totals[g] + (TBM - 1)) // TBM
        ends.append(acc)
    nvalid = ends[-1]
    prev_grp = jnp.int32(-1)
    for i in range(n_work):
        idc = jnp.minimum(jnp.int32(i), nvalid - 1)
        grp, start, total = jnp.int32(0), starts[0], totals[0]
        for g in range(1, N_GROUPS):
            later = idc >= ends[g - 1]
            grp = jnp.where(later, g, grp)
            start = jnp.where(later, starts[g], start)
            total = jnp.where(later, totals[g], total)
        valid = i < nvalid
        tile = idc - start
        tmeta_ref[t0 + i] = grp * (cap // TBM) + tile
        tmeta_ref[t0 + n_work + i] = grp
        tmeta_ref[t0 + 2 * n_work + i] = jnp.where(valid, jnp.minimum(total - tile * TBM, TBM), 0)
        tmeta_ref[t0 + 3 * n_work + i] = jnp.logical_and(valid, grp != prev_grp).astype(I32)
        prev_grp = grp


def _expert_rows(xs_ref, ys_ref, wg_s, wu_s, wd_s, r0, nrows):
    xb = xs_ref[r0:r0 + nrows, 0:D_MODEL]
    meta = xs_ref[r0:r0 + nrows, D_MODEL:ROW_W].astype(F32)
    gates = meta[:, 0:EXPERTS_PER_GROUP] + meta[:, EXPERTS_PER_GROUP:2 * EXPERTS_PER_GROUP]
    parts = []
    for e in range(EXPERTS_PER_GROUP):
        a = jnp.dot(xb, wg_s[e], preferred_element_type=F32)
        bu = jnp.dot(xb, wu_s[e], preferred_element_type=F32)
        hid = (a * _sigmoid(a)) * bu
        parts.append((hid * gates[:, e:e + 1]).astype(BF16))
    hid_all = jnp.concatenate(parts, axis=1)
    ys_ref[r0:r0 + nrows, :] = jnp.dot(hid_all, wd_s[...],
                                       preferred_element_type=F32).astype(BF16)


def _moe_kernel(tm_ref, xs_ref, wg_ref, wu_ref, wd_ref, ys_ref, wg_s, wu_s, wd_s, *, t0, n_work):
    i = pl.program_id(0)
    rows = tm_ref[t0 + 2 * n_work + i]
    first_of_group = tm_ref[t0 + 3 * n_work + i] == 1

    @pl.when(first_of_group)
    def _():
        for e in range(EXPERTS_PER_GROUP):
            wg_s[e] = wg_ref[e].astype(BF16)
            wu_s[e] = wu_ref[e].astype(BF16)
        wd_s[...] = wd_ref[0].astype(BF16)

    @pl.when(rows == TBM)
    def _():
        _expert_rows(xs_ref, ys_ref, wg_s, wu_s, wd_s, 0, TBM)

    @pl.when(jnp.logical_and(rows > 0, rows < TBM))
    def _():
        for r0 in range(0, TBM, TBM_PART):
            @pl.when(r0 < rows)
            def _():
                _expert_rows(xs_ref, ys_ref, wg_s, wu_s, wd_s, r0, TBM_PART)

            @pl.when(r0 >= rows)
            def _():
                ys_ref[r0:r0 + TBM_PART, :] = jnp.zeros((TBM_PART, D_MODEL), BF16)


def _moe_call(xs, tmeta, layer, wg, wu, wd, n_tiles, n_work):
    t0 = (n_tiles + 1) * META_W
    row_map = lambda i, tm: (tm[t0 + i], 0)
    grp_map = lambda i, tm: (layer, tm[t0 + n_work + i], 0, 0)
    grid_spec = pltpu.PrefetchScalarGridSpec(
        num_scalar_prefetch=1, grid=(n_work,),
        in_specs=[
            pl.BlockSpec((TBM, ROW_W), row_map),
            pl.BlockSpec((None, EXPERTS_PER_GROUP, D_MODEL, EXPERT_FF), grp_map),
            pl.BlockSpec((None, EXPERTS_PER_GROUP, D_MODEL, EXPERT_FF), grp_map),
            pl.BlockSpec((None, 1, GROUP_FF, D_MODEL), grp_map),
        ],
        out_specs=pl.BlockSpec((TBM, D_MODEL), row_map),
        scratch_shapes=[pltpu.VMEM((EXPERTS_PER_GROUP, D_MODEL, EXPERT_FF), BF16),
                        pltpu.VMEM((EXPERTS_PER_GROUP, D_MODEL, EXPERT_FF), BF16),
                        pltpu.VMEM((GROUP_FF, D_MODEL), BF16)],
    )
    return pl.pallas_call(
        functools.partial(_moe_kernel, t0=t0, n_work=n_work), grid_spec=grid_spec,
        out_shape=jax.ShapeDtypeStruct((xs.shape[0], D_MODEL), BF16),
        compiler_params=pltpu.CompilerParams(
            dimension_semantics=("arbitrary",), vmem_limit_bytes=VMEM_LIMIT),
        name="moe_experts",
    )(tmeta, xs, wg, wu, wd)


def _final_kernel(pm_ref, hprev_ref, modprev_ref, tokprev_ref, ys_hbm, nf_ref, o_ref,
                  ystage_ref, gsem, *, nb, n_steps):
    t = pl.program_id(0)
    drain = _head_dma(t, n_steps, nb, pm_ref, ys_hbm, ystage_ref, gsem)
    for b in range(nb):
        h = _head_combine(b, t % 2, hprev_ref, modprev_ref, tokprev_ref, ystage_ref)
        ms = jnp.mean(h * h, axis=-1, keepdims=True)
        o_ref[b] = (h * lax.rsqrt(ms + EPS)) * nf_ref[0:1]
    pl.when(t == n_steps - 1)(drain)


def _final_call(head, mod_all, norm_final):
    pm, hprev, tokprev, ys = head
    nb, seq = tokprev.shape[:2]
    n_steps = seq // TB
    grid_spec = pltpu.PrefetchScalarGridSpec(
        num_scalar_prefetch=1, grid=(n_steps,),
        in_specs=[_nat(nb, D_MODEL), _mod_spec(nb, DEPTH - 1), _nat(nb, LANES),
                  pl.BlockSpec(memory_space=pl.ANY), _full((1, D_MODEL))],
        out_specs=_nat(nb, D_MODEL),
        scratch_shapes=_head_scratch(nb))
    return pl.pallas_call(
        functools.partial(_final_kernel, nb=nb, n_steps=n_steps), grid_spec=grid_spec,
        out_shape=jax.ShapeDtypeStruct((nb, seq, D_MODEL), F32),
        compiler_params=pltpu.CompilerParams(
            dimension_semantics=("arbitrary",), vmem_limit_bytes=VMEM_LIMIT),
        name="final_norm",
    )(pm, hprev, mod_all, tokprev, ys, norm_final.reshape(1, D_MODEL))


def _pool_band_table():
    r = jnp.arange(TB)[:, None]
    c = jnp.arange(TB + LANES)[None, :]
    return jnp.stack([(c > r + LANES - w) & (c <= r + LANES) for w in POOL_SIZES]).astype(BF16)


def _attn_bias_table():
    slopes = 2.0 ** (-8.0 * (jnp.arange(N_HEADS, dtype=F32) + 1.0) / N_HEADS)
    dist = (jnp.arange(BLOCK)[:, None] + BLOCK) - jnp.arange(2 * BLOCK)[None, :]
    ok = (dist >= 0) & (dist < WINDOW)
    per_head = jnp.where(ok[None], -slopes[:, None, None] * dist.astype(F32)[None], NEG_BIG)
    return per_head.reshape(N_HEADS // 2, 2, BLOCK, 2 * BLOCK).transpose(0, 2, 1, 3).reshape(
        N_HEADS // 4, 2 * BLOCK, 4 * BLOCK)


def _router_operands(w_router, b_router):
    def reorder(a):
        return a.reshape(N_GROUPS, EXPERTS_PER_GROUP, -1).transpose(1, 0, 2).reshape(N_EXPERTS, -1)
    w = reorder(w_router.T)
    wh = w.astype(BF16)
    wl = (w - wh.astype(F32)).astype(BF16)
    return jnp.concatenate([wh, wl], axis=0), reorder(b_router[:, None])


def kernel(x, c, w_ada, b_ada, norm_mix, norm_ffn, w_in, w_out, sinks, conv_w, w_pool,
           pool_scale, w_router, b_router, w_gate, w_up, w_down, norm_final):
    b, s, _ = x.shape
    n_tiles, cap, n_work = _sizes(b * s)

    mod_all = _mod_call(c, w_ada, b_ada)
    bias_tab = _attn_bias_table()
    band_tab = _pool_band_table()
    wrt, br = _router_operands(w_router, b_router)
    tri = jnp.triu(jnp.ones((TB, TB), F32)).astype(BF16)
    w_down4 = w_down.reshape(DEPTH, N_GROUPS, GROUP_FF, D_MODEL)

    h = x
    head = None
    for layer in range(DEPTH):
        if layer % 2 == 0:
            kind = "even"
            weights = (w_in, w_out, bias_tab, sinks, conv_w)
        else:
            kind = "odd"
            weights = (w_pool, band_tab, pool_scale)
        h1, tokmeta, tmeta, xs = _mixer_call(kind, layer, head, h, mod_all, norm_mix, norm_ffn,
                                             weights, wrt, br, tri)
        ys = _moe_call(xs, tmeta, layer, w_gate, w_up, w_down4, n_tiles, n_work)
        head = (tmeta, h1, tokmeta, ys)
    return _final_call(head, mod_all, norm_final)
```

```python
import functools

import jax
import jax.numpy as jnp
from jax import lax
from jax.experimental import pallas as pl
from jax.experimental.pallas import tpu as pltpu

F32 = jnp.float32
BF16 = jnp.bfloat16
I32 = jnp.int32

D_MODEL = 1024
DEPTH = 4
EPS = 1e-6
N_MOD = 6

ATTN_WIDTH = 512
HEAD_DIM = 64
N_HEADS = 8
KV_WIDTH = 128
WINDOW = 128
BLOCK = 128
CONV_WIDTH = 512
IN_WIDTH = 2304

POOL_SIZES = (2, 4, 8, 16)
POOL_GROUP = 256
POOL_HALO = 16

N_EXPERTS = 16
N_GROUPS = 4
EXPERTS_PER_GROUP = 4
EXPERT_FF = 256
GROUP_FF = EXPERTS_PER_GROUP * EXPERT_FF

LANES = 128
NEG_BIG = -1e30

TB = 256
SUB = 16
N_SUB = TB // SUB + N_GROUPS - 1
STG_P = 320
STG_C = 384
TBM = 1024
TBM_PART = 256
ROW_W = D_MODEL + LANES
META_W = 32
LP_LANE = 8
MAIN_START = 2
MOD_COLS = 3072
VMEM_LIMIT = 56 * 1024 * 1024
NT_DIMS = (((1,), (1,)), ((), ()))

assert N_SUB * SUB <= STG_P <= STG_C and STG_C % LANES == 0 and N_SUB <= META_W


def _sizes(n_tok):
    n_tiles = n_tok // TB
    cap = -(-(n_tok + n_tiles * (SUB - 1) + TBM) // TBM) * TBM
    n_work = (n_tok + n_tiles * N_GROUPS * (SUB - 1)) // TBM + N_GROUPS
    return n_tiles, cap, n_work


def _sigmoid(x):
    return 1.0 / (1.0 + jnp.exp(-x))


def _rms_mod(x, g, sc, sh):
    ms = jnp.mean(x * x, axis=-1, keepdims=True)
    return (x * lax.rsqrt(ms + EPS)) * (g * (1.0 + sc)) + sh


def _onehot(cond):
    return jnp.where(cond, 1.0, 0.0).astype(BF16)


def _mod_kernel(ct_ref, w_ref, b_ref, o_ref, *, nb):
    j = pl.program_id(1)
    ct = ct_ref[...]
    cond = ct * _sigmoid(ct)
    vec_per_step = MOD_COLS // D_MODEL
    for b in range(nb):
        col = cond[:, b:b + 1]
        acc = jnp.zeros((8, MOD_COLS), F32)
        for r in range(D_MODEL // 8):
            acc = acc + w_ref[0, 8 * r:8 * r + 8, :] * col[8 * r:8 * r + 8]
        res = jnp.sum(acc, axis=0, keepdims=True) + b_ref[0]
        for jj in range(N_MOD // vec_per_step):
            @pl.when(j == jj)
            def _():
                for k in range(vec_per_step):
                    row = jj * vec_per_step + k
                    o_ref[0, b, row:row + 1, :] = res[:, k * D_MODEL:(k + 1) * D_MODEL]

        @pl.when(j == 0)
        def _():
            o_ref[0, b, N_MOD:8, :] = jnp.zeros((8 - N_MOD, D_MODEL), F32)


def _mod_call(c, w_ada, b_ada):
    nb = c.shape[0]
    ct = jnp.pad(c.T, ((0, 0), (0, LANES - nb)))
    n_col = (N_MOD * D_MODEL) // MOD_COLS
    return pl.pallas_call(
        functools.partial(_mod_kernel, nb=nb),
        grid=(DEPTH, n_col),
        in_specs=[
            pl.BlockSpec((D_MODEL, LANES), lambda l, j: (0, 0)),
            pl.BlockSpec((1, D_MODEL, MOD_COLS), lambda l, j: (l, 0, j)),
            pl.BlockSpec((1, 1, MOD_COLS), lambda l, j: (l, 0, j)),
        ],
        out_specs=pl.BlockSpec((1, nb, 8, D_MODEL), lambda l, j: (l, 0, 0, 0)),
        out_shape=jax.ShapeDtypeStruct((DEPTH, nb, 8, D_MODEL), F32),
        compiler_params=pltpu.CompilerParams(
            dimension_semantics=("arbitrary", "arbitrary"), vmem_limit_bytes=VMEM_LIMIT),
        name="adaln_mod",
    )(ct, w_ada, b_ada.reshape(DEPTH, 1, N_MOD * D_MODEL))


def _sub_copies(hbm_ref, stage_ref, sem, rows, to_hbm):
    out = []
    for s in range(N_SUB):
        s_view = stage_ref.at[pl.ds(s * SUB, SUB), :]
        h_view = hbm_ref.at[pl.ds(pl.multiple_of(rows[s], SUB), SUB), :]
        out.append(pltpu.make_async_copy(s_view, h_view, sem) if to_hbm
                   else pltpu.make_async_copy(h_view, s_view, sem))
    return out


def _head_dma(t, n_steps, nb, pm_ref, ys_hbm, ystage_ref, gsem):
    slot = t % 2

    def copies(step, b, slot_):
        base = (jnp.minimum(step, n_steps - 1) * nb + b) * META_W
        rows = [pm_ref[base + s] for s in range(N_SUB)]
        return _sub_copies(ys_hbm, ystage_ref.at[b, slot_], gsem.at[b, slot_], rows, False)

    @pl.when(t == 0)
    def _():
        ystage_ref[...] = jnp.zeros_like(ystage_ref)
        for b in range(nb):
            for cp in copies(0, b, 0):
                cp.start()

    nxt = t + 1
    for b in range(nb):
        for cp in copies(nxt, b, 1 - slot):
            cp.start()

    for b in range(nb):
        for cp in copies(t, b, slot):
            cp.wait()

    def drain():
        for b in range(nb):
            for cp in copies(nxt, b, 1 - slot):
                cp.wait()
    return drain


def _head_combine(b, slot, hprev_ref, modprev_ref, tokmeta_ref, ystage_ref):
    lp = tokmeta_ref[b, :, LP_LANE:LP_LANE + 1].astype(I32)
    unsort = _onehot(lax.broadcasted_iota(I32, (TB, STG_C), 1) == lp)
    y = jnp.dot(unsort, ystage_ref[b, slot], preferred_element_type=F32)
    return hprev_ref[b] + modprev_ref[b, 5:6] * y


def _route(u2, wrt_ref, br_ref):
    uh = u2.astype(BF16)
    ul = (u2 - uh.astype(F32)).astype(BF16)
    w = wrt_ref[...]
    r1 = lax.dot_general(w, uh, NT_DIMS, preferred_element_type=F32)
    r2 = lax.dot_general(w[0:N_EXPERTS], ul, NT_DIMS, preferred_element_type=F32)
    yield
    scores = _sigmoid(r1[0:N_EXPERTS] + r1[N_EXPERTS:] + r2)
    biased = scores + br_ref[...]
    s = [scores[4 * j:4 * j + 4] for j in range(EXPERTS_PER_GROUP)]
    b = [biased[4 * j:4 * j + 4] for j in range(EXPERTS_PER_GROUP)]

    hi1, lo1 = jnp.maximum(b[0], b[1]), jnp.minimum(b[0], b[1])
    hi2, lo2 = jnp.maximum(b[2], b[3]), jnp.minimum(b[2], b[3])
    second = jnp.maximum(jnp.minimum(hi1, hi2), jnp.where(hi1 >= hi2, lo1, lo2))
    gscore = jnp.maximum(hi1, hi2) + second

    best = gscore[0:1]
    bgrp = jnp.zeros((1, TB), I32)
    for g in range(1, N_GROUPS):
        better = gscore[g:g + 1] > best
        bgrp = jnp.where(better, g, bgrp)
        best = jnp.where(better, gscore[g:g + 1], best)
    selmask = lax.broadcasted_iota(I32, (N_GROUPS, TB), 0) == bgrp
    yield

    m1 = b[0]
    i1 = jnp.zeros((N_GROUPS, TB), I32)
    for j in range(1, EXPERTS_PER_GROUP):
        gt = b[j] > m1
        i1 = jnp.where(gt, j, i1)
        m1 = jnp.where(gt, b[j], m1)
    cands = [jnp.where(i1 == j, -jnp.inf, b[j]) for j in range(EXPERTS_PER_GROUP)]
    m2 = cands[0]
    i2 = jnp.zeros_like(i1)
    for j in range(1, EXPERTS_PER_GROUP):
        gt = cands[j] > m2
        i2 = jnp.where(gt, j, i2)
        m2 = jnp.where(gt, cands[j], m2)
    w1 = s[0]
    w2 = s[0]
    for j in range(1, EXPERTS_PER_GROUP):
        w1 = jnp.where(i1 == j, s[j], w1)
        w2 = jnp.where(i2 == j, s[j], w2)
    yield
    tot = w1 + w2
    w1n = w1 / tot
    w2n = w2 / tot
    gates = []
    for j in range(EXPERTS_PER_GROUP):
        gj = jnp.where(i1 == j, w1n, 0.0) + jnp.where(i2 == j, w2n, 0.0)
        gates.append(jnp.sum(jnp.where(selmask, gj, 0.0), axis=0, keepdims=True))
    return bgrp, gates, selmask


def _tail_compute(b, slot, h1, mod, g_ffn, wrt_ref, br_ref, tri_ref, tokmeta_ref, stage_ref):
    sh2, sc2 = mod[3:4], mod[4:5]
    u2 = _rms_mod(h1, g_ffn, sc2, sh2)
    bgrp, gates, selmask = yield from _route(u2, wrt_ref, br_ref)
    yield

    onehot = jnp.concatenate([jnp.where(selmask, 1.0, 0.0), jnp.zeros((8 - N_GROUPS, TB), F32)],
                             axis=0).astype(BF16)
    rank_incl = jnp.dot(onehot, tri_ref[...], preferred_element_type=F32)
    rank = jnp.sum(jnp.where(selmask, rank_incl[0:N_GROUPS], 0.0), axis=0, keepdims=True) - 1.0
    cnt = lax.dot_general(jnp.ones((8, TB), BF16), onehot, NT_DIMS,
                          preferred_element_type=F32)
    sub_rows = jnp.floor((cnt[0:1, :] + (SUB - 1.0)) * (1.0 / SUB)) * SUB
    lp = rank
    first_row = jnp.zeros((1, 1), F32)
    for g in range(1, N_GROUPS):
        first_row = first_row + sub_rows[:, g - 1:g]
        lp = lp + jnp.where(bgrp == g, first_row, 0.0)

    hi = [x.astype(BF16).astype(F32) for x in gates]
    lo = [(x - h).astype(BF16).astype(F32) for x, h in zip(gates, hi)]
    meta_src = jnp.concatenate(hi + lo + [lp, jnp.zeros((LANES - 9, TB), F32)], axis=0)
    meta_t = meta_src.T
    tokmeta_ref[b] = meta_t
    yield

    rowdata = jnp.concatenate([u2.astype(BF16), meta_t.astype(BF16)], axis=1)
    sort = _onehot(lax.broadcasted_iota(I32, (STG_P, TB), 0) == lp.astype(I32))
    stage_ref[b, slot] = jnp.dot(sort, rowdata, preferred_element_type=F32).astype(BF16)
    return [cnt[0, g].astype(I32) for g in range(N_GROUPS)]


def _prime_scatter(nb, cap, xs_hbm, stage_ref, prev_ref, ssem):
    stage_ref[:, 1] = jnp.zeros((nb,) + stage_ref.shape[2:], BF16)
    for b in range(nb):
        rows = [jnp.int32(N_GROUPS * cap + (b * N_SUB + s) * SUB) for s in range(N_SUB)]
        for s in range(N_SUB):
            prev_ref[b * N_SUB + s] = rows[s]
        for cp in _sub_copies(xs_hbm, stage_ref.at[b, 1], ssem.at[b, 1], rows, True):
            cp.start()


def _tail_dma(t, lag, n_steps, nb, cap, n_work, counts, tmeta_ref, xs_hbm, stage_ref, zeros_ref,
              run_ref, prev_ref, ssem, zsem, drain_head):
    slot = t % 2
    run = [run_ref[g] for g in range(N_GROUPS)]
    junk = N_GROUPS * cap
    scatter_rows = []
    for b in range(nb):
        first_sub = [jnp.int32(0)]
        for g in range(N_GROUPS):
            first_sub.append(first_sub[-1] + (counts[b][g] + (SUB - 1)) // SUB)
        dst = [g * cap + run[g] for g in range(N_GROUPS)]
        rows_b, base = [], (jnp.maximum(t - lag, 0) * nb + b) * META_W
        for s in range(N_SUB):
            grp_first, grp_dst = first_sub[0], dst[0]
            for g in range(1, N_GROUPS):
                later = s >= first_sub[g]
                grp_first = jnp.where(later, first_sub[g], grp_first)
                grp_dst = jnp.where(later, dst[g], grp_dst)
            row = grp_dst + (s - grp_first) * SUB
            used = s < first_sub[N_GROUPS]
            rows_b.append(jnp.where(used, row, junk + (b * N_SUB + s) * SUB))
            tmeta_ref[base + s] = jnp.where(used, row, row if s == 0 else first_row)
            if s == 0:
                first_row = row
        for k in range(N_SUB, META_W):
            tmeta_ref[base + k] = jnp.int32(0)
        for g in range(N_GROUPS):
            run[g] = run[g] + (first_sub[g + 1] - first_sub[g]) * SUB
        scatter_rows.append(rows_b)
    for g in range(N_GROUPS):
        run_ref[g] = run[g]

    def copies(b, slot_, rows):
        return _sub_copies(xs_hbm, stage_ref.at[b, slot_], ssem.at[b, slot_], rows, True)

    for b in range(nb):
        for cp in copies(b, 1 - slot, [prev_ref[b * N_SUB + s] for s in range(N_SUB)]):
            cp.wait()

    for b in range(nb):
        for cp in copies(b, slot, scatter_rows[b]):
            cp.start()
        for s in range(N_SUB):
            prev_ref[b * N_SUB + s] = scatter_rows[b][s]

    @pl.when(t == n_steps - 1 + lag)
    def _():
        for b in range(nb):
            for cp in copies(b, slot, scatter_rows[b]):
                cp.wait()
        if drain_head is not None:
            drain_head()
        zeros_ref[...] = jnp.zeros_like(zeros_ref)
        tails = []
        base = n_steps * nb * META_W
        for g in range(N_GROUPS):
            tmeta_ref[base + g] = run[g]
            start = pl.multiple_of(g * cap + run[g], SUB)
            tails.append(pltpu.make_async_copy(zeros_ref, xs_hbm.at[pl.ds(start, TBM), :], zsem))
        for k in range(N_GROUPS, META_W):
            tmeta_ref[base + k] = jnp.int32(0)
        for cp in tails:
            cp.start()
        _work_tables(run, cap, n_work, tmeta_ref, base + META_W)
        for cp in tails:
            cp.wait()


def _attn_conv_mix(x, t, mod, g_mix, win_ref, wout_ref, bias_ref, sink_at, convw_ref,
                   kvprev_ref, cprev_ref):
    tb = TB
    first = t == 0
    sh1, sc1, g1 = mod[0:1], mod[1:2], mod[2:3]
    u = _rms_mod(x, g_mix, sc1, sh1)
    proj = jnp.dot(u.astype(BF16), win_ref[...], preferred_element_type=F32)

    yield
    q = (proj[:, 0:ATTN_WIDTH] * HEAD_DIM ** -0.5).astype(BF16)
    kf = proj[:, 512:640]
    vf = proj[:, 640:768]
    bgate = proj[:, 768:1280]
    cgate = proj[:, 1280:1792]
    xv = proj[:, 1792:2304]

    kv_prev = kvprev_ref[...]
    kext = jnp.concatenate([kv_prev[:, 0:KV_WIDTH], kf], axis=0)
    vext = jnp.concatenate([kv_prev[:, KV_WIDTH:], vf], axis=0)
    kvprev_ref[:, 0:KV_WIDTH] = kf[tb - BLOCK:tb]
    kvprev_ref[:, KV_WIDTH:] = vf[tb - BLOCK:tb]

    lane = lax.broadcasted_iota(I32, kext.shape, 1)
    lo = lane < HEAD_DIM
    krol = pltpu.roll(kext, HEAD_DIM, axis=1)
    vrol = pltpu.roll(vext, HEAD_DIM, axis=1)
    zero = jnp.zeros_like(kext)
    one_at_64 = jnp.where(lane == HEAD_DIM, 1.0, 0.0)
    one_at_0 = jnp.where(lane == 0, 1.0, 0.0)
    k_ops = [(jnp.where(lo, kext, zero).astype(BF16), jnp.where(lo, zero, krol).astype(BF16)),
             (jnp.where(lo, krol, zero).astype(BF16), jnp.where(lo, zero, kext).astype(BF16))]
    v_ops = [(jnp.where(lo, vext, one_at_64).astype(BF16), jnp.where(lo, one_at_0, vrol).astype(BF16)),
             (jnp.where(lo, vrol, one_at_64).astype(BF16), jnp.where(lo, one_at_0, vext).astype(BF16))]

    col = lax.broadcasted_iota(I32, (2 * BLOCK, 4 * BLOCK), 1)
    prev_cols = (col % (2 * BLOCK)) < BLOCK
    upper = lax.broadcasted_iota(I32, (2 * BLOCK, 1), 0) < BLOCK
    lane_o = lax.broadcasted_iota(I32, (2 * BLOCK, LANES), 1)
    lo_o = lane_o < HEAD_DIM

    yield
    attn_rows = []
    for bi in range(tb // BLOCK):
        r0 = bi * BLOCK
        pair_out = []
        for kvh in range(N_HEADS // 4):
            ka, kb = k_ops[kvh]
            rhs = jnp.concatenate([ka[r0:r0 + 2 * BLOCK], kb[r0:r0 + 2 * BLOCK]], axis=0)
            qrows = q[r0:r0 + BLOCK]
            qp = jnp.concatenate([qrows[:, (2 * kvh) * LANES:(2 * kvh + 1) * LANES],
                                  qrows[:, (2 * kvh + 1) * LANES:(2 * kvh + 2) * LANES]], axis=0)
            s = lax.dot_general(qp, rhs, NT_DIMS, preferred_element_type=F32)
            bias = bias_ref[kvh]
            if bi == 0:
                bias = jnp.where(prev_cols & first, NEG_BIG, bias)
            s = s + bias
            outs = []
            for hh in range(2):
                sh = s[:, hh * 2 * BLOCK:(hh + 1) * 2 * BLOCK]
                sink = jnp.where(upper, sink_at(4 * kvh + hh), sink_at(4 * kvh + 2 + hh))
                m = jnp.maximum(jnp.max(sh, axis=-1, keepdims=True), sink)
                p = jnp.exp(sh - m).astype(BF16)
                vop = v_ops[kvh][hh][r0:r0 + 2 * BLOCK]
                o = jnp.dot(p, vop, preferred_element_type=F32)
                den_col = HEAD_DIM if hh == 0 else 0
                den = o[:, den_col:den_col + 1] + jnp.exp(sink - m)
                outs.append(o / den)
            both = jnp.where(lo_o, outs[0], outs[1])
            pair_out += [both[0:BLOCK], both[BLOCK:]]
            yield
        attn_rows.append(jnp.concatenate(pair_out, axis=1))
    attn = jnp.concatenate(attn_rows, axis=0)

    uc = cgate * xv
    cprev = jnp.where(first, 0.0, cprev_ref[...])
    row = lax.broadcasted_iota(I32, uc.shape, 0)
    r1 = jnp.where(row == 0, cprev[7:8], pltpu.roll(uc, 1, axis=0))
    r2 = jnp.where(row == 0, cprev[6:7], jnp.where(row == 1, cprev[7:8], pltpu.roll(uc, 2, axis=0)))
    cprev_ref[...] = uc[tb - 8:tb]
    cw = convw_ref[...]
    conv = bgate * (cw[0:1] * r2 + cw[1:2] * r1 + cw[2:3] * uc)
    yield

    mix = (jnp.dot(attn.astype(BF16), wout_ref[0:ATTN_WIDTH, :], preferred_element_type=F32)
           + jnp.dot(conv.astype(BF16), wout_ref[ATTN_WIDTH:, :], preferred_element_type=F32))
    return x + g1 * mix


def _pool_mix(x, t, mod, g_mix, wpool_ref, band_ref, pscale, uprev_ref):
    tb = TB
    sh1, sc1, g1 = mod[0:1], mod[1:2], mod[2:3]
    u = _rms_mod(x, g_mix, sc1, sh1)
    halo = jnp.where(t == 0, 0.0, uprev_ref[...]).astype(BF16)
    ext = jnp.concatenate([jnp.zeros((LANES - POOL_HALO, D_MODEL), BF16), halo, u.astype(BF16)],
                          axis=0)
    uprev_ref[...] = u[tb - POOL_HALO:tb]

    yield
    pos = (t * tb + 1 + lax.broadcasted_iota(I32, (tb, 1), 0)).astype(F32)
    cols = [slice(gi * POOL_GROUP, (gi + 1) * POOL_GROUP) for gi in range(len(POOL_SIZES))]
    wsums = [jnp.dot(band_ref[gi], ext[:, sl], preferred_element_type=F32)
             for gi, sl in enumerate(cols)]
    yield
    mixed = []
    for gi, w in enumerate(POOL_SIZES):
        mean = wsums[gi] / jnp.minimum(pos, float(w))
        pooled = mean - u[:, cols[gi]]
        mixed.append(jnp.dot(pooled.astype(BF16), wpool_ref[gi].astype(BF16),
                             preferred_element_type=F32))
    yield
    mix = jnp.concatenate(mixed, axis=1) * pscale
    return x + g1 * mix


def _interleave(chains, starts):
    results = [None] * len(chains)
    live = list(range(len(chains)))
    rnd = 0
    while live:
        for k in list(live):
            if rnd < starts[k]:
                continue
            try:
                next(chains[k])
            except StopIteration as done:
                results[k] = done.value
                live.remove(k)
        rnd += 1
    return results


def _mixer_kernel(*refs, kind, layer, lag, has_head, nb, n_steps, cap, n_work):
    refs = list(refs)
    t = pl.program_id(0)
    slot = t % 2
    if has_head:
        pm_ref = refs.pop(0)
        hprev_ref, modprev_ref, tokprev_ref, ys_hbm = refs[:4]
        refs = refs[4:]
    else:
        h_ref = refs.pop(0)
    mod_ref, nmix_ref, nffn_ref = refs[:3]
    refs = refs[3:]
    n_w = 5 if kind == "even" else 3
    weight_refs = refs[:n_w]
    refs = refs[n_w:]
    wrt_ref, br_ref, tri_ref = refs[:3]
    refs = refs[3:]
    hout_ref, tokmeta_ref, tmeta_ref, xs_hbm = refs[:4]
    refs = refs[4:]
    n_c = 2 if kind == "even" else 1
    carry_refs = refs[:n_c]
    refs = refs[n_c:]
    if kind == "even":
        cast_refs = refs[:2]
        refs = refs[2:]
    if lag:
        h1s_ref = refs.pop(0)
    stage_ref, zeros_ref, run_ref, prev_ref, ssem, zsem = refs[:6]
    refs = refs[6:]

    tile = jnp.minimum(t, n_steps - 1)

    @pl.when(t == 0)
    def _():
        if lag:
            h1s_ref[...] = jnp.zeros_like(h1s_ref)
        for r in carry_refs:
            r[...] = jnp.zeros_like(r)
        for g in range(N_GROUPS):
            run_ref[g] = jnp.int32(0)
        if kind == "even":
            for src, dst in zip(weight_refs[:2], cast_refs):
                for c0 in range(0, src.shape[1], 2 * LANES):
                    dst[:, c0:c0 + 2 * LANES] = src[:, c0:c0 + 2 * LANES].astype(BF16)
        _prime_scatter(nb, cap, xs_hbm, stage_ref, prev_ref, ssem)

    j = layer // 2
    g_mix = nmix_ref[layer:layer + 1]
    g_ffn = nffn_ref[layer:layer + 1]
    if kind == "even":
        sink_ref = weight_refs[3]
        weight_refs = list(cast_refs) + [weight_refs[2], lambda i: sink_ref[j, i], weight_refs[4]]
    else:
        weight_refs = list(weight_refs[:2]) + [weight_refs[2][j:j + 1]]

    drain_head = None
    if has_head:
        ystage_ref, gsem = refs
        drain_head = _head_dma(t, n_steps, nb, pm_ref, ys_hbm, ystage_ref, gsem)

    def main_chain(b):
        if has_head:
            x = _head_combine(b, slot, hprev_ref, modprev_ref, tokprev_ref, ystage_ref)
            yield
        else:
            x = h_ref[b]
        carries = [r.at[b] for r in carry_refs]
        mix = _attn_conv_mix if kind == "even" else _pool_mix
        h1 = yield from mix(x, tile, mod_ref[b], g_mix, *weight_refs, *carries)
        hout_ref[b] = h1
        if lag:
            h1s_ref[b, slot] = h1
        yield
        return h1

    def tail_chain(b, h1):
        return (yield from _tail_compute(b, slot, h1, mod_ref[b], g_ffn, wrt_ref, br_ref,
                                         tri_ref, tokmeta_ref, stage_ref))

    def lagged_tail_chain(b):
        cnt = yield from tail_chain(b, h1s_ref[b, 1 - slot])
        return [jnp.where(t >= lag, c, 0) for c in cnt]

    def whole_chain(b):
        h1 = yield from main_chain(b)
        return (yield from tail_chain(b, h1))

    if lag:
        chains = [main_chain(b) for b in range(nb)] + [lagged_tail_chain(b) for b in range(nb)]
        counts = _interleave(chains, [MAIN_START] * nb + [0] * nb)[nb:]
    else:
        counts = _interleave([whole_chain(b) for b in range(nb)], [0] * nb)
    _tail_dma(t, lag, n_steps, nb, cap, n_work, counts, tmeta_ref, xs_hbm, stage_ref, zeros_ref,
              run_ref, prev_ref, ssem, zsem, drain_head)


def _nat(nb, cols):
    return pl.BlockSpec((nb, TB, cols), lambda t, *_: (0, t, 0))


def _full(shape):
    nd = len(shape)
    return pl.BlockSpec(shape, lambda t, *_: (0,) * nd)


def _head_scratch(nb):
    return [pltpu.VMEM((nb, 2, STG_C, D_MODEL), BF16), pltpu.SemaphoreType.DMA((nb, 2))]


def _mod_spec(nb, layer):
    return pl.BlockSpec((None, nb, 8, D_MODEL), lambda t, *_: (layer, 0, 0, 0))


def _mixer_call(kind, layer, head, h, mod_all, norm_mix, norm_ffn, weights, wrt, br, tri):
    j = layer // 2
    nb, seq = (h if head is None else head[2]).shape[:2]
    n_steps = seq // TB
    n_tiles, cap, n_work = _sizes(nb * seq)
    has_head = head is not None
    any_spec = pl.BlockSpec(memory_space=pl.ANY)
    smem_spec = pl.BlockSpec(memory_space=pltpu.SMEM)

    def main(cols):
        return pl.BlockSpec((nb, TB, cols), lambda t, *_: (0, jnp.minimum(t, n_steps - 1), 0))

    lag = 1 if kind == "odd" else 0

    def lagged(cols):
        return pl.BlockSpec((nb, TB, cols), lambda t, *_: (0, jnp.maximum(t - lag, 0), 0))

    if has_head:
        pm, hprev, tokprev, ys = head
        args = [hprev, mod_all, tokprev, ys]
        in_specs = [main(D_MODEL), _mod_spec(nb, layer - 1), main(LANES), any_spec]
    else:
        args = [h]
        in_specs = [main(D_MODEL)]
    args += [mod_all, norm_mix, norm_ffn]
    in_specs += [_mod_spec(nb, layer), _full(norm_mix.shape), _full(norm_ffn.shape)]
    if kind == "even":
        args += list(weights)
        once = dict(pipeline_mode=pl.Buffered(1))
        in_specs += [pl.BlockSpec((None, D_MODEL, IN_WIDTH), lambda t, *_: (j, 0, 0), **once),
                     pl.BlockSpec((None, D_MODEL, D_MODEL), lambda t, *_: (j, 0, 0), **once),
                     _full((N_HEADS // 4, 2 * BLOCK, 4 * BLOCK)), smem_spec,
                     pl.BlockSpec((None,) + weights[4].shape[1:], lambda t, *_: (j, 0, 0))]
        mix_scratch = [pltpu.VMEM((nb, BLOCK, 2 * KV_WIDTH), F32),
                       pltpu.VMEM((nb, 8, CONV_WIDTH), F32),
                       pltpu.VMEM((D_MODEL, IN_WIDTH), BF16),
                       pltpu.VMEM((D_MODEL, D_MODEL), BF16)]
    else:
        args += list(weights)
        in_specs += [pl.BlockSpec((None, len(POOL_SIZES), POOL_GROUP, POOL_GROUP),
                                  lambda t, *_: (j, 0, 0, 0)),
                     _full((len(POOL_SIZES), TB, TB + LANES)), _full(weights[2].shape)]
        mix_scratch = [pltpu.VMEM((nb, POOL_HALO, D_MODEL), F32)]
    args += [wrt, br, tri]
    in_specs += [_full((2 * N_EXPERTS, D_MODEL)), _full((N_EXPERTS, 1)), _full((TB, TB))]

    out_shape = (jax.ShapeDtypeStruct((nb, seq + lag * TB, D_MODEL), F32),
                 jax.ShapeDtypeStruct((nb, seq, LANES), F32),
                 jax.ShapeDtypeStruct(((n_tiles + 1) * META_W + 4 * n_work,), I32),
                 jax.ShapeDtypeStruct((N_GROUPS * cap + nb * N_SUB * SUB, ROW_W), BF16))
    out_specs = (_nat(nb, D_MODEL), lagged(LANES), smem_spec, any_spec)
    scratch = mix_scratch + ([pltpu.VMEM((nb, 2, TB, D_MODEL), F32)] if lag else []) + [
        pltpu.VMEM((nb, 2, STG_P, ROW_W), BF16),
        pltpu.VMEM((TBM, ROW_W), BF16),
        pltpu.SMEM((N_GROUPS,), I32),
        pltpu.SMEM((nb * N_SUB,), I32),
        pltpu.SemaphoreType.DMA((nb, 2)),
        pltpu.SemaphoreType.DMA(()),
    ]
    if has_head:
        scratch += _head_scratch(nb)

    body = functools.partial(_mixer_kernel, kind=kind, layer=layer, lag=lag, has_head=has_head,
                             nb=nb, n_steps=n_steps, cap=cap, n_work=n_work)
    grid_spec = pltpu.PrefetchScalarGridSpec(
        num_scalar_prefetch=1 if has_head else 0, grid=(n_steps + lag,),
        in_specs=in_specs, out_specs=out_specs, scratch_shapes=scratch)
    call = pl.pallas_call(
        body, grid_spec=grid_spec, out_shape=out_shape,
        compiler_params=pltpu.CompilerParams(
            dimension_semantics=("arbitrary",), vmem_limit_bytes=VMEM_LIMIT),
        name=kind + "_mixer")
    if has_head:
        return call(pm, *args)
    return call(*args)


def _work_tables(totals, cap, n_work, tmeta_ref, t0):
    ends, starts = [], []
    acc = jnp.int32(0)
    for g in range(N_GROUPS):
        starts.append(acc)
        acc = acc + (totals[g] + (TBM - 1)) // TBM
        ends.append(acc)
    nvalid = ends[-1]
    prev_grp = jnp.int32(-1)
    for i in range(n_work):
        idc = jnp.minimum(jnp.int32(i), nvalid - 1)
        grp, start, total = jnp.int32(0), starts[0], totals[0]
        for g in range(1, N_GROUPS):
            later = idc >= ends[g - 1]
            grp = jnp.where(later, g, grp)
            start = jnp.where(later, starts[g], start)
            total = jnp.where(later, totals[g], total)
        valid = i < nvalid
        tile = idc - start
        tmeta_ref[t0 + i] = grp * (cap // TBM) + tile
        tmeta_ref[t0 + n_work + i] = grp
        tmeta_ref[t0 + 2 * n_work + i] = jnp.where(valid, jnp.minimum(total - tile * TBM, TBM), 0)
        tmeta_ref[t0 + 3 * n_work + i] = jnp.logical_and(valid, grp != prev_grp).astype(I32)
        prev_grp = grp


def _expert_rows(xs_ref, ys_ref, wg_s, wu_s, wd_s, r0, nrows):
    xb = xs_ref[r0:r0 + nrows, 0:D_MODEL]
    meta = xs_ref[r0:r0 + nrows, D_MODEL:ROW_W].astype(F32)
    gates = meta[:, 0:EXPERTS_PER_GROUP] + meta[:, EXPERTS_PER_GROUP:2 * EXPERTS_PER_GROUP]
    parts = []
    for e in range(EXPERTS_PER_GROUP):
        a = jnp.dot(xb, wg_s[e], preferred_element_type=F32)
        bu = jnp.dot(xb, wu_s[e], preferred_element_type=F32)
        hid = (a * _sigmoid(a)) * bu
        parts.append((hid * gates[:, e:e + 1]).astype(BF16))
    hid_all = jnp.concatenate(parts, axis=1)
    ys_ref[r0:r0 + nrows, :] = jnp.dot(hid_all, wd_s[...],
                                       preferred_element_type=F32).astype(BF16)


def _moe_kernel(tm_ref, xs_ref, wg_ref, wu_ref, wd_ref, ys_ref, wg_s, wu_s, wd_s, *, t0, n_work):
    i = pl.program_id(0)
    rows = tm_ref[t0 + 2 * n_work + i]
    first_of_group = tm_ref[t0 + 3 * n_work + i] == 1

    @pl.when(first_of_group)
    def _():
        for e in range(EXPERTS_PER_GROUP):
            wg_s[e] = wg_ref[e].astype(BF16)
            wu_s[e] = wu_ref[e].astype(BF16)
        wd_s[...] = wd_ref[0].astype(BF16)

    @pl.when(rows == TBM)
    def _():
        _expert_rows(xs_ref, ys_ref, wg_s, wu_s, wd_s, 0, TBM)

    @pl.when(jnp.logical_and(rows > 0, rows < TBM))
    def _():
        for r0 in range(0, TBM, TBM_PART):
            @pl.when(r0 < rows)
            def _():
                _expert_rows(xs_ref, ys_ref, wg_s, wu_s, wd_s, r0, TBM_PART)

            @pl.when(r0 >= rows)
            def _():
                ys_ref[r0:r0 + TBM_PART, :] = jnp.zeros((TBM_PART, D_MODEL), BF16)


def _moe_call(xs, tmeta, layer, wg, wu, wd, n_tiles, n_work):
    t0 = (n_tiles + 1) * META_W
    row_map = lambda i, tm: (tm[t0 + i], 0)
    grp_map = lambda i, tm: (layer, tm[t0 + n_work + i], 0, 0)
    grid_spec = pltpu.PrefetchScalarGridSpec(
        num_scalar_prefetch=1, grid=(n_work,),
        in_specs=[
            pl.BlockSpec((TBM, ROW_W), row_map),
            pl.BlockSpec((None, EXPERTS_PER_GROUP, D_MODEL, EXPERT_FF), grp_map),
            pl.BlockSpec((None, EXPERTS_PER_GROUP, D_MODEL, EXPERT_FF), grp_map),
            pl.BlockSpec((None, 1, GROUP_FF, D_MODEL), grp_map),
        ],
        out_specs=pl.BlockSpec((TBM, D_MODEL), row_map),
        scratch_shapes=[pltpu.VMEM((EXPERTS_PER_GROUP, D_MODEL, EXPERT_FF), BF16),
                        pltpu.VMEM((EXPERTS_PER_GROUP, D_MODEL, EXPERT_FF), BF16),
                        pltpu.VMEM((GROUP_FF, D_MODEL), BF16)],
    )
    return pl.pallas_call(
        functools.partial(_moe_kernel, t0=t0, n_work=n_work), grid_spec=grid_spec,
        out_shape=jax.ShapeDtypeStruct((xs.shape[0], D_MODEL), BF16),
        compiler_params=pltpu.CompilerParams(
            dimension_semantics=("arbitrary",), vmem_limit_bytes=VMEM_LIMIT),
        name="moe_experts",
    )(tmeta, xs, wg, wu, wd)


def _final_kernel(pm_ref, hprev_ref, modprev_ref, tokprev_ref, ys_hbm, nf_ref, o_ref,
                  ystage_ref, gsem, *, nb, n_steps):
    t = pl.program_id(0)
    drain = _head_dma(t, n_steps, nb, pm_ref, ys_hbm, ystage_ref, gsem)
    for b in range(nb):
        h = _head_combine(b, t % 2, hprev_ref, modprev_ref, tokprev_ref, ystage_ref)
        ms = jnp.mean(h * h, axis=-1, keepdims=True)
        o_ref[b] = (h * lax.rsqrt(ms + EPS)) * nf_ref[0:1]
    pl.when(t == n_steps - 1)(drain)


def _final_call(head, mod_all, norm_final):
    pm, hprev, tokprev, ys = head
    nb, seq = tokprev.shape[:2]
    n_steps = seq // TB
    grid_spec = pltpu.PrefetchScalarGridSpec(
        num_scalar_prefetch=1, grid=(n_steps,),
        in_specs=[_nat(nb, D_MODEL), _mod_spec(nb, DEPTH - 1), _nat(nb, LANES),
                  pl.BlockSpec(memory_space=pl.ANY), _full((1, D_MODEL))],
        out_specs=_nat(nb, D_MODEL),
        scratch_shapes=_head_scratch(nb))
    return pl.pallas_call(
        functools.partial(_final_kernel, nb=nb, n_steps=n_steps), grid_spec=grid_spec,
        out_shape=jax.ShapeDtypeStruct((nb, seq, D_MODEL), F32),
        compiler_params=pltpu.CompilerParams(
            dimension_semantics=("arbitrary",), vmem_limit_bytes=VMEM_LIMIT),
        name="final_norm",
    )(pm, hprev, mod_all, tokprev, ys, norm_final.reshape(1, D_MODEL))


def _pool_band_table():
    r = jnp.arange(TB)[:, None]
    c = jnp.arange(TB + LANES)[None, :]
    return jnp.stack([(c > r + LANES - w) & (c <= r + LANES) for w in POOL_SIZES]).astype(BF16)


def _attn_bias_table():
    slopes = 2.0 ** (-8.0 * (jnp.arange(N_HEADS, dtype=F32) + 1.0) / N_HEADS)
    dist = (jnp.arange(BLOCK)[:, None] + BLOCK) - jnp.arange(2 * BLOCK)[None, :]
    ok = (dist >= 0) & (dist < WINDOW)
    per_head = jnp.where(ok[None], -slopes[:, None, None] * dist.astype(F32)[None], NEG_BIG)
    return per_head.reshape(N_HEADS // 2, 2, BLOCK, 2 * BLOCK).transpose(0, 2, 1, 3).reshape(
        N_HEADS // 4, 2 * BLOCK, 4 * BLOCK)


def _router_operands(w_router, b_router):
    def reorder(a):
        return a.reshape(N_GROUPS, EXPERTS_PER_GROUP, -1).transpose(1, 0, 2).reshape(N_EXPERTS, -1)
    w = reorder(w_router.T)
    wh = w.astype(BF16)
    wl = (w - wh.astype(F32)).astype(BF16)
    return jnp.concatenate([wh, wl], axis=0), reorder(b_router[:, None])


def kernel(x, c, w_ada, b_ada, norm_mix, norm_ffn, w_in, w_out, sinks, conv_w, w_pool,
           pool_scale, w_router, b_router, w_gate, w_up, w_down, norm_final):
    b, s, _ = x.shape
    n_tiles, cap, n_work = _sizes(b * s)

    mod_all = _mod_call(c, w_ada, b_ada)
    bias_tab = _attn_bias_table()
    band_tab = _pool_band_table()
    wrt, br = _router_operands(w_router, b_router)
    tri = jnp.triu(jnp.ones((TB, TB), F32)).astype(BF16)
    w_down4 = w_down.reshape(DEPTH, N_GROUPS, GROUP_FF, D_MODEL)

    h = x
    head = None
    for layer in range(DEPTH):
        if layer % 2 == 0:
            kind = "even"
            weights = (w_in, w_out, bias_tab, sinks, conv_w)
        else:
            kind = "odd"
            weights = (w_pool, band_tab, pool_scale)
        h1, tokmeta, tmeta, xs = _mixer_call(kind, layer, head, h, mod_all, norm_mix, norm_ffn,
                                             weights, wrt, br, tri)
        ys = _moe_call(xs, tmeta, layer, w_gate, w_up, w_down4, n_tiles, n_work)
        head = (tmeta, h1, tokmeta, ys)
    return _final_call(head, mod_all, norm_final)
```

```python
import functools

import jax
import jax.numpy as jnp
from jax import lax
from jax.experimental import pallas as pl
from jax.experimental.pallas import tpu as pltpu

F32 = jnp.float32
BF16 = jnp.bfloat16
I32 = jnp.int32

D_MODEL = 1024
DEPTH = 4
EPS = 1e-6
N_MOD = 6

ATTN_WIDTH = 512
HEAD_DIM = 64
N_HEADS = 8
KV_WIDTH = 128
WINDOW = 128
BLOCK = 128
CONV_WIDTH = 512
IN_WIDTH = 2304

POOL_SIZES = (2, 4, 8, 16)
POOL_GROUP = 256
POOL_HALO = 16

N_EXPERTS = 16
N_GROUPS = 4
EXPERTS_PER_GROUP = 4
EXPERT_FF = 256
GROUP_FF = EXPERTS_PER_GROUP * EXPERT_FF

LANES = 128
NEG_BIG = -1e30

TB = 256
SUB = 16
N_SUB = TB // SUB + N_GROUPS - 1
STG_P = 320
STG_C = 384
TBM = 1024
TBM_PART = 256
ROW_W = D_MODEL + LANES
META_W = 32
LP_LANE = 8
MAIN_START = 2
MOD_COLS = 3072
VMEM_LIMIT = 56 * 1024 * 1024
NT_DIMS = (((1,), (1,)), ((), ()))

assert N_SUB * SUB <= STG_P <= STG_C and STG_C % LANES == 0 and N_SUB <= META_W


def _sizes(n_tok):
    n_tiles = n_tok // TB
    cap = -(-(n_tok + n_tiles * (SUB - 1) + TBM) // TBM) * TBM
    n_work = (n_tok + n_tiles * N_GROUPS * (SUB - 1)) // TBM + N_GROUPS
    return n_tiles, cap, n_work


def _sigmoid(x):
    return 1.0 / (1.0 + jnp.exp(-x))


def _rms_mod(x, g, sc, sh):
    ms = jnp.mean(x * x, axis=-1, keepdims=True)
    return (x * lax.rsqrt(ms + EPS)) * (g * (1.0 + sc)) + sh


def _onehot(cond):
    return jnp.where(cond, 1.0, 0.0).astype(BF16)


def _mod_kernel(ct_ref, w_ref, b_ref, o_ref, *, nb):
    j = pl.program_id(1)
    ct = ct_ref[...]
    cond = ct * _sigmoid(ct)
    vec_per_step = MOD_COLS // D_MODEL
    for b in range(nb):
        col = cond[:, b:b + 1]
        acc = jnp.zeros((8, MOD_COLS), F32)
        for r in range(D_MODEL // 8):
            acc = acc + w_ref[0, 8 * r:8 * r + 8, :] * col[8 * r:8 * r + 8]
        res = jnp.sum(acc, axis=0, keepdims=True) + b_ref[0]
        for jj in range(N_MOD // vec_per_step):
            @pl.when(j == jj)
            def _():
                for k in range(vec_per_step):
                    row = jj * vec_per_step + k
                    o_ref[0, b, row:row + 1, :] = res[:, k * D_MODEL:(k + 1) * D_MODEL]

        @pl.when(j == 0)
        def _():
            o_ref[0, b, N_MOD:8, :] = jnp.zeros((8 - N_MOD, D_MODEL), F32)


def _mod_call(c, w_ada, b_ada):
    nb = c.shape[0]
    ct = jnp.pad(c.T, ((0, 0), (0, LANES - nb)))
    n_col = (N_MOD * D_MODEL) // MOD_COLS
    return pl.pallas_call(
        functools.partial(_mod_kernel, nb=nb),
        grid=(DEPTH, n_col),
        in_specs=[
            pl.BlockSpec((D_MODEL, LANES), lambda l, j: (0, 0)),
            pl.BlockSpec((1, D_MODEL, MOD_COLS), lambda l, j: (l, 0, j)),
            pl.BlockSpec((1, 1, MOD_COLS), lambda l, j: (l, 0, j)),
        ],
        out_specs=pl.BlockSpec((1, nb, 8, D_MODEL), lambda l, j: (l, 0, 0, 0)),
        out_shape=jax.ShapeDtypeStruct((DEPTH, nb, 8, D_MODEL), F32),
        compiler_params=pltpu.CompilerParams(
            dimension_semantics=("arbitrary", "arbitrary"), vmem_limit_bytes=VMEM_LIMIT),
        name="adaln_mod",
    )(ct, w_ada, b_ada.reshape(DEPTH, 1, N_MOD * D_MODEL))


def _sub_copies(hbm_ref, stage_ref, sem, rows, to_hbm):
    out = []
    for s in range(N_SUB):
        s_view = stage_ref.at[pl.ds(s * SUB, SUB), :]
        h_view = hbm_ref.at[pl.ds(pl.multiple_of(rows[s], SUB), SUB), :]
        out.append(pltpu.make_async_copy(s_view, h_view, sem) if to_hbm
                   else pltpu.make_async_copy(h_view, s_view, sem))
    return out


def _head_dma(t, n_steps, nb, pm_ref, ys_hbm, ystage_ref, gsem):
    slot = t % 2

    def copies(step, b, slot_):
        base = (jnp.minimum(step, n_steps - 1) * nb + b) * META_W
        rows = [pm_ref[base + s] for s in range(N_SUB)]
        return _sub_copies(ys_hbm, ystage_ref.at[b, slot_], gsem.at[b, slot_], rows, False)

    @pl.when(t == 0)
    def _():
        ystage_ref[...] = jnp.zeros_like(ystage_ref)
        for b in range(nb):
            for cp in copies(0, b, 0):
                cp.start()

    nxt = t + 1
    for b in range(nb):
        for cp in copies(nxt, b, 1 - slot):
            cp.start()

    for b in range(nb):
        for cp in copies(t, b, slot):
            cp.wait()

    def drain():
        for b in range(nb):
            for cp in copies(nxt, b, 1 - slot):
                cp.wait()
    return drain


def _head_combine(b, slot, hprev_ref, modprev_ref, tokmeta_ref, ystage_ref):
    lp = tokmeta_ref[b, :, LP_LANE:LP_LANE + 1].astype(I32)
    unsort = _onehot(lax.broadcasted_iota(I32, (TB, STG_C), 1) == lp)
    y = jnp.dot(unsort, ystage_ref[b, slot], preferred_element_type=F32)
    return hprev_ref[b] + modprev_ref[b, 5:6] * y


def _route(u2, wrt_ref, br_ref):
    uh = u2.astype(BF16)
    ul = (u2 - uh.astype(F32)).astype(BF16)
    w = wrt_ref[...]
    r1 = lax.dot_general(w, uh, NT_DIMS, preferred_element_type=F32)
    r2 = lax.dot_general(w[0:N_EXPERTS], ul, NT_DIMS, preferred_element_type=F32)
    yield
    scores = _sigmoid(r1[0:N_EXPERTS] + r1[N_EXPERTS:] + r2)
    biased = scores + br_ref[...]
    s = [scores[4 * j:4 * j + 4] for j in range(EXPERTS_PER_GROUP)]
    b = [biased[4 * j:4 * j + 4] for j in range(EXPERTS_PER_GROUP)]

    hi1, lo1 = jnp.maximum(b[0], b[1]), jnp.minimum(b[0], b[1])
    hi2, lo2 = jnp.maximum(b[2], b[3]), jnp.minimum(b[2], b[3])
    second = jnp.maximum(jnp.minimum(hi1, hi2), jnp.where(hi1 >= hi2, lo1, lo2))
    gscore = jnp.maximum(hi1, hi2) + second

    best = gscore[0:1]
    bgrp = jnp.zeros((1, TB), I32)
    for g in range(1, N_GROUPS):
        better = gscore[g:g + 1] > best
        bgrp = jnp.where(better, g, bgrp)
        best = jnp.where(better, gscore[g:g + 1], best)
    selmask = lax.broadcasted_iota(I32, (N_GROUPS, TB), 0) == bgrp
    yield

    m1 = b[0]
    i1 = jnp.zeros((N_GROUPS, TB), I32)
    for j in range(1, EXPERTS_PER_GROUP):
        gt = b[j] > m1
        i1 = jnp.where(gt, j, i1)
        m1 = jnp.where(gt, b[j], m1)
    cands = [jnp.where(i1 == j, -jnp.inf, b[j]) for j in range(EXPERTS_PER_GROUP)]
    m2 = cands[0]
    i2 = jnp.zeros_like(i1)
    for j in range(1, EXPERTS_PER_GROUP):
        gt = cands[j] > m2
        i2 = jnp.where(gt, j, i2)
        m2 = jnp.where(gt, cands[j], m2)
    w1 = s[0]
    w2 = s[0]
    for j in range(1, EXPERTS_PER_GROUP):
        w1 = jnp.where(i1 == j, s[j], w1)
        w2 = jnp.where(i2 == j, s[j], w2)
    yield
    tot = w1 + w2
    w1n = w1 / tot
    w2n = w2 / tot
    gates = []
    for j in range(EXPERTS_PER_GROUP):
        gj = jnp.where(i1 == j, w1n, 0.0) + jnp.where(i2 == j, w2n, 0.0)
        gates.append(jnp.sum(jnp.where(selmask, gj, 0.0), axis=0, keepdims=True))
    return bgrp, gates, selmask


def _tail_compute(b, slot, h1, mod, g_ffn, wrt_ref, br_ref, tri_ref, tokmeta_ref, stage_ref):
    sh2, sc2 = mod[3:4], mod[4:5]
    u2 = _rms_mod(h1, g_ffn, sc2, sh2)
    bgrp, gates, selmask = yield from _route(u2, wrt_ref, br_ref)
    yield

    onehot = jnp.concatenate([jnp.where(selmask, 1.0, 0.0), jnp.zeros((8 - N_GROUPS, TB), F32)],
                             axis=0).astype(BF16)
    rank_incl = jnp.dot(onehot, tri_ref[...], preferred_element_type=F32)
    rank = jnp.sum(jnp.where(selmask, rank_incl[0:N_GROUPS], 0.0), axis=0, keepdims=True) - 1.0
    cnt = lax.dot_general(jnp.ones((8, TB), BF16), onehot, NT_DIMS,
                          preferred_element_type=F32)
    sub_rows = jnp.floor((cnt[0:1, :] + (SUB - 1.0)) * (1.0 / SUB)) * SUB
    lp = rank
    first_row = jnp.zeros((1, 1), F32)
    for g in range(1, N_GROUPS):
        first_row = first_row + sub_rows[:, g - 1:g]
        lp = lp + jnp.where(bgrp == g, first_row, 0.0)

    hi = [x.astype(BF16).astype(F32) for x in gates]
    lo = [(x - h).astype(BF16).astype(F32) for x, h in zip(gates, hi)]
    meta_src = jnp.concatenate(hi + lo + [lp, jnp.zeros((LANES - 9, TB), F32)], axis=0)
    meta_t = meta_src.T
    tokmeta_ref[b] = meta_t
    yield

    rowdata = jnp.concatenate([u2.astype(BF16), meta_t.astype(BF16)], axis=1)
    sort = _onehot(lax.broadcasted_iota(I32, (STG_P, TB), 0) == lp.astype(I32))
    stage_ref[b, slot] = jnp.dot(sort, rowdata, preferred_element_type=F32).astype(BF16)
    return [cnt[0, g].astype(I32) for g in range(N_GROUPS)]


def _prime_scatter(nb, cap, xs_hbm, stage_ref, prev_ref, ssem):
    stage_ref[:, 1] = jnp.zeros((nb,) + stage_ref.shape[2:], BF16)
    for b in range(nb):
        rows = [jnp.int32(N_GROUPS * cap + (b * N_SUB + s) * SUB) for s in range(N_SUB)]
        for s in range(N_SUB):
            prev_ref[b * N_SUB + s] = rows[s]
        for cp in _sub_copies(xs_hbm, stage_ref.at[b, 1], ssem.at[b, 1], rows, True):
            cp.start()


def _tail_dma(t, lag, n_steps, nb, cap, n_work, counts, tmeta_ref, xs_hbm, stage_ref, zeros_ref,
              run_ref, prev_ref, ssem, zsem, drain_head):
    slot = t % 2
    run = [run_ref[g] for g in range(N_GROUPS)]
    junk = N_GROUPS * cap
    scatter_rows = []
    for b in range(nb):
        first_sub = [jnp.int32(0)]
        for g in range(N_GROUPS):
            first_sub.append(first_sub[-1] + (counts[b][g] + (SUB - 1)) // SUB)
        dst = [g * cap + run[g] for g in range(N_GROUPS)]
        rows_b, base = [], (jnp.maximum(t - lag, 0) * nb + b) * META_W
        for s in range(N_SUB):
            grp_first, grp_dst = first_sub[0], dst[0]
            for g in range(1, N_GROUPS):
                later = s >= first_sub[g]
                grp_first = jnp.where(later, first_sub[g], grp_first)
                grp_dst = jnp.where(later, dst[g], grp_dst)
            row = grp_dst + (s - grp_first) * SUB
            used = s < first_sub[N_GROUPS]
            rows_b.append(jnp.where(used, row, junk + (b * N_SUB + s) * SUB))
            tmeta_ref[base + s] = jnp.where(used, row, row if s == 0 else first_row)
            if s == 0:
                first_row = row
        for k in range(N_SUB, META_W):
            tmeta_ref[base + k] = jnp.int32(0)
        for g in range(N_GROUPS):
            run[g] = run[g] + (first_sub[g + 1] - first_sub[g]) * SUB
        scatter_rows.append(rows_b)
    for g in range(N_GROUPS):
        run_ref[g] = run[g]

    def copies(b, slot_, rows):
        return _sub_copies(xs_hbm, stage_ref.at[b, slot_], ssem.at[b, slot_], rows, True)

    for b in range(nb):
        for cp in copies(b, 1 - slot, [prev_ref[b * N_SUB + s] for s in range(N_SUB)]):
            cp.wait()

    for b in range(nb):
        for cp in copies(b, slot, scatter_rows[b]):
            cp.start()
        for s in range(N_SUB):
            prev_ref[b * N_SUB + s] = scatter_rows[b][s]

    @pl.when(t == n_steps - 1 + lag)
    def _():
        for b in range(nb):
            for cp in copies(b, slot, scatter_rows[b]):
                cp.wait()
        if drain_head is not None:
            drain_head()
        zeros_ref[...] = jnp.zeros_like(zeros_ref)
        tails = []
        base = n_steps * nb * META_W
        for g in range(N_GROUPS):
            tmeta_ref[base + g] = run[g]
            start = pl.multiple_of(g * cap + run[g], SUB)
            tails.append(pltpu.make_async_copy(zeros_ref, xs_hbm.at[pl.ds(start, TBM), :], zsem))
        for k in range(N_GROUPS, META_W):
            tmeta_ref[base + k] = jnp.int32(0)
        for cp in tails:
            cp.start()
        _work_tables(run, cap, n_work, tmeta_ref, base + META_W)
        for cp in tails:
            cp.wait()


def _attn_conv_mix(x, t, mod, g_mix, win_ref, wout_ref, bias_ref, sink_at, convw_ref,
                   kvprev_ref, cprev_ref):
    tb = TB
    first = t == 0
    sh1, sc1, g1 = mod[0:1], mod[1:2], mod[2:3]
    u = _rms_mod(x, g_mix, sc1, sh1)
    proj = jnp.dot(u.astype(BF16), win_ref[...], preferred_element_type=F32)

    yield
    q = (proj[:, 0:ATTN_WIDTH] * HEAD_DIM ** -0.5).astype(BF16)
    kf = proj[:, 512:640]
    vf = proj[:, 640:768]
    bgate = proj[:, 768:1280]
    cgate = proj[:, 1280:1792]
    xv = proj[:, 1792:2304]

    kv_prev = kvprev_ref[...]
    kext = jnp.concatenate([kv_prev[:, 0:KV_WIDTH], kf], axis=0)
    vext = jnp.concatenate([kv_prev[:, KV_WIDTH:], vf], axis=0)
    kvprev_ref[:, 0:KV_WIDTH] = kf[tb - BLOCK:tb]
    kvprev_ref[:, KV_WIDTH:] = vf[tb - BLOCK:tb]

    lane = lax.broadcasted_iota(I32, kext.shape, 1)
    lo = lane < HEAD_DIM
    krol = pltpu.roll(kext, HEAD_DIM, axis=1)
    vrol = pltpu.roll(vext, HEAD_DIM, axis=1)
    zero = jnp.zeros_like(kext)
    one_at_64 = jnp.where(lane == HEAD_DIM, 1.0, 0.0)
    one_at_0 = jnp.where(lane == 0, 1.0, 0.0)
    k_ops = [(jnp.where(lo, kext, zero).astype(BF16), jnp.where(lo, zero, krol).astype(BF16)),
             (jnp.where(lo, krol, zero).astype(BF16), jnp.where(lo, zero, kext).astype(BF16))]
    v_ops = [(jnp.where(lo, vext, one_at_64).astype(BF16), jnp.where(lo, one_at_0, vrol).astype(BF16)),
             (jnp.where(lo, vrol, one_at_64).astype(BF16), jnp.where(lo, one_at_0, vext).astype(BF16))]

    col = lax.broadcasted_iota(I32, (2 * BLOCK, 4 * BLOCK), 1)
    prev_cols = (col % (2 * BLOCK)) < BLOCK
    upper = lax.broadcasted_iota(I32, (2 * BLOCK, 1), 0) < BLOCK
    lane_o = lax.broadcasted_iota(I32, (2 * BLOCK, LANES), 1)
    lo_o = lane_o < HEAD_DIM

    yield
    units = [(bi, kvh) for bi in range(tb // BLOCK) for kvh in range(N_HEADS // 4)]

    def scores(bi, kvh):
        r0 = bi * BLOCK
        ka, kb = k_ops[kvh]
        rhs = jnp.concatenate([ka[r0:r0 + 2 * BLOCK], kb[r0:r0 + 2 * BLOCK]], axis=0)
        qrows = q[r0:r0 + BLOCK]
        qp = jnp.concatenate([qrows[:, (2 * kvh) * LANES:(2 * kvh + 1) * LANES],
                              qrows[:, (2 * kvh + 1) * LANES:(2 * kvh + 2) * LANES]], axis=0)
        s = lax.dot_general(qp, rhs, NT_DIMS, preferred_element_type=F32)
        bias = bias_ref[kvh]
        if bi == 0:
            bias = jnp.where(prev_cols & first, NEG_BIG, bias)
        return s + bias

    attn_rows = []
    s_next = scores(*units[0])
    for ui, (bi, kvh) in enumerate(units):
        r0 = bi * BLOCK
        if kvh == 0:
            pair_out = []
        s = s_next
        if ui + 1 < len(units):
            s_next = scores(*units[ui + 1])
        outs = []
        for hh in range(2):
            sh = s[:, hh * 2 * BLOCK:(hh + 1) * 2 * BLOCK]
            sink = jnp.where(upper, sink_at(4 * kvh + hh), sink_at(4 * kvh + 2 + hh))
            m = jnp.maximum(jnp.max(sh, axis=-1, keepdims=True), sink)
            p = jnp.exp(sh - m).astype(BF16)
            vop = v_ops[kvh][hh][r0:r0 + 2 * BLOCK]
            o = jnp.dot(p, vop, preferred_element_type=F32)
            den_col = HEAD_DIM if hh == 0 else 0
            den = o[:, den_col:den_col + 1] + jnp.exp(sink - m)
            outs.append(o / den)
        both = jnp.where(lo_o, outs[0], outs[1])
        pair_out += [both[0:BLOCK], both[BLOCK:]]
        if kvh == N_HEADS // 4 - 1:
            attn_rows.append(jnp.concatenate(pair_out, axis=1))
        yield
    attn = jnp.concatenate(attn_rows, axis=0)

    uc = cgate * xv
    cprev = jnp.where(first, 0.0, cprev_ref[...])
    row = lax.broadcasted_iota(I32, uc.shape, 0)
    r1 = jnp.where(row == 0, cprev[7:8], pltpu.roll(uc, 1, axis=0))
    r2 = jnp.where(row == 0, cprev[6:7], jnp.where(row == 1, cprev[7:8], pltpu.roll(uc, 2, axis=0)))
    cprev_ref[...] = uc[tb - 8:tb]
    cw = convw_ref[...]
    conv = bgate * (cw[0:1] * r2 + cw[1:2] * r1 + cw[2:3] * uc)
    yield

    mix = (jnp.dot(attn.astype(BF16), wout_ref[0:ATTN_WIDTH, :], preferred_element_type=F32)
           + jnp.dot(conv.astype(BF16), wout_ref[ATTN_WIDTH:, :], preferred_element_type=F32))
    return x + g1 * mix


def _pool_mix(x, t, mod, g_mix, wpool_ref, band_ref, pscale, uprev_ref):
    tb = TB
    sh1, sc1, g1 = mod[0:1], mod[1:2], mod[2:3]
    u = _rms_mod(x, g_mix, sc1, sh1)
    halo = jnp.where(t == 0, 0.0, uprev_ref[...]).astype(BF16)
    ext = jnp.concatenate([jnp.zeros((LANES - POOL_HALO, D_MODEL), BF16), halo, u.astype(BF16)],
                          axis=0)
    uprev_ref[...] = u[tb - POOL_HALO:tb]

    yield
    pos = (t * tb + 1 + lax.broadcasted_iota(I32, (tb, 1), 0)).astype(F32)
    cols = [slice(gi * POOL_GROUP, (gi + 1) * POOL_GROUP) for gi in range(len(POOL_SIZES))]
    wsums = [jnp.dot(band_ref[gi], ext[:, sl], preferred_element_type=F32)
             for gi, sl in enumerate(cols)]
    yield
    mixed = []
    for gi, w in enumerate(POOL_SIZES):
        mean = wsums[gi] / jnp.minimum(pos, float(w))
        pooled = mean - u[:, cols[gi]]
        mixed.append(jnp.dot(pooled.astype(BF16), wpool_ref[gi].astype(BF16),
                             preferred_element_type=F32))
    yield
    mix = jnp.concatenate(mixed, axis=1) * pscale
    return x + g1 * mix


def _interleave(chains, starts):
    results = [None] * len(chains)
    live = list(range(len(chains)))
    rnd = 0
    while live:
        for k in list(live):
            if rnd < starts[k]:
                continue
            try:
                next(chains[k])
            except StopIteration as done:
                results[k] = done.value
                live.remove(k)
        rnd += 1
    return results


def _mixer_kernel(*refs, kind, layer, lag, has_head, nb, n_steps, cap, n_work):
    refs = list(refs)
    t = pl.program_id(0)
    slot = t % 2
    if has_head:
        pm_ref = refs.pop(0)
        hprev_ref, modprev_ref, tokprev_ref, ys_hbm = refs[:4]
        refs = refs[4:]
    else:
        h_ref = refs.pop(0)
    mod_ref, nmix_ref, nffn_ref = refs[:3]
    refs = refs[3:]
    n_w = 5 if kind == "even" else 3
    weight_refs = refs[:n_w]
    refs = refs[n_w:]
    wrt_ref, br_ref, tri_ref = refs[:3]
    refs = refs[3:]
    hout_ref, tokmeta_ref, tmeta_ref, xs_hbm = refs[:4]
    refs = refs[4:]
    n_c = 2 if kind == "even" else 1
    carry_refs = refs[:n_c]
    refs = refs[n_c:]
    if kind == "even":
        cast_refs = refs[:2]
        refs = refs[2:]
    if lag:
        h1s_ref = refs.pop(0)
    stage_ref, zeros_ref, run_ref, prev_ref, ssem, zsem = refs[:6]
    refs = refs[6:]

    tile = jnp.minimum(t, n_steps - 1)

    @pl.when(t == 0)
    def _():
        if lag:
            h1s_ref[...] = jnp.zeros_like(h1s_ref)
        for r in carry_refs:
            r[...] = jnp.zeros_like(r)
        for g in range(N_GROUPS):
            run_ref[g] = jnp.int32(0)
        if kind == "even":
            for src, dst in zip(weight_refs[:2], cast_refs):
                for c0 in range(0, src.shape[1], 2 * LANES):
                    dst[:, c0:c0 + 2 * LANES] = src[:, c0:c0 + 2 * LANES].astype(BF16)
        _prime_scatter(nb, cap, xs_hbm, stage_ref, prev_ref, ssem)

    j = layer // 2
    g_mix = nmix_ref[layer:layer + 1]
    g_ffn = nffn_ref[layer:layer + 1]
    if kind == "even":
        sink_ref = weight_refs[3]
        weight_refs = list(cast_refs) + [weight_refs[2], lambda i: sink_ref[j, i], weight_refs[4]]
    else:
        weight_refs = list(weight_refs[:2]) + [weight_refs[2][j:j + 1]]

    drain_head = None
    if has_head:
        ystage_ref, gsem = refs
        drain_head = _head_dma(t, n_steps, nb, pm_ref, ys_hbm, ystage_ref, gsem)

    def main_chain(b):
        if has_head:
            x = _head_combine(b, slot, hprev_ref, modprev_ref, tokprev_ref, ystage_ref)
            yield
        else:
            x = h_ref[b]
        carries = [r.at[b] for r in carry_refs]
        mix = _attn_conv_mix if kind == "even" else _pool_mix
        h1 = yield from mix(x, tile, mod_ref[b], g_mix, *weight_refs, *carries)
        hout_ref[b] = h1
        if lag:
            h1s_ref[b, slot] = h1
        yield
        return h1

    def tail_chain(b, h1):
        return (yield from _tail_compute(b, slot, h1, mod_ref[b], g_ffn, wrt_ref, br_ref,
                                         tri_ref, tokmeta_ref, stage_ref))

    def lagged_tail_chain(b):
        cnt = yield from tail_chain(b, h1s_ref[b, 1 - slot])
        return [jnp.where(t >= lag, c, 0) for c in cnt]

    def whole_chain(b):
        h1 = yield from main_chain(b)
        return (yield from tail_chain(b, h1))

    if lag:
        chains = [main_chain(b) for b in range(nb)] + [lagged_tail_chain(b) for b in range(nb)]
        counts = _interleave(chains, [MAIN_START] * nb + [0] * nb)[nb:]
    else:
        counts = _interleave([whole_chain(b) for b in range(nb)], [0] * nb)
    _tail_dma(t, lag, n_steps, nb, cap, n_work, counts, tmeta_ref, xs_hbm, stage_ref, zeros_ref,
              run_ref, prev_ref, ssem, zsem, drain_head)


def _nat(nb, cols):
    return pl.BlockSpec((nb, TB, cols), lambda t, *_: (0, t, 0))


def _full(shape):
    nd = len(shape)
    return pl.BlockSpec(shape, lambda t, *_: (0,) * nd)


def _head_scratch(nb):
    return [pltpu.VMEM((nb, 2, STG_C, D_MODEL), BF16), pltpu.SemaphoreType.DMA((nb, 2))]


def _mod_spec(nb, layer):
    return pl.BlockSpec((None, nb, 8, D_MODEL), lambda t, *_: (layer, 0, 0, 0))


def _mixer_call(kind, layer, head, h, mod_all, norm_mix, norm_ffn, weights, wrt, br, tri):
    j = layer // 2
    nb, seq = (h if head is None else head[2]).shape[:2]
    n_steps = seq // TB
    n_tiles, cap, n_work = _sizes(nb * seq)
    has_head = head is not None
    any_spec = pl.BlockSpec(memory_space=pl.ANY)
    smem_spec = pl.BlockSpec(memory_space=pltpu.SMEM)

    def main(cols):
        return pl.BlockSpec((nb, TB, cols), lambda t, *_: (0, jnp.minimum(t, n_steps - 1), 0))

    lag = 1 if kind == "odd" else 0

    def lagged(cols):
        return pl.BlockSpec((nb, TB, cols), lambda t, *_: (0, jnp.maximum(t - lag, 0), 0))

    if has_head:
        pm, hprev, tokprev, ys = head
        args = [hprev, mod_all, tokprev, ys]
        in_specs = [main(D_MODEL), _mod_spec(nb, layer - 1), main(LANES), any_spec]
    else:
        args = [h]
        in_specs = [main(D_MODEL)]
    args += [mod_all, norm_mix, norm_ffn]
    in_specs += [_mod_spec(nb, layer), _full(norm_mix.shape), _full(norm_ffn.shape)]
    if kind == "even":
        args += list(weights)
        once = dict(pipeline_mode=pl.Buffered(1))
        in_specs += [pl.BlockSpec((None, D_MODEL, IN_WIDTH), lambda t, *_: (j, 0, 0), **once),
                     pl.BlockSpec((None, D_MODEL, D_MODEL), lambda t, *_: (j, 0, 0), **once),
                     _full((N_HEADS // 4, 2 * BLOCK, 4 * BLOCK)), smem_spec,
                     pl.BlockSpec((None,) + weights[4].shape[1:], lambda t, *_: (j, 0, 0))]
        mix_scratch = [pltpu.VMEM((nb, BLOCK, 2 * KV_WIDTH), F32),
                       pltpu.VMEM((nb, 8, CONV_WIDTH), F32),
                       pltpu.VMEM((D_MODEL, IN_WIDTH), BF16),
                       pltpu.VMEM((D_MODEL, D_MODEL), BF16)]
    else:
        args += list(weights)
        in_specs += [pl.BlockSpec((None, len(POOL_SIZES), POOL_GROUP, POOL_GROUP),
                                  lambda t, *_: (j, 0, 0, 0)),
                     _full((len(POOL_SIZES), TB, TB + LANES)), _full(weights[2].shape)]
        mix_scratch = [pltpu.VMEM((nb, POOL_HALO, D_MODEL), F32)]
    args += [wrt, br, tri]
    in_specs += [_full((2 * N_EXPERTS, D_MODEL)), _full((N_EXPERTS, 1)), _full((TB, TB))]

    out_shape = (jax.ShapeDtypeStruct((nb, seq + lag * TB, D_MODEL), F32),
                 jax.ShapeDtypeStruct((nb, seq, LANES), F32),
                 jax.ShapeDtypeStruct(((n_tiles + 1) * META_W + 4 * n_work,), I32),
                 jax.ShapeDtypeStruct((N_GROUPS * cap + nb * N_SUB * SUB, ROW_W), BF16))
    out_specs = (_nat(nb, D_MODEL), lagged(LANES), smem_spec, any_spec)
    scratch = mix_scratch + ([pltpu.VMEM((nb, 2, TB, D_MODEL), F32)] if lag else []) + [
        pltpu.VMEM((nb, 2, STG_P, ROW_W), BF16),
        pltpu.VMEM((TBM, ROW_W), BF16),
        pltpu.SMEM((N_GROUPS,), I32),
        pltpu.SMEM((nb * N_SUB,), I32),
        pltpu.SemaphoreType.DMA((nb, 2)),
        pltpu.SemaphoreType.DMA(()),
    ]
    if has_head:
        scratch += _head_scratch(nb)

    body = functools.partial(_mixer_kernel, kind=kind, layer=layer, lag=lag, has_head=has_head,
                             nb=nb, n_steps=n_steps, cap=cap, n_work=n_work)
    grid_spec = pltpu.PrefetchScalarGridSpec(
        num_scalar_prefetch=1 if has_head else 0, grid=(n_steps + lag,),
        in_specs=in_specs, out_specs=out_specs, scratch_shapes=scratch)
    call = pl.pallas_call(
        body, grid_spec=grid_spec, out_shape=out_shape,
        compiler_params=pltpu.CompilerParams(
            dimension_semantics=("arbitrary",), vmem_limit_bytes=VMEM_LIMIT),
        name=kind + "_mixer")
    if has_head:
        return call(pm, *args)
    return call(*args)


def _work_tables(totals, cap, n_work, tmeta_ref, t0):
    ends, starts = [], []
    acc = jnp.int32(0)
    for g in range(N_GROUPS):
        starts.append(acc)
        acc = acc + (totals[g] + (TBM - 1)) // TBM
        ends.append(acc)
    nvalid = ends[-1]
    prev_grp = jnp.int32(-1)
    for i in range(n_work):
        idc = jnp.minimum(jnp.int32(i), nvalid - 1)
        grp, start, total = jnp.int32(0), starts[0], totals[0]
        for g in range(1, N_GROUPS):
            later = idc >= ends[g - 1]
            grp = jnp.where(later, g, grp)
            start = jnp.where(later, starts[g], start)
            total = jnp.where(later, totals[g], total)
        valid = i < nvalid
        tile = idc - start
        tmeta_ref[t0 + i] = grp * (cap // TBM) + tile
        tmeta_ref[t0 + n_work + i] = grp
        tmeta_ref[t0 + 2 * n_work + i] = jnp.where(valid, jnp.minimum(total - tile * TBM, TBM), 0)
        tmeta_ref[t0 + 3 * n_work + i] = jnp.logical_and(valid, grp != prev_grp).astype(I32)
        prev_grp = grp


def _expert_rows(xs_ref, ys_ref, wg_s, wu_s, wd_s, r0, nrows):
    xb = xs_ref[r0:r0 + nrows, 0:D_MODEL]
    meta = xs_ref[r0:r0 + nrows, D_MODEL:ROW_W].astype(F32)
    gates = meta[:, 0:EXPERTS_PER_GROUP] + meta[:, EXPERTS_PER_GROUP:2 * EXPERTS_PER_GROUP]
    parts = []
    for e in range(EXPERTS_PER_GROUP):
        a = jnp.dot(xb, wg_s[e], preferred_element_type=F32)
        bu = jnp.dot(xb, wu_s[e], preferred_element_type=F32)
        hid = (a * _sigmoid(a)) * bu
        parts.append((hid * gates[:, e:e + 1]).astype(BF16))
    hid_all = jnp.concatenate(parts, axis=1)
    ys_ref[r0:r0 + nrows, :] = jnp.dot(hid_all, wd_s[...],
                                       preferred_element_type=F32).astype(BF16)


def _moe_kernel(tm_ref, xs_ref, wg_ref, wu_ref, wd_ref, ys_ref, wg_s, wu_s, wd_s, *, t0, n_work):
    i = pl.program_id(0)
    rows = tm_ref[t0 + 2 * n_work + i]
    first_of_group = tm_ref[t0 + 3 * n_work + i] == 1

    @pl.when(first_of_group)
    def _():
        for e in range(EXPERTS_PER_GROUP):
            wg_s[e] = wg_ref[e].astype(BF16)
            wu_s[e] = wu_ref[e].astype(BF16)
        wd_s[...] = wd_ref[0].astype(BF16)

    @pl.when(rows == TBM)
    def _():
        _expert_rows(xs_ref, ys_ref, wg_s, wu_s, wd_s, 0, TBM)

    @pl.when(jnp.logical_and(rows > 0, rows < TBM))
    def _():
        for r0 in range(0, TBM, TBM_PART):
            @pl.when(r0 < rows)
            def _():
                _expert_rows(xs_ref, ys_ref, wg_s, wu_s, wd_s, r0, TBM_PART)

            @pl.when(r0 >= rows)
            def _():
                ys_ref[r0:r0 + TBM_PART, :] = jnp.zeros((TBM_PART, D_MODEL), BF16)


def _moe_call(xs, tmeta, layer, wg, wu, wd, n_tiles, n_work):
    t0 = (n_tiles + 1) * META_W
    row_map = lambda i, tm: (tm[t0 + i], 0)
    grp_map = lambda i, tm: (layer, tm[t0 + n_work + i], 0, 0)
    grid_spec = pltpu.PrefetchScalarGridSpec(
        num_scalar_prefetch=1, grid=(n_work,),
        in_specs=[
            pl.BlockSpec((TBM, ROW_W), row_map),
            pl.BlockSpec((None, EXPERTS_PER_GROUP, D_MODEL, EXPERT_FF), grp_map),
            pl.BlockSpec((None, EXPERTS_PER_GROUP, D_MODEL, EXPERT_FF), grp_map),
            pl.BlockSpec((None, 1, GROUP_FF, D_MODEL), grp_map),
        ],
        out_specs=pl.BlockSpec((TBM, D_MODEL), row_map),
        scratch_shapes=[pltpu.VMEM((EXPERTS_PER_GROUP, D_MODEL, EXPERT_FF), BF16),
                        pltpu.VMEM((EXPERTS_PER_GROUP, D_MODEL, EXPERT_FF), BF16),
                        pltpu.VMEM((GROUP_FF, D_MODEL), BF16)],
    )
    return pl.pallas_call(
        functools.partial(_moe_kernel, t0=t0, n_work=n_work), grid_spec=grid_spec,
        out_shape=jax.ShapeDtypeStruct((xs.shape[0], D_MODEL), BF16),
        compiler_params=pltpu.CompilerParams(
            dimension_semantics=("arbitrary",), vmem_limit_bytes=VMEM_LIMIT),
        name="moe_experts",
    )(tmeta, xs, wg, wu, wd)


def _final_kernel(pm_ref, hprev_ref, modprev_ref, tokprev_ref, ys_hbm, nf_ref, o_ref,
                  ystage_ref, gsem, *, nb, n_steps):
    t = pl.program_id(0)
    drain = _head_dma(t, n_steps, nb, pm_ref, ys_hbm, ystage_ref, gsem)
    for b in range(nb):
        h = _head_combine(b, t % 2, hprev_ref, modprev_ref, tokprev_ref, ystage_ref)
        ms = jnp.mean(h * h, axis=-1, keepdims=True)
        o_ref[b] = (h * lax.rsqrt(ms + EPS)) * nf_ref[0:1]
    pl.when(t == n_steps - 1)(drain)


def _final_call(head, mod_all, norm_final):
    pm, hprev, tokprev, ys = head
    nb, seq = tokprev.shape[:2]
    n_steps = seq // TB
    grid_spec = pltpu.PrefetchScalarGridSpec(
        num_scalar_prefetch=1, grid=(n_steps,),
        in_specs=[_nat(nb, D_MODEL), _mod_spec(nb, DEPTH - 1), _nat(nb, LANES),
                  pl.BlockSpec(memory_space=pl.ANY), _full((1, D_MODEL))],
        out_specs=_nat(nb, D_MODEL),
        scratch_shapes=_head_scratch(nb))
    return pl.pallas_call(
        functools.partial(_final_kernel, nb=nb, n_steps=n_steps), grid_spec=grid_spec,
        out_shape=jax.ShapeDtypeStruct((nb, seq, D_MODEL), F32),
        compiler_params=pltpu.CompilerParams(
            dimension_semantics=("arbitrary",), vmem_limit_bytes=VMEM_LIMIT),
        name="final_norm",
    )(pm, hprev, mod_all, tokprev, ys, norm_final.reshape(1, D_MODEL))


def _pool_band_table():
    r = jnp.arange(TB)[:, None]
    c = jnp.arange(TB + LANES)[None, :]
    return jnp.stack([(c > r + LANES - w) & (c <= r + LANES) for w in POOL_SIZES]).astype(BF16)


def _attn_bias_table():
    slopes = 2.0 ** (-8.0 * (jnp.arange(N_HEADS, dtype=F32) + 1.0) / N_HEADS)
    dist = (jnp.arange(BLOCK)[:, None] + BLOCK) - jnp.arange(2 * BLOCK)[None, :]
    ok = (dist >= 0) & (dist < WINDOW)
    per_head = jnp.where(ok[None], -slopes[:, None, None] * dist.astype(F32)[None], NEG_BIG)
    return per_head.reshape(N_HEADS // 2, 2, BLOCK, 2 * BLOCK).transpose(0, 2, 1, 3).reshape(
        N_HEADS // 4, 2 * BLOCK, 4 * BLOCK)


def _router_operands(w_router, b_router):
    def reorder(a):
        return a.reshape(N_GROUPS, EXPERTS_PER_GROUP, -1).transpose(1, 0, 2).reshape(N_EXPERTS, -1)
    w = reorder(w_router.T)
    wh = w.astype(BF16)
    wl = (w - wh.astype(F32)).astype(BF16)
    return jnp.concatenate([wh, wl], axis=0), reorder(b_router[:, None])


def kernel(x, c, w_ada, b_ada, norm_mix, norm_ffn, w_in, w_out, sinks, conv_w, w_pool,
           pool_scale, w_router, b_router, w_gate, w_up, w_down, norm_final):
    b, s, _ = x.shape
    n_tiles, cap, n_work = _sizes(b * s)

    mod_all = _mod_call(c, w_ada, b_ada)
    bias_tab = _attn_bias_table()
    band_tab = _pool_band_table()
    wrt, br = _router_operands(w_router, b_router)
    tri = jnp.triu(jnp.ones((TB, TB), F32)).astype(BF16)
    w_down4 = w_down.reshape(DEPTH, N_GROUPS, GROUP_FF, D_MODEL)

    h = x
    head = None
    for layer in range(DEPTH):
        if layer % 2 == 0:
            kind = "even"
            weights = (w_in, w_out, bias_tab, sinks, conv_w)
        else:
            kind = "odd"
            weights = (w_pool, band_tab, pool_scale)
        h1, tokmeta, tmeta, xs = _mixer_call(kind, layer, head, h, mod_all, norm_mix, norm_ffn,
                                             weights, wrt, br, tri)
        ys = _moe_call(xs, tmeta, layer, w_gate, w_up, w_down4, n_tiles, n_work)
        head = (tmeta, h1, tokmeta, ys)
    return _final_call(head, mod_all, norm_final)
```

```python
import functools

import jax
import jax.numpy as jnp
from jax import lax
from jax.experimental import pallas as pl
from jax.experimental.pallas import tpu as pltpu

F32 = jnp.float32
BF16 = jnp.bfloat16
I32 = jnp.int32

D_MODEL = 1024
DEPTH = 4
EPS = 1e-6
N_MOD = 6

ATTN_WIDTH = 512
HEAD_DIM = 64
N_HEADS = 8
KV_WIDTH = 128
WINDOW = 128
BLOCK = 128
CONV_WIDTH = 512
IN_WIDTH = 2304

POOL_SIZES = (2, 4, 8, 16)
POOL_GROUP = 256
POOL_HALO = 16

N_EXPERTS = 16
N_GROUPS = 4
EXPERTS_PER_GROUP = 4
EXPERT_FF = 256
GROUP_FF = EXPERTS_PER_GROUP * EXPERT_FF

LANES = 128
NEG_BIG = -1e30

TB = 256
SUB = 16
N_SUB = TB // SUB + N_GROUPS - 1
STG_P = 320
STG_C = 384
TBM = 1024
TBM_PART = 256
ROW_W = D_MODEL + LANES
META_W = 32
LP_LANE = 8
MAIN_START = 2
MOD_COLS = 3072
VMEM_LIMIT = 56 * 1024 * 1024
NT_DIMS = (((1,), (1,)), ((), ()))

assert N_SUB * SUB <= STG_P <= STG_C and STG_C % LANES == 0 and N_SUB <= META_W


def _sizes(n_tok):
    n_tiles = n_tok // TB
    cap = -(-(n_tok + n_tiles * (SUB - 1) + TBM) // TBM) * TBM
    n_work = (n_tok + n_tiles * N_GROUPS * (SUB - 1)) // TBM + N_GROUPS
    return n_tiles, cap, n_work


def _sigmoid(x):
    return 1.0 / (1.0 + jnp.exp(-x))


def _rms_mod(x, g, sc, sh):
    ms = jnp.mean(x * x, axis=-1, keepdims=True)
    return (x * lax.rsqrt(ms + EPS)) * (g * (1.0 + sc)) + sh


def _onehot(cond):
    return jnp.where(cond, 1.0, 0.0).astype(BF16)


def _mod_kernel(ct_ref, w_ref, b_ref, o_ref, *, nb):
    j = pl.program_id(1)
    ct = ct_ref[...]
    cond = ct * _sigmoid(ct)
    vec_per_step = MOD_COLS // D_MODEL
    for b in range(nb):
        col = cond[:, b:b + 1]
        acc = jnp.zeros((8, MOD_COLS), F32)
        for r in range(D_MODEL // 8):
            acc = acc + w_ref[0, 8 * r:8 * r + 8, :] * col[8 * r:8 * r + 8]
        res = jnp.sum(acc, axis=0, keepdims=True) + b_ref[0]
        for jj in range(N_MOD // vec_per_step):
            @pl.when(j == jj)
            def _():
                for k in range(vec_per_step):
                    row = jj * vec_per_step + k
                    o_ref[0, b, row:row + 1, :] = res[:, k * D_MODEL:(k + 1) * D_MODEL]

        @pl.when(j == 0)
        def _():
            o_ref[0, b, N_MOD:8, :] = jnp.zeros((8 - N_MOD, D_MODEL), F32)


def _mod_call(c, w_ada, b_ada):
    nb = c.shape[0]
    ct = jnp.pad(c.T, ((0, 0), (0, LANES - nb)))
    n_col = (N_MOD * D_MODEL) // MOD_COLS
    return pl.pallas_call(
        functools.partial(_mod_kernel, nb=nb),
        grid=(DEPTH, n_col),
        in_specs=[
            pl.BlockSpec((D_MODEL, LANES), lambda l, j: (0, 0)),
            pl.BlockSpec((1, D_MODEL, MOD_COLS), lambda l, j: (l, 0, j)),
            pl.BlockSpec((1, 1, MOD_COLS), lambda l, j: (l, 0, j)),
        ],
        out_specs=pl.BlockSpec((1, nb, 8, D_MODEL), lambda l, j: (l, 0, 0, 0)),
        out_shape=jax.ShapeDtypeStruct((DEPTH, nb, 8, D_MODEL), F32),
        compiler_params=pltpu.CompilerParams(
            dimension_semantics=("arbitrary", "arbitrary"), vmem_limit_bytes=VMEM_LIMIT),
        name="adaln_mod",
    )(ct, w_ada, b_ada.reshape(DEPTH, 1, N_MOD * D_MODEL))


def _sub_copies(hbm_ref, stage_ref, sem, rows, to_hbm):
    out = []
    for s in range(N_SUB):
        s_view = stage_ref.at[pl.ds(s * SUB, SUB), :]
        h_view = hbm_ref.at[pl.ds(pl.multiple_of(rows[s], SUB), SUB), :]
        out.append(pltpu.make_async_copy(s_view, h_view, sem) if to_hbm
                   else pltpu.make_async_copy(h_view, s_view, sem))
    return out


def _head_dma(t, n_steps, nb, pm_ref, ys_hbm, ystage_ref, gsem):
    slot = t % 2

    def copies(step, b, slot_):
        base = (jnp.minimum(step, n_steps - 1) * nb + b) * META_W
        rows = [pm_ref[base + s] for s in range(N_SUB)]
        return _sub_copies(ys_hbm, ystage_ref.at[b, slot_], gsem.at[b, slot_], rows, False)

    @pl.when(t == 0)
    def _():
        ystage_ref[...] = jnp.zeros_like(ystage_ref)
        for b in range(nb):
            for cp in copies(0, b, 0):
                cp.start()

    nxt = t + 1
    for b in range(nb):
        for cp in copies(nxt, b, 1 - slot):
            cp.start()

    for b in range(nb):
        for cp in copies(t, b, slot):
            cp.wait()

    def drain():
        for b in range(nb):
            for cp in copies(nxt, b, 1 - slot):
                cp.wait()
    return drain


def _head_combine(b, slot, hprev_ref, modprev_ref, tokmeta_ref, ystage_ref):
    lp = tokmeta_ref[b, :, LP_LANE:LP_LANE + 1].astype(I32)
    unsort = _onehot(lax.broadcasted_iota(I32, (TB, STG_C), 1) == lp)
    y = jnp.dot(unsort, ystage_ref[b, slot], preferred_element_type=F32)
    return hprev_ref[b] + modprev_ref[b, 5:6] * y


def _route(u2, wrt_ref, br_ref):
    uh = u2.astype(BF16)
    ul = (u2 - uh.astype(F32)).astype(BF16)
    w = wrt_ref[...]
    r1 = lax.dot_general(w, uh, NT_DIMS, preferred_element_type=F32)
    r2 = lax.dot_general(w[0:N_EXPERTS], ul, NT_DIMS, preferred_element_type=F32)
    yield
    scores = _sigmoid(r1[0:N_EXPERTS] + r1[N_EXPERTS:] + r2)
    biased = scores + br_ref[...]
    s = [scores[4 * j:4 * j + 4] for j in range(EXPERTS_PER_GROUP)]
    b = [biased[4 * j:4 * j + 4] for j in range(EXPERTS_PER_GROUP)]

    hi1, lo1 = jnp.maximum(b[0], b[1]), jnp.minimum(b[0], b[1])
    hi2, lo2 = jnp.maximum(b[2], b[3]), jnp.minimum(b[2], b[3])
    second = jnp.maximum(jnp.minimum(hi1, hi2), jnp.where(hi1 >= hi2, lo1, lo2))
    gscore = jnp.maximum(hi1, hi2) + second

    best = gscore[0:1]
    bgrp = jnp.zeros((1, TB), I32)
    for g in range(1, N_GROUPS):
        better = gscore[g:g + 1] > best
        bgrp = jnp.where(better, g, bgrp)
        best = jnp.where(better, gscore[g:g + 1], best)
    selmask = lax.broadcasted_iota(I32, (N_GROUPS, TB), 0) == bgrp
    yield

    m1 = b[0]
    i1 = jnp.zeros((N_GROUPS, TB), I32)
    for j in range(1, EXPERTS_PER_GROUP):
        gt = b[j] > m1
        i1 = jnp.where(gt, j, i1)
        m1 = jnp.where(gt, b[j], m1)
    cands = [jnp.where(i1 == j, -jnp.inf, b[j]) for j in range(EXPERTS_PER_GROUP)]
    m2 = cands[0]
    i2 = jnp.zeros_like(i1)
    for j in range(1, EXPERTS_PER_GROUP):
        gt = cands[j] > m2
        i2 = jnp.where(gt, j, i2)
        m2 = jnp.where(gt, cands[j], m2)
    w1 = s[0]
    w2 = s[0]
    for j in range(1, EXPERTS_PER_GROUP):
        w1 = jnp.where(i1 == j, s[j], w1)
        w2 = jnp.where(i2 == j, s[j], w2)
    yield
    tot = w1 + w2
    w1n = w1 / tot
    w2n = w2 / tot
    gates = []
    for j in range(EXPERTS_PER_GROUP):
        gj = jnp.where(i1 == j, w1n, 0.0) + jnp.where(i2 == j, w2n, 0.0)
        gates.append(jnp.sum(jnp.where(selmask, gj, 0.0), axis=0, keepdims=True))
    return bgrp, gates, selmask


def _tail_compute(b, slot, h1, mod, g_ffn, wrt_ref, br_ref, tri_ref, tokmeta_ref, stage_ref):
    sh2, sc2 = mod[3:4], mod[4:5]
    u2 = _rms_mod(h1, g_ffn, sc2, sh2)
    bgrp, gates, selmask = yield from _route(u2, wrt_ref, br_ref)
    yield

    onehot = jnp.concatenate([jnp.where(selmask, 1.0, 0.0), jnp.zeros((8 - N_GROUPS, TB), F32)],
                             axis=0).astype(BF16)
    rank_incl = jnp.dot(onehot, tri_ref[...], preferred_element_type=F32)
    rank = jnp.sum(jnp.where(selmask, rank_incl[0:N_GROUPS], 0.0), axis=0, keepdims=True) - 1.0
    cnt = lax.dot_general(jnp.ones((8, TB), BF16), onehot, NT_DIMS,
                          preferred_element_type=F32)
    sub_rows = jnp.floor((cnt[0:1, :] + (SUB - 1.0)) * (1.0 / SUB)) * SUB
    lp = rank
    first_row = jnp.zeros((1, 1), F32)
    for g in range(1, N_GROUPS):
        first_row = first_row + sub_rows[:, g - 1:g]
        lp = lp + jnp.where(bgrp == g, first_row, 0.0)

    hi = [x.astype(BF16).astype(F32) for x in gates]
    lo = [(x - h).astype(BF16).astype(F32) for x, h in zip(gates, hi)]
    meta_src = jnp.concatenate(hi + lo + [lp, jnp.zeros((LANES - 9, TB), F32)], axis=0)
    meta_t = meta_src.T
    tokmeta_ref[b] = meta_t
    yield

    rowdata = jnp.concatenate([u2.astype(BF16), meta_t.astype(BF16)], axis=1)
    sort = _onehot(lax.broadcasted_iota(I32, (STG_P, TB), 0) == lp.astype(I32))
    stage_ref[b, slot] = jnp.dot(sort, rowdata, preferred_element_type=F32).astype(BF16)
    return [cnt[0, g].astype(I32) for g in range(N_GROUPS)]


def _prime_scatter(nb, cap, xs_hbm, stage_ref, prev_ref, ssem):
    stage_ref[:, 1] = jnp.zeros((nb,) + stage_ref.shape[2:], BF16)
    for b in range(nb):
        rows = [jnp.int32(N_GROUPS * cap + (b * N_SUB + s) * SUB) for s in range(N_SUB)]
        for s in range(N_SUB):
            prev_ref[b * N_SUB + s] = rows[s]
        for cp in _sub_copies(xs_hbm, stage_ref.at[b, 1], ssem.at[b, 1], rows, True):
            cp.start()


def _tail_dma(t, lag, n_steps, nb, cap, n_work, counts, tmeta_ref, xs_hbm, stage_ref, zeros_ref,
              run_ref, prev_ref, ssem, zsem, drain_head):
    slot = t % 2
    run = [run_ref[g] for g in range(N_GROUPS)]
    junk = N_GROUPS * cap
    scatter_rows = []
    for b in range(nb):
        first_sub = [jnp.int32(0)]
        for g in range(N_GROUPS):
            first_sub.append(first_sub[-1] + (counts[b][g] + (SUB - 1)) // SUB)
        dst = [g * cap + run[g] for g in range(N_GROUPS)]
        rows_b, base = [], (jnp.maximum(t - lag, 0) * nb + b) * META_W
        for s in range(N_SUB):
            grp_first, grp_dst = first_sub[0], dst[0]
            for g in range(1, N_GROUPS):
                later = s >= first_sub[g]
                grp_first = jnp.where(later, first_sub[g], grp_first)
                grp_dst = jnp.where(later, dst[g], grp_dst)
            row = grp_dst + (s - grp_first) * SUB
            used = s < first_sub[N_GROUPS]
            rows_b.append(jnp.where(used, row, junk + (b * N_SUB + s) * SUB))
            tmeta_ref[base + s] = jnp.where(used, row, row if s == 0 else first_row)
            if s == 0:
                first_row = row
        for k in range(N_SUB, META_W):
            tmeta_ref[base + k] = jnp.int32(0)
        for g in range(N_GROUPS):
            run[g] = run[g] + (first_sub[g + 1] - first_sub[g]) * SUB
        scatter_rows.append(rows_b)
    for g in range(N_GROUPS):
        run_ref[g] = run[g]

    def copies(b, slot_, rows):
        return _sub_copies(xs_hbm, stage_ref.at[b, slot_], ssem.at[b, slot_], rows, True)

    for b in range(nb):
        for cp in copies(b, 1 - slot, [prev_ref[b * N_SUB + s] for s in range(N_SUB)]):
            cp.wait()

    for b in range(nb):
        for cp in copies(b, slot, scatter_rows[b]):
            cp.start()
        for s in range(N_SUB):
            prev_ref[b * N_SUB + s] = scatter_rows[b][s]

    @pl.when(t == n_steps - 1 + lag)
    def _():
        for b in range(nb):
            for cp in copies(b, slot, scatter_rows[b]):
                cp.wait()
        if drain_head is not None:
            drain_head()
        zeros_ref[...] = jnp.zeros_like(zeros_ref)
        tails = []
        base = n_steps * nb * META_W
        for g in range(N_GROUPS):
            tmeta_ref[base + g] = run[g]
            start = pl.multiple_of(g * cap + run[g], SUB)
            tails.append(pltpu.make_async_copy(zeros_ref, xs_hbm.at[pl.ds(start, TBM), :], zsem))
        for k in range(N_GROUPS, META_W):
            tmeta_ref[base + k] = jnp.int32(0)
        for cp in tails:
            cp.start()
        _work_tables(run, cap, n_work, tmeta_ref, base + META_W)
        for cp in tails:
            cp.wait()


def _attn_conv_mix(x, t, mod, g_mix, win_ref, wout_ref, bias_ref, sink_at, convw_ref,
                   kvprev_ref, cprev_ref):
    tb = TB
    first = t == 0
    sh1, sc1, g1 = mod[0:1], mod[1:2], mod[2:3]
    ub = _rms_mod(x, g_mix, sc1, sh1).astype(BF16)
    qkv_w = ATTN_WIDTH + 2 * KV_WIDTH
    proj = jnp.dot(ub, win_ref[:, 0:qkv_w], preferred_element_type=F32)

    yield
    q = (proj[:, 0:ATTN_WIDTH] * HEAD_DIM ** -0.5).astype(BF16)
    kf = proj[:, ATTN_WIDTH:ATTN_WIDTH + KV_WIDTH]
    vf = proj[:, ATTN_WIDTH + KV_WIDTH:qkv_w]

    kv_prev = kvprev_ref[...]
    kext = jnp.concatenate([kv_prev[:, 0:KV_WIDTH], kf], axis=0)
    vext = jnp.concatenate([kv_prev[:, KV_WIDTH:], vf], axis=0)
    kvprev_ref[:, 0:KV_WIDTH] = kf[tb - BLOCK:tb]
    kvprev_ref[:, KV_WIDTH:] = vf[tb - BLOCK:tb]

    lane = lax.broadcasted_iota(I32, kext.shape, 1)
    lo = lane < HEAD_DIM
    krol = pltpu.roll(kext, HEAD_DIM, axis=1)
    vrol = pltpu.roll(vext, HEAD_DIM, axis=1)
    zero = jnp.zeros_like(kext)
    one_at_64 = jnp.where(lane == HEAD_DIM, 1.0, 0.0)
    one_at_0 = jnp.where(lane == 0, 1.0, 0.0)
    k_ops = [(jnp.where(lo, kext, zero).astype(BF16), jnp.where(lo, zero, krol).astype(BF16)),
             (jnp.where(lo, krol, zero).astype(BF16), jnp.where(lo, zero, kext).astype(BF16))]
    v_ops = [(jnp.where(lo, vext, one_at_64).astype(BF16), jnp.where(lo, one_at_0, vrol).astype(BF16)),
             (jnp.where(lo, vrol, one_at_64).astype(BF16), jnp.where(lo, one_at_0, vext).astype(BF16))]

    col = lax.broadcasted_iota(I32, (2 * BLOCK, 4 * BLOCK), 1)
    prev_cols = (col % (2 * BLOCK)) < BLOCK
    upper = lax.broadcasted_iota(I32, (2 * BLOCK, 1), 0) < BLOCK
    lane_o = lax.broadcasted_iota(I32, (2 * BLOCK, LANES), 1)
    lo_o = lane_o < HEAD_DIM

    yield
    units = [(bi, kvh) for bi in range(tb // BLOCK) for kvh in range(N_HEADS // 4)]

    def scores(bi, kvh):
        r0 = bi * BLOCK
        ka, kb = k_ops[kvh]
        rhs = jnp.concatenate([ka[r0:r0 + 2 * BLOCK], kb[r0:r0 + 2 * BLOCK]], axis=0)
        qrows = q[r0:r0 + BLOCK]
        qp = jnp.concatenate([qrows[:, (2 * kvh) * LANES:(2 * kvh + 1) * LANES],
                              qrows[:, (2 * kvh + 1) * LANES:(2 * kvh + 2) * LANES]], axis=0)
        s = lax.dot_general(qp, rhs, NT_DIMS, preferred_element_type=F32)
        bias = bias_ref[kvh]
        if bi == 0:
            bias = jnp.where(prev_cols & first, NEG_BIG, bias)
        return s + bias

    side = {}

    def conv_proj():
        side["gates"] = jnp.dot(ub, win_ref[:, qkv_w:IN_WIDTH], preferred_element_type=F32)

    def conv():
        g = side["gates"]
        bgate, cgate, xv = (g[:, i * CONV_WIDTH:(i + 1) * CONV_WIDTH] for i in range(3))
        uc = cgate * xv
        cprev = jnp.where(first, 0.0, cprev_ref[...])
        row = lax.broadcasted_iota(I32, uc.shape, 0)
        r1 = jnp.where(row == 0, cprev[7:8], pltpu.roll(uc, 1, axis=0))
        r2 = jnp.where(row == 0, cprev[6:7],
                       jnp.where(row == 1, cprev[7:8], pltpu.roll(uc, 2, axis=0)))
        cprev_ref[...] = uc[tb - 8:tb]
        cw = convw_ref[...]
        side["conv"] = (bgate * (cw[0:1] * r2 + cw[1:2] * r1 + cw[2:3] * uc)).astype(BF16)

    def conv_out():
        side["mix"] = jnp.dot(side["conv"], wout_ref[ATTN_WIDTH:, :], preferred_element_type=F32)

    fillers = [conv_proj, conv, conv_out]

    attn_rows = []
    s_next = scores(*units[0])
    for ui, (bi, kvh) in enumerate(units):
        r0 = bi * BLOCK
        if kvh == 0:
            pair_out = []
        s = s_next
        if ui + 1 < len(units):
            s_next = scores(*units[ui + 1])
        if ui < len(fillers):
            fillers[ui]()
        shs = [s[:, hh * 2 * BLOCK:(hh + 1) * 2 * BLOCK] for hh in range(2)]
        sinks = [jnp.where(upper, sink_at(4 * kvh + hh), sink_at(4 * kvh + 2 + hh))
                 for hh in range(2)]
        ms = [jnp.maximum(jnp.max(sh, axis=-1, keepdims=True), sk) for sh, sk in zip(shs, sinks)]
        ps = [jnp.exp(sh - m).astype(BF16) for sh, m in zip(shs, ms)]
        os_ = [jnp.dot(p, v_ops[kvh][hh][r0:r0 + 2 * BLOCK], preferred_element_type=F32)
               for hh, p in enumerate(ps)]
        outs = []
        for hh in range(2):
            den_col = HEAD_DIM if hh == 0 else 0
            den = os_[hh][:, den_col:den_col + 1] + jnp.exp(sinks[hh] - ms[hh])
            outs.append(os_[hh] / den)
        both = jnp.where(lo_o, outs[0], outs[1])
        pair_out += [both[0:BLOCK], both[BLOCK:]]
        if kvh == N_HEADS // 4 - 1:
            attn_rows.append(jnp.concatenate(pair_out, axis=1))
        yield
    attn = jnp.concatenate(attn_rows, axis=0)

    mix = side["mix"] + jnp.dot(attn.astype(BF16), wout_ref[0:ATTN_WIDTH, :],
                                preferred_element_type=F32)
    return x + g1 * mix


def _pool_mix(x, t, mod, g_mix, wpool_ref, band_ref, pscale, uprev_ref):
    tb = TB
    sh1, sc1, g1 = mod[0:1], mod[1:2], mod[2:3]
    u = _rms_mod(x, g_mix, sc1, sh1)
    halo = jnp.where(t == 0, 0.0, uprev_ref[...]).astype(BF16)
    ext = jnp.concatenate([jnp.zeros((LANES - POOL_HALO, D_MODEL), BF16), halo, u.astype(BF16)],
                          axis=0)
    uprev_ref[...] = u[tb - POOL_HALO:tb]

    yield
    pos = (t * tb + 1 + lax.broadcasted_iota(I32, (tb, 1), 0)).astype(F32)
    cols = [slice(gi * POOL_GROUP, (gi + 1) * POOL_GROUP) for gi in range(len(POOL_SIZES))]
    wsums = [jnp.dot(band_ref[gi], ext[:, sl], preferred_element_type=F32)
             for gi, sl in enumerate(cols)]
    yield
    mixed = []
    for gi, w in enumerate(POOL_SIZES):
        mean = wsums[gi] / jnp.minimum(pos, float(w))
        pooled = mean - u[:, cols[gi]]
        mixed.append(jnp.dot(pooled.astype(BF16), wpool_ref[gi].astype(BF16),
                             preferred_element_type=F32))
    yield
    mix = jnp.concatenate(mixed, axis=1) * pscale
    return x + g1 * mix


def _interleave(chains, starts):
    results = [None] * len(chains)
    live = list(range(len(chains)))
    rnd = 0
    while live:
        for k in list(live):
            if rnd < starts[k]:
                continue
            try:
                next(chains[k])
            except StopIteration as done:
                results[k] = done.value
                live.remove(k)
        rnd += 1
    return results


def _mixer_kernel(*refs, kind, layer, lag, has_head, nb, n_steps, cap, n_work):
    refs = list(refs)
    t = pl.program_id(0)
    slot = t % 2
    if has_head:
        pm_ref = refs.pop(0)
        hprev_ref, modprev_ref, tokprev_ref, ys_hbm = refs[:4]
        refs = refs[4:]
    else:
        h_ref = refs.pop(0)
    mod_ref, nmix_ref, nffn_ref = refs[:3]
    refs = refs[3:]
    n_w = 5 if kind == "even" else 3
    weight_refs = refs[:n_w]
    refs = refs[n_w:]
    wrt_ref, br_ref, tri_ref = refs[:3]
    refs = refs[3:]
    hout_ref, tokmeta_ref, tmeta_ref, xs_hbm = refs[:4]
    refs = refs[4:]
    n_c = 2 if kind == "even" else 1
    carry_refs = refs[:n_c]
    refs = refs[n_c:]
    if kind == "even":
        cast_refs = refs[:2]
        refs = refs[2:]
    if lag:
        h1s_ref = refs.pop(0)
    stage_ref, zeros_ref, run_ref, prev_ref, ssem, zsem = refs[:6]
    refs = refs[6:]

    tile = jnp.minimum(t, n_steps - 1)

    @pl.when(t == 0)
    def _():
        if lag:
            h1s_ref[...] = jnp.zeros_like(h1s_ref)
        for r in carry_refs:
            r[...] = jnp.zeros_like(r)
        for g in range(N_GROUPS):
            run_ref[g] = jnp.int32(0)
        if kind == "even":
            for src, dst in zip(weight_refs[:2], cast_refs):
                for c0 in range(0, src.shape[1], 2 * LANES):
                    dst[:, c0:c0 + 2 * LANES] = src[:, c0:c0 + 2 * LANES].astype(BF16)
        _prime_scatter(nb, cap, xs_hbm, stage_ref, prev_ref, ssem)

    j = layer // 2
    g_mix = nmix_ref[layer:layer + 1]
    g_ffn = nffn_ref[layer:layer + 1]
    if kind == "even":
        sink_ref = weight_refs[3]
        weight_refs = list(cast_refs) + [weight_refs[2], lambda i: sink_ref[j, i], weight_refs[4]]
    else:
        weight_refs = list(weight_refs[:2]) + [weight_refs[2][j:j + 1]]

    drain_head = None
    if has_head:
        ystage_ref, gsem = refs
        drain_head = _head_dma(t, n_steps, nb, pm_ref, ys_hbm, ystage_ref, gsem)

    def main_chain(b):
        if has_head:
            x = _head_combine(b, slot, hprev_ref, modprev_ref, tokprev_ref, ystage_ref)
            yield
        else:
            x = h_ref[b]
        carries = [r.at[b] for r in carry_refs]
        mix = _attn_conv_mix if kind == "even" else _pool_mix
        h1 = yield from mix(x, tile, mod_ref[b], g_mix, *weight_refs, *carries)
        hout_ref[b] = h1
        if lag:
            h1s_ref[b, slot] = h1
        yield
        return h1

    def tail_chain(b, h1):
        return (yield from _tail_compute(b, slot, h1, mod_ref[b], g_ffn, wrt_ref, br_ref,
                                         tri_ref, tokmeta_ref, stage_ref))

    def lagged_tail_chain(b):
        cnt = yield from tail_chain(b, h1s_ref[b, 1 - slot])
        return [jnp.where(t >= lag, c, 0) for c in cnt]

    def whole_chain(b):
        h1 = yield from main_chain(b)
        return (yield from tail_chain(b, h1))

    if lag:
        chains = [main_chain(b) for b in range(nb)] + [lagged_tail_chain(b) for b in range(nb)]
        counts = _interleave(chains, [MAIN_START] * nb + [0] * nb)[nb:]
    else:
        counts = _interleave([whole_chain(b) for b in range(nb)], [0] * nb)
    _tail_dma(t, lag, n_steps, nb, cap, n_work, counts, tmeta_ref, xs_hbm, stage_ref, zeros_ref,
              run_ref, prev_ref, ssem, zsem, drain_head)


def _nat(nb, cols):
    return pl.BlockSpec((nb, TB, cols), lambda t, *_: (0, t, 0))


def _full(shape):
    nd = len(shape)
    return pl.BlockSpec(shape, lambda t, *_: (0,) * nd)


def _head_scratch(nb):
    return [pltpu.VMEM((nb, 2, STG_C, D_MODEL), BF16), pltpu.SemaphoreType.DMA((nb, 2))]


def _mod_spec(nb, layer):
    return pl.BlockSpec((None, nb, 8, D_MODEL), lambda t, *_: (layer, 0, 0, 0))


def _mixer_call(kind, layer, head, h, mod_all, norm_mix, norm_ffn, weights, wrt, br, tri):
    j = layer // 2
    nb, seq = (h if head is None else head[2]).shape[:2]
    n_steps = seq // TB
    n_tiles, cap, n_work = _sizes(nb * seq)
    has_head = head is not None
    any_spec = pl.BlockSpec(memory_space=pl.ANY)
    smem_spec = pl.BlockSpec(memory_space=pltpu.SMEM)

    def main(cols):
        return pl.BlockSpec((nb, TB, cols), lambda t, *_: (0, jnp.minimum(t, n_steps - 1), 0))

    lag = 1 if kind == "odd" else 0

    def lagged(cols):
        return pl.BlockSpec((nb, TB, cols), lambda t, *_: (0, jnp.maximum(t - lag, 0), 0))

    if has_head:
        pm, hprev, tokprev, ys = head
        args = [hprev, mod_all, tokprev, ys]
        in_specs = [main(D_MODEL), _mod_spec(nb, layer - 1), main(LANES), any_spec]
    else:
        args = [h]
        in_specs = [main(D_MODEL)]
    args += [mod_all, norm_mix, norm_ffn]
    in_specs += [_mod_spec(nb, layer), _full(norm_mix.shape), _full(norm_ffn.shape)]
    if kind == "even":
        args += list(weights)
        once = dict(pipeline_mode=pl.Buffered(1))
        in_specs += [pl.BlockSpec((None, D_MODEL, IN_WIDTH), lambda t, *_: (j, 0, 0), **once),
                     pl.BlockSpec((None, D_MODEL, D_MODEL), lambda t, *_: (j, 0, 0), **once),
                     _full((N_HEADS // 4, 2 * BLOCK, 4 * BLOCK)), smem_spec,
                     pl.BlockSpec((None,) + weights[4].shape[1:], lambda t, *_: (j, 0, 0))]
        mix_scratch = [pltpu.VMEM((nb, BLOCK, 2 * KV_WIDTH), F32),
                       pltpu.VMEM((nb, 8, CONV_WIDTH), F32),
                       pltpu.VMEM((D_MODEL, IN_WIDTH), BF16),
                       pltpu.VMEM((D_MODEL, D_MODEL), BF16)]
    else:
        args += list(weights)
        in_specs += [pl.BlockSpec((None, len(POOL_SIZES), POOL_GROUP, POOL_GROUP),
                                  lambda t, *_: (j, 0, 0, 0)),
                     _full((len(POOL_SIZES), TB, TB + LANES)), _full(weights[2].shape)]
        mix_scratch = [pltpu.VMEM((nb, POOL_HALO, D_MODEL), F32)]
    args += [wrt, br, tri]
    in_specs += [_full((2 * N_EXPERTS, D_MODEL)), _full((N_EXPERTS, 1)), _full((TB, TB))]

    out_shape = (jax.ShapeDtypeStruct((nb, seq + lag * TB, D_MODEL), F32),
                 jax.ShapeDtypeStruct((nb, seq, LANES), F32),
                 jax.ShapeDtypeStruct(((n_tiles + 1) * META_W + 4 * n_work,), I32),
                 jax.ShapeDtypeStruct((N_GROUPS * cap + nb * N_SUB * SUB, ROW_W), BF16))
    out_specs = (_nat(nb, D_MODEL), lagged(LANES), smem_spec, any_spec)
    scratch = mix_scratch + ([pltpu.VMEM((nb, 2, TB, D_MODEL), F32)] if lag else []) + [
        pltpu.VMEM((nb, 2, STG_P, ROW_W), BF16),
        pltpu.VMEM((TBM, ROW_W), BF16),
        pltpu.SMEM((N_GROUPS,), I32),
        pltpu.SMEM((nb * N_SUB,), I32),
        pltpu.SemaphoreType.DMA((nb, 2)),
        pltpu.SemaphoreType.DMA(()),
    ]
    if has_head:
        scratch += _head_scratch(nb)

    body = functools.partial(_mixer_kernel, kind=kind, layer=layer, lag=lag, has_head=has_head,
                             nb=nb, n_steps=n_steps, cap=cap, n_work=n_work)
    grid_spec = pltpu.PrefetchScalarGridSpec(
        num_scalar_prefetch=1 if has_head else 0, grid=(n_steps + lag,),
        in_specs=in_specs, out_specs=out_specs, scratch_shapes=scratch)
    call = pl.pallas_call(
        body, grid_spec=grid_spec, out_shape=out_shape,
        compiler_params=pltpu.CompilerParams(
            dimension_semantics=("arbitrary",), vmem_limit_bytes=VMEM_LIMIT),
        name=kind + "_mixer")
    if has_head:
        return call(pm, *args)
    return call(*args)


def _work_tables(totals, cap, n_work, tmeta_ref, t0):
    ends, starts = [], []
    acc = jnp.int32(0)
    for g in range(N_GROUPS):
        starts.append(acc)
        acc = acc + (totals[g] + (TBM - 1)) // TBM
        ends.append(acc)
    nvalid = ends[-1]
    prev_grp = jnp.int32(-1)
    for i in range(n_work):
        idc = jnp.minimum(jnp.int32(i), nvalid - 1)
        grp, start, total = jnp.int32(0), starts[0], totals[0]
        for g in range(1, N_GROUPS):
            later = idc >= ends[g - 1]
            grp = jnp.where(later, g, grp)
            start = jnp.where(later, starts[g], start)
            total = jnp.where(later, totals[g], total)
        valid = i < nvalid
        tile = idc - start
        tmeta_ref[t0 + i] = grp * (cap // TBM) + tile
        tmeta_ref[t0 + n_work + i] = grp
        tmeta_ref[t0 + 2 * n_work + i] = jnp.where(valid, jnp.minimum(total - tile * TBM, TBM), 0)
        tmeta_ref[t0 + 3 * n_work + i] = jnp.logical_and(valid, grp != prev_grp).astype(I32)
        prev_grp = grp


def _expert_rows(xs_ref, ys_ref, wg_s, wu_s, wd_s, r0, nrows):
    xb = xs_ref[r0:r0 + nrows, 0:D_MODEL]
    meta = xs_ref[r0:r0 + nrows, D_MODEL:ROW_W].astype(F32)
    gates = meta[:, 0:EXPERTS_PER_GROUP] + meta[:, EXPERTS_PER_GROUP:2 * EXPERTS_PER_GROUP]
    parts = []
    for e in range(EXPERTS_PER_GROUP):
        a = jnp.dot(xb, wg_s[e], preferred_element_type=F32)
        bu = jnp.dot(xb, wu_s[e], preferred_element_type=F32)
        hid = (a * _sigmoid(a)) * bu
        parts.append((hid * gates[:, e:e + 1]).astype(BF16))
    hid_all = jnp.concatenate(parts, axis=1)
    ys_ref[r0:r0 + nrows, :] = jnp.dot(hid_all, wd_s[...],
                                       preferred_element_type=F32).astype(BF16)


def _moe_kernel(tm_ref, xs_ref, wg_ref, wu_ref, wd_ref, ys_ref, wg_s, wu_s, wd_s, *, t0, n_work):
    i = pl.program_id(0)
    rows = tm_ref[t0 + 2 * n_work + i]
    first_of_group = tm_ref[t0 + 3 * n_work + i] == 1

    @pl.when(first_of_group)
    def _():
        for e in range(EXPERTS_PER_GROUP):
            wg_s[e] = wg_ref[e].astype(BF16)
            wu_s[e] = wu_ref[e].astype(BF16)
        wd_s[...] = wd_ref[0].astype(BF16)

    @pl.when(rows == TBM)
    def _():
        _expert_rows(xs_ref, ys_ref, wg_s, wu_s, wd_s, 0, TBM)

    @pl.when(jnp.logical_and(rows > 0, rows < TBM))
    def _():
        for r0 in range(0, TBM, TBM_PART):
            @pl.when(r0 < rows)
            def _():
                _expert_rows(xs_ref, ys_ref, wg_s, wu_s, wd_s, r0, TBM_PART)

            @pl.when(r0 >= rows)
            def _():
                ys_ref[r0:r0 + TBM_PART, :] = jnp.zeros((TBM_PART, D_MODEL), BF16)


def _moe_call(xs, tmeta, layer, wg, wu, wd, n_tiles, n_work):
    t0 = (n_tiles + 1) * META_W
    row_map = lambda i, tm: (tm[t0 + i], 0)
    grp_map = lambda i, tm: (layer, tm[t0 + n_work + i], 0, 0)
    grid_spec = pltpu.PrefetchScalarGridSpec(
        num_scalar_prefetch=1, grid=(n_work,),
        in_specs=[
            pl.BlockSpec((TBM, ROW_W), row_map),
            pl.BlockSpec((None, EXPERTS_PER_GROUP, D_MODEL, EXPERT_FF), grp_map),
            pl.BlockSpec((None, EXPERTS_PER_GROUP, D_MODEL, EXPERT_FF), grp_map),
            pl.BlockSpec((None, 1, GROUP_FF, D_MODEL), grp_map),
        ],
        out_specs=pl.BlockSpec((TBM, D_MODEL), row_map),
        scratch_shapes=[pltpu.VMEM((EXPERTS_PER_GROUP, D_MODEL, EXPERT_FF), BF16),
                        pltpu.VMEM((EXPERTS_PER_GROUP, D_MODEL, EXPERT_FF), BF16),
                        pltpu.VMEM((GROUP_FF, D_MODEL), BF16)],
    )
    return pl.pallas_call(
        functools.partial(_moe_kernel, t0=t0, n_work=n_work), grid_spec=grid_spec,
        out_shape=jax.ShapeDtypeStruct((xs.shape[0], D_MODEL), BF16),
        compiler_params=pltpu.CompilerParams(
            dimension_semantics=("arbitrary",), vmem_limit_bytes=VMEM_LIMIT),
        name="moe_experts",
    )(tmeta, xs, wg, wu, wd)


def _final_kernel(pm_ref, hprev_ref, modprev_ref, tokprev_ref, ys_hbm, nf_ref, o_ref,
                  ystage_ref, gsem, *, nb, n_steps):
    t = pl.program_id(0)
    drain = _head_dma(t, n_steps, nb, pm_ref, ys_hbm, ystage_ref, gsem)
    for b in range(nb):
        h = _head_combine(b, t % 2, hprev_ref, modprev_ref, tokprev_ref, ystage_ref)
        ms = jnp.mean(h * h, axis=-1, keepdims=True)
        o_ref[b] = (h * lax.rsqrt(ms + EPS)) * nf_ref[0:1]
    pl.when(t == n_steps - 1)(drain)


def _final_call(head, mod_all, norm_final):
    pm, hprev, tokprev, ys = head
    nb, seq = tokprev.shape[:2]
    n_steps = seq // TB
    grid_spec = pltpu.PrefetchScalarGridSpec(
        num_scalar_prefetch=1, grid=(n_steps,),
        in_specs=[_nat(nb, D_MODEL), _mod_spec(nb, DEPTH - 1), _nat(nb, LANES),
                  pl.BlockSpec(memory_space=pl.ANY), _full((1, D_MODEL))],
        out_specs=_nat(nb, D_MODEL),
        scratch_shapes=_head_scratch(nb))
    return pl.pallas_call(
        functools.partial(_final_kernel, nb=nb, n_steps=n_steps), grid_spec=grid_spec,
        out_shape=jax.ShapeDtypeStruct((nb, seq, D_MODEL), F32),
        compiler_params=pltpu.CompilerParams(
            dimension_semantics=("arbitrary",), vmem_limit_bytes=VMEM_LIMIT),
        name="final_norm",
    )(pm, hprev, mod_all, tokprev, ys, norm_final.reshape(1, D_MODEL))


def _pool_band_table():
    r = jnp.arange(TB)[:, None]
    c = jnp.arange(TB + LANES)[None, :]
    return jnp.stack([(c > r + LANES - w) & (c <= r + LANES) for w in POOL_SIZES]).astype(BF16)


def _attn_bias_table():
    slopes = 2.0 ** (-8.0 * (jnp.arange(N_HEADS, dtype=F32) + 1.0) / N_HEADS)
    dist = (jnp.arange(BLOCK)[:, None] + BLOCK) - jnp.arange(2 * BLOCK)[None, :]
    ok = (dist >= 0) & (dist < WINDOW)
    per_head = jnp.where(ok[None], -slopes[:, None, None] * dist.astype(F32)[None], NEG_BIG)
    return per_head.reshape(N_HEADS // 2, 2, BLOCK, 2 * BLOCK).transpose(0, 2, 1, 3).reshape(
        N_HEADS // 4, 2 * BLOCK, 4 * BLOCK)


def _router_operands(w_router, b_router):
    def reorder(a):
        return a.reshape(N_GROUPS, EXPERTS_PER_GROUP, -1).transpose(1, 0, 2).reshape(N_EXPERTS, -1)
    w = reorder(w_router.T)
    wh = w.astype(BF16)
    wl = (w - wh.astype(F32)).astype(BF16)
    return jnp.concatenate([wh, wl], axis=0), reorder(b_router[:, None])


def kernel(x, c, w_ada, b_ada, norm_mix, norm_ffn, w_in, w_out, sinks, conv_w, w_pool,
           pool_scale, w_router, b_router, w_gate, w_up, w_down, norm_final):
    b, s, _ = x.shape
    n_tiles, cap, n_work = _sizes(b * s)

    mod_all = _mod_call(c, w_ada, b_ada)
    bias_tab = _attn_bias_table()
    band_tab = _pool_band_table()
    wrt, br = _router_operands(w_router, b_router)
    tri = jnp.triu(jnp.ones((TB, TB), F32)).astype(BF16)
    w_down4 = w_down.reshape(DEPTH, N_GROUPS, GROUP_FF, D_MODEL)

    h = x
    head = None
    for layer in range(DEPTH):
        if layer % 2 == 0:
            kind = "even"
            weights = (w_in, w_out, bias_tab, sinks, conv_w)
        else:
            kind = "odd"
            weights = (w_pool, band_tab, pool_scale)
        h1, tokmeta, tmeta, xs = _mixer_call(kind, layer, head, h, mod_all, norm_mix, norm_ffn,
                                             weights, wrt, br, tri)
        ys = _moe_call(xs, tmeta, layer, w_gate, w_up, w_down4, n_tiles, n_work)
        head = (tmeta, h1, tokmeta, ys)
    return _final_call(head, mod_all, norm_final)
```

```python
import functools

import jax
import jax.numpy as jnp
from jax import lax
from jax.experimental import pallas as pl
from jax.experimental.pallas import tpu as pltpu

F32 = jnp.float32
BF16 = jnp.bfloat16
I32 = jnp.int32

D_MODEL = 1024
DEPTH = 4
EPS = 1e-6
N_MOD = 6

ATTN_WIDTH = 512
HEAD_DIM = 64
N_HEADS = 8
KV_WIDTH = 128
WINDOW = 128
BLOCK = 128
CONV_WIDTH = 512
IN_WIDTH = 2304

POOL_SIZES = (2, 4, 8, 16)
POOL_GROUP = 256
POOL_HALO = 16

N_EXPERTS = 16
N_GROUPS = 4
EXPERTS_PER_GROUP = 4
EXPERT_FF = 256
GROUP_FF = EXPERTS_PER_GROUP * EXPERT_FF

LANES = 128
NEG_BIG = -1e30

TB = 256
SUB = 16
N_SUB = TB // SUB + N_GROUPS - 1
STG_P = 320
STG_C = 384
TBM = 1024
TBM_PART = 256
ROW_W = D_MODEL + LANES
META_W = 32
LP_LANE = 8
MAIN_START = 2
MOD_COLS = 3072
VMEM_LIMIT = 56 * 1024 * 1024
NT_DIMS = (((1,), (1,)), ((), ()))

assert N_SUB * SUB <= STG_P <= STG_C and STG_C % LANES == 0 and N_SUB <= META_W


def _sizes(n_tok):
    n_tiles = n_tok // TB
    cap = -(-(n_tok + n_tiles * (SUB - 1) + TBM) // TBM) * TBM
    n_work = (n_tok + n_tiles * N_GROUPS * (SUB - 1)) // TBM + N_GROUPS
    return n_tiles, cap, n_work


def _sigmoid(x):
    return 1.0 / (1.0 + jnp.exp(-x))


def _rms_mod(x, g, sc, sh):
    ms = jnp.mean(x * x, axis=-1, keepdims=True)
    return (x * lax.rsqrt(ms + EPS)) * (g * (1.0 + sc)) + sh


def _onehot(cond):
    return jnp.where(cond, 1.0, 0.0).astype(BF16)


def _mod_kernel(ct_ref, w_ref, b_ref, o_ref, *, nb):
    j = pl.program_id(1)
    ct = ct_ref[...]
    cond = ct * _sigmoid(ct)
    vec_per_step = MOD_COLS // D_MODEL
    for b in range(nb):
        col = cond[:, b:b + 1]
        acc = jnp.zeros((8, MOD_COLS), F32)
        for r in range(D_MODEL // 8):
            acc = acc + w_ref[0, 8 * r:8 * r + 8, :] * col[8 * r:8 * r + 8]
        res = jnp.sum(acc, axis=0, keepdims=True) + b_ref[0]
        for jj in range(N_MOD // vec_per_step):
            @pl.when(j == jj)
            def _():
                for k in range(vec_per_step):
                    row = jj * vec_per_step + k
                    o_ref[0, b, row:row + 1, :] = res[:, k * D_MODEL:(k + 1) * D_MODEL]

        @pl.when(j == 0)
        def _():
            o_ref[0, b, N_MOD:8, :] = jnp.zeros((8 - N_MOD, D_MODEL), F32)


def _mod_call(c, w_ada, b_ada):
    nb = c.shape[0]
    ct = jnp.pad(c.T, ((0, 0), (0, LANES - nb)))
    n_col = (N_MOD * D_MODEL) // MOD_COLS
    return pl.pallas_call(
        functools.partial(_mod_kernel, nb=nb),
        grid=(DEPTH, n_col),
        in_specs=[
            pl.BlockSpec((D_MODEL, LANES), lambda l, j: (0, 0)),
            pl.BlockSpec((1, D_MODEL, MOD_COLS), lambda l, j: (l, 0, j)),
            pl.BlockSpec((1, 1, MOD_COLS), lambda l, j: (l, 0, j)),
        ],
        out_specs=pl.BlockSpec((1, nb, 8, D_MODEL), lambda l, j: (l, 0, 0, 0)),
        out_shape=jax.ShapeDtypeStruct((DEPTH, nb, 8, D_MODEL), F32),
        compiler_params=pltpu.CompilerParams(
            dimension_semantics=("arbitrary", "arbitrary"), vmem_limit_bytes=VMEM_LIMIT),
        name="adaln_mod",
    )(ct, w_ada, b_ada.reshape(DEPTH, 1, N_MOD * D_MODEL))


def _sub_copies(hbm_ref, stage_ref, sem, rows, to_hbm):
    out = []
    for s in range(N_SUB):
        s_view = stage_ref.at[pl.ds(s * SUB, SUB), :]
        h_view = hbm_ref.at[pl.ds(pl.multiple_of(rows[s], SUB), SUB), :]
        out.append(pltpu.make_async_copy(s_view, h_view, sem) if to_hbm
                   else pltpu.make_async_copy(h_view, s_view, sem))
    return out


def _head_dma(t, n_steps, nb, pm_ref, ys_hbm, ystage_ref, gsem):
    slot = t % 2

    def copies(step, b, slot_):
        base = (jnp.minimum(step, n_steps - 1) * nb + b) * META_W
        rows = [pm_ref[base + s] for s in range(N_SUB)]
        return _sub_copies(ys_hbm, ystage_ref.at[b, slot_], gsem.at[b, slot_], rows, False)

    @pl.when(t == 0)
    def _():
        ystage_ref[...] = jnp.zeros_like(ystage_ref)
        for b in range(nb):
            for cp in copies(0, b, 0):
                cp.start()

    nxt = t + 1
    for b in range(nb):
        for cp in copies(nxt, b, 1 - slot):
            cp.start()

    for b in range(nb):
        for cp in copies(t, b, slot):
            cp.wait()

    def drain():
        for b in range(nb):
            for cp in copies(nxt, b, 1 - slot):
                cp.wait()
    return drain


def _head_combine(b, slot, hprev_ref, modprev_ref, tokmeta_ref, ystage_ref):
    lp = tokmeta_ref[b, :, LP_LANE:LP_LANE + 1].astype(I32)
    unsort = _onehot(lax.broadcasted_iota(I32, (TB, STG_C), 1) == lp)
    y = jnp.dot(unsort, ystage_ref[b, slot], preferred_element_type=F32)
    return hprev_ref[b] + modprev_ref[b, 5:6] * y


def _route(u2, wrt_ref, br_ref):
    uh = u2.astype(BF16)
    ul = (u2 - uh.astype(F32)).astype(BF16)
    w = wrt_ref[...]
    r1 = lax.dot_general(w, uh, NT_DIMS, preferred_element_type=F32)
    r2 = lax.dot_general(w[0:N_EXPERTS], ul, NT_DIMS, preferred_element_type=F32)
    yield
    scores = _sigmoid(r1[0:N_EXPERTS] + r1[N_EXPERTS:] + r2)
    biased = scores + br_ref[...]
    s = [scores[4 * j:4 * j + 4] for j in range(EXPERTS_PER_GROUP)]
    b = [biased[4 * j:4 * j + 4] for j in range(EXPERTS_PER_GROUP)]

    hi1, lo1 = jnp.maximum(b[0], b[1]), jnp.minimum(b[0], b[1])
    hi2, lo2 = jnp.maximum(b[2], b[3]), jnp.minimum(b[2], b[3])
    second = jnp.maximum(jnp.minimum(hi1, hi2), jnp.where(hi1 >= hi2, lo1, lo2))
    gscore = jnp.maximum(hi1, hi2) + second

    best = gscore[0:1]
    bgrp = jnp.zeros((1, TB), I32)
    for g in range(1, N_GROUPS):
        better = gscore[g:g + 1] > best
        bgrp = jnp.where(better, g, bgrp)
        best = jnp.where(better, gscore[g:g + 1], best)
    selmask = lax.broadcasted_iota(I32, (N_GROUPS, TB), 0) == bgrp
    yield

    m1 = b[0]
    i1 = jnp.zeros((N_GROUPS, TB), I32)
    for j in range(1, EXPERTS_PER_GROUP):
        gt = b[j] > m1
        i1 = jnp.where(gt, j, i1)
        m1 = jnp.where(gt, b[j], m1)
    cands = [jnp.where(i1 == j, -jnp.inf, b[j]) for j in range(EXPERTS_PER_GROUP)]
    m2 = cands[0]
    i2 = jnp.zeros_like(i1)
    for j in range(1, EXPERTS_PER_GROUP):
        gt = cands[j] > m2
        i2 = jnp.where(gt, j, i2)
        m2 = jnp.where(gt, cands[j], m2)
    w1 = s[0]
    w2 = s[0]
    for j in range(1, EXPERTS_PER_GROUP):
        w1 = jnp.where(i1 == j, s[j], w1)
        w2 = jnp.where(i2 == j, s[j], w2)
    yield
    tot = w1 + w2
    w1n = w1 / tot
    w2n = w2 / tot
    gates = []
    for j in range(EXPERTS_PER_GROUP):
        gj = jnp.where(i1 == j, w1n, 0.0) + jnp.where(i2 == j, w2n, 0.0)
        gates.append(jnp.sum(jnp.where(selmask, gj, 0.0), axis=0, keepdims=True))
    return bgrp, gates, selmask


def _tail_compute(b, slot, h1, mod, g_ffn, wrt_ref, br_ref, tri_ref, tokmeta_ref, stage_ref):
    sh2, sc2 = mod[3:4], mod[4:5]
    u2 = _rms_mod(h1, g_ffn, sc2, sh2)
    bgrp, gates, selmask = yield from _route(u2, wrt_ref, br_ref)
    yield

    onehot = jnp.concatenate([jnp.where(selmask, 1.0, 0.0), jnp.zeros((8 - N_GROUPS, TB), F32)],
                             axis=0).astype(BF16)
    rank_incl = jnp.dot(onehot, tri_ref[...], preferred_element_type=F32)
    rank = jnp.sum(jnp.where(selmask, rank_incl[0:N_GROUPS], 0.0), axis=0, keepdims=True) - 1.0
    cnt = lax.dot_general(jnp.ones((8, TB), BF16), onehot, NT_DIMS,
                          preferred_element_type=F32)
    sub_rows = jnp.floor((cnt[0:1, :] + (SUB - 1.0)) * (1.0 / SUB)) * SUB
    lp = rank
    first_row = jnp.zeros((1, 1), F32)
    for g in range(1, N_GROUPS):
        first_row = first_row + sub_rows[:, g - 1:g]
        lp = lp + jnp.where(bgrp == g, first_row, 0.0)

    hi = [x.astype(BF16).astype(F32) for x in gates]
    lo = [(x - h).astype(BF16).astype(F32) for x, h in zip(gates, hi)]
    meta_src = jnp.concatenate(hi + lo + [lp, jnp.zeros((LANES - 9, TB), F32)], axis=0)
    meta_t = meta_src.T
    tokmeta_ref[b] = meta_t
    yield

    rowdata = jnp.concatenate([u2.astype(BF16), meta_t.astype(BF16)], axis=1)
    sort = _onehot(lax.broadcasted_iota(I32, (STG_P, TB), 0) == lp.astype(I32))
    stage_ref[b, slot] = jnp.dot(sort, rowdata, preferred_element_type=F32).astype(BF16)
    return [cnt[0, g].astype(I32) for g in range(N_GROUPS)]


def _prime_scatter(nb, cap, xs_hbm, stage_ref, prev_ref, ssem):
    stage_ref[:, 1] = jnp.zeros((nb,) + stage_ref.shape[2:], BF16)
    for b in range(nb):
        rows = [jnp.int32(N_GROUPS * cap + (b * N_SUB + s) * SUB) for s in range(N_SUB)]
        for s in range(N_SUB):
            prev_ref[b * N_SUB + s] = rows[s]
        for cp in _sub_copies(xs_hbm, stage_ref.at[b, 1], ssem.at[b, 1], rows, True):
            cp.start()


def _tail_dma(t, lag, n_steps, nb, cap, n_work, counts, tmeta_ref, xs_hbm, stage_ref, zeros_ref,
              run_ref, prev_ref, ssem, zsem, drain_head):
    slot = t % 2
    run = [run_ref[g] for g in range(N_GROUPS)]
    junk = N_GROUPS * cap
    scatter_rows = []
    for b in range(nb):
        first_sub = [jnp.int32(0)]
        for g in range(N_GROUPS):
            first_sub.append(first_sub[-1] + (counts[b][g] + (SUB - 1)) // SUB)
        dst = [g * cap + run[g] for g in range(N_GROUPS)]
        rows_b, base = [], (jnp.maximum(t - lag, 0) * nb + b) * META_W
        for s in range(N_SUB):
            grp_first, grp_dst = first_sub[0], dst[0]
            for g in range(1, N_GROUPS):
                later = s >= first_sub[g]
                grp_first = jnp.where(later, first_sub[g], grp_first)
                grp_dst = jnp.where(later, dst[g], grp_dst)
            row = grp_dst + (s - grp_first) * SUB
            used = s < first_sub[N_GROUPS]
            rows_b.append(jnp.where(used, row, junk + (b * N_SUB + s) * SUB))
            tmeta_ref[base + s] = jnp.where(used, row, row if s == 0 else first_row)
            if s == 0:
                first_row = row
        for k in range(N_SUB, META_W):
            tmeta_ref[base + k] = jnp.int32(0)
        for g in range(N_GROUPS):
            run[g] = run[g] + (first_sub[g + 1] - first_sub[g]) * SUB
        scatter_rows.append(rows_b)
    for g in range(N_GROUPS):
        run_ref[g] = run[g]

    def copies(b, slot_, rows):
        return _sub_copies(xs_hbm, stage_ref.at[b, slot_], ssem.at[b, slot_], rows, True)

    for b in range(nb):
        for cp in copies(b, 1 - slot, [prev_ref[b * N_SUB + s] for s in range(N_SUB)]):
            cp.wait()

    for b in range(nb):
        for cp in copies(b, slot, scatter_rows[b]):
            cp.start()
        for s in range(N_SUB):
            prev_ref[b * N_SUB + s] = scatter_rows[b][s]

    @pl.when(t == n_steps - 1 + lag)
    def _():
        for b in range(nb):
            for cp in copies(b, slot, scatter_rows[b]):
                cp.wait()
        if drain_head is not None:
            drain_head()
        zeros_ref[...] = jnp.zeros_like(zeros_ref)
        tails = []
        base = n_steps * nb * META_W
        for g in range(N_GROUPS):
            tmeta_ref[base + g] = run[g]
            start = pl.multiple_of(g * cap + run[g], SUB)
            tails.append(pltpu.make_async_copy(zeros_ref, xs_hbm.at[pl.ds(start, TBM), :], zsem))
        for k in range(N_GROUPS, META_W):
            tmeta_ref[base + k] = jnp.int32(0)
        for cp in tails:
            cp.start()
        _work_tables(run, cap, n_work, tmeta_ref, base + META_W)
        for cp in tails:
            cp.wait()


def _attn_conv_mix(x, t, mod, g_mix, win_ref, wout_ref, bias_ref, convw_ref,
                   kvprev_ref, cprev_ref):
    tb = TB
    first = t == 0
    sh1, sc1, g1 = mod[0:1], mod[1:2], mod[2:3]
    ub = _rms_mod(x, g_mix, sc1, sh1).astype(BF16)
    qkv_w = ATTN_WIDTH + 2 * KV_WIDTH
    proj = jnp.dot(ub, win_ref[:, 0:qkv_w], preferred_element_type=F32)

    yield
    q = (proj[:, 0:ATTN_WIDTH] * HEAD_DIM ** -0.5).astype(BF16)
    kf = proj[:, ATTN_WIDTH:ATTN_WIDTH + KV_WIDTH]
    vf = proj[:, ATTN_WIDTH + KV_WIDTH:qkv_w]

    kv_prev = kvprev_ref[...]
    kext = jnp.concatenate([kv_prev[:, 0:KV_WIDTH], kf], axis=0)
    vext = jnp.concatenate([kv_prev[:, KV_WIDTH:], vf], axis=0)
    kvprev_ref[:, 0:KV_WIDTH] = kf[tb - BLOCK:tb]
    kvprev_ref[:, KV_WIDTH:] = vf[tb - BLOCK:tb]

    lane = lax.broadcasted_iota(I32, kext.shape, 1)
    lo = lane < HEAD_DIM
    krol = pltpu.roll(kext, HEAD_DIM, axis=1)
    vrol = pltpu.roll(vext, HEAD_DIM, axis=1)
    zero = jnp.zeros_like(kext)
    one_at_64 = jnp.where(lane == HEAD_DIM, 1.0, 0.0)
    one_at_0 = jnp.where(lane == 0, 1.0, 0.0)
    k_ops = [(jnp.where(lo, kext, zero).astype(BF16), jnp.where(lo, zero, krol).astype(BF16)),
             (jnp.where(lo, krol, zero).astype(BF16), jnp.where(lo, zero, kext).astype(BF16))]
    v_ops = [(jnp.where(lo, vext, one_at_64).astype(BF16), jnp.where(lo, one_at_0, vrol).astype(BF16)),
             (jnp.where(lo, vrol, one_at_64).astype(BF16), jnp.where(lo, one_at_0, vext).astype(BF16))]

    col = lax.broadcasted_iota(I32, (2 * BLOCK, 4 * BLOCK), 1) % (2 * BLOCK)
    masked_at_start = (col < BLOCK) & (col > 0)
    sink_key = lax.broadcasted_iota(I32, (4 * BLOCK, 1), 0) % (2 * BLOCK) == 0
    sink_val = lax.broadcasted_iota(I32, (2 * BLOCK, 1), 0) == 0
    lane_o = lax.broadcasted_iota(I32, (2 * BLOCK, LANES), 1)
    lo_o = lane_o < HEAD_DIM
    ones_entry = [one_at_64[0:1].astype(BF16), one_at_0[0:1].astype(BF16)]

    yield
    units = [(bi, kvh) for bi in range(tb // BLOCK) for kvh in range(N_HEADS // 4)]

    def scores(bi, kvh):
        r0 = bi * BLOCK
        ka, kb = k_ops[kvh]
        rhs = jnp.concatenate([ka[r0:r0 + 2 * BLOCK], kb[r0:r0 + 2 * BLOCK]], axis=0)
        rhs = jnp.where(sink_key, jnp.zeros_like(rhs), rhs)
        qrows = q[r0:r0 + BLOCK]
        qp = jnp.concatenate([qrows[:, (2 * kvh) * LANES:(2 * kvh + 1) * LANES],
                              qrows[:, (2 * kvh + 1) * LANES:(2 * kvh + 2) * LANES]], axis=0)
        s = lax.dot_general(qp, rhs, NT_DIMS, preferred_element_type=F32)
        bias = bias_ref[kvh]
        if bi == 0:
            bias = jnp.where(masked_at_start & first, NEG_BIG, bias)
        return s + bias

    side = {}

    def conv_proj():
        side["gates"] = jnp.dot(ub, win_ref[:, qkv_w:IN_WIDTH], preferred_element_type=F32)

    def conv():
        g = side["gates"]
        bgate, cgate, xv = (g[:, i * CONV_WIDTH:(i + 1) * CONV_WIDTH] for i in range(3))
        uc = cgate * xv
        cprev = jnp.where(first, 0.0, cprev_ref[...])
        row = lax.broadcasted_iota(I32, uc.shape, 0)
        r1 = jnp.where(row == 0, cprev[7:8], pltpu.roll(uc, 1, axis=0))
        r2 = jnp.where(row == 0, cprev[6:7],
                       jnp.where(row == 1, cprev[7:8], pltpu.roll(uc, 2, axis=0)))
        cprev_ref[...] = uc[tb - 8:tb]
        cw = convw_ref[...]
        side["conv"] = (bgate * (cw[0:1] * r2 + cw[1:2] * r1 + cw[2:3] * uc)).astype(BF16)

    def conv_out():
        side["mix"] = jnp.dot(side["conv"], wout_ref[ATTN_WIDTH:, :], preferred_element_type=F32)

    fillers = [conv_proj, conv, conv_out]

    attn_rows = []
    s_next = scores(*units[0])
    for ui, (bi, kvh) in enumerate(units):
        r0 = bi * BLOCK
        if kvh == 0:
            pair_out = []
        s = s_next
        if ui + 1 < len(units):
            s_next = scores(*units[ui + 1])
        if ui < len(fillers):
            fillers[ui]()
        shs = [s[:, hh * 2 * BLOCK:(hh + 1) * 2 * BLOCK] for hh in range(2)]
        ms = [jnp.max(sh, axis=-1, keepdims=True) for sh in shs]
        ps = [jnp.exp(sh - m).astype(BF16) for sh, m in zip(shs, ms)]
        vops = [jnp.where(sink_val, ones_entry[hh], v_ops[kvh][hh][r0:r0 + 2 * BLOCK])
                for hh in range(2)]
        os_ = [jnp.dot(p, vop, preferred_element_type=F32) for p, vop in zip(ps, vops)]
        outs = []
        for hh in range(2):
            den_col = HEAD_DIM if hh == 0 else 0
            outs.append(os_[hh] / os_[hh][:, den_col:den_col + 1])
        both = jnp.where(lo_o, outs[0], outs[1])
        pair_out += [both[0:BLOCK], both[BLOCK:]]
        if kvh == N_HEADS // 4 - 1:
            attn_rows.append(jnp.concatenate(pair_out, axis=1))
        yield
    attn = jnp.concatenate(attn_rows, axis=0)

    mix = side["mix"] + jnp.dot(attn.astype(BF16), wout_ref[0:ATTN_WIDTH, :],
                                preferred_element_type=F32)
    return x + g1 * mix


def _pool_mix(x, t, mod, g_mix, wpool_ref, band_ref, pscale, uprev_ref):
    tb = TB
    sh1, sc1, g1 = mod[0:1], mod[1:2], mod[2:3]
    u = _rms_mod(x, g_mix, sc1, sh1)
    halo = jnp.where(t == 0, 0.0, uprev_ref[...]).astype(BF16)
    ext = jnp.concatenate([jnp.zeros((LANES - POOL_HALO, D_MODEL), BF16), halo, u.astype(BF16)],
                          axis=0)
    uprev_ref[...] = u[tb - POOL_HALO:tb]

    yield
    pos = (t * tb + 1 + lax.broadcasted_iota(I32, (tb, 1), 0)).astype(F32)
    cols = [slice(gi * POOL_GROUP, (gi + 1) * POOL_GROUP) for gi in range(len(POOL_SIZES))]
    wsums = [jnp.dot(band_ref[gi], ext[:, sl], preferred_element_type=F32)
             for gi, sl in enumerate(cols)]
    yield
    mixed = []
    for gi, w in enumerate(POOL_SIZES):
        mean = wsums[gi] / jnp.minimum(pos, float(w))
        pooled = mean - u[:, cols[gi]]
        mixed.append(jnp.dot(pooled.astype(BF16), wpool_ref[gi].astype(BF16),
                             preferred_element_type=F32))
    yield
    mix = jnp.concatenate(mixed, axis=1) * pscale
    return x + g1 * mix


def _interleave(chains, starts):
    results = [None] * len(chains)
    live = list(range(len(chains)))
    rnd = 0
    while live:
        for k in list(live):
            if rnd < starts[k]:
                continue
            try:
                next(chains[k])
            except StopIteration as done:
                results[k] = done.value
                live.remove(k)
        rnd += 1
    return results


def _mixer_kernel(*refs, kind, layer, lag, has_head, nb, n_steps, cap, n_work):
    refs = list(refs)
    t = pl.program_id(0)
    slot = t % 2
    if has_head:
        pm_ref = refs.pop(0)
        hprev_ref, modprev_ref, tokprev_ref, ys_hbm = refs[:4]
        refs = refs[4:]
    else:
        h_ref = refs.pop(0)
    mod_ref, nmix_ref, nffn_ref = refs[:3]
    refs = refs[3:]
    n_w = 5 if kind == "even" else 3
    weight_refs = refs[:n_w]
    refs = refs[n_w:]
    wrt_ref, br_ref, tri_ref = refs[:3]
    refs = refs[3:]
    hout_ref, tokmeta_ref, tmeta_ref, xs_hbm = refs[:4]
    refs = refs[4:]
    n_c = 2 if kind == "even" else 1
    carry_refs = refs[:n_c]
    refs = refs[n_c:]
    if kind == "even":
        cast_refs = refs[:2]
        sbias_ref = refs[2]
        refs = refs[3:]
    if lag:
        h1s_ref = refs.pop(0)
    stage_ref, zeros_ref, run_ref, prev_ref, ssem, zsem = refs[:6]
    refs = refs[6:]

    tile = jnp.minimum(t, n_steps - 1)

    @pl.when(t == 0)
    def _():
        if lag:
            h1s_ref[...] = jnp.zeros_like(h1s_ref)
        for r in carry_refs:
            r[...] = jnp.zeros_like(r)
        for g in range(N_GROUPS):
            run_ref[g] = jnp.int32(0)
        if kind == "even":
            for src, dst in zip(weight_refs[:2], cast_refs):
                for c0 in range(0, src.shape[1], 2 * LANES):
                    dst[:, c0:c0 + 2 * LANES] = src[:, c0:c0 + 2 * LANES].astype(BF16)
            bias_ref, sink_ref = weight_refs[2:4]
            col = lax.broadcasted_iota(I32, (2 * BLOCK, 4 * BLOCK), 1)
            upper = lax.broadcasted_iota(I32, (2 * BLOCK, 1), 0) < BLOCK
            for kvh in range(N_HEADS // 4):
                tab = bias_ref[kvh]
                for hh in range(2):
                    sink = jnp.where(upper, sink_ref[layer // 2, 4 * kvh + hh],
                                     sink_ref[layer // 2, 4 * kvh + 2 + hh])
                    tab = jnp.where(col == hh * 2 * BLOCK, sink, tab)
                sbias_ref[kvh] = tab
        _prime_scatter(nb, cap, xs_hbm, stage_ref, prev_ref, ssem)

    j = layer // 2
    g_mix = nmix_ref[layer:layer + 1]
    g_ffn = nffn_ref[layer:layer + 1]
    if kind == "even":
        weight_refs = list(cast_refs) + [sbias_ref, weight_refs[4]]
    else:
        weight_refs = list(weight_refs[:2]) + [weight_refs[2][j:j + 1]]

    drain_head = None
    if has_head:
        ystage_ref, gsem = refs
        drain_head = _head_dma(t, n_steps, nb, pm_ref, ys_hbm, ystage_ref, gsem)

    def main_chain(b):
        if has_head:
            x = _head_combine(b, slot, hprev_ref, modprev_ref, tokprev_ref, ystage_ref)
            yield
        else:
            x = h_ref[b]
        carries = [r.at[b] for r in carry_refs]
        mix = _attn_conv_mix if kind == "even" else _pool_mix
        h1 = yield from mix(x, tile, mod_ref[b], g_mix, *weight_refs, *carries)
        hout_ref[b] = h1
        if lag:
            h1s_ref[b, slot] = h1
        yield
        return h1

    def tail_chain(b, h1):
        return (yield from _tail_compute(b, slot, h1, mod_ref[b], g_ffn, wrt_ref, br_ref,
                                         tri_ref, tokmeta_ref, stage_ref))

    def lagged_tail_chain(b):
        cnt = yield from tail_chain(b, h1s_ref[b, 1 - slot])
        return [jnp.where(t >= lag, c, 0) for c in cnt]

    def whole_chain(b):
        h1 = yield from main_chain(b)
        return (yield from tail_chain(b, h1))

    if lag:
        chains = [main_chain(b) for b in range(nb)] + [lagged_tail_chain(b) for b in range(nb)]
        counts = _interleave(chains, [MAIN_START] * nb + [0] * nb)[nb:]
    else:
        counts = _interleave([whole_chain(b) for b in range(nb)], [0] * nb)
    _tail_dma(t, lag, n_steps, nb, cap, n_work, counts, tmeta_ref, xs_hbm, stage_ref, zeros_ref,
              run_ref, prev_ref, ssem, zsem, drain_head)


def _nat(nb, cols):
    return pl.BlockSpec((nb, TB, cols), lambda t, *_: (0, t, 0))


def _full(shape):
    nd = len(shape)
    return pl.BlockSpec(shape, lambda t, *_: (0,) * nd)


def _head_scratch(nb):
    return [pltpu.VMEM((nb, 2, STG_C, D_MODEL), BF16), pltpu.SemaphoreType.DMA((nb, 2))]


def _mod_spec(nb, layer):
    return pl.BlockSpec((None, nb, 8, D_MODEL), lambda t, *_: (layer, 0, 0, 0))


def _mixer_call(kind, layer, head, h, mod_all, norm_mix, norm_ffn, weights, wrt, br, tri):
    j = layer // 2
    nb, seq = (h if head is None else head[2]).shape[:2]
    n_steps = seq // TB
    n_tiles, cap, n_work = _sizes(nb * seq)
    has_head = head is not None
    any_spec = pl.BlockSpec(memory_space=pl.ANY)
    smem_spec = pl.BlockSpec(memory_space=pltpu.SMEM)

    def main(cols):
        return pl.BlockSpec((nb, TB, cols), lambda t, *_: (0, jnp.minimum(t, n_steps - 1), 0))

    lag = 1 if kind == "odd" else 0

    def lagged(cols):
        return pl.BlockSpec((nb, TB, cols), lambda t, *_: (0, jnp.maximum(t - lag, 0), 0))

    if has_head:
        pm, hprev, tokprev, ys = head
        args = [hprev, mod_all, tokprev, ys]
        in_specs = [main(D_MODEL), _mod_spec(nb, layer - 1), main(LANES), any_spec]
    else:
        args = [h]
        in_specs = [main(D_MODEL)]
    args += [mod_all, norm_mix, norm_ffn]
    in_specs += [_mod_spec(nb, layer), _full(norm_mix.shape), _full(norm_ffn.shape)]
    if kind == "even":
        args += list(weights)
        once = dict(pipeline_mode=pl.Buffered(1))
        in_specs += [pl.BlockSpec((None, D_MODEL, IN_WIDTH), lambda t, *_: (j, 0, 0), **once),
                     pl.BlockSpec((None, D_MODEL, D_MODEL), lambda t, *_: (j, 0, 0), **once),
                     _full((N_HEADS // 4, 2 * BLOCK, 4 * BLOCK)), smem_spec,
                     pl.BlockSpec((None,) + weights[4].shape[1:], lambda t, *_: (j, 0, 0))]
        mix_scratch = [pltpu.VMEM((nb, BLOCK, 2 * KV_WIDTH), F32),
                       pltpu.VMEM((nb, 8, CONV_WIDTH), F32),
                       pltpu.VMEM((D_MODEL, IN_WIDTH), BF16),
                       pltpu.VMEM((D_MODEL, D_MODEL), BF16),
                       pltpu.VMEM((N_HEADS // 4, 2 * BLOCK, 4 * BLOCK), F32)]
    else:
        args += list(weights)
        in_specs += [pl.BlockSpec((None, len(POOL_SIZES), POOL_GROUP, POOL_GROUP),
                                  lambda t, *_: (j, 0, 0, 0)),
                     _full((len(POOL_SIZES), TB, TB + LANES)), _full(weights[2].shape)]
        mix_scratch = [pltpu.VMEM((nb, POOL_HALO, D_MODEL), F32)]
    args += [wrt, br, tri]
    in_specs += [_full((2 * N_EXPERTS, D_MODEL)), _full((N_EXPERTS, 1)), _full((TB, TB))]

    out_shape = (jax.ShapeDtypeStruct((nb, seq + lag * TB, D_MODEL), F32),
                 jax.ShapeDtypeStruct((nb, seq, LANES), F32),
                 jax.ShapeDtypeStruct(((n_tiles + 1) * META_W + 4 * n_work,), I32),
                 jax.ShapeDtypeStruct((N_GROUPS * cap + nb * N_SUB * SUB, ROW_W), BF16))
    out_specs = (_nat(nb, D_MODEL), lagged(LANES), smem_spec, any_spec)
    scratch = mix_scratch + ([pltpu.VMEM((nb, 2, TB, D_MODEL), F32)] if lag else []) + [
        pltpu.VMEM((nb, 2, STG_P, ROW_W), BF16),
        pltpu.VMEM((TBM, ROW_W), BF16),
        pltpu.SMEM((N_GROUPS,), I32),
        pltpu.SMEM((nb * N_SUB,), I32),
        pltpu.SemaphoreType.DMA((nb, 2)),
        pltpu.SemaphoreType.DMA(()),
    ]
    if has_head:
        scratch += _head_scratch(nb)

    body = functools.partial(_mixer_kernel, kind=kind, layer=layer, lag=lag, has_head=has_head,
                             nb=nb, n_steps=n_steps, cap=cap, n_work=n_work)
    grid_spec = pltpu.PrefetchScalarGridSpec(
        num_scalar_prefetch=1 if has_head else 0, grid=(n_steps + lag,),
        in_specs=in_specs, out_specs=out_specs, scratch_shapes=scratch)
    call = pl.pallas_call(
        body, grid_spec=grid_spec, out_shape=out_shape,
        compiler_params=pltpu.CompilerParams(
            dimension_semantics=("arbitrary",), vmem_limit_bytes=VMEM_LIMIT),
        name=kind + "_mixer")
    if has_head:
        return call(pm, *args)
    return call(*args)


def _work_tables(totals, cap, n_work, tmeta_ref, t0):
    ends, starts = [], []
    acc = jnp.int32(0)
    for g in range(N_GROUPS):
        starts.append(acc)
        acc = acc + (totals[g] + (TBM - 1)) // TBM
        ends.append(acc)
    nvalid = ends[-1]
    prev_grp = jnp.int32(-1)
    for i in range(n_work):
        idc = jnp.minimum(jnp.int32(i), nvalid - 1)
        grp, start, total = jnp.int32(0), starts[0], totals[0]
        for g in range(1, N_GROUPS):
            later = idc >= ends[g - 1]
            grp = jnp.where(later, g, grp)
            start = jnp.where(later, starts[g], start)
            total = jnp.where(later, totals[g], total)
        valid = i < nvalid
        tile = idc - start
        tmeta_ref[t0 + i] = grp * (cap // TBM) + tile
        tmeta_ref[t0 + n_work + i] = grp
        tmeta_ref[t0 + 2 * n_work + i] = jnp.where(valid, jnp.minimum(total - tile * TBM, TBM), 0)
        tmeta_ref[t0 + 3 * n_work + i] = jnp.logical_and(valid, grp != prev_grp).astype(I32)
        prev_grp = grp


def _expert_rows(xs_ref, ys_ref, wg_s, wu_s, wd_s, r0, nrows):
    xb = xs_ref[r0:r0 + nrows, 0:D_MODEL]
    meta = xs_ref[r0:r0 + nrows, D_MODEL:ROW_W].astype(F32)
    gates = meta[:, 0:EXPERTS_PER_GROUP] + meta[:, EXPERTS_PER_GROUP:2 * EXPERTS_PER_GROUP]
    parts = []
    for e in range(EXPERTS_PER_GROUP):
        a = jnp.dot(xb, wg_s[e], preferred_element_type=F32)
        bu = jnp.dot(xb, wu_s[e], preferred_element_type=F32)
        hid = (a * _sigmoid(a)) * bu
        parts.append((hid * gates[:, e:e + 1]).astype(BF16))
    hid_all = jnp.concatenate(parts, axis=1)
    ys_ref[r0:r0 + nrows, :] = jnp.dot(hid_all, wd_s[...],
                                       preferred_element_type=F32).astype(BF16)


def _moe_kernel(tm_ref, xs_ref, wg_ref, wu_ref, wd_ref, ys_ref, wg_s, wu_s, wd_s, *, t0, n_work):
    i = pl.program_id(0)
    rows = tm_ref[t0 + 2 * n_work + i]
    first_of_group = tm_ref[t0 + 3 * n_work + i] == 1

    @pl.when(first_of_group)
    def _():
        for e in range(EXPERTS_PER_GROUP):
            wg_s[e] = wg_ref[e].astype(BF16)
            wu_s[e] = wu_ref[e].astype(BF16)
        wd_s[...] = wd_ref[0].astype(BF16)

    @pl.when(rows == TBM)
    def _():
        _expert_rows(xs_ref, ys_ref, wg_s, wu_s, wd_s, 0, TBM)

    @pl.when(jnp.logical_and(rows > 0, rows < TBM))
    def _():
        for r0 in range(0, TBM, TBM_PART):
            @pl.when(r0 < rows)
            def _():
                _expert_rows(xs_ref, ys_ref, wg_s, wu_s, wd_s, r0, TBM_PART)

            @pl.when(r0 >= rows)
            def _():
                ys_ref[r0:r0 + TBM_PART, :] = jnp.zeros((TBM_PART, D_MODEL), BF16)


def _moe_call(xs, tmeta, layer, wg, wu, wd, n_tiles, n_work):
    t0 = (n_tiles + 1) * META_W
    row_map = lambda i, tm: (tm[t0 + i], 0)
    grp_map = lambda i, tm: (layer, tm[t0 + n_work + i], 0, 0)
    grid_spec = pltpu.PrefetchScalarGridSpec(
        num_scalar_prefetch=1, grid=(n_work,),
        in_specs=[
            pl.BlockSpec((TBM, ROW_W), row_map),
            pl.BlockSpec((None, EXPERTS_PER_GROUP, D_MODEL, EXPERT_FF), grp_map),
            pl.BlockSpec((None, EXPERTS_PER_GROUP, D_MODEL, EXPERT_FF), grp_map),
            pl.BlockSpec((None, 1, GROUP_FF, D_MODEL), grp_map),
        ],
        out_specs=pl.BlockSpec((TBM, D_MODEL), row_map),
        scratch_shapes=[pltpu.VMEM((EXPERTS_PER_GROUP, D_MODEL, EXPERT_FF), BF16),
                        pltpu.VMEM((EXPERTS_PER_GROUP, D_MODEL, EXPERT_FF), BF16),
                        pltpu.VMEM((GROUP_FF, D_MODEL), BF16)],
    )
    return pl.pallas_call(
        functools.partial(_moe_kernel, t0=t0, n_work=n_work), grid_spec=grid_spec,
        out_shape=jax.ShapeDtypeStruct((xs.shape[0], D_MODEL), BF16),
        compiler_params=pltpu.CompilerParams(
            dimension_semantics=("arbitrary",), vmem_limit_bytes=VMEM_LIMIT),
        name="moe_experts",
    )(tmeta, xs, wg, wu, wd)


def _final_kernel(pm_ref, hprev_ref, modprev_ref, tokprev_ref, ys_hbm, nf_ref, o_ref,
                  ystage_ref, gsem, *, nb, n_steps):
    t = pl.program_id(0)
    drain = _head_dma(t, n_steps, nb, pm_ref, ys_hbm, ystage_ref, gsem)
    for b in range(nb):
        h = _head_combine(b, t % 2, hprev_ref, modprev_ref, tokprev_ref, ystage_ref)
        ms = jnp.mean(h * h, axis=-1, keepdims=True)
        o_ref[b] = (h * lax.rsqrt(ms + EPS)) * nf_ref[0:1]
    pl.when(t == n_steps - 1)(drain)


def _final_call(head, mod_all, norm_final):
    pm, hprev, tokprev, ys = head
    nb, seq = tokprev.shape[:2]
    n_steps = seq // TB
    grid_spec = pltpu.PrefetchScalarGridSpec(
        num_scalar_prefetch=1, grid=(n_steps,),
        in_specs=[_nat(nb, D_MODEL), _mod_spec(nb, DEPTH - 1), _nat(nb, LANES),
                  pl.BlockSpec(memory_space=pl.ANY), _full((1, D_MODEL))],
        out_specs=_nat(nb, D_MODEL),
        scratch_shapes=_head_scratch(nb))
    return pl.pallas_call(
        functools.partial(_final_kernel, nb=nb, n_steps=n_steps), grid_spec=grid_spec,
        out_shape=jax.ShapeDtypeStruct((nb, seq, D_MODEL), F32),
        compiler_params=pltpu.CompilerParams(
            dimension_semantics=("arbitrary",), vmem_limit_bytes=VMEM_LIMIT),
        name="final_norm",
    )(pm, hprev, mod_all, tokprev, ys, norm_final.reshape(1, D_MODEL))


def _pool_band_table():
    r = jnp.arange(TB)[:, None]
    c = jnp.arange(TB + LANES)[None, :]
    return jnp.stack([(c > r + LANES - w) & (c <= r + LANES) for w in POOL_SIZES]).astype(BF16)


def _attn_bias_table():
    slopes = 2.0 ** (-8.0 * (jnp.arange(N_HEADS, dtype=F32) + 1.0) / N_HEADS)
    dist = (jnp.arange(BLOCK)[:, None] + BLOCK) - jnp.arange(2 * BLOCK)[None, :]
    ok = (dist >= 0) & (dist < WINDOW)
    per_head = jnp.where(ok[None], -slopes[:, None, None] * dist.astype(F32)[None], NEG_BIG)
    return per_head.reshape(N_HEADS // 2, 2, BLOCK, 2 * BLOCK).transpose(0, 2, 1, 3).reshape(
        N_HEADS // 4, 2 * BLOCK, 4 * BLOCK)


def _router_operands(w_router, b_router):
    def reorder(a):
        return a.reshape(N_GROUPS, EXPERTS_PER_GROUP, -1).transpose(1, 0, 2).reshape(N_EXPERTS, -1)
    w = reorder(w_router.T)
    wh = w.astype(BF16)
    wl = (w - wh.astype(F32)).astype(BF16)
    return jnp.concatenate([wh, wl], axis=0), reorder(b_router[:, None])


def kernel(x, c, w_ada, b_ada, norm_mix, norm_ffn, w_in, w_out, sinks, conv_w, w_pool,
           pool_scale, w_router, b_router, w_gate, w_up, w_down, norm_final):
    b, s, _ = x.shape
    n_tiles, cap, n_work = _sizes(b * s)

    mod_all = _mod_call(c, w_ada, b_ada)
    bias_tab = _attn_bias_table()
    band_tab = _pool_band_table()
    wrt, br = _router_operands(w_router, b_router)
    tri = jnp.triu(jnp.ones((TB, TB), F32)).astype(BF16)
    w_down4 = w_down.reshape(DEPTH, N_GROUPS, GROUP_FF, D_MODEL)

    h = x
    head = None
    for layer in range(DEPTH):
        if layer % 2 == 0:
            kind = "even"
            weights = (w_in, w_out, bias_tab, sinks, conv_w)
        else:
            kind = "odd"
            weights = (w_pool, band_tab, pool_scale)
        h1, tokmeta, tmeta, xs = _mixer_call(kind, layer, head, h, mod_all, norm_mix, norm_ffn,
                                             weights, wrt, br, tri)
        ys = _moe_call(xs, tmeta, layer, w_gate, w_up, w_down4, n_tiles, n_work)
        head = (tmeta, h1, tokmeta, ys)
    return _final_call(head, mod_all, norm_final)
```

```python
import functools

import jax
import jax.numpy as jnp
from jax import lax
from jax.experimental import pallas as pl
from jax.experimental.pallas import tpu as pltpu

F32 = jnp.float32
BF16 = jnp.bfloat16
I32 = jnp.int32

D_MODEL = 1024
DEPTH = 4
EPS = 1e-6
N_MOD = 6

ATTN_WIDTH = 512
HEAD_DIM = 64
N_HEADS = 8
KV_WIDTH = 128
WINDOW = 128
BLOCK = 128
CONV_WIDTH = 512
IN_WIDTH = 2304

POOL_SIZES = (2, 4, 8, 16)
POOL_GROUP = 256
POOL_HALO = 16

N_EXPERTS = 16
N_GROUPS = 4
EXPERTS_PER_GROUP = 4
EXPERT_FF = 256
GROUP_FF = EXPERTS_PER_GROUP * EXPERT_FF

LANES = 128
NEG_BIG = -1e30

TB = 256
SUB = 16
N_SUB = TB // SUB + N_GROUPS - 1
STG_P = 320
STG_C = 384
TBM = 1024
TBM_PART = 256
ROW_W = D_MODEL + LANES
META_W = 32
LP_LANE = 8
MAIN_START = 2
VMEM_LIMIT = 56 * 1024 * 1024
NT_DIMS = (((1,), (1,)), ((), ()))

assert N_SUB * SUB <= STG_P <= STG_C and STG_C % LANES == 0 and N_SUB <= META_W


def _sizes(n_tok):
    n_tiles = n_tok // TB
    cap = -(-(n_tok + n_tiles * (SUB - 1) + TBM) // TBM) * TBM
    n_work = (n_tok + n_tiles * N_GROUPS * (SUB - 1)) // TBM + N_GROUPS
    return n_tiles, cap, n_work


def _sigmoid(x):
    return 1.0 / (1.0 + jnp.exp(-x))


def _rms_mod(x, g, sc, sh):
    ms = jnp.mean(x * x, axis=-1, keepdims=True)
    return (x * lax.rsqrt(ms + EPS)) * (g * (1.0 + sc)) + sh


def _onehot(cond):
    return jnp.where(cond, 1.0, 0.0).astype(BF16)


def _mod_rows(ct_ref, w_ref, b_ref, nb):
    ct = ct_ref[...]
    cond = ct * _sigmoid(ct)
    cols = [cond[:, b:b + 1] for b in range(nb)]
    accs = [jnp.zeros((8, w_ref.shape[-1]), F32) for _ in range(nb)]
    for r in range(D_MODEL // 8):
        slab = w_ref[8 * r:8 * r + 8, :]
        accs = [acc + slab * col[8 * r:8 * r + 8] for acc, col in zip(accs, cols)]
    return [jnp.sum(acc, axis=0, keepdims=True) + b_ref[...] for acc in accs]


def _mod_store(mod_ref, rows, k):
    sub = lax.broadcasted_iota(I32, (8, D_MODEL), 0)
    for b, row in enumerate(rows):
        mod_ref[b] = jnp.where(sub == k, row, mod_ref[b])


def _mod_kernel(ct_ref, w_ref, b_ref, o_ref, *, nb):
    k = pl.program_id(0)

    @pl.when(k == 0)
    def _():
        o_ref[...] = jnp.zeros_like(o_ref)

    _mod_store(o_ref, _mod_rows(ct_ref, w_ref, b_ref, nb), k)


def _mod_call(ct, nb, w_ada, b_ada3, layer):
    return pl.pallas_call(
        functools.partial(_mod_kernel, nb=nb),
        grid=(N_MOD,),
        in_specs=[
            pl.BlockSpec((D_MODEL, LANES), lambda k: (0, 0)),
            pl.BlockSpec((None, D_MODEL, D_MODEL), lambda k: (layer, 0, k)),
            pl.BlockSpec((None, 1, D_MODEL), lambda k: (layer, 0, k)),
        ],
        out_specs=pl.BlockSpec((nb, 8, D_MODEL), lambda k: (0, 0, 0)),
        out_shape=jax.ShapeDtypeStruct((nb, 8, D_MODEL), F32),
        compiler_params=pltpu.CompilerParams(
            dimension_semantics=("arbitrary",), vmem_limit_bytes=VMEM_LIMIT),
        name="adaln_mod",
    )(ct, w_ada, b_ada3)


def _sub_copies(hbm_ref, stage_ref, sem, rows, to_hbm):
    out = []
    for s in range(N_SUB):
        s_view = stage_ref.at[pl.ds(s * SUB, SUB), :]
        h_view = hbm_ref.at[pl.ds(pl.multiple_of(rows[s], SUB), SUB), :]
        out.append(pltpu.make_async_copy(s_view, h_view, sem) if to_hbm
                   else pltpu.make_async_copy(h_view, s_view, sem))
    return out


def _head_dma(t, n_steps, nb, pm_ref, ys_hbm, ystage_ref, gsem):
    slot = t % 2

    def copies(step, b, slot_):
        base = (jnp.minimum(step, n_steps - 1) * nb + b) * META_W
        rows = [pm_ref[base + s] for s in range(N_SUB)]
        return _sub_copies(ys_hbm, ystage_ref.at[b, slot_], gsem.at[b, slot_], rows, False)

    @pl.when(t == 0)
    def _():
        ystage_ref[...] = jnp.zeros_like(ystage_ref)
        for b in range(nb):
            for cp in copies(0, b, 0):
                cp.start()

    nxt = t + 1
    for b in range(nb):
        for cp in copies(nxt, b, 1 - slot):
            cp.start()

    for b in range(nb):
        for cp in copies(t, b, slot):
            cp.wait()

    def drain():
        for b in range(nb):
            for cp in copies(nxt, b, 1 - slot):
                cp.wait()
    return drain


def _head_combine(b, slot, hprev_ref, modprev_ref, tokmeta_ref, ystage_ref):
    lp = tokmeta_ref[b, :, LP_LANE:LP_LANE + 1].astype(I32)
    unsort = _onehot(lax.broadcasted_iota(I32, (TB, STG_C), 1) == lp)
    y = jnp.dot(unsort, ystage_ref[b, slot], preferred_element_type=F32)
    return hprev_ref[b] + modprev_ref[b, 5:6] * y


def _route(u2, wrt_ref, br_ref):
    uh = u2.astype(BF16)
    ul = (u2 - uh.astype(F32)).astype(BF16)
    w = wrt_ref[...]
    r1 = lax.dot_general(w, uh, NT_DIMS, preferred_element_type=F32)
    r2 = lax.dot_general(w[0:N_EXPERTS], ul, NT_DIMS, preferred_element_type=F32)
    yield
    scores = _sigmoid(r1[0:N_EXPERTS] + r1[N_EXPERTS:] + r2)
    biased = scores + br_ref[...]
    s = [scores[4 * j:4 * j + 4] for j in range(EXPERTS_PER_GROUP)]
    b = [biased[4 * j:4 * j + 4] for j in range(EXPERTS_PER_GROUP)]

    hi1, lo1 = jnp.maximum(b[0], b[1]), jnp.minimum(b[0], b[1])
    hi2, lo2 = jnp.maximum(b[2], b[3]), jnp.minimum(b[2], b[3])
    second = jnp.maximum(jnp.minimum(hi1, hi2), jnp.where(hi1 >= hi2, lo1, lo2))
    gscore = jnp.maximum(hi1, hi2) + second

    best = gscore[0:1]
    bgrp = jnp.zeros((1, TB), I32)
    for g in range(1, N_GROUPS):
        better = gscore[g:g + 1] > best
        bgrp = jnp.where(better, g, bgrp)
        best = jnp.where(better, gscore[g:g + 1], best)
    selmask = lax.broadcasted_iota(I32, (N_GROUPS, TB), 0) == bgrp
    yield

    m1 = b[0]
    i1 = jnp.zeros((N_GROUPS, TB), I32)
    for j in range(1, EXPERTS_PER_GROUP):
        gt = b[j] > m1
        i1 = jnp.where(gt, j, i1)
        m1 = jnp.where(gt, b[j], m1)
    cands = [jnp.where(i1 == j, -jnp.inf, b[j]) for j in range(EXPERTS_PER_GROUP)]
    m2 = cands[0]
    i2 = jnp.zeros_like(i1)
    for j in range(1, EXPERTS_PER_GROUP):
        gt = cands[j] > m2
        i2 = jnp.where(gt, j, i2)
        m2 = jnp.where(gt, cands[j], m2)
    w1 = s[0]
    w2 = s[0]
    for j in range(1, EXPERTS_PER_GROUP):
        w1 = jnp.where(i1 == j, s[j], w1)
        w2 = jnp.where(i2 == j, s[j], w2)
    yield
    tot = w1 + w2
    w1n = w1 / tot
    w2n = w2 / tot
    gates = []
    for j in range(EXPERTS_PER_GROUP):
        gj = jnp.where(i1 == j, w1n, 0.0) + jnp.where(i2 == j, w2n, 0.0)
        gates.append(jnp.sum(jnp.where(selmask, gj, 0.0), axis=0, keepdims=True))
    return bgrp, gates, selmask


def _tail_compute(b, slot, h1, mod, g_ffn, wrt_ref, br_ref, tri_ref, tokmeta_ref, stage_ref):
    sh2, sc2 = mod[3:4], mod[4:5]
    u2 = _rms_mod(h1, g_ffn, sc2, sh2)
    bgrp, gates, selmask = yield from _route(u2, wrt_ref, br_ref)
    yield

    onehot = jnp.concatenate([jnp.where(selmask, 1.0, 0.0), jnp.zeros((8 - N_GROUPS, TB), F32)],
                             axis=0).astype(BF16)
    rank_incl = jnp.dot(onehot, tri_ref[...], preferred_element_type=F32)
    rank = jnp.sum(jnp.where(selmask, rank_incl[0:N_GROUPS], 0.0), axis=0, keepdims=True) - 1.0
    cnt = lax.dot_general(jnp.ones((8, TB), BF16), onehot, NT_DIMS,
                          preferred_element_type=F32)
    sub_rows = jnp.floor((cnt[0:1, :] + (SUB - 1.0)) * (1.0 / SUB)) * SUB
    lp = rank
    first_row = jnp.zeros((1, 1), F32)
    for g in range(1, N_GROUPS):
        first_row = first_row + sub_rows[:, g - 1:g]
        lp = lp + jnp.where(bgrp == g, first_row, 0.0)

    hi = [x.astype(BF16).astype(F32) for x in gates]
    lo = [(x - h).astype(BF16).astype(F32) for x, h in zip(gates, hi)]
    meta_src = jnp.concatenate(hi + lo + [lp, jnp.zeros((LANES - 9, TB), F32)], axis=0)
    meta_t = meta_src.T
    tokmeta_ref[b] = meta_t
    yield

    rowdata = jnp.concatenate([u2.astype(BF16), meta_t.astype(BF16)], axis=1)
    sort = _onehot(lax.broadcasted_iota(I32, (STG_P, TB), 0) == lp.astype(I32))
    stage_ref[b, slot] = jnp.dot(sort, rowdata, preferred_element_type=F32).astype(BF16)
    return [cnt[0, g].astype(I32) for g in range(N_GROUPS)]


def _prime_scatter(nb, cap, xs_hbm, stage_ref, prev_ref, ssem):
    stage_ref[:, 1] = jnp.zeros((nb,) + stage_ref.shape[2:], BF16)
    for b in range(nb):
        rows = [jnp.int32(N_GROUPS * cap + (b * N_SUB + s) * SUB) for s in range(N_SUB)]
        for s in range(N_SUB):
            prev_ref[b * N_SUB + s] = rows[s]
        for cp in _sub_copies(xs_hbm, stage_ref.at[b, 1], ssem.at[b, 1], rows, True):
            cp.start()


def _tail_dma(t, lag, n_steps, nb, cap, n_work, counts, tmeta_ref, xs_hbm, stage_ref, zeros_ref,
              run_ref, prev_ref, ssem, zsem, drain_head):
    slot = t % 2
    run = [run_ref[g] for g in range(N_GROUPS)]
    junk = N_GROUPS * cap
    scatter_rows = []
    for b in range(nb):
        first_sub = [jnp.int32(0)]
        for g in range(N_GROUPS):
            first_sub.append(first_sub[-1] + (counts[b][g] + (SUB - 1)) // SUB)
        dst = [g * cap + run[g] for g in range(N_GROUPS)]
        rows_b, base = [], (jnp.maximum(t - lag, 0) * nb + b) * META_W
        for s in range(N_SUB):
            grp_first, grp_dst = first_sub[0], dst[0]
            for g in range(1, N_GROUPS):
                later = s >= first_sub[g]
                grp_first = jnp.where(later, first_sub[g], grp_first)
                grp_dst = jnp.where(later, dst[g], grp_dst)
            row = grp_dst + (s - grp_first) * SUB
            used = s < first_sub[N_GROUPS]
            rows_b.append(jnp.where(used, row, junk + (b * N_SUB + s) * SUB))
            tmeta_ref[base + s] = jnp.where(used, row, row if s == 0 else first_row)
            if s == 0:
                first_row = row
        for k in range(N_SUB, META_W):
            tmeta_ref[base + k] = jnp.int32(0)
        for g in range(N_GROUPS):
            run[g] = run[g] + (first_sub[g + 1] - first_sub[g]) * SUB
        scatter_rows.append(rows_b)
    for g in range(N_GROUPS):
        run_ref[g] = run[g]

    def copies(b, slot_, rows):
        return _sub_copies(xs_hbm, stage_ref.at[b, slot_], ssem.at[b, slot_], rows, True)

    for b in range(nb):
        for cp in copies(b, 1 - slot, [prev_ref[b * N_SUB + s] for s in range(N_SUB)]):
            cp.wait()

    for b in range(nb):
        for cp in copies(b, slot, scatter_rows[b]):
            cp.start()
        for s in range(N_SUB):
            prev_ref[b * N_SUB + s] = scatter_rows[b][s]

    @pl.when(t == n_steps - 1 + lag)
    def _():
        for b in range(nb):
            for cp in copies(b, slot, scatter_rows[b]):
                cp.wait()
        if drain_head is not None:
            drain_head()
        zeros_ref[...] = jnp.zeros_like(zeros_ref)
        tails = []
        base = n_steps * nb * META_W
        for g in range(N_GROUPS):
            tmeta_ref[base + g] = run[g]
            start = pl.multiple_of(g * cap + run[g], SUB)
            tails.append(pltpu.make_async_copy(zeros_ref, xs_hbm.at[pl.ds(start, TBM), :], zsem))
        for k in range(N_GROUPS, META_W):
            tmeta_ref[base + k] = jnp.int32(0)
        for cp in tails:
            cp.start()
        _work_tables(run, cap, n_work, tmeta_ref, base + META_W)
        for cp in tails:
            cp.wait()


def _attn_conv_mix(x, t, mod, g_mix, win_ref, wout_ref, bias_ref, convw_ref,
                   kvprev_ref, cprev_ref):
    tb = TB
    first = t == 0
    sh1, sc1, g1 = mod[0:1], mod[1:2], mod[2:3]
    ub = _rms_mod(x, g_mix, sc1, sh1).astype(BF16)
    qkv_w = ATTN_WIDTH + 2 * KV_WIDTH
    proj = jnp.dot(ub, win_ref[:, 0:qkv_w], preferred_element_type=F32)

    yield
    q = (proj[:, 0:ATTN_WIDTH] * HEAD_DIM ** -0.5).astype(BF16)
    kf = proj[:, ATTN_WIDTH:ATTN_WIDTH + KV_WIDTH]
    vf = proj[:, ATTN_WIDTH + KV_WIDTH:qkv_w]

    kv_prev = kvprev_ref[...]
    kext = jnp.concatenate([kv_prev[:, 0:KV_WIDTH], kf], axis=0)
    vext = jnp.concatenate([kv_prev[:, KV_WIDTH:], vf], axis=0)
    kvprev_ref[:, 0:KV_WIDTH] = kf[tb - BLOCK:tb]
    kvprev_ref[:, KV_WIDTH:] = vf[tb - BLOCK:tb]

    lane = lax.broadcasted_iota(I32, kext.shape, 1)
    lo = lane < HEAD_DIM
    krol = pltpu.roll(kext, HEAD_DIM, axis=1)
    vrol = pltpu.roll(vext, HEAD_DIM, axis=1)
    zero = jnp.zeros_like(kext)
    one_at_64 = jnp.where(lane == HEAD_DIM, 1.0, 0.0)
    one_at_0 = jnp.where(lane == 0, 1.0, 0.0)
    k_ops = [(jnp.where(lo, kext, zero).astype(BF16), jnp.where(lo, zero, krol).astype(BF16)),
             (jnp.where(lo, krol, zero).astype(BF16), jnp.where(lo, zero, kext).astype(BF16))]
    v_ops = [(jnp.where(lo, vext, one_at_64).astype(BF16), jnp.where(lo, one_at_0, vrol).astype(BF16)),
             (jnp.where(lo, vrol, one_at_64).astype(BF16), jnp.where(lo, one_at_0, vext).astype(BF16))]

    col = lax.broadcasted_iota(I32, (2 * BLOCK, 4 * BLOCK), 1) % (2 * BLOCK)
    masked_at_start = (col < BLOCK) & (col > 0)
    sink_key = lax.broadcasted_iota(I32, (4 * BLOCK, 1), 0) % (2 * BLOCK) == 0
    sink_val = lax.broadcasted_iota(I32, (2 * BLOCK, 1), 0) == 0
    lane_o = lax.broadcasted_iota(I32, (2 * BLOCK, LANES), 1)
    lo_o = lane_o < HEAD_DIM
    ones_entry = [one_at_64[0:1].astype(BF16), one_at_0[0:1].astype(BF16)]

    yield
    units = [(bi, kvh) for bi in range(tb // BLOCK) for kvh in range(N_HEADS // 4)]

    def scores(bi, kvh):
        r0 = bi * BLOCK
        ka, kb = k_ops[kvh]
        rhs = jnp.concatenate([ka[r0:r0 + 2 * BLOCK], kb[r0:r0 + 2 * BLOCK]], axis=0)
        rhs = jnp.where(sink_key, jnp.zeros_like(rhs), rhs)
        qrows = q[r0:r0 + BLOCK]
        qp = jnp.concatenate([qrows[:, (2 * kvh) * LANES:(2 * kvh + 1) * LANES],
                              qrows[:, (2 * kvh + 1) * LANES:(2 * kvh + 2) * LANES]], axis=0)
        s = lax.dot_general(qp, rhs, NT_DIMS, preferred_element_type=F32)
        bias = bias_ref[kvh]
        if bi == 0:
            bias = jnp.where(masked_at_start & first, NEG_BIG, bias)
        return s + bias

    side = {}

    def conv_proj():
        side["gates"] = jnp.dot(ub, win_ref[:, qkv_w:IN_WIDTH], preferred_element_type=F32)

    def conv():
        g = side["gates"]
        bgate, cgate, xv = (g[:, i * CONV_WIDTH:(i + 1) * CONV_WIDTH] for i in range(3))
        uc = cgate * xv
        cprev = jnp.where(first, 0.0, cprev_ref[...])
        row = lax.broadcasted_iota(I32, uc.shape, 0)
        r1 = jnp.where(row == 0, cprev[7:8], pltpu.roll(uc, 1, axis=0))
        r2 = jnp.where(row == 0, cprev[6:7],
                       jnp.where(row == 1, cprev[7:8], pltpu.roll(uc, 2, axis=0)))
        cprev_ref[...] = uc[tb - 8:tb]
        cw = convw_ref[...]
        side["conv"] = (bgate * (cw[0:1] * r2 + cw[1:2] * r1 + cw[2:3] * uc)).astype(BF16)

    def conv_out():
        side["mix"] = jnp.dot(side["conv"], wout_ref[ATTN_WIDTH:, :], preferred_element_type=F32)

    fillers = [conv_proj, conv, conv_out]

    attn_rows = []
    s_next = scores(*units[0])
    for ui, (bi, kvh) in enumerate(units):
        r0 = bi * BLOCK
        if kvh == 0:
            pair_out = []
        s = s_next
        if ui + 1 < len(units):
            s_next = scores(*units[ui + 1])
        if ui < len(fillers):
            fillers[ui]()
        shs = [s[:, hh * 2 * BLOCK:(hh + 1) * 2 * BLOCK] for hh in range(2)]
        ms = [jnp.max(sh, axis=-1, keepdims=True) for sh in shs]
        ps = [jnp.exp(sh - m).astype(BF16) for sh, m in zip(shs, ms)]
        vops = [jnp.where(sink_val, ones_entry[hh], v_ops[kvh][hh][r0:r0 + 2 * BLOCK])
                for hh in range(2)]
        os_ = [jnp.dot(p, vop, preferred_element_type=F32) for p, vop in zip(ps, vops)]
        outs = []
        for hh in range(2):
            den_col = HEAD_DIM if hh == 0 else 0
            outs.append(os_[hh] / os_[hh][:, den_col:den_col + 1])
        both = jnp.where(lo_o, outs[0], outs[1])
        pair_out += [both[0:BLOCK], both[BLOCK:]]
        if kvh == N_HEADS // 4 - 1:
            attn_rows.append(jnp.concatenate(pair_out, axis=1))
        yield
    attn = jnp.concatenate(attn_rows, axis=0)

    mix = side["mix"] + jnp.dot(attn.astype(BF16), wout_ref[0:ATTN_WIDTH, :],
                                preferred_element_type=F32)
    return x + g1 * mix


def _pool_mix(x, t, mod, g_mix, wpool_ref, band_ref, pscale, uprev_ref):
    tb = TB
    sh1, sc1, g1 = mod[0:1], mod[1:2], mod[2:3]
    u = _rms_mod(x, g_mix, sc1, sh1)
    halo = jnp.where(t == 0, 0.0, uprev_ref[...]).astype(BF16)
    ext = jnp.concatenate([jnp.zeros((LANES - POOL_HALO, D_MODEL), BF16), halo, u.astype(BF16)],
                          axis=0)
    uprev_ref[...] = u[tb - POOL_HALO:tb]

    yield
    pos = (t * tb + 1 + lax.broadcasted_iota(I32, (tb, 1), 0)).astype(F32)
    cols = [slice(gi * POOL_GROUP, (gi + 1) * POOL_GROUP) for gi in range(len(POOL_SIZES))]
    wsums = [jnp.dot(band_ref[gi], ext[:, sl], preferred_element_type=F32)
             for gi, sl in enumerate(cols)]
    yield
    mixed = []
    for gi, w in enumerate(POOL_SIZES):
        mean = wsums[gi] / jnp.minimum(pos, float(w))
        pooled = mean - u[:, cols[gi]]
        mixed.append(jnp.dot(pooled.astype(BF16), wpool_ref[gi].astype(BF16),
                             preferred_element_type=F32))
    yield
    mix = jnp.concatenate(mixed, axis=1) * pscale
    return x + g1 * mix


def _interleave(chains, starts):
    results = [None] * len(chains)
    live = list(range(len(chains)))
    rnd = 0
    while live:
        for k in list(live):
            if rnd < starts[k]:
                continue
            try:
                next(chains[k])
            except StopIteration as done:
                results[k] = done.value
                live.remove(k)
        rnd += 1
    return results


def _mixer_kernel(*refs, kind, layer, lag, has_head, nb, n_steps, cap, n_work):
    refs = list(refs)
    t = pl.program_id(0)
    slot = t % 2
    if has_head:
        pm_ref = refs.pop(0)
        hprev_ref, modprev_ref, tokprev_ref, ys_hbm = refs[:4]
        refs = refs[4:]
    else:
        h_ref = refs.pop(0)
    mod_ref, nmix_ref, nffn_ref = refs[:3]
    refs = refs[3:]
    n_w = 5 if kind == "even" else 3
    weight_refs = refs[:n_w]
    refs = refs[n_w:]
    wrt_ref, br_ref, tri_ref = refs[:3]
    refs = refs[3:]
    hout_ref, tokmeta_ref, tmeta_ref, xs_hbm = refs[:4]
    refs = refs[4:]
    n_c = 2 if kind == "even" else 1
    carry_refs = refs[:n_c]
    refs = refs[n_c:]
    if kind == "even":
        cast_refs = refs[:2]
        sbias_ref = refs[2]
        refs = refs[3:]
    if lag:
        h1s_ref = refs.pop(0)
    stage_ref, zeros_ref, run_ref, prev_ref, ssem, zsem = refs[:6]
    refs = refs[6:]

    tile = jnp.minimum(t, n_steps - 1)

    @pl.when(t == 0)
    def _():
        if lag:
            h1s_ref[...] = jnp.zeros_like(h1s_ref)
        for r in carry_refs:
            r[...] = jnp.zeros_like(r)
        for g in range(N_GROUPS):
            run_ref[g] = jnp.int32(0)
        if kind == "even":
            for src, dst in zip(weight_refs[:2], cast_refs):
                for c0 in range(0, src.shape[1], 2 * LANES):
                    dst[:, c0:c0 + 2 * LANES] = src[:, c0:c0 + 2 * LANES].astype(BF16)
            bias_ref, sink_ref = weight_refs[2:4]
            col = lax.broadcasted_iota(I32, (2 * BLOCK, 4 * BLOCK), 1)
            upper = lax.broadcasted_iota(I32, (2 * BLOCK, 1), 0) < BLOCK
            for kvh in range(N_HEADS // 4):
                tab = bias_ref[kvh]
                for hh in range(2):
                    sink = jnp.where(upper, sink_ref[layer // 2, 4 * kvh + hh],
                                     sink_ref[layer // 2, 4 * kvh + 2 + hh])
                    tab = jnp.where(col == hh * 2 * BLOCK, sink, tab)
                sbias_ref[kvh] = tab
        _prime_scatter(nb, cap, xs_hbm, stage_ref, prev_ref, ssem)

    j = layer // 2
    g_mix = nmix_ref[layer:layer + 1]
    g_ffn = nffn_ref[layer:layer + 1]
    if kind == "even":
        weight_refs = list(cast_refs) + [sbias_ref, weight_refs[4]]
    else:
        weight_refs = list(weight_refs[:2]) + [weight_refs[2][j:j + 1]]

    drain_head = None
    if has_head:
        ystage_ref, gsem = refs
        drain_head = _head_dma(t, n_steps, nb, pm_ref, ys_hbm, ystage_ref, gsem)

    def main_chain(b):
        if has_head:
            x = _head_combine(b, slot, hprev_ref, modprev_ref, tokprev_ref, ystage_ref)
            yield
        else:
            x = h_ref[b]
        carries = [r.at[b] for r in carry_refs]
        mix = _attn_conv_mix if kind == "even" else _pool_mix
        h1 = yield from mix(x, tile, mod_ref[b], g_mix, *weight_refs, *carries)
        hout_ref[b] = h1
        if lag:
            h1s_ref[b, slot] = h1
        yield
        return h1

    def tail_chain(b, h1):
        return (yield from _tail_compute(b, slot, h1, mod_ref[b], g_ffn, wrt_ref, br_ref,
                                         tri_ref, tokmeta_ref, stage_ref))

    def lagged_tail_chain(b):
        cnt = yield from tail_chain(b, h1s_ref[b, 1 - slot])
        return [jnp.where(t >= lag, c, 0) for c in cnt]

    def whole_chain(b):
        h1 = yield from main_chain(b)
        return (yield from tail_chain(b, h1))

    if lag:
        chains = [main_chain(b) for b in range(nb)] + [lagged_tail_chain(b) for b in range(nb)]
        counts = _interleave(chains, [MAIN_START] * nb + [0] * nb)[nb:]
    else:
        counts = _interleave([whole_chain(b) for b in range(nb)], [0] * nb)
    _tail_dma(t, lag, n_steps, nb, cap, n_work, counts, tmeta_ref, xs_hbm, stage_ref, zeros_ref,
              run_ref, prev_ref, ssem, zsem, drain_head)


def _nat(nb, cols):
    return pl.BlockSpec((nb, TB, cols), lambda t, *_: (0, t, 0))


def _full(shape):
    nd = len(shape)
    return pl.BlockSpec(shape, lambda t, *_: (0,) * nd)


def _head_scratch(nb):
    return [pltpu.VMEM((nb, 2, STG_C, D_MODEL), BF16), pltpu.SemaphoreType.DMA((nb, 2))]


def _mixer_call(kind, layer, head, h, mod, norm_mix, norm_ffn, weights, wrt, br, tri):
    j = layer // 2
    nb, seq = (h if head is None else head[3]).shape[:2]
    n_steps = seq // TB
    n_tiles, cap, n_work = _sizes(nb * seq)
    has_head = head is not None
    any_spec = pl.BlockSpec(memory_space=pl.ANY)
    smem_spec = pl.BlockSpec(memory_space=pltpu.SMEM)

    def main(cols):
        return pl.BlockSpec((nb, TB, cols), lambda t, *_: (0, jnp.minimum(t, n_steps - 1), 0))

    lag = 1 if kind == "odd" else 0

    def lagged(cols):
        return pl.BlockSpec((nb, TB, cols), lambda t, *_: (0, jnp.maximum(t - lag, 0), 0))

    if has_head:
        pm, hprev, modprev, tokprev, ys = head
        args = [hprev, modprev, tokprev, ys]
        in_specs = [main(D_MODEL), _full(modprev.shape), main(LANES), any_spec]
    else:
        args = [h]
        in_specs = [main(D_MODEL)]
    args += [mod, norm_mix, norm_ffn]
    in_specs += [_full(mod.shape), _full(norm_mix.shape), _full(norm_ffn.shape)]
    if kind == "even":
        args += list(weights)
        once = dict(pipeline_mode=pl.Buffered(1))
        in_specs += [pl.BlockSpec((None, D_MODEL, IN_WIDTH), lambda t, *_: (j, 0, 0), **once),
                     pl.BlockSpec((None, D_MODEL, D_MODEL), lambda t, *_: (j, 0, 0), **once),
                     _full((N_HEADS // 4, 2 * BLOCK, 4 * BLOCK)), smem_spec,
                     pl.BlockSpec((None,) + weights[4].shape[1:], lambda t, *_: (j, 0, 0))]
        mix_scratch = [pltpu.VMEM((nb, BLOCK, 2 * KV_WIDTH), F32),
                       pltpu.VMEM((nb, 8, CONV_WIDTH), F32),
                       pltpu.VMEM((D_MODEL, IN_WIDTH), BF16),
                       pltpu.VMEM((D_MODEL, D_MODEL), BF16),
                       pltpu.VMEM((N_HEADS // 4, 2 * BLOCK, 4 * BLOCK), F32)]
    else:
        args += list(weights)
        in_specs += [pl.BlockSpec((None, len(POOL_SIZES), POOL_GROUP, POOL_GROUP),
                                  lambda t, *_: (j, 0, 0, 0)),
                     _full((len(POOL_SIZES), TB, TB + LANES)), _full(weights[2].shape)]
        mix_scratch = [pltpu.VMEM((nb, POOL_HALO, D_MODEL), F32)]
    args += [wrt, br, tri]
    in_specs += [_full((2 * N_EXPERTS, D_MODEL)), _full((N_EXPERTS, 1)), _full((TB, TB))]

    out_shape = (jax.ShapeDtypeStruct((nb, seq + lag * TB, D_MODEL), F32),
                 jax.ShapeDtypeStruct((nb, seq, LANES), F32),
                 jax.ShapeDtypeStruct(((n_tiles + 1) * META_W + 4 * n_work,), I32),
                 jax.ShapeDtypeStruct((N_GROUPS * cap + nb * N_SUB * SUB, ROW_W), BF16))
    out_specs = (_nat(nb, D_MODEL), lagged(LANES), smem_spec, any_spec)
    scratch = mix_scratch + ([pltpu.VMEM((nb, 2, TB, D_MODEL), F32)] if lag else []) + [
        pltpu.VMEM((nb, 2, STG_P, ROW_W), BF16),
        pltpu.VMEM((TBM, ROW_W), BF16),
        pltpu.SMEM((N_GROUPS,), I32),
        pltpu.SMEM((nb * N_SUB,), I32),
        pltpu.SemaphoreType.DMA((nb, 2)),
        pltpu.SemaphoreType.DMA(()),
    ]
    if has_head:
        scratch += _head_scratch(nb)

    body = functools.partial(_mixer_kernel, kind=kind, layer=layer, lag=lag, has_head=has_head,
                             nb=nb, n_steps=n_steps, cap=cap, n_work=n_work)
    grid_spec = pltpu.PrefetchScalarGridSpec(
        num_scalar_prefetch=1 if has_head else 0, grid=(n_steps + lag,),
        in_specs=in_specs, out_specs=out_specs, scratch_shapes=scratch)
    call = pl.pallas_call(
        body, grid_spec=grid_spec, out_shape=out_shape,
        compiler_params=pltpu.CompilerParams(
            dimension_semantics=("arbitrary",), vmem_limit_bytes=VMEM_LIMIT),
        name=kind + "_mixer")
    if has_head:
        return call(pm, *args)
    return call(*args)


def _work_tables(totals, cap, n_work, tmeta_ref, t0):
    ends, starts = [], []
    acc = jnp.int32(0)
    for g in range(N_GROUPS):
        starts.append(acc)
        acc = acc + (totals[g] + (TBM - 1)) // TBM
        ends.append(acc)
    nvalid = ends[-1]
    prev_grp = jnp.int32(-1)
    for i in range(n_work):
        idc = jnp.minimum(jnp.int32(i), nvalid - 1)
        grp, start, total = jnp.int32(0), starts[0], totals[0]
        for g in range(1, N_GROUPS):
            later = idc >= ends[g - 1]
            grp = jnp.where(later, g, grp)
            start = jnp.where(later, starts[g], start)
            total = jnp.where(later, totals[g], total)
        valid = i < nvalid
        tile = idc - start
        tmeta_ref[t0 + i] = grp * (cap // TBM) + tile
        tmeta_ref[t0 + n_work + i] = grp
        tmeta_ref[t0 + 2 * n_work + i] = jnp.where(valid, jnp.minimum(total - tile * TBM, TBM), 0)
        tmeta_ref[t0 + 3 * n_work + i] = jnp.logical_and(valid, grp != prev_grp).astype(I32)
        prev_grp = grp


def _expert_rows(xs_ref, ys_ref, wg_s, wu_s, wd_s, r0, nrows):
    xb = xs_ref[r0:r0 + nrows, 0:D_MODEL]
    meta = xs_ref[r0:r0 + nrows, D_MODEL:ROW_W].astype(F32)
    gates = meta[:, 0:EXPERTS_PER_GROUP] + meta[:, EXPERTS_PER_GROUP:2 * EXPERTS_PER_GROUP]
    parts = []
    for e in range(EXPERTS_PER_GROUP):
        a = jnp.dot(xb, wg_s[e], preferred_element_type=F32)
        bu = jnp.dot(xb, wu_s[e], preferred_element_type=F32)
        hid = (a * _sigmoid(a)) * bu
        parts.append((hid * gates[:, e:e + 1]).astype(BF16))
    hid_all = jnp.concatenate(parts, axis=1)
    ys_ref[r0:r0 + nrows, :] = jnp.dot(hid_all, wd_s[...],
                                       preferred_element_type=F32).astype(BF16)


def _moe_kernel(tm_ref, *refs, t0, n_work, nb):
    if nb:
        xs_ref, wg_ref, wu_ref, wd_ref, ct_ref, wada_ref, bada_ref = refs[:7]
        ys_ref, mod_ref, wg_s, wu_s, wd_s = refs[7:]
    else:
        xs_ref, wg_ref, wu_ref, wd_ref, ys_ref, wg_s, wu_s, wd_s = refs
    i = pl.program_id(0)
    rows = tm_ref[t0 + 2 * n_work + i]
    first_of_group = tm_ref[t0 + 3 * n_work + i] == 1

    @pl.when(first_of_group)
    def _():
        for e in range(EXPERTS_PER_GROUP):
            wg_s[e] = wg_ref[e].astype(BF16)
            wu_s[e] = wu_ref[e].astype(BF16)
        wd_s[...] = wd_ref[0].astype(BF16)

    full = rows == TBM
    if nb:
        mod_step = i < N_MOD

        def next_mod():
            _mod_store(mod_ref, _mod_rows(ct_ref, wada_ref, bada_ref, nb), i)

        @pl.when(i == 0)
        def _():
            mod_ref[...] = jnp.zeros_like(mod_ref)

        @pl.when(jnp.logical_and(full, mod_step))
        def _():
            next_mod()
            _expert_rows(xs_ref, ys_ref, wg_s, wu_s, wd_s, 0, TBM)

        full = jnp.logical_and(full, jnp.logical_not(mod_step))

    @pl.when(full)
    def _():
        _expert_rows(xs_ref, ys_ref, wg_s, wu_s, wd_s, 0, TBM)

    @pl.when(jnp.logical_and(rows > 0, rows < TBM))
    def _():
        if nb:
            pl.when(mod_step)(next_mod)
        for r0 in range(0, TBM, TBM_PART):
            @pl.when(r0 < rows)
            def _():
                _expert_rows(xs_ref, ys_ref, wg_s, wu_s, wd_s, r0, TBM_PART)

            @pl.when(r0 >= rows)
            def _():
                ys_ref[r0:r0 + TBM_PART, :] = jnp.zeros((TBM_PART, D_MODEL), BF16)


def _moe_call(xs, tmeta, layer, wg, wu, wd, n_tiles, n_work, next_mod=None):
    t0 = (n_tiles + 1) * META_W
    row_map = lambda i, tm: (tm[t0 + i], 0)
    grp_map = lambda i, tm: (layer, tm[t0 + n_work + i], 0, 0)
    args = [xs, wg, wu, wd]
    in_specs = [
        pl.BlockSpec((TBM, ROW_W), row_map),
        pl.BlockSpec((None, EXPERTS_PER_GROUP, D_MODEL, EXPERT_FF), grp_map),
        pl.BlockSpec((None, EXPERTS_PER_GROUP, D_MODEL, EXPERT_FF), grp_map),
        pl.BlockSpec((None, 1, GROUP_FF, D_MODEL), grp_map),
    ]
    out_shape = [jax.ShapeDtypeStruct((xs.shape[0], D_MODEL), BF16)]
    out_specs = [pl.BlockSpec((TBM, D_MODEL), row_map)]
    nb = 0
    if next_mod is not None:
        ct, nb, w_ada, b_ada3 = next_mod
        vec_map = lambda i, tm: (layer + 1, 0, jnp.minimum(i, N_MOD - 1))
        args += [ct, w_ada, b_ada3]
        in_specs += [pl.BlockSpec((D_MODEL, LANES), lambda i, tm: (0, 0)),
                     pl.BlockSpec((None, D_MODEL, D_MODEL), vec_map),
                     pl.BlockSpec((None, 1, D_MODEL), vec_map)]
        out_shape.append(jax.ShapeDtypeStruct((nb, 8, D_MODEL), F32))
        out_specs.append(pl.BlockSpec((nb, 8, D_MODEL), lambda i, tm: (0, 0, 0)))
    grid_spec = pltpu.PrefetchScalarGridSpec(
        num_scalar_prefetch=1, grid=(n_work,),
        in_specs=in_specs, out_specs=out_specs,
        scratch_shapes=[pltpu.VMEM((EXPERTS_PER_GROUP, D_MODEL, EXPERT_FF), BF16),
                        pltpu.VMEM((EXPERTS_PER_GROUP, D_MODEL, EXPERT_FF), BF16),
                        pltpu.VMEM((GROUP_FF, D_MODEL), BF16)],
    )
    out = pl.pallas_call(
        functools.partial(_moe_kernel, t0=t0, n_work=n_work, nb=nb), grid_spec=grid_spec,
        out_shape=out_shape,
        compiler_params=pltpu.CompilerParams(
            dimension_semantics=("arbitrary",), vmem_limit_bytes=VMEM_LIMIT),
        name="moe_experts",
    )(tmeta, *args)
    return out if next_mod is not None else (out[0], None)


def _final_kernel(pm_ref, hprev_ref, modprev_ref, tokprev_ref, ys_hbm, nf_ref, o_ref,
                  ystage_ref, gsem, *, nb, n_steps):
    t = pl.program_id(0)
    drain = _head_dma(t, n_steps, nb, pm_ref, ys_hbm, ystage_ref, gsem)
    for b in range(nb):
        h = _head_combine(b, t % 2, hprev_ref, modprev_ref, tokprev_ref, ystage_ref)
        ms = jnp.mean(h * h, axis=-1, keepdims=True)
        o_ref[b] = (h * lax.rsqrt(ms + EPS)) * nf_ref[0:1]
    pl.when(t == n_steps - 1)(drain)


def _final_call(head, norm_final):
    pm, hprev, modprev, tokprev, ys = head
    nb, seq = tokprev.shape[:2]
    n_steps = seq // TB
    grid_spec = pltpu.PrefetchScalarGridSpec(
        num_scalar_prefetch=1, grid=(n_steps,),
        in_specs=[_nat(nb, D_MODEL), _full(modprev.shape), _nat(nb, LANES),
                  pl.BlockSpec(memory_space=pl.ANY), _full((1, D_MODEL))],
        out_specs=_nat(nb, D_MODEL),
        scratch_shapes=_head_scratch(nb))
    return pl.pallas_call(
        functools.partial(_final_kernel, nb=nb, n_steps=n_steps), grid_spec=grid_spec,
        out_shape=jax.ShapeDtypeStruct((nb, seq, D_MODEL), F32),
        compiler_params=pltpu.CompilerParams(
            dimension_semantics=("arbitrary",), vmem_limit_bytes=VMEM_LIMIT),
        name="final_norm",
    )(pm, hprev, modprev, tokprev, ys, norm_final.reshape(1, D_MODEL))


def _pool_band_table():
    r = jnp.arange(TB)[:, None]
    c = jnp.arange(TB + LANES)[None, :]
    return jnp.stack([(c > r + LANES - w) & (c <= r + LANES) for w in POOL_SIZES]).astype(BF16)


def _attn_bias_table():
    slopes = 2.0 ** (-8.0 * (jnp.arange(N_HEADS, dtype=F32) + 1.0) / N_HEADS)
    dist = (jnp.arange(BLOCK)[:, None] + BLOCK) - jnp.arange(2 * BLOCK)[None, :]
    ok = (dist >= 0) & (dist < WINDOW)
    per_head = jnp.where(ok[None], -slopes[:, None, None] * dist.astype(F32)[None], NEG_BIG)
    return per_head.reshape(N_HEADS // 2, 2, BLOCK, 2 * BLOCK).transpose(0, 2, 1, 3).reshape(
        N_HEADS // 4, 2 * BLOCK, 4 * BLOCK)


def _router_operands(w_router, b_router):
    def reorder(a):
        return a.reshape(N_GROUPS, EXPERTS_PER_GROUP, -1).transpose(1, 0, 2).reshape(N_EXPERTS, -1)
    w = reorder(w_router.T)
    wh = w.astype(BF16)
    wl = (w - wh.astype(F32)).astype(BF16)
    return jnp.concatenate([wh, wl], axis=0), reorder(b_router[:, None])


def kernel(x, c, w_ada, b_ada, norm_mix, norm_ffn, w_in, w_out, sinks, conv_w, w_pool,
           pool_scale, w_router, b_router, w_gate, w_up, w_down, norm_final):
    b, s, _ = x.shape
    n_tiles, cap, n_work = _sizes(b * s)

    ct = jnp.pad(c.T, ((0, 0), (0, LANES - b)))
    b_ada3 = b_ada.reshape(DEPTH, 1, N_MOD * D_MODEL)
    mod = _mod_call(ct, b, w_ada, b_ada3, 0)
    bias_tab = _attn_bias_table()
    band_tab = _pool_band_table()
    wrt, br = _router_operands(w_router, b_router)
    tri = jnp.triu(jnp.ones((TB, TB), F32)).astype(BF16)
    w_down4 = w_down.reshape(DEPTH, N_GROUPS, GROUP_FF, D_MODEL)

    h = x
    head = None
    for layer in range(DEPTH):
        if layer % 2 == 0:
            kind = "even"
            weights = (w_in, w_out, bias_tab, sinks, conv_w)
        else:
            kind = "odd"
            weights = (w_pool, band_tab, pool_scale)
        h1, tokmeta, tmeta, xs = _mixer_call(kind, layer, head, h, mod, norm_mix, norm_ffn,
                                             weights, wrt, br, tri)
        next_mod = (ct, b, w_ada, b_ada3) if layer + 1 < DEPTH else None
        ys, mod_next = _moe_call(xs, tmeta, layer, w_gate, w_up, w_down4, n_tiles, n_work, next_mod)
        head = (tmeta, h1, mod, tokmeta, ys)
        mod = mod_next
    return _final_call(head, norm_final)
```
